```python
import math
import jax, jax.numpy as jnp
from jax import lax
import numpy as np

D_MODEL = 2048
BATCH = 8
SEQ = 8192
DEPTH = 4

CHUNK = 64
N_MIXERS = 4
N_CONV_LAYERS = len(range(0, DEPTH, N_MIXERS))
N_POOL_LAYERS = len(range(1, DEPTH, N_MIXERS))
N_ATT_LAYERS = len(range(2, DEPTH, N_MIXERS))
N_SSM_LAYERS = len(range(3, DEPTH, N_MIXERS))
D_FF = 4 * D_MODEL
CONV_WIDTH = 3
POOL_WINDOWS = (2, 4, 8, 16)
N_POOL_GROUPS = len(POOL_WINDOWS)
POOL_GROUP = D_MODEL // N_POOL_GROUPS
ATT_HEAD_DIM = 128
ATT_HEADS = D_MODEL // ATT_HEAD_DIM
ATT_LEFT_CHUNKS = 8
ATT_PAD = ATT_LEFT_CHUNKS * CHUNK
ATT_BAND = ATT_PAD + CHUNK
REL_CLIP = 256
MASK_VALUE = -1e30
SSM_GROUP = 16
SSM_GROUPS = D_MODEL // SSM_GROUP
SSM_STATE = 64
SSM_BLOCK = 16
SSM_N_BLOCKS = SSM_GROUPS // SSM_BLOCK
DT_MIN = 1e-3
DT_MAX = 1e-1
RMS_EPS = 1e-6

kernel_name = 'interleaved_hybrid_streaming_encoder'


def rms_norm(x, gain):
    xf = x.astype(jnp.float32)
    y = xf * lax.rsqrt(jnp.mean(xf * xf, axis=-1, keepdims=True) + RMS_EPS)
    return (y * gain.astype(jnp.float32)).astype(x.dtype)


def squared_relu_mlp(h, w1, w2):
    a = jax.nn.relu(h @ w1)
    return (a * a) @ w2


def short_conv_mixer(h, w_in, conv_w, w_out):
    b_gate, c_gate, v = jnp.split(h @ w_in, 3, axis=-1)
    u = c_gate * v
    conv = lax.conv_general_dilated(
        u, conv_w.reshape(CONV_WIDTH, 1, D_MODEL).astype(u.dtype),
        window_strides=(1,), padding=[(CONV_WIDTH - 1, 0)],
        dimension_numbers=('NWC', 'WIO', 'NWC'), feature_group_count=D_MODEL)
    return (b_gate * conv) @ w_out


def pool_mixer(h, w_in, w_group, scale):
    b, s, _ = h.shape
    u = (h @ w_in).astype(jnp.float32).reshape(b, s, N_POOL_GROUPS, POOL_GROUP)
    csum = jnp.cumsum(u, axis=1)
    pos = jnp.arange(1, s + 1, dtype=jnp.float32)
    outs = []
    for gi, w in enumerate(POOL_WINDOWS):
        c = csum[:, :, gi]
        lagged = jnp.pad(c, ((0, 0), (w, 0), (0, 0)))[:, :s]
        count = jnp.minimum(pos, float(w))[None, :, None]
        outs.append((c - lagged) / count - u[:, :, gi])
    pooled = jnp.stack(outs, axis=2).astype(h.dtype)
    y = jnp.einsum('bsgc,gcd->bsgd', pooled, w_group)
    return y.reshape(b, s, D_MODEL) * scale


def chunk_attention_mixer(h, w_qkv, q_gain, k_gain, rel_bias, w_out):
    b, s, _ = h.shape
    nc = s // CHUNK
    qkv = (h @ w_qkv).reshape(b, s, 3, ATT_HEADS, ATT_HEAD_DIM)
    q = rms_norm(qkv[:, :, 0], q_gain)
    k = rms_norm(qkv[:, :, 1], k_gain)
    v = qkv[:, :, 2]
    k_pad = jnp.pad(k, ((0, 0), (ATT_PAD, 0), (0, 0), (0, 0)))
    v_pad = jnp.pad(v, ((0, 0), (ATT_PAD, 0), (0, 0), (0, 0)))
    q_idx = jnp.arange(CHUNK)[:, None] + ATT_PAD
    k_idx = jnp.arange(ATT_BAND)[None, :]
    rel = jnp.clip(q_idx - k_idx, -REL_CLIP, REL_CLIP) + REL_CLIP
    bias = rel_bias[:, rel].astype(jnp.float32)
    q_chunks = q.reshape(b, nc, CHUNK, ATT_HEADS, ATT_HEAD_DIM).transpose(1, 0, 2, 3, 4)
    scale = ATT_HEAD_DIM ** -0.5

    def one_chunk(args):
        c, q_c = args
        start = c * CHUNK
        k_band = lax.dynamic_slice_in_dim(k_pad, start, ATT_BAND, axis=1)
        v_band = lax.dynamic_slice_in_dim(v_pad, start, ATT_BAND, axis=1)
        scores = jnp.einsum('bqhd,bkhd->bhqk', q_c, k_band).astype(jnp.float32) * scale + bias
        key_pos = start - ATT_PAD + jnp.arange(ATT_BAND)
        scores = jnp.where((key_pos >= 0)[None, None, None, :], scores, MASK_VALUE)
        probs = jax.nn.softmax(scores, axis=-1).astype(v_band.dtype)
        return jnp.einsum('bhqk,bkhd->bqhd', probs, v_band)

    out = lax.map(one_chunk, (jnp.arange(nc), q_chunks))
    out = out.transpose(1, 0, 2, 3, 4).reshape(b, s, D_MODEL)
    return out @ w_out


def _ssm_combine(left, right):
    a1, b1 = left
    a2, b2 = right
    return a1 * a2, a2 * b1 + b2


def s5_mixer(h, a_re, a_im, log_dt, b_re, b_im, c_re, c_im, d_skip, w_glu):
    b, s, _ = h.shape
    f32 = jnp.float32
    u_flat = h.astype(f32)
    lam = lax.complex(a_re.astype(f32), a_im.astype(f32))
    dt = jnp.exp(log_dt.astype(f32))[:, None]
    a_bar = jnp.exp(lam * dt)
    b_mat = lax.complex(b_re.astype(f32), b_im.astype(f32))
    b_bar = ((a_bar - 1.0) / lam)[..., None] * b_mat
    c_mat = lax.complex(c_re.astype(f32), c_im.astype(f32))

    def to_blocks(t):
        return t.reshape(SSM_N_BLOCKS, SSM_BLOCK, *t.shape[1:])

    u_blk = u_flat.reshape(b, s, SSM_N_BLOCKS, SSM_BLOCK, SSM_GROUP).transpose(2, 0, 1, 3, 4)

    def scan_block(args):
        u_b, a_b, bb_b, c_b = args
        bu = jnp.einsum('bsgc,gnc->bsgn', u_b.astype(jnp.complex64), bb_b)
        a_t = jnp.broadcast_to(a_b, bu.shape)
        _, states = lax.associative_scan(_ssm_combine, (a_t, bu), axis=1)
        return jnp.real(jnp.einsum('bsgn,gcn->bsgc', states, c_b))

    y = lax.map(scan_block, (u_blk, to_blocks(a_bar), to_blocks(b_bar), to_blocks(c_mat)))
    y = y.transpose(1, 2, 0, 3, 4).reshape(b, s, D_MODEL) + d_skip.astype(f32) * u_flat
    z = jax.nn.gelu(y).astype(h.dtype)
    val, gate = jnp.split(z @ w_glu, 2, axis=-1)
    return val * jax.nn.sigmoid(gate)


def _fwd_setup_inputs(seed: int = 0) -> dict:
    key = jax.random.key(seed)
    ks = jax.random.split(key, 32)
    f32 = jnp.float32

    def nrm(k, shape, scale):
        return jax.random.normal(k, shape, f32) * scale

    nA, nB, nC, nD = N_CONV_LAYERS, N_POOL_LAYERS, N_ATT_LAYERS, N_SSM_LAYERS
    G, N = SSM_GROUPS, SSM_STATE
    inv_d = D_MODEL ** -0.5
    return {
        'x': nrm(ks[0], (BATCH, SEQ, D_MODEL), 1.0),
        'norm_mix': 1.0 + nrm(ks[1], (DEPTH, D_MODEL), 0.02),
        'norm_mlp': 1.0 + nrm(ks[2], (DEPTH, D_MODEL), 0.02),
        'mlp_w1': nrm(ks[3], (DEPTH, D_MODEL, D_FF), inv_d),
        'mlp_w2': nrm(ks[4], (DEPTH, D_FF, D_MODEL), D_FF ** -0.5),
        'conv_w_in': nrm(ks[5], (nA, D_MODEL, 3 * D_MODEL), inv_d),
        'conv_w': nrm(ks[6], (nA, CONV_WIDTH, D_MODEL), CONV_WIDTH ** -0.5),
        'conv_w_out': nrm(ks[7], (nA, D_MODEL, D_MODEL), inv_d),
        'pool_w_in': nrm(ks[8], (nB, D_MODEL, D_MODEL), inv_d),
        'pool_w_group': nrm(ks[9], (nB, N_POOL_GROUPS, POOL_GROUP, POOL_GROUP), POOL_GROUP ** -0.5),
        'pool_scale': 1.0 + nrm(ks[10], (nB, D_MODEL), 0.1),
        'att_w_qkv': nrm(ks[11], (nC, D_MODEL, 3 * D_MODEL), inv_d),
        'att_q_norm': 1.0 + nrm(ks[12], (nC, ATT_HEAD_DIM), 0.02),
        'att_k_norm': 1.0 + nrm(ks[13], (nC, ATT_HEAD_DIM), 0.02),
        'att_rel_bias': nrm(ks[14], (nC, ATT_HEADS, 2 * REL_CLIP + 1), 0.5),
        'att_w_out': nrm(ks[15], (nC, D_MODEL, D_MODEL), inv_d),
        'ssm_a_re': -0.5 + nrm(ks[16], (nD, G, N), 0.01),
        'ssm_a_im': math.pi * jnp.arange(N, dtype=f32) + nrm(ks[17], (nD, G, N), 0.01),
        'ssm_log_dt': jax.random.uniform(ks[18], (nD, G), f32, math.log(DT_MIN), math.log(DT_MAX)),
        'ssm_b_re': nrm(ks[19], (nD, G, N, SSM_GROUP), (2 * SSM_GROUP) ** -0.5),
        'ssm_b_im': nrm(ks[20], (nD, G, N, SSM_GROUP), (2 * SSM_GROUP) ** -0.5),
        'ssm_c_re': nrm(ks[21], (nD, G, SSM_GROUP, N), (2 * N) ** -0.5 * 4.0),
        'ssm_c_im': nrm(ks[22], (nD, G, SSM_GROUP, N), (2 * N) ** -0.5 * 4.0),
        'ssm_d': nrm(ks[23], (nD, D_MODEL), 1.0),
        'ssm_w_glu': nrm(ks[24], (nD, D_MODEL, 2 * D_MODEL), inv_d),
    }


def _fwd_reference(x, norm_mix, norm_mlp, mlp_w1, mlp_w2, conv_w_in, conv_w, conv_w_out,
              pool_w_in, pool_w_group, pool_scale, att_w_qkv, att_q_norm, att_k_norm,
              att_rel_bias, att_w_out, ssm_a_re, ssm_a_im, ssm_log_dt, ssm_b_re, ssm_b_im,
              ssm_c_re, ssm_c_im, ssm_d, ssm_w_glu):
    for i in range(DEPTH):
        kind = i % N_MIXERS
        j = i // N_MIXERS
        h = rms_norm(x, norm_mix[i])
        if kind == 0:
            m = short_conv_mixer(h, conv_w_in[j], conv_w[j], conv_w_out[j])
        elif kind == 1:
            m = pool_mixer(h, pool_w_in[j], pool_w_group[j], pool_scale[j])
        elif kind == 2:
            m = chunk_attention_mixer(h, att_w_qkv[j], att_q_norm[j], att_k_norm[j],
                                      att_rel_bias[j], att_w_out[j])
        else:
            m = s5_mixer(h, ssm_a_re[j], ssm_a_im[j], ssm_log_dt[j], ssm_b_re[j], ssm_b_im[j],
                         ssm_c_re[j], ssm_c_im[j], ssm_d[j], ssm_w_glu[j])
        x = x + m.astype(x.dtype)
        h = rms_norm(x, norm_mlp[i])
        x = x + squared_relu_mlp(h, mlp_w1[i], mlp_w2[i]).astype(x.dtype)
    return x


import jax as _jax
import jax.numpy as _jnp

TWIN_FORMAT = 'train_step'
FWD_PARAMS = ['x', 'norm_mix', 'norm_mlp', 'mlp_w1', 'mlp_w2', 'conv_w_in', 'conv_w', 'conv_w_out', 'pool_w_in', 'pool_w_group', 'pool_scale', 'att_w_qkv', 'att_q_norm', 'att_k_norm', 'att_rel_bias', 'att_w_out', 'ssm_a_re', 'ssm_a_im', 'ssm_log_dt', 'ssm_b_re', 'ssm_b_im', 'ssm_c_re', 'ssm_c_im', 'ssm_d', 'ssm_w_glu']
TWIN_WEIGHTS = ['norm_mix', 'norm_mlp', 'mlp_w1', 'mlp_w2', 'conv_w_in', 'conv_w', 'conv_w_out', 'pool_w_in', 'pool_w_group', 'pool_scale', 'att_w_qkv', 'att_q_norm', 'att_k_norm', 'att_rel_bias', 'att_w_out', 'ssm_a_re', 'ssm_a_im', 'ssm_log_dt', 'ssm_b_re', 'ssm_b_im', 'ssm_c_re', 'ssm_c_im', 'ssm_d', 'ssm_w_glu']
TWIN_DIFF_INPUT = 'x'
TWIN_INPUTS = ['x', 'norm_mix', 'norm_mlp', 'mlp_w1', 'mlp_w2', 'conv_w_in', 'conv_w', 'conv_w_out', 'pool_w_in', 'pool_w_group', 'pool_scale', 'att_w_qkv', 'att_q_norm', 'att_k_norm', 'att_rel_bias', 'att_w_out', 'ssm_a_re', 'ssm_a_im', 'ssm_log_dt', 'ssm_b_re', 'ssm_b_im', 'ssm_c_re', 'ssm_c_im', 'ssm_d', 'ssm_w_glu', 'loss_target', 'm_norm_mix', 'm_norm_mlp', 'm_mlp_w1', 'm_mlp_w2', 'm_conv_w_in', 'm_conv_w', 'm_conv_w_out', 'm_pool_w_in', 'm_pool_w_group', 'm_pool_scale', 'm_att_w_qkv', 'm_att_q_norm', 'm_att_k_norm', 'm_att_rel_bias', 'm_att_w_out', 'm_ssm_a_re', 'm_ssm_a_im', 'm_ssm_log_dt', 'm_ssm_b_re', 'm_ssm_b_im', 'm_ssm_c_re', 'm_ssm_c_im', 'm_ssm_d', 'm_ssm_w_glu', 'v_norm_mix', 'v_norm_mlp', 'v_mlp_w1', 'v_mlp_w2', 'v_conv_w_in', 'v_conv_w', 'v_conv_w_out', 'v_pool_w_in', 'v_pool_w_group', 'v_pool_scale', 'v_att_w_qkv', 'v_att_q_norm', 'v_att_k_norm', 'v_att_rel_bias', 'v_att_w_out', 'v_ssm_a_re', 'v_ssm_a_im', 'v_ssm_log_dt', 'v_ssm_b_re', 'v_ssm_b_im', 'v_ssm_c_re', 'v_ssm_c_im', 'v_ssm_d', 'v_ssm_w_glu']
TWIN_OUTPUTS = ['loss', 'grad_x', 'grad_norm_mix', 'grad_norm_mlp', 'grad_mlp_w1', 'grad_mlp_w2', 'grad_conv_w_in', 'grad_conv_w', 'grad_conv_w_out', 'grad_pool_w_in', 'grad_pool_w_group', 'grad_pool_scale', 'grad_att_w_qkv', 'grad_att_q_norm', 'grad_att_k_norm', 'grad_att_rel_bias', 'grad_att_w_out', 'grad_ssm_a_re', 'grad_ssm_a_im', 'grad_ssm_log_dt', 'grad_ssm_b_re', 'grad_ssm_b_im', 'grad_ssm_c_re', 'grad_ssm_c_im', 'grad_ssm_d', 'grad_ssm_w_glu', 'delta_norm_mix', 'delta_norm_mlp', 'delta_mlp_w1', 'delta_mlp_w2', 'delta_conv_w_in', 'delta_conv_w', 'delta_conv_w_out', 'delta_pool_w_in', 'delta_pool_w_group', 'delta_pool_scale', 'delta_att_w_qkv', 'delta_att_q_norm', 'delta_att_k_norm', 'delta_att_rel_bias', 'delta_att_w_out', 'delta_ssm_a_re', 'delta_ssm_a_im', 'delta_ssm_log_dt', 'delta_ssm_b_re', 'delta_ssm_b_im', 'delta_ssm_c_re', 'delta_ssm_c_im', 'delta_ssm_d', 'delta_ssm_w_glu', 'new_m_norm_mix', 'new_m_norm_mlp', 'new_m_mlp_w1', 'new_m_mlp_w2', 'new_m_conv_w_in', 'new_m_conv_w', 'new_m_conv_w_out', 'new_m_pool_w_in', 'new_m_pool_w_group', 'new_m_pool_scale', 'new_m_att_w_qkv', 'new_m_att_q_norm', 'new_m_att_k_norm', 'new_m_att_rel_bias', 'new_m_att_w_out', 'new_m_ssm_a_re', 'new_m_ssm_a_im', 'new_m_ssm_log_dt', 'new_m_ssm_b_re', 'new_m_ssm_b_im', 'new_m_ssm_c_re', 'new_m_ssm_c_im', 'new_m_ssm_d', 'new_m_ssm_w_glu', 'new_v_norm_mix', 'new_v_norm_mlp', 'new_v_mlp_w1', 'new_v_mlp_w2', 'new_v_conv_w_in', 'new_v_conv_w', 'new_v_conv_w_out', 'new_v_pool_w_in', 'new_v_pool_w_group', 'new_v_pool_scale', 'new_v_att_w_qkv', 'new_v_att_q_norm', 'new_v_att_k_norm', 'new_v_att_rel_bias', 'new_v_att_w_out', 'new_v_ssm_a_re', 'new_v_ssm_a_im', 'new_v_ssm_log_dt', 'new_v_ssm_b_re', 'new_v_ssm_b_im', 'new_v_ssm_c_re', 'new_v_ssm_c_im', 'new_v_ssm_d', 'new_v_ssm_w_glu']
TWIN_LEAF_KINDS = {'loss': 'loss', 'grad_x': 'grad_x', 'grad_norm_mix': 'grad_w', 'grad_norm_mlp': 'grad_w', 'grad_mlp_w1': 'grad_w', 'grad_mlp_w2': 'grad_w', 'grad_conv_w_in': 'grad_w', 'grad_conv_w': 'grad_w', 'grad_conv_w_out': 'grad_w', 'grad_pool_w_in': 'grad_w', 'grad_pool_w_group': 'grad_w', 'grad_pool_scale': 'grad_w', 'grad_att_w_qkv': 'grad_w', 'grad_att_q_norm': 'grad_w', 'grad_att_k_norm': 'grad_w', 'grad_att_rel_bias': 'grad_w', 'grad_att_w_out': 'grad_w', 'grad_ssm_a_re': 'grad_w', 'grad_ssm_a_im': 'grad_w', 'grad_ssm_log_dt': 'grad_w', 'grad_ssm_b_re': 'grad_w', 'grad_ssm_b_im': 'grad_w', 'grad_ssm_c_re': 'grad_w', 'grad_ssm_c_im': 'grad_w', 'grad_ssm_d': 'grad_w', 'grad_ssm_w_glu': 'grad_w', 'delta_norm_mix': 'delta_w', 'delta_norm_mlp': 'delta_w', 'delta_mlp_w1': 'delta_w', 'delta_mlp_w2': 'delta_w', 'delta_conv_w_in': 'delta_w', 'delta_conv_w': 'delta_w', 'delta_conv_w_out': 'delta_w', 'delta_pool_w_in': 'delta_w', 'delta_pool_w_group': 'delta_w', 'delta_pool_scale': 'delta_w', 'delta_att_w_qkv': 'delta_w', 'delta_att_q_norm': 'delta_w', 'delta_att_k_norm': 'delta_w', 'delta_att_rel_bias': 'delta_w', 'delta_att_w_out': 'delta_w', 'delta_ssm_a_re': 'delta_w', 'delta_ssm_a_im': 'delta_w', 'delta_ssm_log_dt': 'delta_w', 'delta_ssm_b_re': 'delta_w', 'delta_ssm_b_im': 'delta_w', 'delta_ssm_c_re': 'delta_w', 'delta_ssm_c_im': 'delta_w', 'delta_ssm_d': 'delta_w', 'delta_ssm_w_glu': 'delta_w', 'new_m_norm_mix': 'new_m', 'new_m_norm_mlp': 'new_m', 'new_m_mlp_w1': 'new_m', 'new_m_mlp_w2': 'new_m', 'new_m_conv_w_in': 'new_m', 'new_m_conv_w': 'new_m', 'new_m_conv_w_out': 'new_m', 'new_m_pool_w_in': 'new_m', 'new_m_pool_w_group': 'new_m', 'new_m_pool_scale': 'new_m', 'new_m_att_w_qkv': 'new_m', 'new_m_att_q_norm': 'new_m', 'new_m_att_k_norm': 'new_m', 'new_m_att_rel_bias': 'new_m', 'new_m_att_w_out': 'new_m', 'new_m_ssm_a_re': 'new_m', 'new_m_ssm_a_im': 'new_m', 'new_m_ssm_log_dt': 'new_m', 'new_m_ssm_b_re': 'new_m', 'new_m_ssm_b_im': 'new_m', 'new_m_ssm_c_re': 'new_m', 'new_m_ssm_c_im': 'new_m', 'new_m_ssm_d': 'new_m', 'new_m_ssm_w_glu': 'new_m', 'new_v_norm_mix': 'new_v', 'new_v_norm_mlp': 'new_v', 'new_v_mlp_w1': 'new_v', 'new_v_mlp_w2': 'new_v', 'new_v_conv_w_in': 'new_v', 'new_v_conv_w': 'new_v', 'new_v_conv_w_out': 'new_v', 'new_v_pool_w_in': 'new_v', 'new_v_pool_w_group': 'new_v', 'new_v_pool_scale': 'new_v', 'new_v_att_w_qkv': 'new_v', 'new_v_att_q_norm': 'new_v', 'new_v_att_k_norm': 'new_v', 'new_v_att_rel_bias': 'new_v', 'new_v_att_w_out': 'new_v', 'new_v_ssm_a_re': 'new_v', 'new_v_ssm_a_im': 'new_v', 'new_v_ssm_log_dt': 'new_v', 'new_v_ssm_b_re': 'new_v', 'new_v_ssm_b_im': 'new_v', 'new_v_ssm_c_re': 'new_v', 'new_v_ssm_c_im': 'new_v', 'new_v_ssm_d': 'new_v', 'new_v_ssm_w_glu': 'new_v'}


def _forward(args):
    return _fwd_reference(*[args[k] for k in FWD_PARAMS])


def _output_shape():
    def fwd():
        inp = _fwd_setup_inputs(0)
        return _fwd_reference(*[inp[k] for k in FWD_PARAMS])
    out = _jax.eval_shape(fwd)
    return out.shape, out.dtype

N_MICROBATCH = 1
ADAM_LR = 0.001
ADAM_B1 = 0.9
ADAM_B2 = 0.999
ADAM_EPS = 1e-08
ADAM_WD = 0.01
ADAM_STEP = 10
PER_EXAMPLE_BATCH_AXIS = {'x': 0, 'loss_target': 0}
SHARED_INPUTS = []
_WEIGHT_DTYPES = {'norm_mix': _jnp.float32, 'norm_mlp': _jnp.float32, 'mlp_w1': _jnp.float32, 'mlp_w2': _jnp.float32, 'conv_w_in': _jnp.float32, 'conv_w': _jnp.float32, 'conv_w_out': _jnp.float32, 'pool_w_in': _jnp.float32, 'pool_w_group': _jnp.float32, 'pool_scale': _jnp.float32, 'att_w_qkv': _jnp.float32, 'att_q_norm': _jnp.float32, 'att_k_norm': _jnp.float32, 'att_rel_bias': _jnp.float32, 'att_w_out': _jnp.float32, 'ssm_a_re': _jnp.float32, 'ssm_a_im': _jnp.float32, 'ssm_log_dt': _jnp.float32, 'ssm_b_re': _jnp.float32, 'ssm_b_im': _jnp.float32, 'ssm_c_re': _jnp.float32, 'ssm_c_im': _jnp.float32, 'ssm_d': _jnp.float32, 'ssm_w_glu': _jnp.float32}
MOMENT_SCALE = {'norm_mix': 5.061479e+01, 'norm_mlp': 9.734197e+01, 'mlp_w1': 6.213360e+00, 'mlp_w2': 2.488296e+01, 'conv_w_in': 1.310314e+00, 'conv_w': 1.781580e+01, 'conv_w_out': 1.898453e+00, 'pool_w_in': 1.722864e+00, 'pool_w_group': 1.936511e+00, 'pool_scale': 2.327066e+01, 'att_w_qkv': 8.241769e+00, 'att_q_norm': 1.629877e+00, 'att_k_norm': 1.626537e+00, 'att_rel_bias': 4.556888e-02, 'att_w_out': 1.346235e+01, 'ssm_a_re': 8.400747e-01, 'ssm_a_im': 9.844467e-01, 'ssm_log_dt': 2.568035e+01, 'ssm_b_re': 8.223070e-01, 'ssm_b_im': 6.934180e-01, 'ssm_c_re': 4.149246e-01, 'ssm_c_im': 4.145831e-01, 'ssm_d': 9.348885e+00, 'ssm_w_glu': 5.568989e+00}


def _to_microbatches(a, axis):
    t = _jnp.moveaxis(a, axis, 0)
    t = t.reshape((N_MICROBATCH, t.shape[0] // N_MICROBATCH) + t.shape[1:])
    return _jnp.moveaxis(t, 1, axis + 1)


def setup_inputs(seed: int = 0) -> dict:
    inp = _fwd_setup_inputs(seed)
    key = _jax.random.fold_in(_jax.random.key(seed), 7919)
    shape, _ = _output_shape()
    out = dict(inp)
    out["loss_target"] = _jax.random.normal(_jax.random.fold_in(key, 0), shape, _jnp.float32)
    for i, name in enumerate(TWIN_WEIGHTS):
        w = inp[name].astype(_jnp.float32)
        if MOMENT_SCALE is None:
            s = _jnp.sqrt(_jnp.mean(_jnp.square(w)) + 1e-30)
        else:
            s = MOMENT_SCALE[name]
        km, kv = _jax.random.split(_jax.random.fold_in(key, i + 1))
        out[name] = w
        out["m_" + name] = s * _jax.random.normal(km, w.shape, _jnp.float32)
        out["v_" + name] = (s * s) * _jax.random.uniform(kv, w.shape, _jnp.float32, 0.5, 1.5)
    if N_MICROBATCH > 1:
        for name, axis in PER_EXAMPLE_BATCH_AXIS.items():
            out[name] = _to_microbatches(out[name], axis)
    return {'x': out['x'], 'norm_mix': out['norm_mix'], 'norm_mlp': out['norm_mlp'], 'mlp_w1': out['mlp_w1'], 'mlp_w2': out['mlp_w2'], 'conv_w_in': out['conv_w_in'], 'conv_w': out['conv_w'], 'conv_w_out': out['conv_w_out'], 'pool_w_in': out['pool_w_in'], 'pool_w_group': out['pool_w_group'], 'pool_scale': out['pool_scale'], 'att_w_qkv': out['att_w_qkv'], 'att_q_norm': out['att_q_norm'], 'att_k_norm': out['att_k_norm'], 'att_rel_bias': out['att_rel_bias'], 'att_w_out': out['att_w_out'], 'ssm_a_re': out['ssm_a_re'], 'ssm_a_im': out['ssm_a_im'], 'ssm_log_dt': out['ssm_log_dt'], 'ssm_b_re': out['ssm_b_re'], 'ssm_b_im': out['ssm_b_im'], 'ssm_c_re': out['ssm_c_re'], 'ssm_c_im': out['ssm_c_im'], 'ssm_d': out['ssm_d'], 'ssm_w_glu': out['ssm_w_glu'], 'loss_target': out['loss_target'], 'm_norm_mix': out['m_norm_mix'], 'm_norm_mlp': out['m_norm_mlp'], 'm_mlp_w1': out['m_mlp_w1'], 'm_mlp_w2': out['m_mlp_w2'], 'm_conv_w_in': out['m_conv_w_in'], 'm_conv_w': out['m_conv_w'], 'm_conv_w_out': out['m_conv_w_out'], 'm_pool_w_in': out['m_pool_w_in'], 'm_pool_w_group': out['m_pool_w_group'], 'm_pool_scale': out['m_pool_scale'], 'm_att_w_qkv': out['m_att_w_qkv'], 'm_att_q_norm': out['m_att_q_norm'], 'm_att_k_norm': out['m_att_k_norm'], 'm_att_rel_bias': out['m_att_rel_bias'], 'm_att_w_out': out['m_att_w_out'], 'm_ssm_a_re': out['m_ssm_a_re'], 'm_ssm_a_im': out['m_ssm_a_im'], 'm_ssm_log_dt': out['m_ssm_log_dt'], 'm_ssm_b_re': out['m_ssm_b_re'], 'm_ssm_b_im': out['m_ssm_b_im'], 'm_ssm_c_re': out['m_ssm_c_re'], 'm_ssm_c_im': out['m_ssm_c_im'], 'm_ssm_d': out['m_ssm_d'], 'm_ssm_w_glu': out['m_ssm_w_glu'], 'v_norm_mix': out['v_norm_mix'], 'v_norm_mlp': out['v_norm_mlp'], 'v_mlp_w1': out['v_mlp_w1'], 'v_mlp_w2': out['v_mlp_w2'], 'v_conv_w_in': out['v_conv_w_in'], 'v_conv_w': out['v_conv_w'], 'v_conv_w_out': out['v_conv_w_out'], 'v_pool_w_in': out['v_pool_w_in'], 'v_pool_w_group': out['v_pool_w_group'], 'v_pool_scale': out['v_pool_scale'], 'v_att_w_qkv': out['v_att_w_qkv'], 'v_att_q_norm': out['v_att_q_norm'], 'v_att_k_norm': out['v_att_k_norm'], 'v_att_rel_bias': out['v_att_rel_bias'], 'v_att_w_out': out['v_att_w_out'], 'v_ssm_a_re': out['v_ssm_a_re'], 'v_ssm_a_im': out['v_ssm_a_im'], 'v_ssm_log_dt': out['v_ssm_log_dt'], 'v_ssm_b_re': out['v_ssm_b_re'], 'v_ssm_b_im': out['v_ssm_b_im'], 'v_ssm_c_re': out['v_ssm_c_re'], 'v_ssm_c_im': out['v_ssm_c_im'], 'v_ssm_d': out['v_ssm_d'], 'v_ssm_w_glu': out['v_ssm_w_glu']}


def _loss(weights, diff, rest, loss_target):
    with _jax.named_scope("forward"):
        args = {**rest, TWIN_DIFF_INPUT: diff, **{k: w.astype(_WEIGHT_DTYPES[k]) for k, w in weights.items()}}
        y = _forward(args)
    with _jax.named_scope("loss_head"):
        err = _jnp.square(y.astype(_jnp.float32) - loss_target)
        return 0.5 * _jnp.sum(_jnp.mean(err, axis=-1)) if err.ndim else 0.5 * err


def _adamw(w, g, m, v):
    m = ADAM_B1 * m + (1.0 - ADAM_B1) * g
    v = ADAM_B2 * v + (1.0 - ADAM_B2) * _jnp.square(g)
    m_hat = m / (1.0 - ADAM_B1 ** ADAM_STEP)
    v_hat = v / (1.0 - ADAM_B2 ** ADAM_STEP)
    delta = -ADAM_LR * (m_hat / (_jnp.sqrt(v_hat) + ADAM_EPS) + ADAM_WD * w)
    return delta, m, v


def reference(x, norm_mix, norm_mlp, mlp_w1, mlp_w2, conv_w_in, conv_w, conv_w_out, pool_w_in, pool_w_group, pool_scale, att_w_qkv, att_q_norm, att_k_norm, att_rel_bias, att_w_out, ssm_a_re, ssm_a_im, ssm_log_dt, ssm_b_re, ssm_b_im, ssm_c_re, ssm_c_im, ssm_d, ssm_w_glu, loss_target, m_norm_mix, m_norm_mlp, m_mlp_w1, m_mlp_w2, m_conv_w_in, m_conv_w, m_conv_w_out, m_pool_w_in, m_pool_w_group, m_pool_scale, m_att_w_qkv, m_att_q_norm, m_att_k_norm, m_att_rel_bias, m_att_w_out, m_ssm_a_re, m_ssm_a_im, m_ssm_log_dt, m_ssm_b_re, m_ssm_b_im, m_ssm_c_re, m_ssm_c_im, m_ssm_d, m_ssm_w_glu, v_norm_mix, v_norm_mlp, v_mlp_w1, v_mlp_w2, v_conv_w_in, v_conv_w, v_conv_w_out, v_pool_w_in, v_pool_w_group, v_pool_scale, v_att_w_qkv, v_att_q_norm, v_att_k_norm, v_att_rel_bias, v_att_w_out, v_ssm_a_re, v_ssm_a_im, v_ssm_log_dt, v_ssm_b_re, v_ssm_b_im, v_ssm_c_re, v_ssm_c_im, v_ssm_d, v_ssm_w_glu):
    given = dict(x=x, norm_mix=norm_mix, norm_mlp=norm_mlp, mlp_w1=mlp_w1, mlp_w2=mlp_w2, conv_w_in=conv_w_in, conv_w=conv_w, conv_w_out=conv_w_out, pool_w_in=pool_w_in, pool_w_group=pool_w_group, pool_scale=pool_scale, att_w_qkv=att_w_qkv, att_q_norm=att_q_norm, att_k_norm=att_k_norm, att_rel_bias=att_rel_bias, att_w_out=att_w_out, ssm_a_re=ssm_a_re, ssm_a_im=ssm_a_im, ssm_log_dt=ssm_log_dt, ssm_b_re=ssm_b_re, ssm_b_im=ssm_b_im, ssm_c_re=ssm_c_re, ssm_c_im=ssm_c_im, ssm_d=ssm_d, ssm_w_glu=ssm_w_glu, loss_target=loss_target, m_norm_mix=m_norm_mix, m_norm_mlp=m_norm_mlp, m_mlp_w1=m_mlp_w1, m_mlp_w2=m_mlp_w2, m_conv_w_in=m_conv_w_in, m_conv_w=m_conv_w, m_conv_w_out=m_conv_w_out, m_pool_w_in=m_pool_w_in, m_pool_w_group=m_pool_w_group, m_pool_scale=m_pool_scale, m_att_w_qkv=m_att_w_qkv, m_att_q_norm=m_att_q_norm, m_att_k_norm=m_att_k_norm, m_att_rel_bias=m_att_rel_bias, m_att_w_out=m_att_w_out, m_ssm_a_re=m_ssm_a_re, m_ssm_a_im=m_ssm_a_im, m_ssm_log_dt=m_ssm_log_dt, m_ssm_b_re=m_ssm_b_re, m_ssm_b_im=m_ssm_b_im, m_ssm_c_re=m_ssm_c_re, m_ssm_c_im=m_ssm_c_im, m_ssm_d=m_ssm_d, m_ssm_w_glu=m_ssm_w_glu, v_norm_mix=v_norm_mix, v_norm_mlp=v_norm_mlp, v_mlp_w1=v_mlp_w1, v_mlp_w2=v_mlp_w2, v_conv_w_in=v_conv_w_in, v_conv_w=v_conv_w, v_conv_w_out=v_conv_w_out, v_pool_w_in=v_pool_w_in, v_pool_w_group=v_pool_w_group, v_pool_scale=v_pool_scale, v_att_w_qkv=v_att_w_qkv, v_att_q_norm=v_att_q_norm, v_att_k_norm=v_att_k_norm, v_att_rel_bias=v_att_rel_bias, v_att_w_out=v_att_w_out, v_ssm_a_re=v_ssm_a_re, v_ssm_a_im=v_ssm_a_im, v_ssm_log_dt=v_ssm_log_dt, v_ssm_b_re=v_ssm_b_re, v_ssm_b_im=v_ssm_b_im, v_ssm_c_re=v_ssm_c_re, v_ssm_c_im=v_ssm_c_im, v_ssm_d=v_ssm_d, v_ssm_w_glu=v_ssm_w_glu)
    weights = {n: given[n] for n in TWIN_WEIGHTS}
    shared = {n: given[n] for n in SHARED_INPUTS}
    per_example = {n: given[n] for n in ['x']}
    grad_fn = _jax.value_and_grad(_loss, argnums=(0, 1))

    def one_microbatch(ex, loss_target):
        ex = dict(ex)
        diff = ex.pop(TWIN_DIFF_INPUT)
        return grad_fn(weights, diff, {**shared, **ex}, loss_target)

    if N_MICROBATCH == 1:
        loss, (grad_w, grad_x) = one_microbatch(per_example, given["loss_target"])
    else:
        def body(carry, xs):
            loss_sum, grad_sum = carry
            l_k, (gw_k, gx_k) = one_microbatch(xs[0], xs[1])
            with _jax.named_scope("update"):
                return (loss_sum + l_k, _jax.tree.map(_jnp.add, grad_sum, gw_k)), gx_k

        init = (_jnp.zeros((), _jnp.float32), _jax.tree.map(_jnp.zeros_like, weights))
        (loss, grad_w), grad_x = _jax.lax.scan(body, init, (per_example, given["loss_target"]))
    with _jax.named_scope("update"):
        delta_w, new_m, new_v = {}, {}, {}
        for n in TWIN_WEIGHTS:
            delta_w[n], new_m[n], new_v[n] = _adamw(weights[n], grad_w[n], given["m_" + n], given["v_" + n])
    return (loss, grad_x, *[grad_w[n] for n in TWIN_WEIGHTS], *[delta_w[n] for n in TWIN_WEIGHTS],
            *[new_m[n] for n in TWIN_WEIGHTS], *[new_v[n] for n in TWIN_WEIGHTS])
```

```python
import functools
import math

import numpy as np
import jax
import jax.numpy as jnp
from jax import lax
from jax.experimental import pallas as pl
from jax.experimental.pallas import tpu as pltpu

F32 = jnp.float32
BF16 = jnp.bfloat16
MESH = pl.DeviceIdType.MESH

V7X_VMEM_BYTES = 64 * 1024 * 1024
VMEM_LIMIT = V7X_VMEM_BYTES - 12 * 1024 * 1024
LANES = 128
SUBLANES = 8

CHUNK = 64
ATT_HEAD_DIM = 128
ATT_PAD = 8 * CHUNK
REL_CLIP = 256
MASK_VALUE = -1e30
POOL_WINDOWS = (2, 4, 8, 16)
POOL_HALO = 16
SSM_GROUP = 16
SSM_STATE = 64
SSM_SLAB_GROUPS = LANES // SSM_GROUP
SSM_SLAB_STATE = SSM_SLAB_GROUPS * SSM_STATE
RMS_EPS = 1e-6
ADAM_LR, ADAM_B1, ADAM_B2, ADAM_EPS, ADAM_WD, ADAM_STEP = 0.001, 0.9, 0.999, 1e-08, 0.01, 10
ATT_TQ = 256
N_CHIPS = 4
N_DEV = 8


def _cparams(sem=None, **kw):
    return pltpu.CompilerParams(dimension_semantics=sem, vmem_limit_bytes=VMEM_LIMIT, **kw)


def _tile(n, target, mult):
    if n <= target:
        return n
    t = (target // mult) * mult
    while t > mult and n % t:
        t -= mult
    assert n % t == 0, (n, target, mult)
    return t


def _mm(a, b, *, ta=False, tb=False, extras=(), epilogue=None, out_dtypes=(F32,), name,
        tm=1024, tn=1024, tk=1024):
    M, K = (a.shape[1], a.shape[0]) if ta else a.shape
    N = b.shape[0] if tb else b.shape[1]
    assert (b.shape[1] if tb else b.shape[0]) == K, (a.shape, b.shape, ta, tb)
    tm, tn, tk = _tile(M, tm, LANES), _tile(N, tn, LANES), _tile(K, tk, LANES)
    nk = K // tk
    n_ex, n_out = len(extras), len(out_dtypes)
    dn = (((0 if ta else 1,), (1 if tb else 0,)), ((), ()))

    def body(*refs):
        a_ref, b_ref = refs[0], refs[1]
        ex_refs = refs[2:2 + n_ex]
        o_refs = refs[2 + n_ex:2 + n_ex + n_out]
        p = lax.dot_general(a_ref[...], b_ref[...], dn, preferred_element_type=F32)

        def finish(acc):
            outs = (acc,) if epilogue is None else epilogue(acc, *[r[...] for r in ex_refs])
            for o_ref, o in zip(o_refs, outs):
                o_ref[...] = o.astype(o_ref.dtype)

        if nk == 1:
            finish(p)
        else:
            acc_ref = refs[-1]
            k = pl.program_id(2)

            @pl.when(k == 0)
            def _():
                acc_ref[...] = p

            @pl.when(k > 0)
            def _():
                acc_ref[...] += p

            @pl.when(k == nk - 1)
            def _():
                finish(acc_ref[...])

    a_spec = pl.BlockSpec((tk, tm), lambda i, j, k: (k, i)) if ta else pl.BlockSpec((tm, tk), lambda i, j, k: (i, k))
    b_spec = pl.BlockSpec((tn, tk), lambda i, j, k: (j, k)) if tb else pl.BlockSpec((tk, tn), lambda i, j, k: (k, j))
    mn_spec = pl.BlockSpec((tm, tn), lambda i, j, k: (i, j))
    outs = pl.pallas_call(
        body, name=name, grid=(M // tm, N // tn, nk),
        in_specs=[a_spec, b_spec] + [mn_spec] * n_ex,
        out_specs=[mn_spec] * n_out,
        out_shape=[jax.ShapeDtypeStruct((M, N), d) for d in out_dtypes],
        scratch_shapes=[pltpu.VMEM((tm, tn), F32)] if nk > 1 else [],
        compiler_params=_cparams(("parallel", "parallel", "arbitrary")),
    )(a, b, *extras)
    return outs[0] if n_out == 1 else tuple(outs)


def _row_call(body, ins, outs, *, name, tr, n_rows, acc_outs=(), scratch=(), halo=None):
    nb = n_rows // tr
    hb = halo or SUBLANES
    per = tr // hb
    last = n_rows // hb - 1

    def spec(arr_shape, kind):
        if kind == 'rows':
            return pl.BlockSpec((tr,) + tuple(arr_shape[1:]), lambda i: (i,) + (0,) * (len(arr_shape) - 1))
        if kind == 'full' or kind == 'acc':
            return pl.BlockSpec(tuple(arr_shape), lambda i: (0,) * len(arr_shape))
        tag, w, j = kind
        if tag == 'cols':
            return pl.BlockSpec((tr, w), lambda i: (i, j))
        if tag == 'prev':
            return pl.BlockSpec((hb, w), lambda i: (jnp.maximum(i * per - 1, 0), j))
        if tag == 'next':
            return pl.BlockSpec((hb, w), lambda i: (jnp.minimum((i + 1) * per, last), j))
        raise ValueError(kind)

    return pl.pallas_call(
        body, name=name, grid=(nb,),
        in_specs=[spec(a.shape, k) for a, k in ins],
        out_specs=[spec(s, k) for s, _, k in outs],
        out_shape=[jax.ShapeDtypeStruct(s, d) for s, d, _ in outs],
        scratch_shapes=list(scratch),
        compiler_params=_cparams(("arbitrary",)),
    )(*[a for a, _ in ins])


def _rms_fwd(x, gain, *, name, with_f32=False):
    S, D = x.shape
    tr = _tile(S, 512, SUBLANES)

    def body(x_ref, g_ref, *o_refs):
        xv = x_ref[...]
        r = lax.rsqrt(jnp.mean(xv * xv, axis=-1, keepdims=True) + RMS_EPS)
        h = xv * r * g_ref[...]
        o_refs[0][...] = h.astype(BF16)
        if with_f32:
            o_refs[1][...] = h

    outs = [((S, D), BF16, 'rows')] + ([((S, D), F32, 'rows')] if with_f32 else [])
    res = _row_call(body, [(x, 'rows'), (gain, 'full')], outs, name=name, tr=tr, n_rows=S)
    return tuple(res) if with_f32 else res[0]


def _rms_bwd(x, gain, dh, dres, *, name):
    S, D = x.shape
    tr = _tile(S, 256, SUBLANES)

    def body(x_ref, g_ref, dh_ref, dr_ref, dx_ref, dxb_ref, dg_ref):
        i = pl.program_id(0)
        xv = x_ref[...]
        r = lax.rsqrt(jnp.mean(xv * xv, axis=-1, keepdims=True) + RMS_EPS)
        xn = xv * r
        dhv = dh_ref[...]
        dxn = dhv * g_ref[...]
        dx = r * (dxn - xn * jnp.mean(dxn * xn, axis=-1, keepdims=True)) + dr_ref[...]
        dx_ref[...] = dx
        dxb_ref[...] = dx.astype(BF16)
        part = jnp.sum(dhv * xn, axis=0, keepdims=True)

        @pl.when(i == 0)
        def _():
            dg_ref[...] = part

        @pl.when(i > 0)
        def _():
            dg_ref[...] += part

    return _row_call(body, [(x, 'rows'), (gain, 'full'), (dh, 'rows'), (dres, 'rows')],
                     [((S, D), F32, 'rows'), ((S, D), BF16, 'rows'), ((1, D), F32, 'acc')],
                     name=name, tr=tr, n_rows=S)


def _loss_head(y, target, *, name):
    S, D = y.shape
    tr = _tile(S, 512, SUBLANES)

    def body(y_ref, t_ref, d_ref, db_ref, l_ref):
        i = pl.program_id(0)
        e = y_ref[...] - t_ref[...]
        d = e * (1.0 / D)
        d_ref[...] = d
        db_ref[...] = d.astype(BF16)
        part = jnp.sum(e * e, axis=0, keepdims=True)

        @pl.when(i == 0)
        def _():
            l_ref[...] = part

        @pl.when(i > 0)
        def _():
            l_ref[...] += part

    return _row_call(body, [(y, 'rows'), (target, 'rows')],
                     [((S, D), F32, 'rows'), ((S, D), BF16, 'rows'), ((1, D), F32, 'acc')],
                     name=name, tr=tr, n_rows=S)


def _relu2_epilogue(acc):
    r = jnp.maximum(acc, 0.0)
    return r, r * r


def _mlp_fwd(x, gain, w1, w2, *, tag):
    h = _rms_fwd(x, gain, name=f"mlp_norm_{tag}")
    r, act = _mm(h, w1, epilogue=_relu2_epilogue, out_dtypes=(BF16, BF16), name=f"mlp_up_{tag}")
    y = _mm(act, w2, extras=(x,), epilogue=lambda acc, res: (acc + res,), name=f"mlp_down_{tag}")
    return y, (h, r, act)


def _mlp_bwd(x, gain, w1, w2, saved, dy, dyb, *, tag):
    h, r, act = saved
    da = _mm(dyb, w2, tb=True, extras=(r,), epilogue=lambda acc, rr: (acc * (2.0 * rr.astype(F32)),),
             out_dtypes=(BF16,), name=f"mlp_dact_{tag}")
    dw2 = _mm(act, dyb, ta=True, out_dtypes=(BF16,), name=f"mlp_dw2_{tag}")
    dw1 = _mm(h, da, ta=True, out_dtypes=(BF16,), name=f"mlp_dw1_{tag}")
    dh = _mm(da, w1, tb=True, name=f"mlp_dh_{tag}")
    dx, dxb, dgain = _rms_bwd(x, gain, dh, dy, name=f"mlp_dnorm_{tag}")
    return dx, dxb, dgain, dw1, dw2


def _conv_gate_fwd(z, conv_w):
    S, D3 = z.shape
    D = D3 // 3
    tr = _tile(S, 256, SUBLANES)

    def body(b_ref, c_ref, v_ref, cp_ref, vp_ref, w_ref, g_ref, scr):
        i = pl.program_id(0)
        u = c_ref[...] * v_ref[...]
        scr[0:SUBLANES, :] = cp_ref[...] * vp_ref[...] * (i > 0).astype(F32)
        scr[SUBLANES:, :] = u
        conv = (w_ref[0:1, :] * scr[pl.ds(SUBLANES - 2, tr), :] + w_ref[1:2, :] * scr[pl.ds(SUBLANES - 1, tr), :]
                + w_ref[2:3, :] * u)
        g_ref[...] = (b_ref[...] * conv).astype(BF16)

    ins = [(z, ('cols', D, 0)), (z, ('cols', D, 1)), (z, ('cols', D, 2)), (z, ('prev', D, 1)), (z, ('prev', D, 2)),
           (conv_w, 'full')]
    return _row_call(body, ins, [((S, D), BF16, 'rows')], name="conv_gate_fwd", tr=tr, n_rows=S,
                     scratch=[pltpu.VMEM((tr + SUBLANES, D), F32)])[0]


def _conv_gate_bwd(z, conv_w, dg):
    S, D3 = z.shape
    D = D3 // 3
    tr = _tile(S, 128, SUBLANES)
    nb = S // tr

    def body(b_ref, c_ref, v_ref, cp_ref, vp_ref, bn_ref, dg_ref, dgn_ref, w_ref, dz_ref, dw_ref, scr, scr2):
        i = pl.program_id(0)
        c, v, b, dgv = c_ref[...], v_ref[...], b_ref[...], dg_ref[...]
        u = c * v
        scr[0:SUBLANES, :] = cp_ref[...] * vp_ref[...] * (i > 0).astype(F32)
        scr[SUBLANES:, :] = u
        u1 = scr[pl.ds(SUBLANES - 1, tr), :]
        u2 = scr[pl.ds(SUBLANES - 2, tr), :]
        conv = w_ref[0:1, :] * u2 + w_ref[1:2, :] * u1 + w_ref[2:3, :] * u
        dconv = dgv * b
        scr2[0:tr, :] = dconv
        scr2[tr:, :] = dgn_ref[...] * bn_ref[...] * (i < nb - 1).astype(F32)
        du = (w_ref[2:3, :] * dconv + w_ref[1:2, :] * scr2[pl.ds(1, tr), :] + w_ref[0:1, :] * scr2[pl.ds(2, tr), :])
        dz_ref[:, 0:D] = (dgv * conv).astype(BF16)
        dz_ref[:, D:2 * D] = (du * v).astype(BF16)
        dz_ref[:, 2 * D:] = (du * c).astype(BF16)
        parts = [jnp.sum(dconv * t, axis=0, keepdims=True) for t in (u2, u1, u)]

        @pl.when(i == 0)
        def _():
            for k in range(3):
                dw_ref[k:k + 1, :] = parts[k]

        @pl.when(i > 0)
        def _():
            for k in range(3):
                dw_ref[k:k + 1, :] += parts[k]

    ins = [(z, ('cols', D, 0)), (z, ('cols', D, 1)), (z, ('cols', D, 2)), (z, ('prev', D, 1)), (z, ('prev', D, 2)),
           (z, ('next', D, 0)), (dg, 'rows'), (dg, ('next', D, 0)), (conv_w, 'full')]
    return _row_call(body, ins, [((S, D3), BF16, 'rows'), ((3, D), F32, 'acc')], name="conv_gate_bwd", tr=tr,
                     n_rows=S, scratch=[pltpu.VMEM((tr + SUBLANES, D), F32), pltpu.VMEM((tr + SUBLANES, D), F32)])


def _conv_mixer_fwd(x, gain, w_in, conv_w, w_out):
    h = _rms_fwd(x, gain, name="conv_norm")
    z = _mm(h, w_in, name="conv_in")
    g = _conv_gate_fwd(z, conv_w)
    y = _mm(g, w_out, extras=(x,), epilogue=lambda acc, res: (acc + res,), name="conv_out")
    return y, (h, z, g)


def _conv_mixer_bwd(x, gain, w_in, conv_w, w_out, saved, dy, dyb):
    h, z, g = saved
    dg = _mm(dyb, w_out, tb=True, name="conv_dg")
    dw_out = _mm(g, dyb, ta=True, out_dtypes=(BF16,), name="conv_dwout")
    dz, dconv_w = _conv_gate_bwd(z, conv_w, dg)
    dh = _mm(dz, w_in, tb=True, name="conv_dh")
    dw_in = _mm(h, dz, ta=True, out_dtypes=(BF16,), name="conv_dwin")
    dx, dxb, dgain = _rms_bwd(x, gain, dh, dy, name="conv_dnorm")
    return dx, dxb, dgain, dw_in, dconv_w, dw_out


def _pool_fwd(u):
    S, D = u.shape
    G = D // len(POOL_WINDOWS)
    tr = _tile(S, 256, SUBLANES)
    H = POOL_HALO

    def body(u_ref, up_ref, p_ref, scr):
        i = pl.program_id(0)
        uv = u_ref[...]
        scr[0:H, :] = up_ref[...] * (i > 0).astype(F32)
        scr[H:, :] = uv
        t = (lax.broadcasted_iota(jnp.int32, (tr, 1), 0) + i * tr + 1).astype(F32)
        for gi, w in enumerate(POOL_WINDOWS):
            cols = slice(gi * G, (gi + 1) * G)
            acc = uv[:, cols]
            for j in range(1, w):
                acc = acc + scr[pl.ds(H - j, tr), cols]
            p_ref[:, cols] = (acc / jnp.minimum(t, float(w)) - uv[:, cols]).astype(BF16)

    return _row_call(body, [(u, 'rows'), (u, ('prev', D, 0))], [((S, D), BF16, 'rows')], name="pool_fwd", tr=tr,
                     n_rows=S, halo=H, scratch=[pltpu.VMEM((tr + H, D), F32)])[0]


def _pool_bwd(dp):
    S, D = dp.shape
    G = D // len(POOL_WINDOWS)
    tr = _tile(S, 256, SUBLANES)
    H = POOL_HALO
    nb = S // tr

    def body(d_ref, dn_ref, o_ref, scr):
        i = pl.program_id(0)
        dv = d_ref[...]
        t = (lax.broadcasted_iota(jnp.int32, (tr, 1), 0) + i * tr + 1).astype(F32)
        tn = (lax.broadcasted_iota(jnp.int32, (H, 1), 0) + (i + 1) * tr + 1).astype(F32)
        for gi, w in enumerate(POOL_WINDOWS):
            cols = slice(gi * G, (gi + 1) * G)
            scr[0:tr, cols] = dv[:, cols] / jnp.minimum(t, float(w))
            scr[tr:, cols] = dn_ref[:, cols] / jnp.minimum(tn, float(w)) * (i < nb - 1).astype(F32)
        for gi, w in enumerate(POOL_WINDOWS):
            cols = slice(gi * G, (gi + 1) * G)
            acc = scr[0:tr, cols]
            for j in range(1, w):
                acc = acc + scr[pl.ds(j, tr), cols]
            o_ref[:, cols] = (acc - dv[:, cols]).astype(BF16)

    return _row_call(body, [(dp, 'rows'), (dp, ('next', D, 0))], [((S, D), BF16, 'rows')], name="pool_bwd", tr=tr,
                     n_rows=S, halo=H, scratch=[pltpu.VMEM((tr + H, D), F32)])[0]


def _pool_group_fwd(p, wg, scale, x):
    S, D = p.shape
    NG, G, _ = wg.shape
    tm = _tile(S, 1024, SUBLANES)

    def body(p_ref, w_ref, s_ref, x_ref, o_ref, y_ref):
        y = jnp.dot(p_ref[...], w_ref[0], preferred_element_type=F32)
        y_ref[...] = y
        o_ref[...] = x_ref[...] + y * s_ref[...]

    blk = pl.BlockSpec((tm, G), lambda i, g: (i, g))
    return pl.pallas_call(
        body, name="pool_group_fwd", grid=(S // tm, NG),
        in_specs=[blk, pl.BlockSpec((1, G, G), lambda i, g: (g, 0, 0)), pl.BlockSpec((1, G), lambda i, g: (0, g)), blk],
        out_specs=[blk, blk],
        out_shape=[jax.ShapeDtypeStruct((S, D), F32), jax.ShapeDtypeStruct((S, D), F32)],
        compiler_params=_cparams(("parallel", "arbitrary")),
    )(p, wg, scale, x)


def _pool_group_bwd(p, wg, scale, y, dm):
    S, D = p.shape
    NG, G, _ = wg.shape
    tm = _tile(S, 1024, SUBLANES)
    nb = S // tm

    def body(p_ref, w_ref, s_ref, y_ref, dm_ref, dp_ref, dw_ref, ds_ref, acc_ref):
        i = pl.program_id(1)
        dmv = dm_ref[...]
        dy = (dmv * s_ref[...]).astype(BF16)
        dp_ref[...] = lax.dot_general(dy, w_ref[0], (((1,), (1,)), ((), ())), preferred_element_type=F32)
        dw = lax.dot_general(p_ref[...], dy, (((0,), (0,)), ((), ())), preferred_element_type=F32)
        dsp = jnp.sum(dmv * y_ref[...], axis=0, keepdims=True)

        @pl.when(i == 0)
        def _():
            acc_ref[...] = dw
            ds_ref[...] = dsp

        @pl.when(i > 0)
        def _():
            acc_ref[...] += dw
            ds_ref[...] += dsp

        @pl.when(i == nb - 1)
        def _():
            dw_ref[0] = acc_ref[...].astype(BF16)

    blk = pl.BlockSpec((tm, G), lambda g, i: (i, g))
    wspec = pl.BlockSpec((1, G, G), lambda g, i: (g, 0, 0))
    sspec = pl.BlockSpec((1, G), lambda g, i: (0, g))
    return pl.pallas_call(
        body, name="pool_group_bwd", grid=(NG, nb),
        in_specs=[blk, wspec, sspec, blk, blk],
        out_specs=[blk, wspec, sspec],
        out_shape=[jax.ShapeDtypeStruct((S, D), F32), jax.ShapeDtypeStruct((NG, G, G), BF16),
                   jax.ShapeDtypeStruct((1, D), F32)],
        scratch_shapes=[pltpu.VMEM((G, G), F32)],
        compiler_params=_cparams(("parallel", "arbitrary")),
    )(p, wg, scale, y, dm)


def _pool_mixer_fwd(x, gain, w_in, wg, scale):
    h = _rms_fwd(x, gain, name="pool_norm")
    u = _mm(h, w_in, name="pool_in")
    p = _pool_fwd(u)
    y, yg = _pool_group_fwd(p, wg, scale, x)
    return y, (h, p, yg)


def _pool_mixer_bwd(x, gain, w_in, wg, scale, saved, dy, dyb):
    h, p, yg = saved
    dp, dwg, dscale = _pool_group_bwd(p, wg, scale, yg, dy)
    du = _pool_bwd(dp)
    dh = _mm(du, w_in, tb=True, name="pool_dh")
    dw_in = _mm(h, du, ta=True, out_dtypes=(BF16,), name="pool_dwin")
    dx, dxb, dgain = _rms_bwd(x, gain, dh, dy, name="pool_dnorm")
    return dx, dxb, dgain, dw_in, dwg, dscale


def _qk_norm_fwd(qkv, qg, kg):
    S, D3 = qkv.shape
    D = D3 // 3
    NH = D // ATT_HEAD_DIM
    tr = _tile(S, 256, SUBLANES)

    def body(q_ref, k_ref, v_ref, qg_ref, kg_ref, qo_ref, ko_ref, vo_ref):
        for src, g_ref, dst in ((q_ref, qg_ref, qo_ref), (k_ref, kg_ref, ko_ref)):
            for hd in range(NH):
                cols = slice(hd * ATT_HEAD_DIM, (hd + 1) * ATT_HEAD_DIM)
                t = src[:, cols]
                r = lax.rsqrt(jnp.mean(t * t, axis=-1, keepdims=True) + RMS_EPS)
                dst[:, cols] = (t * r * g_ref[...]).astype(BF16)
        vo_ref[...] = v_ref[...].astype(BF16)

    ins = [(qkv, ('cols', D, 0)), (qkv, ('cols', D, 1)), (qkv, ('cols', D, 2)), (qg, 'full'), (kg, 'full')]
    return _row_call(body, ins, [((S, D), BF16, 'rows')] * 3, name="att_qknorm_fwd", tr=tr, n_rows=S)


def _qk_norm_bwd(qkv, qg, kg, dqn, dkn, dv):
    S, D3 = qkv.shape
    D = D3 // 3
    NH = D // ATT_HEAD_DIM
    tr = _tile(S, 128, SUBLANES)

    def body(q_ref, k_ref, qg_ref, kg_ref, dq_ref, dk_ref, dv_ref, o_ref, dqg_ref, dkg_ref):
        i = pl.program_id(0)
        for sec, (src, g_ref, d_ref, dg_ref) in enumerate(((q_ref, qg_ref, dq_ref, dqg_ref),
                                                            (k_ref, kg_ref, dk_ref, dkg_ref))):
            part = jnp.zeros((1, ATT_HEAD_DIM), F32)
            for hd in range(NH):
                cols = slice(hd * ATT_HEAD_DIM, (hd + 1) * ATT_HEAD_DIM)
                t = src[:, cols]
                r = lax.rsqrt(jnp.mean(t * t, axis=-1, keepdims=True) + RMS_EPS)
                tn = t * r
                d = d_ref[:, cols]
                dn = d * g_ref[...]
                dt = r * (dn - tn * jnp.mean(dn * tn, axis=-1, keepdims=True))
                o_ref[:, sec * D + hd * ATT_HEAD_DIM:sec * D + (hd + 1) * ATT_HEAD_DIM] = dt.astype(BF16)
                part = part + jnp.sum(d * tn, axis=0, keepdims=True)

            @pl.when(i == 0)
            def _():
                dg_ref[...] = part

            @pl.when(i > 0)
            def _():
                dg_ref[...] += part

        o_ref[:, 2 * D:] = dv_ref[...].astype(BF16)

    ins = [(qkv, ('cols', D, 0)), (qkv, ('cols', D, 1)), (qg, 'full'), (kg, 'full'), (dqn, 'rows'), (dkn, 'rows'),
           (dv, 'rows')]
    return _row_call(body, ins, [((S, D3), BF16, 'rows'), ((1, ATT_HEAD_DIM), F32, 'acc'),
                                 ((1, ATT_HEAD_DIM), F32, 'acc')], name="att_qknorm_bwd", tr=tr, n_rows=S)


def _att_rel_index():
    r = np.arange(ATT_TQ)[:, None]
    c = np.arange(ATT_TQ + ATT_PAD)[None, :]
    dist = r - c + ATT_PAD
    lo = (r // CHUNK) * CHUNK
    inside = (c >= lo) & (c < lo + ATT_PAD + CHUNK)
    return np.where(inside, np.clip(dist, -REL_CLIP, REL_CLIP) + REL_CLIP, -1)


def _att_bias_tile(rel_bias):
    idx = _att_rel_index()
    return jnp.where(idx[None] >= 0, rel_bias[:, np.maximum(idx, 0)], MASK_VALUE).astype(F32)


def _att_bias_grad(dtile):
    H, R, C = dtile.shape
    idx = _att_rel_index()
    d = jnp.where(idx[None] >= 0, dtile, 0.0)
    d = d[:, ::-1, :]
    padded = jnp.pad(d, ((0, 0), (0, 0), (0, R)))
    flat = padded.reshape(H, R * (C + R))[:, :R * (C + R - 1)]
    skew = flat.reshape(H, R, C + R - 1)
    diag = jnp.sum(skew, axis=1)
    dist = ATT_PAD + R - 1 - np.arange(C + R - 1)
    col = np.clip(dist, -REL_CLIP, REL_CLIP) + REL_CLIP
    onehot = jnp.asarray(col[:, None] == np.arange(2 * REL_CLIP + 1)[None, :], F32)
    return jnp.dot(diag, onehot, precision=lax.Precision.HIGHEST)


def _att_core_fwd(qn, kp, vp, bias):
    S, D = qn.shape
    NH = D // ATT_HEAD_DIM
    KW = ATT_TQ + ATT_PAD
    scale = ATT_HEAD_DIM ** -0.5

    def body(q_ref, k_ref, v_ref, b_ref, o_ref):
        qb = pl.program_id(1)
        start = pl.multiple_of(qb * ATT_TQ, ATT_TQ)
        ks = k_ref[pl.ds(start, KW), :]
        vs = v_ref[pl.ds(start, KW), :]
        s = lax.dot_general(q_ref[...], ks, (((1,), (1,)), ((), ())), preferred_element_type=F32) * scale + b_ref[0]
        kpos = lax.broadcasted_iota(jnp.int32, (1, KW), 1) + (qb * ATT_TQ - ATT_PAD)
        s = jnp.where(kpos >= 0, s, MASK_VALUE)
        m = jnp.max(s, axis=-1, keepdims=True)
        p = jnp.exp(s - m)
        l = jnp.sum(p, axis=-1, keepdims=True)
        o = jnp.dot(p.astype(BF16), vs, preferred_element_type=F32) / l
        o_ref[...] = o.astype(BF16)

    qspec = pl.BlockSpec((ATT_TQ, ATT_HEAD_DIM), lambda h, qb: (qb, h))
    kvspec = pl.BlockSpec((S + ATT_PAD, ATT_HEAD_DIM), lambda h, qb: (0, h))
    return pl.pallas_call(
        body, name="att_core_fwd", grid=(NH, S // ATT_TQ),
        in_specs=[qspec, kvspec, kvspec, pl.BlockSpec((1, ATT_TQ, KW), lambda h, qb: (h, 0, 0))],
        out_specs=qspec, out_shape=jax.ShapeDtypeStruct((S, D), BF16),
        compiler_params=_cparams(("parallel", "arbitrary")),
    )(qn, kp, vp, bias)


def _att_core_bwd(qn, kp, vp, bias, do):
    S, D = qn.shape
    NH = D // ATT_HEAD_DIM
    KW = ATT_TQ + ATT_PAD
    scale = ATT_HEAD_DIM ** -0.5

    def body(q_ref, k_ref, v_ref, b_ref, do_ref, dq_ref, dk_ref, dv_ref, db_ref):
        qb = pl.program_id(1)
        start = pl.multiple_of(qb * ATT_TQ, ATT_TQ)
        q = q_ref[...]
        dov = do_ref[...]
        ks = k_ref[pl.ds(start, KW), :]
        vs = v_ref[pl.ds(start, KW), :]
        s = lax.dot_general(q, ks, (((1,), (1,)), ((), ())), preferred_element_type=F32) * scale + b_ref[0]
        kpos = lax.broadcasted_iota(jnp.int32, (1, KW), 1) + (qb * ATT_TQ - ATT_PAD)
        s = jnp.where(kpos >= 0, s, MASK_VALUE)
        m = jnp.max(s, axis=-1, keepdims=True)
        e = jnp.exp(s - m)
        p = e / jnp.sum(e, axis=-1, keepdims=True)
        dp = lax.dot_general(dov, vs, (((1,), (1,)), ((), ())), preferred_element_type=F32)
        ds = p * (dp - jnp.sum(p * dp, axis=-1, keepdims=True))
        dsb = ds.astype(BF16)
        dq_ref[...] = jnp.dot(dsb, ks, preferred_element_type=F32) * scale
        dk = lax.dot_general(dsb, q, (((0,), (0,)), ((), ())), preferred_element_type=F32) * scale
        dv = lax.dot_general(p.astype(BF16), dov, (((0,), (0,)), ((), ())), preferred_element_type=F32)

        @pl.when(qb == 0)
        def _():
            dk_ref[...] = jnp.zeros_like(dk_ref)
            dv_ref[...] = jnp.zeros_like(dv_ref)
            db_ref[0] = ds

        @pl.when(qb > 0)
        def _():
            db_ref[0] += ds

        dk_ref[pl.ds(start, KW), :] += dk
        dv_ref[pl.ds(start, KW), :] += dv

    qspec = pl.BlockSpec((ATT_TQ, ATT_HEAD_DIM), lambda h, qb: (qb, h))
    kvspec = pl.BlockSpec((S + ATT_PAD, ATT_HEAD_DIM), lambda h, qb: (0, h))
    bspec = pl.BlockSpec((1, ATT_TQ, KW), lambda h, qb: (h, 0, 0))
    return pl.pallas_call(
        body, name="att_core_bwd", grid=(NH, S // ATT_TQ),
        in_specs=[qspec, kvspec, kvspec, bspec, qspec],
        out_specs=[qspec, kvspec, kvspec, bspec],
        out_shape=[jax.ShapeDtypeStruct((S, D), F32), jax.ShapeDtypeStruct((S + ATT_PAD, D), F32),
                   jax.ShapeDtypeStruct((S + ATT_PAD, D), F32), jax.ShapeDtypeStruct((NH, ATT_TQ, KW), F32)],
        compiler_params=_cparams(("parallel", "arbitrary")),
    )(qn, kp, vp, bias, do)


def _att_mixer_fwd(x, gain, w_qkv, qg, kg, rel_bias, w_out):
    h = _rms_fwd(x, gain, name="att_norm")
    qkv = _mm(h, w_qkv, name="att_qkv")
    qn, kn, v = _qk_norm_fwd(qkv, qg, kg)
    kp = jnp.pad(kn, ((ATT_PAD, 0), (0, 0)))
    vp = jnp.pad(v, ((ATT_PAD, 0), (0, 0)))
    bias = _att_bias_tile(rel_bias)
    o = _att_core_fwd(qn, kp, vp, bias)
    y = _mm(o, w_out, extras=(x,), epilogue=lambda acc, res: (acc + res,), name="att_out")
    return y, (h, qkv, qn, kp, vp, bias, o)


def _att_mixer_bwd(x, gain, w_qkv, qg, kg, w_out, saved, dy, dyb):
    h, qkv, qn, kp, vp, bias, o = saved
    do = _mm(dyb, w_out, tb=True, out_dtypes=(BF16,), name="att_do")
    dw_out = _mm(o, dyb, ta=True, out_dtypes=(BF16,), name="att_dwout")
    dqn, dkp, dvp, dbt = _att_core_bwd(qn, kp, vp, bias, do)
    drel = _att_bias_grad(dbt)
    dqkv, dqg, dkg = _qk_norm_bwd(qkv, qg, kg, dqn, dkp[ATT_PAD:], dvp[ATT_PAD:])
    dh = _mm(dqkv, w_qkv, tb=True, name="att_dh")
    dw_qkv = _mm(h, dqkv, ta=True, out_dtypes=(BF16,), name="att_dwqkv")
    dx, dxb, dgain = _rms_bwd(x, gain, dh, dy, name="att_dnorm")
    return dx, dxb, dgain, dw_qkv, dqg, dkg, drel, dw_out


def _ssm_tables(a_re, a_im, log_dt, b_re, b_im, c_re, c_im):
    G, N = a_re.shape
    NS = G // SSM_SLAB_GROUPS
    lam = lax.complex(a_re, a_im)
    dt = jnp.exp(log_dt)[:, None]
    abar = jnp.exp(lam * dt)
    coef = (abar - 1.0) / lam
    bbar = coef[..., None] * lax.complex(b_re, b_im)
    eye = jnp.eye(SSM_SLAB_GROUPS, dtype=F32)

    def blockdiag(t):
        P, Q = t.shape[1:]
        t = t.reshape(NS, SSM_SLAB_GROUPS, P, Q)
        return jnp.einsum('sgpq,gh->sgphq', t, eye).reshape(NS, SSM_SLAB_GROUPS * P, SSM_SLAB_GROUPS * Q)

    bt = jnp.swapaxes(bbar, 1, 2)
    bmat = jnp.concatenate([blockdiag(jnp.real(bt)), blockdiag(jnp.imag(bt))], axis=2)
    ct = jnp.swapaxes(lax.complex(c_re, c_im), 1, 2)
    cmat = jnp.concatenate([blockdiag(jnp.real(ct)), -blockdiag(jnp.imag(ct))], axis=1)
    al = abar.reshape(NS, 1, SSM_SLAB_STATE)
    rows = jnp.arange(SUBLANES)[None, :, None]
    fwd, bwd = [], []
    for k in (1, 2, 4):
        ak = al ** k
        f = jnp.where(rows >= k, ak, 0.0)
        b = jnp.where(rows < SUBLANES - k, ak, 0.0)
        fwd += [jnp.real(f), jnp.imag(f)]
        bwd += [jnp.real(b), jnp.imag(b)]
    pf = al ** (rows + 1)
    pb = al ** (SUBLANES - rows)
    fwd += [jnp.real(pf), jnp.imag(pf)]
    bwd += [jnp.real(pb), jnp.imag(pb)]
    coef_f = jnp.concatenate(fwd, axis=1).astype(F32)
    coef_b = jnp.concatenate(bwd, axis=1).astype(F32)
    return dict(lam=lam, dt=dt, abar=abar, coef=coef, bmat=bmat.astype(BF16), cmat=cmat.astype(BF16),
                bmat_t=jnp.swapaxes(bmat, 1, 2).astype(BF16), cmat_t=jnp.swapaxes(cmat, 1, 2).astype(BF16),
                coef_f=coef_f, coef_b=coef_b)


def _ssm_scan_fwd(u, tabs, d_skip):
    S, D = u.shape
    NS = D // LANES
    W = 2 * SSM_SLAB_STATE
    T = _tile(S, 512, SUBLANES)
    HS = SSM_SLAB_STATE

    def body(u_ref, bm_ref, cm_ref, cf_ref, d_ref, y_ref, xs_ref, bu_scr, carry_scr):
        i = pl.program_id(1)

        @pl.when(i == 0)
        def _():
            carry_scr[...] = jnp.zeros_like(carry_scr)

        uv = u_ref[...]
        bu_scr[...] = jnp.dot(uv.astype(BF16), bm_ref[0], preferred_element_type=F32)

        def step(r, carry):
            rows = pl.ds(pl.multiple_of(r * SUBLANES, SUBLANES), SUBLANES)
            xr = bu_scr[rows, 0:HS]
            xi = bu_scr[rows, HS:W]
            for n, k in enumerate((1, 2, 4)):
                ar = cf_ref[0, 16 * n:16 * n + 8, :]
                ai = cf_ref[0, 16 * n + 8:16 * n + 16, :]
                sr = pltpu.roll(xr, k, 0)
                si = pltpu.roll(xi, k, 0)
                xr, xi = xr + ar * sr - ai * si, xi + ar * si + ai * sr
            pr = cf_ref[0, 48:56, :]
            pi_ = cf_ref[0, 56:64, :]
            cr, ci = carry
            xr, xi = xr + pr * cr - pi_ * ci, xi + pr * ci + pi_ * cr
            xs_ref[rows, 0:HS] = xr
            xs_ref[rows, HS:W] = xi
            return xr[SUBLANES - 1:SUBLANES, :], xi[SUBLANES - 1:SUBLANES, :]

        cr, ci = lax.fori_loop(0, T // SUBLANES, step, (carry_scr[0:1, 0:HS], carry_scr[0:1, HS:W]))
        carry_scr[0:1, 0:HS] = cr
        carry_scr[0:1, HS:W] = ci
        y_ref[...] = jnp.dot(xs_ref[...].astype(BF16), cm_ref[0], preferred_element_type=F32) + d_ref[...] * uv

    return pl.pallas_call(
        body, name="ssm_scan_fwd", grid=(NS, S // T),
        in_specs=[pl.BlockSpec((T, LANES), lambda j, i: (i, j)),
                  pl.BlockSpec((1, LANES, W), lambda j, i: (j, 0, 0)),
                  pl.BlockSpec((1, W, LANES), lambda j, i: (j, 0, 0)),
                  pl.BlockSpec((1, 8 * SUBLANES, HS), lambda j, i: (j, 0, 0)),
                  pl.BlockSpec((1, LANES), lambda j, i: (0, j))],
        out_specs=[pl.BlockSpec((T, LANES), lambda j, i: (i, j)), pl.BlockSpec((T, W), lambda j, i: (i, j))],
        out_shape=[jax.ShapeDtypeStruct((S, D), F32), jax.ShapeDtypeStruct((S, NS * W), F32)],
        scratch_shapes=[pltpu.VMEM((T, W), F32), pltpu.VMEM((SUBLANES, W), F32)],
        compiler_params=_cparams(("parallel", "arbitrary")),
    )(u, tabs['bmat'], tabs['cmat'], tabs['coef_f'], d_skip)


def _ssm_scan_bwd(u, xs, dy, tabs, d_skip):
    S, D = u.shape
    NS = D // LANES
    W = 2 * SSM_SLAB_STATE
    T = _tile(S, 512, SUBLANES)
    HS = SSM_SLAB_STATE
    nb = S // T

    def body(u_ref, xs_ref, dy_ref, bt_ref, ct_ref, cf_ref, d_ref, du_ref, gb_ref, gc_ref, q_ref,
             cy_scr, lam_scr, carry_scr):
        i = pl.program_id(1)

        @pl.when(i == 0)
        def _():
            carry_scr[...] = jnp.zeros_like(carry_scr)
            gb_ref[...] = jnp.zeros_like(gb_ref)
            gc_ref[...] = jnp.zeros_like(gc_ref)
            q_ref[...] = jnp.zeros_like(q_ref)

        dyv = dy_ref[...]
        dyb = dyv.astype(BF16)
        cy_scr[...] = jnp.dot(dyb, ct_ref[0], preferred_element_type=F32)

        def step(n, carry):
            r = T // SUBLANES - 1 - n
            rows = pl.ds(pl.multiple_of(r * SUBLANES, SUBLANES), SUBLANES)
            cyr = cy_scr[rows, 0:HS]
            cyi = cy_scr[rows, HS:W]
            lr, li = cyr, cyi
            for m, k in enumerate((1, 2, 4)):
                br = cf_ref[0, 16 * m:16 * m + 8, :]
                bi = cf_ref[0, 16 * m + 8:16 * m + 16, :]
                sr = pltpu.roll(lr, SUBLANES - k, 0)
                si = pltpu.roll(li, SUBLANES - k, 0)
                lr, li = lr + br * sr + bi * si, li + br * si - bi * sr
            pr = cf_ref[0, 48:56, :]
            pi_ = cf_ref[0, 56:64, :]
            cr, ci, qr, qi = carry
            lr, li = lr + pr * cr + pi_ * ci, li + pr * ci - pi_ * cr
            lam_scr[rows, 0:HS] = lr
            lam_scr[rows, HS:W] = li
            mr, mi = lr - cyr, li - cyi
            xr = xs_ref[rows, 0:HS]
            xi = xs_ref[rows, HS:W]
            return lr[0:1, :], li[0:1, :], qr + mr * xr + mi * xi, qi + mi * xr - mr * xi

        zero = jnp.zeros((SUBLANES, HS), F32)
        cr, ci, qr, qi = lax.fori_loop(0, T // SUBLANES, step,
                                       (carry_scr[0:1, 0:HS], carry_scr[0:1, HS:W], zero, zero))
        carry_scr[0:1, 0:HS] = cr
        carry_scr[0:1, HS:W] = ci
        q_ref[0, :, 0:HS] += qr
        q_ref[0, :, HS:W] += qi
        lamb = lam_scr[...].astype(BF16)
        uv = u_ref[...]
        du_ref[...] = jnp.dot(lamb, bt_ref[0], preferred_element_type=F32) + d_ref[...] * dyv
        gb_ref[0] += lax.dot_general(lamb, uv.astype(BF16), (((0,), (0,)), ((), ())), preferred_element_type=F32)
        gc_ref[0] += lax.dot_general(xs_ref[...].astype(BF16), dyb, (((0,), (0,)), ((), ())),
                                     preferred_element_type=F32)

    rev = lambda j, i: (nb - 1 - i, j)
    slab3 = lambda j, i: (j, 0, 0)
    return pl.pallas_call(
        body, name="ssm_scan_bwd", grid=(NS, nb),
        in_specs=[pl.BlockSpec((T, LANES), rev), pl.BlockSpec((T, W), rev), pl.BlockSpec((T, LANES), rev),
                  pl.BlockSpec((1, W, LANES), slab3), pl.BlockSpec((1, LANES, W), slab3),
                  pl.BlockSpec((1, 8 * SUBLANES, HS), slab3), pl.BlockSpec((1, LANES), lambda j, i: (0, j))],
        out_specs=[pl.BlockSpec((T, LANES), rev), pl.BlockSpec((1, W, LANES), slab3),
                   pl.BlockSpec((1, W, LANES), slab3), pl.BlockSpec((1, SUBLANES, W), slab3)],
        out_shape=[jax.ShapeDtypeStruct((S, D), F32), jax.ShapeDtypeStruct((NS, W, LANES), F32),
                   jax.ShapeDtypeStruct((NS, W, LANES), F32), jax.ShapeDtypeStruct((NS, SUBLANES, W), F32)],
        scratch_shapes=[pltpu.VMEM((T, W), F32), pltpu.VMEM((T, W), F32), pltpu.VMEM((SUBLANES, W), F32)],
        compiler_params=_cparams(("parallel", "arbitrary")),
    )(u, xs, dy, tabs['bmat_t'], tabs['cmat_t'], tabs['coef_b'], d_skip)


def _ssm_param_grads(tabs, b_re, b_im, gb, gc, q):
    NS = gb.shape[0]
    G = NS * SSM_SLAB_GROUPS
    N, C = SSM_STATE, SSM_GROUP

    def diag_blocks(t):
        t = t.reshape(NS, SSM_SLAB_GROUPS, N, SSM_SLAB_GROUPS, C)
        t = jnp.einsum('sgnhc,gh->sgnc', t, jnp.eye(SSM_SLAB_GROUPS, dtype=F32))
        return t.reshape(G, N, C)

    HS = SSM_SLAB_STATE
    g_bbar = lax.complex(diag_blocks(gb[:, :HS]), diag_blocks(gb[:, HS:]))
    g_c = lax.complex(diag_blocks(gc[:, :HS]), -diag_blocks(gc[:, HS:]))
    qs = jnp.sum(q, axis=1)
    qc = lax.complex(qs[:, :HS], qs[:, HS:]).reshape(G, N)
    lam, dt, abar, coef = tabs['lam'], tabs['dt'], tabs['abar'], tabs['coef']
    bmat = lax.complex(b_re, b_im)
    g_b = g_bbar * jnp.conj(coef)[..., None]
    g_coef = jnp.sum(g_bbar * jnp.conj(bmat), axis=-1)
    g_abar_coef = g_coef * jnp.conj(1.0 / lam)
    g_lam = g_coef * jnp.conj(-(abar - 1.0) / (lam * lam))
    g_ld = qc + jnp.conj(abar) * g_abar_coef
    g_lam = g_lam + g_ld * dt
    g_dt = jnp.sum(jnp.real(g_ld * jnp.conj(lam)), axis=-1)
    g_logdt = g_dt * dt[:, 0]
    g_ct = jnp.swapaxes(g_c, 1, 2)
    return (jnp.real(g_lam), jnp.imag(g_lam), g_logdt, jnp.real(g_b), jnp.imag(g_b), jnp.real(g_ct), jnp.imag(g_ct))


_GELU_C = math.sqrt(2.0 / math.pi)


def _gelu_fwd(y):
    S, D = y.shape

    def body(y_ref, z_ref):
        v = y_ref[...]
        z_ref[...] = (0.5 * v * (1.0 + jnp.tanh(_GELU_C * (v + 0.044715 * v * v * v)))).astype(BF16)

    return _row_call(body, [(y, 'rows')], [((S, D), BF16, 'rows')], name="ssm_gelu_fwd",
                     tr=_tile(S, 512, SUBLANES), n_rows=S)[0]


def _gelu_bwd(y, dz, u):
    S, D = y.shape

    def body(y_ref, dz_ref, u_ref, dy_ref, dd_ref):
        i = pl.program_id(0)
        v = y_ref[...]
        t = jnp.tanh(_GELU_C * (v + 0.044715 * v * v * v))
        g = 0.5 * (1.0 + t) + 0.5 * v * (1.0 - t * t) * _GELU_C * (1.0 + 3 * 0.044715 * v * v)
        dy = dz_ref[...] * g
        dy_ref[...] = dy
        part = jnp.sum(dy * u_ref[...], axis=0, keepdims=True)

        @pl.when(i == 0)
        def _():
            dd_ref[...] = part

        @pl.when(i > 0)
        def _():
            dd_ref[...] += part

    return _row_call(body, [(y, 'rows'), (dz, 'rows'), (u, 'rows')], [((S, D), F32, 'rows'), ((1, D), F32, 'acc')],
                     name="ssm_gelu_bwd", tr=_tile(S, 256, SUBLANES), n_rows=S)


def _glu_fwd(zz, x):
    S, D = x.shape

    def body(a_ref, g_ref, x_ref, o_ref):
        o_ref[...] = x_ref[...] + a_ref[...] * jax.nn.sigmoid(g_ref[...])

    return _row_call(body, [(zz, ('cols', D, 0)), (zz, ('cols', D, 1)), (x, 'rows')], [((S, D), F32, 'rows')],
                     name="ssm_glu_fwd", tr=_tile(S, 256, SUBLANES), n_rows=S)[0]


def _glu_bwd(zz, dm):
    S, D = dm.shape

    def body(a_ref, g_ref, dm_ref, o_ref):
        s = jax.nn.sigmoid(g_ref[...])
        d = dm_ref[...]
        o_ref[:, 0:D] = (d * s).astype(BF16)
        o_ref[:, D:] = (d * a_ref[...] * s * (1.0 - s)).astype(BF16)

    return _row_call(body, [(zz, ('cols', D, 0)), (zz, ('cols', D, 1)), (dm, 'rows')], [((S, 2 * D), BF16, 'rows')],
                     name="ssm_glu_bwd", tr=_tile(S, 256, SUBLANES), n_rows=S)[0]


def _ssm_mixer_fwd(x, gain, tabs, d_skip, w_glu):
    _, u = _rms_fwd(x, gain, name="ssm_norm", with_f32=True)
    yv, xs = _ssm_scan_fwd(u, tabs, d_skip)
    z = _gelu_fwd(yv)
    zz = _mm(z, w_glu, name="ssm_glu_in")
    y = _glu_fwd(zz, x)
    return y, (u, xs, yv, z, zz)


def _ssm_mixer_bwd(x, gain, tabs, d_skip, w_glu, b_re, b_im, saved, dy, dyb):
    u, xs, yv, z, zz = saved
    dzz = _glu_bwd(zz, dy)
    dz = _mm(dzz, w_glu, tb=True, name="ssm_dz")
    dw_glu = _mm(z, dzz, ta=True, out_dtypes=(BF16,), name="ssm_dwglu")
    dyv, dd = _gelu_bwd(yv, dz, u)
    du, gb, gc, q = _ssm_scan_bwd(u, xs, dyv, tabs, d_skip)
    small = _ssm_param_grads(tabs, b_re, b_im, gb, gc, q)
    dx, dxb, dgain = _rms_bwd(x, gain, du, dy, name="ssm_dnorm")
    return dx, dxb, dgain, small, dd, dw_glu


def _local_step(x, target, p):
    depth = p['norm_mix'].shape[0]
    tabs = _ssm_tables(p['ssm_a_re'], p['ssm_a_im'], p['ssm_log_dt'], p['ssm_b_re'], p['ssm_b_im'], p['ssm_c_re'],
                       p['ssm_c_im'])
    xs_in, saved_mix, saved_mlp = [], [], []
    for i in range(depth):
        gm = p['norm_mix'][i:i + 1]
        xs_in.append(x)
        if i % 4 == 0:
            x, sv = _conv_mixer_fwd(x, gm, p['conv_w_in'], p['conv_w'], p['conv_w_out'])
        elif i % 4 == 1:
            x, sv = _pool_mixer_fwd(x, gm, p['pool_w_in'], p['pool_w_group'], p['pool_scale'])
        elif i % 4 == 2:
            x, sv = _att_mixer_fwd(x, gm, p['att_w_qkv'], p['att_q_norm'], p['att_k_norm'], p['att_rel_bias'],
                                   p['att_w_out'])
        else:
            x, sv = _ssm_mixer_fwd(x, gm, tabs, p['ssm_d'], p['ssm_w_glu'])
        saved_mix.append(sv)
        xs_in.append(x)
        x, sv = _mlp_fwd(x, p['norm_mlp'][i:i + 1], p['mlp_w1'][i], p['mlp_w2'][i], tag=str(i))
        saved_mlp.append(sv)
    dx, dxb, loss_cols = _loss_head(x, target, name="loss_head")
    g = {'norm_mix': [None] * depth, 'norm_mlp': [None] * depth, 'mlp_w1': [None] * depth, 'mlp_w2': [None] * depth}
    for i in reversed(range(depth)):
        dx, dxb, g['norm_mlp'][i], g['mlp_w1'][i], g['mlp_w2'][i] = _mlp_bwd(
            xs_in[2 * i + 1], p['norm_mlp'][i:i + 1], p['mlp_w1'][i], p['mlp_w2'][i], saved_mlp[i], dx, dxb, tag=str(i))
        gm = p['norm_mix'][i:i + 1]
        xin, sv = xs_in[2 * i], saved_mix[i]
        if i % 4 == 0:
            dx, dxb, g['norm_mix'][i], g['conv_w_in'], g['conv_w'], g['conv_w_out'] = _conv_mixer_bwd(
                xin, gm, p['conv_w_in'], p['conv_w'], p['conv_w_out'], sv, dx, dxb)
        elif i % 4 == 1:
            dx, dxb, g['norm_mix'][i], g['pool_w_in'], g['pool_w_group'], g['pool_scale'] = _pool_mixer_bwd(
                xin, gm, p['pool_w_in'], p['pool_w_group'], p['pool_scale'], sv, dx, dxb)
        elif i % 4 == 2:
            (dx, dxb, g['norm_mix'][i], g['att_w_qkv'], g['att_q_norm'], g['att_k_norm'], g['att_rel_bias'],
             g['att_w_out']) = _att_mixer_bwd(xin, gm, p['att_w_qkv'], p['att_q_norm'], p['att_k_norm'],
                                              p['att_w_out'], sv, dx, dxb)
        else:
            dx, dxb, g['norm_mix'][i], small, g['ssm_d'], g['ssm_w_glu'] = _ssm_mixer_bwd(
                xin, gm, tabs, p['ssm_d'], p['ssm_w_glu'], p['ssm_b_re'], p['ssm_b_im'], sv, dx, dxb)
            (g['ssm_a_re'], g['ssm_a_im'], g['ssm_log_dt'], g['ssm_b_re'], g['ssm_b_im'], g['ssm_c_re'],
             g['ssm_c_im']) = small
    g['norm_mix'] = jnp.concatenate(g['norm_mix'], axis=0)
    g['norm_mlp'] = jnp.concatenate(g['norm_mlp'], axis=0)
    return loss_cols, dx, g


_ANY = pl.BlockSpec(memory_space=pl.ANY)
_VM = pl.BlockSpec(memory_space=pltpu.VMEM)
_REL_ALL = [(0, 0, 1), (0, 1, 0), (0, 1, 1), (1, 0, 0), (1, 0, 1), (1, 1, 0), (1, 1, 1)]
_REL_CHIPS = [(1, 0, 0), (0, 1, 0), (1, 1, 0)]


def _me():
    return lax.axis_index("x"), lax.axis_index("y"), lax.axis_index("c")


def _flip(pos, rel):
    return tuple(1 - p if r else p for p, r in zip(pos, rel))


def _chip_of(pos):
    return 2 * pos[0] + pos[1]


def _dev_of(pos):
    return 4 * pos[0] + 2 * pos[1] + pos[2]


def _gather_small(buf, *, reduce, name):
    rows = buf.shape[0]

    def body(in_ref, out_ref, *rest):
        if reduce:
            gath, send_sems, recv_sems = rest
        else:
            gath = out_ref
            send_sems, recv_sems = rest
        me = _me()
        gath[_dev_of(me)] = in_ref[...]
        copies = []
        for k, rel in enumerate(_REL_ALL):
            peer = _flip(me, rel)
            cp = pltpu.make_async_remote_copy(src_ref=in_ref, dst_ref=gath.at[_dev_of(me)], send_sem=send_sems.at[k],
                                              recv_sem=recv_sems.at[k], device_id=peer, device_id_type=MESH)
            cp.start()
            copies.append(cp)
        for k, rel in enumerate(_REL_ALL):
            peer = _flip(me, rel)
            pltpu.make_async_remote_copy(src_ref=in_ref, dst_ref=gath.at[_dev_of(peer)], send_sem=send_sems.at[k],
                                         recv_sem=recv_sems.at[k], device_id=peer, device_id_type=MESH).wait_recv()
        for cp in copies:
            cp.wait_send()
        if reduce:
            acc = gath[0]
            for s in range(1, N_DEV):
                acc = acc + gath[s]
            out_ref[...] = acc

    out_shape = (rows, LANES) if reduce else (N_DEV, rows, LANES)
    scratch = ([pltpu.VMEM((N_DEV, rows, LANES), F32)] if reduce else []) + [
        pltpu.SemaphoreType.DMA((len(_REL_ALL),)), pltpu.SemaphoreType.DMA((len(_REL_ALL),))]
    return pl.pallas_call(
        body, name=name, in_specs=[_VM], out_specs=_VM, out_shape=jax.ShapeDtypeStruct(out_shape, F32),
        scratch_shapes=scratch, compiler_params=pltpu.CompilerParams(vmem_limit_bytes=VMEM_LIMIT),
    )(buf)


def _region(ref, kind, k, shard_shape):
    r, c = shard_shape
    if kind == 'col':
        return ref.at[:, pl.ds(pl.multiple_of(k * c, LANES), c)]
    return ref.at[pl.ds(pl.multiple_of(k * r, 16), r), :]


def _all_gather_matrices(shards, kinds, *, name):
    n = len(shards)
    fulls = [(s.shape[0], s.shape[1] * N_CHIPS) if kd == 'col' else (s.shape[0] * N_CHIPS, s.shape[1])
             for s, kd in zip(shards, kinds)]

    def body(*refs):
        in_refs, out_refs = refs[:n], refs[n:2 * n]
        send_sems, recv_sems, local_sems = refs[2 * n:]
        me = _me()
        sends, locs = [], []
        for m in range(n):
            shp = shards[m].shape
            mine = _region(out_refs[m], kinds[m], _chip_of(me), shp)
            lc = pltpu.make_async_copy(in_refs[m], mine, local_sems.at[m])
            lc.start()
            locs.append(lc)
            for j, rel in enumerate(_REL_CHIPS):
                cp = pltpu.make_async_remote_copy(src_ref=in_refs[m], dst_ref=mine, send_sem=send_sems.at[m, j],
                                                  recv_sem=recv_sems.at[m, j], device_id=_flip(me, rel),
                                                  device_id_type=MESH)
                cp.start()
                sends.append(cp)
        for m in range(n):
            shp = shards[m].shape
            for j, rel in enumerate(_REL_CHIPS):
                peer = _flip(me, rel)
                theirs = _region(out_refs[m], kinds[m], _chip_of(peer), shp)
                pltpu.make_async_remote_copy(src_ref=in_refs[m], dst_ref=theirs, send_sem=send_sems.at[m, j],
                                             recv_sem=recv_sems.at[m, j], device_id=peer,
                                             device_id_type=MESH).wait_recv()
        for cp in sends:
            cp.wait_send()
        for lc in locs:
            lc.wait()

    return pl.pallas_call(
        body, name=name, in_specs=[_ANY] * n, out_specs=[_ANY] * n,
        out_shape=[jax.ShapeDtypeStruct(f, BF16) for f in fulls],
        scratch_shapes=[pltpu.SemaphoreType.DMA((n, len(_REL_CHIPS))), pltpu.SemaphoreType.DMA((n, len(_REL_CHIPS))),
                        pltpu.SemaphoreType.DMA((n,))],
    )(*shards)


def _piece_shape(full, kind):
    R, C = full
    return (R // 2, C // N_CHIPS) if kind == 'col' else (R // N_DEV, C)


def _piece(ref, kind, k, c, full):
    pr, pc = _piece_shape(full, kind)
    if kind == 'col':
        return ref.at[pl.ds(pl.multiple_of(c * pr, 16), pr), pl.ds(pl.multiple_of(k * pc, LANES), pc)]
    return ref.at[pl.ds(pl.multiple_of((2 * k + c) * pr, 16), pr), :]


def _scatter_pieces(grads, kinds, *, name):
    n = len(grads)
    pieces = [_piece_shape(g.shape, kd) for g, kd in zip(grads, kinds)]

    def body(*refs):
        in_refs, out_refs = refs[:n], refs[n:2 * n]
        send_sems, recv_sems, local_sems = refs[2 * n:]
        me = _me()
        sends, locs = [], []
        for m in range(n):
            full = grads[m].shape
            lc = pltpu.make_async_copy(_piece(in_refs[m], kinds[m], _chip_of(me), me[2], full),
                                       out_refs[m].at[_dev_of(me)], local_sems.at[m])
            lc.start()
            locs.append(lc)
            for j, rel in enumerate(_REL_ALL):
                peer = _flip(me, rel)
                cp = pltpu.make_async_remote_copy(
                    src_ref=_piece(in_refs[m], kinds[m], _chip_of(peer), peer[2], full),
                    dst_ref=out_refs[m].at[_dev_of(me)], send_sem=send_sems.at[m, j], recv_sem=recv_sems.at[m, j],
                    device_id=peer, device_id_type=MESH)
                cp.start()
                sends.append(cp)
        for m in range(n):
            full = grads[m].shape
            for j, rel in enumerate(_REL_ALL):
                peer = _flip(me, rel)
                pltpu.make_async_remote_copy(
                    src_ref=_piece(in_refs[m], kinds[m], _chip_of(me), me[2], full),
                    dst_ref=out_refs[m].at[_dev_of(peer)], send_sem=send_sems.at[m, j], recv_sem=recv_sems.at[m, j],
                    device_id=peer, device_id_type=MESH).wait_recv()
        for cp in sends:
            cp.wait_send()
        for lc in locs:
            lc.wait()

    return pl.pallas_call(
        body, name=name, in_specs=[_ANY] * n, out_specs=[_ANY] * n,
        out_shape=[jax.ShapeDtypeStruct((N_DEV,) + p, BF16) for p in pieces],
        scratch_shapes=[pltpu.SemaphoreType.DMA((n, len(_REL_ALL))), pltpu.SemaphoreType.DMA((n, len(_REL_ALL))),
                        pltpu.SemaphoreType.DMA((n,))],
    )(*grads)


def _sum_contributions(parts, *, name):
    _, pr, pc = parts.shape
    tr = _tile(pr, max(16, (1 << 19) // pc // 16 * 16), 16)

    def body(p_ref, o_ref):
        acc = p_ref[0].astype(F32)
        for s in range(1, N_DEV):
            acc = acc + p_ref[s].astype(F32)
        o_ref[...] = acc

    return pl.pallas_call(
        body, name=name, grid=(pr // tr,),
        in_specs=[pl.BlockSpec((N_DEV, tr, pc), lambda i: (0, i, 0))],
        out_specs=pl.BlockSpec((tr, pc), lambda i: (i, 0)),
        out_shape=jax.ShapeDtypeStruct((pr, pc), F32),
        compiler_params=_cparams(("parallel",)),
    )(parts)


def _exchange_halves(halves, layers, *, name):
    n = len(halves)
    n_out = len(layers)
    where = {}
    for o, ls in enumerate(layers):
        for l, m in enumerate(ls):
            where[m] = (o, l)

    def body(*refs):
        in_refs, out_refs = refs[:n], refs[n:n + n_out]
        send_sems, recv_sems, local_sems = refs[n + n_out:]
        me = _me()
        sib = _flip(me, (0, 0, 1))
        sends, locs = [], []
        for m in range(n):
            o, l = where[m]
            hr = halves[m].shape[0]
            mine = out_refs[o].at[l, pl.ds(pl.multiple_of(me[2] * hr, SUBLANES), hr), :]
            lc = pltpu.make_async_copy(in_refs[m], mine, local_sems.at[m])
            lc.start()
            locs.append(lc)
            cp = pltpu.make_async_remote_copy(src_ref=in_refs[m], dst_ref=mine, send_sem=send_sems.at[m],
                                              recv_sem=recv_sems.at[m], device_id=sib, device_id_type=MESH)
            cp.start()
            sends.append(cp)
        for m in range(n):
            o, l = where[m]
            hr = halves[m].shape[0]
            theirs = out_refs[o].at[l, pl.ds(pl.multiple_of(sib[2] * hr, SUBLANES), hr), :]
            pltpu.make_async_remote_copy(src_ref=in_refs[m], dst_ref=theirs, send_sem=send_sems.at[m],
                                         recv_sem=recv_sems.at[m], device_id=sib, device_id_type=MESH).wait_recv()
        for cp in sends:
            cp.wait_send()
        for lc in locs:
            lc.wait()

    out_shapes = [(len(ls), 2 * halves[ls[0]].shape[0], halves[ls[0]].shape[1]) for ls in layers]
    return pl.pallas_call(
        body, name=name, in_specs=[_ANY] * n, out_specs=[_ANY] * n_out,
        out_shape=[jax.ShapeDtypeStruct(s, F32) for s in out_shapes],
        scratch_shapes=[pltpu.SemaphoreType.DMA((n,)), pltpu.SemaphoreType.DMA((n,)), pltpu.SemaphoreType.DMA((n,))],
    )(*halves)


def _adamw(w, g, m, v, *, name):
    R, C = w.shape
    tr = _tile(R, max(SUBLANES, (1 << 19) // C // SUBLANES * SUBLANES), SUBLANES)
    c1 = 1.0 / (1.0 - ADAM_B1 ** ADAM_STEP)
    c2 = 1.0 / (1.0 - ADAM_B2 ** ADAM_STEP)

    def body(w_ref, g_ref, m_ref, v_ref, d_ref, nm_ref, nv_ref):
        gv = g_ref[...]
        nm = ADAM_B1 * m_ref[...] + (1.0 - ADAM_B1) * gv
        nv = ADAM_B2 * v_ref[...] + (1.0 - ADAM_B2) * (gv * gv)
        nm_ref[...] = nm
        nv_ref[...] = nv
        d_ref[...] = -ADAM_LR * ((nm * c1) / (jnp.sqrt(nv * c2) + ADAM_EPS) + ADAM_WD * w_ref[...])

    blk = pl.BlockSpec((tr, C), lambda i: (i, 0))
    return pl.pallas_call(
        body, name=name, grid=(R // tr,), in_specs=[blk] * 4, out_specs=[blk] * 3,
        out_shape=[jax.ShapeDtypeStruct((R, C), F32)] * 3, compiler_params=_cparams(("parallel",)),
    )(w, g, m, v)


_BIG = ['mlp_w1', 'mlp_w2', 'conv_w_in', 'conv_w_out', 'pool_w_in', 'pool_w_group', 'att_w_qkv', 'att_w_out',
        'ssm_w_glu']
_KIND = {'mlp_w1': 'col', 'mlp_w2': 'row', 'conv_w_in': 'col', 'conv_w_out': 'row', 'pool_w_in': 'row',
         'pool_w_group': 'row', 'att_w_qkv': 'col', 'att_w_out': 'row', 'ssm_w_glu': 'col'}
_TINY_SHARDED = ['conv_w', 'pool_scale', 'ssm_d']
_REPLICATED = ['norm_mix', 'norm_mlp', 'att_q_norm', 'att_k_norm', 'att_rel_bias', 'ssm_a_re', 'ssm_a_im',
               'ssm_log_dt', 'ssm_b_re', 'ssm_b_im', 'ssm_c_re', 'ssm_c_im']
_SMALL = _REPLICATED + _TINY_SHARDED
_ORDER = ['norm_mix', 'norm_mlp', 'mlp_w1', 'mlp_w2', 'conv_w_in', 'conv_w', 'conv_w_out', 'pool_w_in',
          'pool_w_group', 'pool_scale', 'att_w_qkv', 'att_q_norm', 'att_k_norm', 'att_rel_bias', 'att_w_out',
          'ssm_a_re', 'ssm_a_im', 'ssm_log_dt', 'ssm_b_re', 'ssm_b_im', 'ssm_c_re', 'ssm_c_im', 'ssm_d', 'ssm_w_glu']
_LAYER_OF = {'conv_w_in': 0, 'conv_w_out': 0, 'pool_w_in': 1, 'pool_w_group': 1, 'att_w_qkv': 2, 'att_w_out': 2,
             'ssm_w_glu': 3}


def _pack(arrays):
    flat = jnp.concatenate([a.reshape(-1).astype(F32) for a in arrays])
    n = flat.shape[0]
    total = -(-n // (SUBLANES * LANES)) * (SUBLANES * LANES)
    return jnp.pad(flat, (0, total - n)).reshape(total // LANES, LANES)


def _unpack(buf, shapes):
    flat = buf.reshape(-1)
    out, off = [], 0
    for s in shapes:
        n = int(np.prod(s))
        out.append(flat[off:off + n].reshape(s))
        off += n
    return out


def _matrices(w, name):
    t = w.reshape((-1,) + w.shape[-2:])
    return [t[l] for l in range(t.shape[0])]


def kernel(x, norm_mix, norm_mlp, mlp_w1, mlp_w2, conv_w_in, conv_w, conv_w_out, pool_w_in, pool_w_group, pool_scale, att_w_qkv, att_q_norm, att_k_norm, att_rel_bias, att_w_out, ssm_a_re, ssm_a_im, ssm_log_dt, ssm_b_re, ssm_b_im, ssm_c_re, ssm_c_im, ssm_d, ssm_w_glu, loss_target, m_norm_mix, m_norm_mlp, m_mlp_w1, m_mlp_w2, m_conv_w_in, m_conv_w, m_conv_w_out, m_pool_w_in, m_pool_w_group, m_pool_scale, m_att_w_qkv, m_att_q_norm, m_att_k_norm, m_att_rel_bias, m_att_w_out, m_ssm_a_re, m_ssm_a_im, m_ssm_log_dt, m_ssm_b_re, m_ssm_b_im, m_ssm_c_re, m_ssm_c_im, m_ssm_d, m_ssm_w_glu, v_norm_mix, v_norm_mlp, v_mlp_w1, v_mlp_w2, v_conv_w_in, v_conv_w, v_conv_w_out, v_pool_w_in, v_pool_w_group, v_pool_scale, v_att_w_qkv, v_att_q_norm, v_att_k_norm, v_att_rel_bias, v_att_w_out, v_ssm_a_re, v_ssm_a_im, v_ssm_log_dt, v_ssm_b_re, v_ssm_b_im, v_ssm_c_re, v_ssm_c_im, v_ssm_d, v_ssm_w_glu):
    args = dict(locals())
    W = {n: args[n] for n in _ORDER}
    M = {n: args['m_' + n] for n in _ORDER}
    V = {n: args['v_' + n] for n in _ORDER}
    depth = norm_mix.shape[0]
    d_model = x.shape[-1]
    chip = 2 * lax.axis_index("x") + lax.axis_index("y")

    tiny = _gather_small(_pack([W[n] for n in _TINY_SHARDED]), reduce=False, name="gather_vectors")
    tiny_shapes = [W[n].shape for n in _TINY_SHARDED]
    per_chip = [_unpack(tiny[2 * k], tiny_shapes) for k in range(N_CHIPS)]
    full_tiny = {n: jnp.concatenate([per_chip[k][i] for k in range(N_CHIPS)], axis=-1)
                 for i, n in enumerate(_TINY_SHARDED)}
    groups = [[] for _ in range(depth)]
    for n in _BIG:
        mats = _matrices(W[n], n)
        for l in range(len(mats)):
            groups[l if n.startswith('mlp') else _LAYER_OF[n]].append((n, l))
    shard_bf = {n: _matrices(W[n].astype(BF16), n) for n in _BIG}
    full = {n: [None] * len(shard_bf[n]) for n in _BIG}
    for i in range(depth):
        got = _all_gather_matrices([shard_bf[n][l] for n, l in groups[i]], [_KIND[n] for n, _ in groups[i]],
                                   name=f"gather_weights_{i}")
        for (n, l), t in zip(groups[i], got):
            full[n][l] = t

    p = dict(
        norm_mix=norm_mix, norm_mlp=norm_mlp, mlp_w1=full['mlp_w1'], mlp_w2=full['mlp_w2'],
        conv_w_in=full['conv_w_in'][0], conv_w=full_tiny['conv_w'][0], conv_w_out=full['conv_w_out'][0],
        pool_w_in=full['pool_w_in'][0], pool_w_group=jnp.stack(full['pool_w_group']),
        pool_scale=full_tiny['pool_scale'], att_w_qkv=full['att_w_qkv'][0], att_q_norm=att_q_norm,
        att_k_norm=att_k_norm, att_rel_bias=att_rel_bias[0], att_w_out=full['att_w_out'][0],
        ssm_a_re=ssm_a_re[0], ssm_a_im=ssm_a_im[0], ssm_log_dt=ssm_log_dt[0], ssm_b_re=ssm_b_re[0],
        ssm_b_im=ssm_b_im[0], ssm_c_re=ssm_c_re[0], ssm_c_im=ssm_c_im[0], ssm_d=full_tiny['ssm_d'],
        ssm_w_glu=full['ssm_w_glu'][0])

    loss_cols, dx, g = _local_step(x[0], loss_target[0], p)
    loss = lax.psum(0.5 * jnp.sum(loss_cols) / d_model, ("x", "y", "c"))

    gmats = {n: (g[n] if isinstance(g[n], list) else _matrices(g[n], n)) for n in _BIG}
    halves, slots = [], {}
    for i in reversed(range(depth)):
        parts = _scatter_pieces([gmats[n][l] for n, l in groups[i]], [_KIND[n] for n, _ in groups[i]],
                                name=f"scatter_grads_{i}")
        for (n, l), t in zip(groups[i], parts):
            slots[(n, l)] = len(halves)
            halves.append(_sum_contributions(t, name=f"sum_grads_{n}_{l}" if not n.startswith('mlp') else f"sum_grads_{n}"))
    layers = [[slots[(n, l)] for l in range(len(gmats[n]))] for n in _BIG]
    reduced = _exchange_halves(halves, layers, name="exchange_halves")
    grads = {n: t.reshape(W[n].shape) for n, t in zip(_BIG, reduced)}

    small_full_shapes = [W[n].shape for n in _REPLICATED] + [full_tiny[n].shape for n in _TINY_SHARDED]
    gsmall = _gather_small(_pack([g[n] for n in _SMALL]), reduce=True, name="reduce_small_grads")
    for n, t in zip(_SMALL, _unpack(gsmall, small_full_shapes)):
        if n in _TINY_SHARDED:
            width = W[n].shape[-1]
            t = lax.dynamic_slice_in_dim(t, chip * width, width, axis=t.ndim - 1)
        grads[n] = t.reshape(W[n].shape)

    delta, new_m, new_v = {}, {}, {}
    for n in _BIG:
        shp = W[n].shape
        two = (int(np.prod(shp[:-1])), shp[-1])
        d, nm, nv = _adamw(W[n].reshape(two), grads[n].reshape(two), M[n].reshape(two), V[n].reshape(two),
                           name=f"adamw_{n}")
        delta[n], new_m[n], new_v[n] = d.reshape(shp), nm.reshape(shp), nv.reshape(shp)
    shapes = [W[n].shape for n in _SMALL]
    d, nm, nv = _adamw(_pack([W[n] for n in _SMALL]), _pack([grads[n] for n in _SMALL]),
                       _pack([M[n] for n in _SMALL]), _pack([V[n] for n in _SMALL]), name="adamw_small")
    for n, a, b, c in zip(_SMALL, _unpack(d, shapes), _unpack(nm, shapes), _unpack(nv, shapes)):
        delta[n], new_m[n], new_v[n] = a, b, c

    return (loss, dx.reshape(x.shape), *[grads[n] for n in _ORDER], *[delta[n] for n in _ORDER],
            *[new_m[n] for n in _ORDER], *[new_v[n] for n in _ORDER])
```

```python
import functools
import math

import numpy as np
import jax
import jax.numpy as jnp
from jax import lax
from jax.experimental import pallas as pl
from jax.experimental.pallas import tpu as pltpu

F32 = jnp.float32
BF16 = jnp.bfloat16
MESH = pl.DeviceIdType.MESH

V7X_VMEM_BYTES = 64 * 1024 * 1024
VMEM_LIMIT = V7X_VMEM_BYTES - 12 * 1024 * 1024
LANES = 128
SUBLANES = 8

CHUNK = 64
ATT_HEAD_DIM = 128
ATT_PAD = 8 * CHUNK
REL_CLIP = 256
MASK_VALUE = -1e30
POOL_WINDOWS = (2, 4, 8, 16)
POOL_HALO = 16
SSM_GROUP = 16
SSM_STATE = 64
SSM_SLAB_GROUPS = LANES // SSM_GROUP
SSM_SLAB_STATE = SSM_SLAB_GROUPS * SSM_STATE
RMS_EPS = 1e-6
ADAM_LR, ADAM_B1, ADAM_B2, ADAM_EPS, ADAM_WD, ADAM_STEP = 0.001, 0.9, 0.999, 1e-08, 0.01, 10
ATT_TQ = 256
N_CHIPS = 4
N_DEV = 8


def _cparams(sem=None, **kw):
    return pltpu.CompilerParams(dimension_semantics=sem, vmem_limit_bytes=VMEM_LIMIT, **kw)


def _tile(n, target, mult):
    if n <= target:
        return n
    t = (target // mult) * mult
    while t > mult and n % t:
        t -= mult
    assert n % t == 0, (n, target, mult)
    return t


def _mm(a, b, *, ta=False, tb=False, extras=(), epilogue=None, out_dtypes=(F32,), name,
        tm=1024, tn=1024, tk=1024):
    M, K = (a.shape[1], a.shape[0]) if ta else a.shape
    N = b.shape[0] if tb else b.shape[1]
    assert (b.shape[1] if tb else b.shape[0]) == K, (a.shape, b.shape, ta, tb)
    tm, tn, tk = _tile(M, tm, LANES), _tile(N, tn, LANES), _tile(K, tk, LANES)
    nk = K // tk
    n_ex, n_out = len(extras), len(out_dtypes)
    dn = (((0 if ta else 1,), (1 if tb else 0,)), ((), ()))

    def body(*refs):
        a_ref, b_ref = refs[0], refs[1]
        ex_refs = refs[2:2 + n_ex]
        o_refs = refs[2 + n_ex:2 + n_ex + n_out]
        p = lax.dot_general(a_ref[...], b_ref[...], dn, preferred_element_type=F32)

        def finish(acc):
            outs = (acc,) if epilogue is None else epilogue(acc, *[r[...] for r in ex_refs])
            for o_ref, o in zip(o_refs, outs):
                o_ref[...] = o.astype(o_ref.dtype)

        if nk == 1:
            finish(p)
        else:
            acc_ref = refs[-1]
            k = pl.program_id(2)

            @pl.when(k == 0)
            def _():
                acc_ref[...] = p

            @pl.when(k > 0)
            def _():
                acc_ref[...] += p

            @pl.when(k == nk - 1)
            def _():
                finish(acc_ref[...])

    a_spec = pl.BlockSpec((tk, tm), lambda i, j, k: (k, i)) if ta else pl.BlockSpec((tm, tk), lambda i, j, k: (i, k))
    b_spec = pl.BlockSpec((tn, tk), lambda i, j, k: (j, k)) if tb else pl.BlockSpec((tk, tn), lambda i, j, k: (k, j))
    mn_spec = pl.BlockSpec((tm, tn), lambda i, j, k: (i, j))
    outs = pl.pallas_call(
        body, name=name, grid=(M // tm, N // tn, nk),
        in_specs=[a_spec, b_spec] + [mn_spec] * n_ex,
        out_specs=[mn_spec] * n_out,
        out_shape=[jax.ShapeDtypeStruct((M, N), d) for d in out_dtypes],
        scratch_shapes=[pltpu.VMEM((tm, tn), F32)] if nk > 1 else [],
        compiler_params=_cparams(("parallel", "parallel", "arbitrary")),
    )(a, b, *extras)
    return outs[0] if n_out == 1 else tuple(outs)


def _row_call(body, ins, outs, *, name, tr, n_rows, acc_outs=(), scratch=(), halo=None):
    nb = n_rows // tr
    hb = halo or SUBLANES
    per = tr // hb
    last = n_rows // hb - 1

    def spec(arr_shape, kind):
        if kind == 'rows':
            return pl.BlockSpec((tr,) + tuple(arr_shape[1:]), lambda i: (i,) + (0,) * (len(arr_shape) - 1))
        if kind == 'full' or kind == 'acc':
            return pl.BlockSpec(tuple(arr_shape), lambda i: (0,) * len(arr_shape))
        tag, w, j = kind
        if tag == 'cols':
            return pl.BlockSpec((tr, w), lambda i: (i, j))
        if tag == 'prev':
            return pl.BlockSpec((hb, w), lambda i: (jnp.maximum(i * per - 1, 0), j))
        if tag == 'next':
            return pl.BlockSpec((hb, w), lambda i: (jnp.minimum((i + 1) * per, last), j))
        raise ValueError(kind)

    return pl.pallas_call(
        body, name=name, grid=(nb,),
        in_specs=[spec(a.shape, k) for a, k in ins],
        out_specs=[spec(s, k) for s, _, k in outs],
        out_shape=[jax.ShapeDtypeStruct(s, d) for s, d, _ in outs],
        scratch_shapes=list(scratch),
        compiler_params=_cparams(("arbitrary",)),
    )(*[a for a, _ in ins])


def _rms_fwd(x, gain, *, name, with_f32=False):
    S, D = x.shape
    tr = _tile(S, 512, SUBLANES)

    def body(x_ref, g_ref, *o_refs):
        xv = x_ref[...]
        r = lax.rsqrt(jnp.mean(xv * xv, axis=-1, keepdims=True) + RMS_EPS)
        h = xv * r * g_ref[...]
        o_refs[0][...] = h.astype(BF16)
        if with_f32:
            o_refs[1][...] = h

    outs = [((S, D), BF16, 'rows')] + ([((S, D), F32, 'rows')] if with_f32 else [])
    res = _row_call(body, [(x, 'rows'), (gain, 'full')], outs, name=name, tr=tr, n_rows=S)
    return tuple(res) if with_f32 else res[0]


def _rms_bwd(x, gain, dh, dres, *, name):
    S, D = x.shape
    tr = _tile(S, 256, SUBLANES)

    def body(x_ref, g_ref, dh_ref, dr_ref, dx_ref, dxb_ref, dg_ref):
        i = pl.program_id(0)
        xv = x_ref[...]
        r = lax.rsqrt(jnp.mean(xv * xv, axis=-1, keepdims=True) + RMS_EPS)
        xn = xv * r
        dhv = dh_ref[...]
        dxn = dhv * g_ref[...]
        dx = r * (dxn - xn * jnp.mean(dxn * xn, axis=-1, keepdims=True)) + dr_ref[...]
        dx_ref[...] = dx
        dxb_ref[...] = dx.astype(BF16)
        part = jnp.sum(dhv * xn, axis=0, keepdims=True)

        @pl.when(i == 0)
        def _():
            dg_ref[...] = part

        @pl.when(i > 0)
        def _():
            dg_ref[...] += part

    return _row_call(body, [(x, 'rows'), (gain, 'full'), (dh, 'rows'), (dres, 'rows')],
                     [((S, D), F32, 'rows'), ((S, D), BF16, 'rows'), ((1, D), F32, 'acc')],
                     name=name, tr=tr, n_rows=S)


def _loss_head(y, target, *, name):
    S, D = y.shape
    tr = _tile(S, 512, SUBLANES)

    def body(y_ref, t_ref, d_ref, db_ref, l_ref):
        i = pl.program_id(0)
        e = y_ref[...] - t_ref[...]
        d = e * (1.0 / D)
        d_ref[...] = d
        db_ref[...] = d.astype(BF16)
        part = jnp.sum(e * e, axis=0, keepdims=True)

        @pl.when(i == 0)
        def _():
            l_ref[...] = part

        @pl.when(i > 0)
        def _():
            l_ref[...] += part

    return _row_call(body, [(y, 'rows'), (target, 'rows')],
                     [((S, D), F32, 'rows'), ((S, D), BF16, 'rows'), ((1, D), F32, 'acc')],
                     name=name, tr=tr, n_rows=S)


def _relu2_epilogue(acc):
    r = jnp.maximum(acc, 0.0)
    return r, r * r


def _mlp_fwd(x, gain, w1, w2, *, tag):
    h = _rms_fwd(x, gain, name=f"mlp_norm_{tag}")
    r, act = _mm(h, w1, epilogue=_relu2_epilogue, out_dtypes=(BF16, BF16), name=f"mlp_up_{tag}")
    y = _mm(act, w2, extras=(x,), epilogue=lambda acc, res: (acc + res,), name=f"mlp_down_{tag}")
    return y, (h, r, act)


def _mlp_bwd(x, gain, w1, w2, saved, dy, dyb, *, tag):
    h, r, act = saved
    da = _mm(dyb, w2, tb=True, extras=(r,), epilogue=lambda acc, rr: (acc * (2.0 * rr.astype(F32)),),
             out_dtypes=(BF16,), name=f"mlp_dact_{tag}")
    dw2 = _mm(act, dyb, ta=True, out_dtypes=(BF16,), name=f"mlp_dw2_{tag}")
    dw1 = _mm(h, da, ta=True, out_dtypes=(BF16,), name=f"mlp_dw1_{tag}")
    dh = _mm(da, w1, tb=True, name=f"mlp_dh_{tag}")
    dx, dxb, dgain = _rms_bwd(x, gain, dh, dy, name=f"mlp_dnorm_{tag}")
    return dx, dxb, dgain, dw1, dw2


def _conv_gate_fwd(z, conv_w):
    S, D3 = z.shape
    D = D3 // 3
    tr = _tile(S, 256, SUBLANES)

    def body(b_ref, c_ref, v_ref, cp_ref, vp_ref, w_ref, g_ref, scr):
        i = pl.program_id(0)
        u = c_ref[...] * v_ref[...]
        scr[0:SUBLANES, :] = cp_ref[...] * vp_ref[...] * (i > 0).astype(F32)
        scr[SUBLANES:, :] = u
        conv = (w_ref[0:1, :] * scr[pl.ds(SUBLANES - 2, tr), :] + w_ref[1:2, :] * scr[pl.ds(SUBLANES - 1, tr), :]
                + w_ref[2:3, :] * u)
        g_ref[...] = (b_ref[...] * conv).astype(BF16)

    ins = [(z, ('cols', D, 0)), (z, ('cols', D, 1)), (z, ('cols', D, 2)), (z, ('prev', D, 1)), (z, ('prev', D, 2)),
           (conv_w, 'full')]
    return _row_call(body, ins, [((S, D), BF16, 'rows')], name="conv_gate_fwd", tr=tr, n_rows=S,
                     scratch=[pltpu.VMEM((tr + SUBLANES, D), F32)])[0]


def _conv_gate_bwd(z, conv_w, dg):
    S, D3 = z.shape
    D = D3 // 3
    tr = _tile(S, 128, SUBLANES)
    nb = S // tr

    def body(b_ref, c_ref, v_ref, cp_ref, vp_ref, bn_ref, dg_ref, dgn_ref, w_ref, dz_ref, dw_ref, scr, scr2):
        i = pl.program_id(0)
        c, v, b, dgv = c_ref[...], v_ref[...], b_ref[...], dg_ref[...]
        u = c * v
        scr[0:SUBLANES, :] = cp_ref[...] * vp_ref[...] * (i > 0).astype(F32)
        scr[SUBLANES:, :] = u
        u1 = scr[pl.ds(SUBLANES - 1, tr), :]
        u2 = scr[pl.ds(SUBLANES - 2, tr), :]
        conv = w_ref[0:1, :] * u2 + w_ref[1:2, :] * u1 + w_ref[2:3, :] * u
        dconv = dgv * b
        scr2[0:tr, :] = dconv
        scr2[tr:, :] = dgn_ref[...] * bn_ref[...] * (i < nb - 1).astype(F32)
        du = (w_ref[2:3, :] * dconv + w_ref[1:2, :] * scr2[pl.ds(1, tr), :] + w_ref[0:1, :] * scr2[pl.ds(2, tr), :])
        dz_ref[:, 0:D] = (dgv * conv).astype(BF16)
        dz_ref[:, D:2 * D] = (du * v).astype(BF16)
        dz_ref[:, 2 * D:] = (du * c).astype(BF16)
        parts = [jnp.sum(dconv * t, axis=0, keepdims=True) for t in (u2, u1, u)]

        @pl.when(i == 0)
        def _():
            for k in range(3):
                dw_ref[k:k + 1, :] = parts[k]

        @pl.when(i > 0)
        def _():
            for k in range(3):
                dw_ref[k:k + 1, :] += parts[k]

    ins = [(z, ('cols', D, 0)), (z, ('cols', D, 1)), (z, ('cols', D, 2)), (z, ('prev', D, 1)), (z, ('prev', D, 2)),
           (z, ('next', D, 0)), (dg, 'rows'), (dg, ('next', D, 0)), (conv_w, 'full')]
    return _row_call(body, ins, [((S, D3), BF16, 'rows'), ((3, D), F32, 'acc')], name="conv_gate_bwd", tr=tr,
                     n_rows=S, scratch=[pltpu.VMEM((tr + SUBLANES, D), F32), pltpu.VMEM((tr + SUBLANES, D), F32)])


def _conv_mixer_fwd(x, gain, w_in, conv_w, w_out):
    h = _rms_fwd(x, gain, name="conv_norm")
    z = _mm(h, w_in, name="conv_in")
    g = _conv_gate_fwd(z, conv_w)
    y = _mm(g, w_out, extras=(x,), epilogue=lambda acc, res: (acc + res,), name="conv_out")
    return y, (h, z, g)


def _conv_mixer_bwd(x, gain, w_in, conv_w, w_out, saved, dy, dyb):
    h, z, g = saved
    dg = _mm(dyb, w_out, tb=True, name="conv_dg")
    dw_out = _mm(g, dyb, ta=True, out_dtypes=(BF16,), name="conv_dwout")
    dz, dconv_w = _conv_gate_bwd(z, conv_w, dg)
    dh = _mm(dz, w_in, tb=True, name="conv_dh")
    dw_in = _mm(h, dz, ta=True, out_dtypes=(BF16,), name="conv_dwin")
    dx, dxb, dgain = _rms_bwd(x, gain, dh, dy, name="conv_dnorm")
    return dx, dxb, dgain, dw_in, dconv_w, dw_out


def _pool_fwd(u):
    S, D = u.shape
    G = D // len(POOL_WINDOWS)
    tr = _tile(S, 256, SUBLANES)
    H = POOL_HALO

    def body(u_ref, up_ref, p_ref, scr):
        i = pl.program_id(0)
        uv = u_ref[...]
        scr[0:H, :] = up_ref[...] * (i > 0).astype(F32)
        scr[H:, :] = uv
        t = (lax.broadcasted_iota(jnp.int32, (tr, 1), 0) + i * tr + 1).astype(F32)
        for gi, w in enumerate(POOL_WINDOWS):
            cols = slice(gi * G, (gi + 1) * G)
            acc = uv[:, cols]
            for j in range(1, w):
                acc = acc + scr[pl.ds(H - j, tr), cols]
            p_ref[:, cols] = (acc / jnp.minimum(t, float(w)) - uv[:, cols]).astype(BF16)

    return _row_call(body, [(u, 'rows'), (u, ('prev', D, 0))], [((S, D), BF16, 'rows')], name="pool_fwd", tr=tr,
                     n_rows=S, halo=H, scratch=[pltpu.VMEM((tr + H, D), F32)])[0]


def _pool_bwd(dp):
    S, D = dp.shape
    G = D // len(POOL_WINDOWS)
    tr = _tile(S, 256, SUBLANES)
    H = POOL_HALO
    nb = S // tr

    def body(d_ref, dn_ref, o_ref, scr):
        i = pl.program_id(0)
        dv = d_ref[...]
        t = (lax.broadcasted_iota(jnp.int32, (tr, 1), 0) + i * tr + 1).astype(F32)
        tn = (lax.broadcasted_iota(jnp.int32, (H, 1), 0) + (i + 1) * tr + 1).astype(F32)
        for gi, w in enumerate(POOL_WINDOWS):
            cols = slice(gi * G, (gi + 1) * G)
            scr[0:tr, cols] = dv[:, cols] / jnp.minimum(t, float(w))
            scr[tr:, cols] = dn_ref[:, cols] / jnp.minimum(tn, float(w)) * (i < nb - 1).astype(F32)
        for gi, w in enumerate(POOL_WINDOWS):
            cols = slice(gi * G, (gi + 1) * G)
            acc = scr[0:tr, cols]
            for j in range(1, w):
                acc = acc + scr[pl.ds(j, tr), cols]
            o_ref[:, cols] = (acc - dv[:, cols]).astype(BF16)

    return _row_call(body, [(dp, 'rows'), (dp, ('next', D, 0))], [((S, D), BF16, 'rows')], name="pool_bwd", tr=tr,
                     n_rows=S, halo=H, scratch=[pltpu.VMEM((tr + H, D), F32)])[0]


def _pool_group_fwd(p, wg, scale, x):
    S, D = p.shape
    NG, G, _ = wg.shape
    tm = _tile(S, 1024, SUBLANES)

    def body(p_ref, w_ref, s_ref, x_ref, o_ref, y_ref):
        y = jnp.dot(p_ref[...], w_ref[0], preferred_element_type=F32)
        y_ref[...] = y
        o_ref[...] = x_ref[...] + y * s_ref[...]

    blk = pl.BlockSpec((tm, G), lambda i, g: (i, g))
    return pl.pallas_call(
        body, name="pool_group_fwd", grid=(S // tm, NG),
        in_specs=[blk, pl.BlockSpec((1, G, G), lambda i, g: (g, 0, 0)), pl.BlockSpec((1, G), lambda i, g: (0, g)), blk],
        out_specs=[blk, blk],
        out_shape=[jax.ShapeDtypeStruct((S, D), F32), jax.ShapeDtypeStruct((S, D), F32)],
        compiler_params=_cparams(("parallel", "arbitrary")),
    )(p, wg, scale, x)


def _pool_group_bwd(p, wg, scale, y, dm):
    S, D = p.shape
    NG, G, _ = wg.shape
    tm = _tile(S, 1024, SUBLANES)
    nb = S // tm

    def body(p_ref, w_ref, s_ref, y_ref, dm_ref, dp_ref, dw_ref, ds_ref, acc_ref):
        i = pl.program_id(1)
        dmv = dm_ref[...]
        dy = (dmv * s_ref[...]).astype(BF16)
        dp_ref[...] = lax.dot_general(dy, w_ref[0], (((1,), (1,)), ((), ())), preferred_element_type=F32)
        dw = lax.dot_general(p_ref[...], dy, (((0,), (0,)), ((), ())), preferred_element_type=F32)
        dsp = jnp.sum(dmv * y_ref[...], axis=0, keepdims=True)

        @pl.when(i == 0)
        def _():
            acc_ref[...] = dw
            ds_ref[...] = dsp

        @pl.when(i > 0)
        def _():
            acc_ref[...] += dw
            ds_ref[...] += dsp

        @pl.when(i == nb - 1)
        def _():
            dw_ref[0] = acc_ref[...].astype(BF16)

    blk = pl.BlockSpec((tm, G), lambda g, i: (i, g))
    wspec = pl.BlockSpec((1, G, G), lambda g, i: (g, 0, 0))
    sspec = pl.BlockSpec((1, G), lambda g, i: (0, g))
    return pl.pallas_call(
        body, name="pool_group_bwd", grid=(NG, nb),
        in_specs=[blk, wspec, sspec, blk, blk],
        out_specs=[blk, wspec, sspec],
        out_shape=[jax.ShapeDtypeStruct((S, D), F32), jax.ShapeDtypeStruct((NG, G, G), BF16),
                   jax.ShapeDtypeStruct((1, D), F32)],
        scratch_shapes=[pltpu.VMEM((G, G), F32)],
        compiler_params=_cparams(("parallel", "arbitrary")),
    )(p, wg, scale, y, dm)


def _pool_mixer_fwd(x, gain, w_in, wg, scale):
    h = _rms_fwd(x, gain, name="pool_norm")
    u = _mm(h, w_in, name="pool_in")
    p = _pool_fwd(u)
    y, yg = _pool_group_fwd(p, wg, scale, x)
    return y, (h, p, yg)


def _pool_mixer_bwd(x, gain, w_in, wg, scale, saved, dy, dyb):
    h, p, yg = saved
    dp, dwg, dscale = _pool_group_bwd(p, wg, scale, yg, dy)
    du = _pool_bwd(dp)
    dh = _mm(du, w_in, tb=True, name="pool_dh")
    dw_in = _mm(h, du, ta=True, out_dtypes=(BF16,), name="pool_dwin")
    dx, dxb, dgain = _rms_bwd(x, gain, dh, dy, name="pool_dnorm")
    return dx, dxb, dgain, dw_in, dwg, dscale


def _qk_norm_fwd(qkv, qg, kg):
    S, D3 = qkv.shape
    D = D3 // 3
    NH = D // ATT_HEAD_DIM
    tr = _tile(S, 256, SUBLANES)

    def body(q_ref, k_ref, v_ref, qg_ref, kg_ref, qo_ref, ko_ref, vo_ref):
        for src, g_ref, dst in ((q_ref, qg_ref, qo_ref), (k_ref, kg_ref, ko_ref)):
            for hd in range(NH):
                cols = slice(hd * ATT_HEAD_DIM, (hd + 1) * ATT_HEAD_DIM)
                t = src[:, cols]
                r = lax.rsqrt(jnp.mean(t * t, axis=-1, keepdims=True) + RMS_EPS)
                dst[:, cols] = (t * r * g_ref[...]).astype(BF16)
        vo_ref[...] = v_ref[...].astype(BF16)

    ins = [(qkv, ('cols', D, 0)), (qkv, ('cols', D, 1)), (qkv, ('cols', D, 2)), (qg, 'full'), (kg, 'full')]
    return _row_call(body, ins, [((S, D), BF16, 'rows')] * 3, name="att_qknorm_fwd", tr=tr, n_rows=S)


def _qk_norm_bwd(qkv, qg, kg, dqn, dkn, dv):
    S, D3 = qkv.shape
    D = D3 // 3
    NH = D // ATT_HEAD_DIM
    tr = _tile(S, 128, SUBLANES)

    def body(q_ref, k_ref, qg_ref, kg_ref, dq_ref, dk_ref, dv_ref, o_ref, dqg_ref, dkg_ref):
        i = pl.program_id(0)
        for sec, (src, g_ref, d_ref, dg_ref) in enumerate(((q_ref, qg_ref, dq_ref, dqg_ref),
                                                            (k_ref, kg_ref, dk_ref, dkg_ref))):
            part = jnp.zeros((1, ATT_HEAD_DIM), F32)
            for hd in range(NH):
                cols = slice(hd * ATT_HEAD_DIM, (hd + 1) * ATT_HEAD_DIM)
                t = src[:, cols]
                r = lax.rsqrt(jnp.mean(t * t, axis=-1, keepdims=True) + RMS_EPS)
                tn = t * r
                d = d_ref[:, cols]
                dn = d * g_ref[...]
                dt = r * (dn - tn * jnp.mean(dn * tn, axis=-1, keepdims=True))
                o_ref[:, sec * D + hd * ATT_HEAD_DIM:sec * D + (hd + 1) * ATT_HEAD_DIM] = dt.astype(BF16)
                part = part + jnp.sum(d * tn, axis=0, keepdims=True)

            @pl.when(i == 0)
            def _():
                dg_ref[...] = part

            @pl.when(i > 0)
            def _():
                dg_ref[...] += part

        o_ref[:, 2 * D:] = dv_ref[...].astype(BF16)

    ins = [(qkv, ('cols', D, 0)), (qkv, ('cols', D, 1)), (qg, 'full'), (kg, 'full'), (dqn, 'rows'), (dkn, 'rows'),
           (dv, 'rows')]
    return _row_call(body, ins, [((S, D3), BF16, 'rows'), ((1, ATT_HEAD_DIM), F32, 'acc'),
                                 ((1, ATT_HEAD_DIM), F32, 'acc')], name="att_qknorm_bwd", tr=tr, n_rows=S)


def _att_band_mask():
    r = np.arange(ATT_TQ)[:, None]
    c = np.arange(ATT_TQ + ATT_PAD)[None, :]
    lo = (r // CHUNK) * CHUNK
    return (c >= lo) & (c < lo + ATT_PAD + CHUNK)


def _att_bias_toeplitz(rel_bias):
    H = rel_bias.shape[0]
    R, C = ATT_TQ, ATT_TQ + ATT_PAD
    L = C + R - 1
    assert R - 1 < REL_CLIP
    near = rel_bias[:, REL_CLIP - (R - 1):2 * REL_CLIP][:, ::-1]
    far = jnp.broadcast_to(rel_bias[:, 2 * REL_CLIP:], (H, L - near.shape[1]))
    v = jnp.concatenate([far, near, jnp.zeros((H, 1), rel_bias.dtype)], axis=1)
    skew = jnp.broadcast_to(v[:, None, :], (H, R, L + 1)).reshape(H, R * (L + 1))[:, :R * L].reshape(H, R, L)
    return skew[:, :, R - 1:R - 1 + C]


def _att_bias_tile(rel_bias):
    return jnp.where(_att_band_mask()[None], _att_bias_toeplitz(rel_bias), MASK_VALUE).astype(F32)


def _att_bias_grad(dtile, rel_bias):
    _, pull = jax.vjp(_att_bias_toeplitz, rel_bias)
    return pull(jnp.where(_att_band_mask()[None], dtile, 0.0))[0]


def _att_core_fwd(qn, kp, vp, bias):
    S, D = qn.shape
    NH = D // ATT_HEAD_DIM
    KW = ATT_TQ + ATT_PAD
    scale = ATT_HEAD_DIM ** -0.5

    def body(q_ref, k_ref, v_ref, b_ref, o_ref):
        qb = pl.program_id(1)
        start = pl.multiple_of(qb * ATT_TQ, ATT_TQ)
        ks = k_ref[pl.ds(start, KW), :]
        vs = v_ref[pl.ds(start, KW), :]
        s = lax.dot_general(q_ref[...], ks, (((1,), (1,)), ((), ())), preferred_element_type=F32) * scale + b_ref[0]
        kpos = lax.broadcasted_iota(jnp.int32, (1, KW), 1) + (qb * ATT_TQ - ATT_PAD)
        s = jnp.where(kpos >= 0, s, MASK_VALUE)
        m = jnp.max(s, axis=-1, keepdims=True)
        p = jnp.exp(s - m)
        l = jnp.sum(p, axis=-1, keepdims=True)
        o = jnp.dot(p.astype(BF16), vs, preferred_element_type=F32) / l
        o_ref[...] = o.astype(BF16)

    qspec = pl.BlockSpec((ATT_TQ, ATT_HEAD_DIM), lambda h, qb: (qb, h))
    kvspec = pl.BlockSpec((S + ATT_PAD, ATT_HEAD_DIM), lambda h, qb: (0, h))
    return pl.pallas_call(
        body, name="att_core_fwd", grid=(NH, S // ATT_TQ),
        in_specs=[qspec, kvspec, kvspec, pl.BlockSpec((1, ATT_TQ, KW), lambda h, qb: (h, 0, 0))],
        out_specs=qspec, out_shape=jax.ShapeDtypeStruct((S, D), BF16),
        compiler_params=_cparams(("parallel", "arbitrary")),
    )(qn, kp, vp, bias)


def _att_core_bwd(qn, kp, vp, bias, do):
    S, D = qn.shape
    NH = D // ATT_HEAD_DIM
    KW = ATT_TQ + ATT_PAD
    scale = ATT_HEAD_DIM ** -0.5

    def body(q_ref, k_ref, v_ref, b_ref, do_ref, dq_ref, dk_ref, dv_ref, db_ref):
        qb = pl.program_id(1)
        start = pl.multiple_of(qb * ATT_TQ, ATT_TQ)
        q = q_ref[...]
        dov = do_ref[...]
        ks = k_ref[pl.ds(start, KW), :]
        vs = v_ref[pl.ds(start, KW), :]
        s = lax.dot_general(q, ks, (((1,), (1,)), ((), ())), preferred_element_type=F32) * scale + b_ref[0]
        kpos = lax.broadcasted_iota(jnp.int32, (1, KW), 1) + (qb * ATT_TQ - ATT_PAD)
        s = jnp.where(kpos >= 0, s, MASK_VALUE)
        m = jnp.max(s, axis=-1, keepdims=True)
        e = jnp.exp(s - m)
        p = e / jnp.sum(e, axis=-1, keepdims=True)
        dp = lax.dot_general(dov, vs, (((1,), (1,)), ((), ())), preferred_element_type=F32)
        ds = p * (dp - jnp.sum(p * dp, axis=-1, keepdims=True))
        dsb = ds.astype(BF16)
        dq_ref[...] = jnp.dot(dsb, ks, preferred_element_type=F32) * scale
        dk = lax.dot_general(dsb, q, (((0,), (0,)), ((), ())), preferred_element_type=F32) * scale
        dv = lax.dot_general(p.astype(BF16), dov, (((0,), (0,)), ((), ())), preferred_element_type=F32)

        @pl.when(qb == 0)
        def _():
            dk_ref[...] = jnp.zeros_like(dk_ref)
            dv_ref[...] = jnp.zeros_like(dv_ref)
            db_ref[0] = ds

        @pl.when(qb > 0)
        def _():
            db_ref[0] += ds

        dk_ref[pl.ds(start, KW), :] += dk
        dv_ref[pl.ds(start, KW), :] += dv

    qspec = pl.BlockSpec((ATT_TQ, ATT_HEAD_DIM), lambda h, qb: (qb, h))
    kvspec = pl.BlockSpec((S + ATT_PAD, ATT_HEAD_DIM), lambda h, qb: (0, h))
    bspec = pl.BlockSpec((1, ATT_TQ, KW), lambda h, qb: (h, 0, 0))
    return pl.pallas_call(
        body, name="att_core_bwd", grid=(NH, S // ATT_TQ),
        in_specs=[qspec, kvspec, kvspec, bspec, qspec],
        out_specs=[qspec, kvspec, kvspec, bspec],
        out_shape=[jax.ShapeDtypeStruct((S, D), F32), jax.ShapeDtypeStruct((S + ATT_PAD, D), F32),
                   jax.ShapeDtypeStruct((S + ATT_PAD, D), F32), jax.ShapeDtypeStruct((NH, ATT_TQ, KW), F32)],
        compiler_params=_cparams(("parallel", "arbitrary")),
    )(qn, kp, vp, bias, do)


def _att_mixer_fwd(x, gain, w_qkv, qg, kg, rel_bias, w_out):
    h = _rms_fwd(x, gain, name="att_norm")
    qkv = _mm(h, w_qkv, name="att_qkv")
    qn, kn, v = _qk_norm_fwd(qkv, qg, kg)
    kp = jnp.pad(kn, ((ATT_PAD, 0), (0, 0)))
    vp = jnp.pad(v, ((ATT_PAD, 0), (0, 0)))
    bias = _att_bias_tile(rel_bias)
    o = _att_core_fwd(qn, kp, vp, bias)
    y = _mm(o, w_out, extras=(x,), epilogue=lambda acc, res: (acc + res,), name="att_out")
    return y, (h, qkv, qn, kp, vp, bias, o, rel_bias)


def _att_mixer_bwd(x, gain, w_qkv, qg, kg, w_out, saved, dy, dyb):
    h, qkv, qn, kp, vp, bias, o, rel_bias = saved
    do = _mm(dyb, w_out, tb=True, out_dtypes=(BF16,), name="att_do")
    dw_out = _mm(o, dyb, ta=True, out_dtypes=(BF16,), name="att_dwout")
    dqn, dkp, dvp, dbt = _att_core_bwd(qn, kp, vp, bias, do)
    drel = _att_bias_grad(dbt, rel_bias)
    dqkv, dqg, dkg = _qk_norm_bwd(qkv, qg, kg, dqn, dkp[ATT_PAD:], dvp[ATT_PAD:])
    dh = _mm(dqkv, w_qkv, tb=True, name="att_dh")
    dw_qkv = _mm(h, dqkv, ta=True, out_dtypes=(BF16,), name="att_dwqkv")
    dx, dxb, dgain = _rms_bwd(x, gain, dh, dy, name="att_dnorm")
    return dx, dxb, dgain, dw_qkv, dqg, dkg, drel, dw_out


def _ssm_tables(a_re, a_im, log_dt, b_re, b_im, c_re, c_im):
    G, N = a_re.shape
    NS = G // SSM_SLAB_GROUPS
    lam = lax.complex(a_re, a_im)
    dt = jnp.exp(log_dt)[:, None]
    abar = jnp.exp(lam * dt)
    coef = (abar - 1.0) / lam
    bbar = coef[..., None] * lax.complex(b_re, b_im)
    eye = jnp.eye(SSM_SLAB_GROUPS, dtype=F32)

    def blockdiag(t):
        P, Q = t.shape[1:]
        t = t.reshape(NS, SSM_SLAB_GROUPS, P, Q)
        return jnp.einsum('sgpq,gh->sgphq', t, eye).reshape(NS, SSM_SLAB_GROUPS * P, SSM_SLAB_GROUPS * Q)

    bt = jnp.swapaxes(bbar, 1, 2)
    bmat = jnp.concatenate([blockdiag(jnp.real(bt)), blockdiag(jnp.imag(bt))], axis=2)
    ct = jnp.swapaxes(lax.complex(c_re, c_im), 1, 2)
    cmat = jnp.concatenate([blockdiag(jnp.real(ct)), -blockdiag(jnp.imag(ct))], axis=1)
    al = abar.reshape(NS, 1, SSM_SLAB_STATE)
    rows = jnp.arange(SUBLANES)[None, :, None]
    fwd, bwd = [], []
    for k in (1, 2, 4):
        ak = al ** k
        f = jnp.where(rows >= k, ak, 0.0)
        b = jnp.where(rows < SUBLANES - k, ak, 0.0)
        fwd += [jnp.real(f), jnp.imag(f)]
        bwd += [jnp.real(b), jnp.imag(b)]
    pf = al ** (rows + 1)
    pb = al ** (SUBLANES - rows)
    fwd += [jnp.real(pf), jnp.imag(pf)]
    bwd += [jnp.real(pb), jnp.imag(pb)]
    coef_f = jnp.concatenate(fwd, axis=1).astype(F32)
    coef_b = jnp.concatenate(bwd, axis=1).astype(F32)
    return dict(lam=lam, dt=dt, abar=abar, coef=coef, bmat=bmat.astype(BF16), cmat=cmat.astype(BF16),
                bmat_t=jnp.swapaxes(bmat, 1, 2).astype(BF16), cmat_t=jnp.swapaxes(cmat, 1, 2).astype(BF16),
                coef_f=coef_f, coef_b=coef_b)


def _ssm_scan_fwd(u, tabs, d_skip):
    S, D = u.shape
    NS = D // LANES
    W = 2 * SSM_SLAB_STATE
    T = _tile(S, 512, SUBLANES)
    HS = SSM_SLAB_STATE

    def body(u_ref, bm_ref, cm_ref, cf_ref, d_ref, y_ref, xs_ref, bu_scr, carry_scr):
        i = pl.program_id(1)

        @pl.when(i == 0)
        def _():
            carry_scr[...] = jnp.zeros_like(carry_scr)

        uv = u_ref[...]
        bu_scr[...] = jnp.dot(uv.astype(BF16), bm_ref[0], preferred_element_type=F32)

        def step(r, carry):
            rows = pl.ds(pl.multiple_of(r * SUBLANES, SUBLANES), SUBLANES)
            xr = bu_scr[rows, 0:HS]
            xi = bu_scr[rows, HS:W]
            for n, k in enumerate((1, 2, 4)):
                ar = cf_ref[0, 16 * n:16 * n + 8, :]
                ai = cf_ref[0, 16 * n + 8:16 * n + 16, :]
                sr = pltpu.roll(xr, k, 0)
                si = pltpu.roll(xi, k, 0)
                xr, xi = xr + ar * sr - ai * si, xi + ar * si + ai * sr
            pr = cf_ref[0, 48:56, :]
            pi_ = cf_ref[0, 56:64, :]
            cr, ci = carry
            xr, xi = xr + pr * cr - pi_ * ci, xi + pr * ci + pi_ * cr
            xs_ref[rows, 0:HS] = xr
            xs_ref[rows, HS:W] = xi
            return xr[SUBLANES - 1:SUBLANES, :], xi[SUBLANES - 1:SUBLANES, :]

        cr, ci = lax.fori_loop(0, T // SUBLANES, step, (carry_scr[0:1, 0:HS], carry_scr[0:1, HS:W]))
        carry_scr[0:1, 0:HS] = cr
        carry_scr[0:1, HS:W] = ci
        y_ref[...] = jnp.dot(xs_ref[...].astype(BF16), cm_ref[0], preferred_element_type=F32) + d_ref[...] * uv

    return pl.pallas_call(
        body, name="ssm_scan_fwd", grid=(NS, S // T),
        in_specs=[pl.BlockSpec((T, LANES), lambda j, i: (i, j)),
                  pl.BlockSpec((1, LANES, W), lambda j, i: (j, 0, 0)),
                  pl.BlockSpec((1, W, LANES), lambda j, i: (j, 0, 0)),
                  pl.BlockSpec((1, 8 * SUBLANES, HS), lambda j, i: (j, 0, 0)),
                  pl.BlockSpec((1, LANES), lambda j, i: (0, j))],
        out_specs=[pl.BlockSpec((T, LANES), lambda j, i: (i, j)), pl.BlockSpec((T, W), lambda j, i: (i, j))],
        out_shape=[jax.ShapeDtypeStruct((S, D), F32), jax.ShapeDtypeStruct((S, NS * W), F32)],
        scratch_shapes=[pltpu.VMEM((T, W), F32), pltpu.VMEM((SUBLANES, W), F32)],
        compiler_params=_cparams(("parallel", "arbitrary")),
    )(u, tabs['bmat'], tabs['cmat'], tabs['coef_f'], d_skip)


def _ssm_scan_bwd(u, xs, dy, tabs, d_skip):
    S, D = u.shape
    NS = D // LANES
    W = 2 * SSM_SLAB_STATE
    T = _tile(S, 512, SUBLANES)
    HS = SSM_SLAB_STATE
    nb = S // T

    def body(u_ref, xs_ref, dy_ref, bt_ref, ct_ref, cf_ref, d_ref, du_ref, gb_ref, gc_ref, q_ref,
             cy_scr, lam_scr, carry_scr):
        i = pl.program_id(1)

        @pl.when(i == 0)
        def _():
            carry_scr[...] = jnp.zeros_like(carry_scr)
            gb_ref[...] = jnp.zeros_like(gb_ref)
            gc_ref[...] = jnp.zeros_like(gc_ref)
            q_ref[...] = jnp.zeros_like(q_ref)

        dyv = dy_ref[...]
        dyb = dyv.astype(BF16)
        cy_scr[...] = jnp.dot(dyb, ct_ref[0], preferred_element_type=F32)

        def step(n, carry):
            r = T // SUBLANES - 1 - n
            rows = pl.ds(pl.multiple_of(r * SUBLANES, SUBLANES), SUBLANES)
            cyr = cy_scr[rows, 0:HS]
            cyi = cy_scr[rows, HS:W]
            lr, li = cyr, cyi
            for m, k in enumerate((1, 2, 4)):
                br = cf_ref[0, 16 * m:16 * m + 8, :]
                bi = cf_ref[0, 16 * m + 8:16 * m + 16, :]
                sr = pltpu.roll(lr, SUBLANES - k, 0)
                si = pltpu.roll(li, SUBLANES - k, 0)
                lr, li = lr + br * sr + bi * si, li + br * si - bi * sr
            pr = cf_ref[0, 48:56, :]
            pi_ = cf_ref[0, 56:64, :]
            cr, ci, qr, qi = carry
            lr, li = lr + pr * cr + pi_ * ci, li + pr * ci - pi_ * cr
            lam_scr[rows, 0:HS] = lr
            lam_scr[rows, HS:W] = li
            mr, mi = lr - cyr, li - cyi
            xr = xs_ref[rows, 0:HS]
            xi = xs_ref[rows, HS:W]
            return lr[0:1, :], li[0:1, :], qr + mr * xr + mi * xi, qi + mi * xr - mr * xi

        zero = jnp.zeros((SUBLANES, HS), F32)
        cr, ci, qr, qi = lax.fori_loop(0, T // SUBLANES, step,
                                       (carry_scr[0:1, 0:HS], carry_scr[0:1, HS:W], zero, zero))
        carry_scr[0:1, 0:HS] = cr
        carry_scr[0:1, HS:W] = ci
        q_ref[0, :, 0:HS] += qr
        q_ref[0, :, HS:W] += qi
        lamb = lam_scr[...].astype(BF16)
        uv = u_ref[...]
        du_ref[...] = jnp.dot(lamb, bt_ref[0], preferred_element_type=F32) + d_ref[...] * dyv
        gb_ref[0] += lax.dot_general(lamb, uv.astype(BF16), (((0,), (0,)), ((), ())), preferred_element_type=F32)
        gc_ref[0] += lax.dot_general(xs_ref[...].astype(BF16), dyb, (((0,), (0,)), ((), ())),
                                     preferred_element_type=F32)

    rev = lambda j, i: (nb - 1 - i, j)
    slab3 = lambda j, i: (j, 0, 0)
    return pl.pallas_call(
        body, name="ssm_scan_bwd", grid=(NS, nb),
        in_specs=[pl.BlockSpec((T, LANES), rev), pl.BlockSpec((T, W), rev), pl.BlockSpec((T, LANES), rev),
                  pl.BlockSpec((1, W, LANES), slab3), pl.BlockSpec((1, LANES, W), slab3),
                  pl.BlockSpec((1, 8 * SUBLANES, HS), slab3), pl.BlockSpec((1, LANES), lambda j, i: (0, j))],
        out_specs=[pl.BlockSpec((T, LANES), rev), pl.BlockSpec((1, W, LANES), slab3),
                   pl.BlockSpec((1, W, LANES), slab3), pl.BlockSpec((1, SUBLANES, W), slab3)],
        out_shape=[jax.ShapeDtypeStruct((S, D), F32), jax.ShapeDtypeStruct((NS, W, LANES), F32),
                   jax.ShapeDtypeStruct((NS, W, LANES), F32), jax.ShapeDtypeStruct((NS, SUBLANES, W), F32)],
        scratch_shapes=[pltpu.VMEM((T, W), F32), pltpu.VMEM((T, W), F32), pltpu.VMEM((SUBLANES, W), F32)],
        compiler_params=_cparams(("parallel", "arbitrary")),
    )(u, xs, dy, tabs['bmat_t'], tabs['cmat_t'], tabs['coef_b'], d_skip)


def _ssm_param_grads(tabs, b_re, b_im, gb, gc, q):
    NS = gb.shape[0]
    G = NS * SSM_SLAB_GROUPS
    N, C = SSM_STATE, SSM_GROUP

    def diag_blocks(t):
        t = t.reshape(NS, SSM_SLAB_GROUPS, N, SSM_SLAB_GROUPS, C)
        t = jnp.einsum('sgnhc,gh->sgnc', t, jnp.eye(SSM_SLAB_GROUPS, dtype=F32))
        return t.reshape(G, N, C)

    HS = SSM_SLAB_STATE
    g_bbar = lax.complex(diag_blocks(gb[:, :HS]), diag_blocks(gb[:, HS:]))
    g_c = lax.complex(diag_blocks(gc[:, :HS]), -diag_blocks(gc[:, HS:]))
    qs = jnp.sum(q, axis=1)
    qc = lax.complex(qs[:, :HS], qs[:, HS:]).reshape(G, N)
    lam, dt, abar, coef = tabs['lam'], tabs['dt'], tabs['abar'], tabs['coef']
    bmat = lax.complex(b_re, b_im)
    g_b = g_bbar * jnp.conj(coef)[..., None]
    g_coef = jnp.sum(g_bbar * jnp.conj(bmat), axis=-1)
    g_abar_coef = g_coef * jnp.conj(1.0 / lam)
    g_lam = g_coef * jnp.conj(-(abar - 1.0) / (lam * lam))
    g_ld = qc + jnp.conj(abar) * g_abar_coef
    g_lam = g_lam + g_ld * dt
    g_dt = jnp.sum(jnp.real(g_ld * jnp.conj(lam)), axis=-1)
    g_logdt = g_dt * dt[:, 0]
    g_ct = jnp.swapaxes(g_c, 1, 2)
    return (jnp.real(g_lam), jnp.imag(g_lam), g_logdt, jnp.real(g_b), jnp.imag(g_b), jnp.real(g_ct), jnp.imag(g_ct))


_GELU_C = math.sqrt(2.0 / math.pi)


def _gelu_fwd(y):
    S, D = y.shape

    def body(y_ref, z_ref):
        v = y_ref[...]
        z_ref[...] = (0.5 * v * (1.0 + jnp.tanh(_GELU_C * (v + 0.044715 * v * v * v)))).astype(BF16)

    return _row_call(body, [(y, 'rows')], [((S, D), BF16, 'rows')], name="ssm_gelu_fwd",
                     tr=_tile(S, 512, SUBLANES), n_rows=S)[0]


def _gelu_bwd(y, dz, u):
    S, D = y.shape

    def body(y_ref, dz_ref, u_ref, dy_ref, dd_ref):
        i = pl.program_id(0)
        v = y_ref[...]
        t = jnp.tanh(_GELU_C * (v + 0.044715 * v * v * v))
        g = 0.5 * (1.0 + t) + 0.5 * v * (1.0 - t * t) * _GELU_C * (1.0 + 3 * 0.044715 * v * v)
        dy = dz_ref[...] * g
        dy_ref[...] = dy
        part = jnp.sum(dy * u_ref[...], axis=0, keepdims=True)

        @pl.when(i == 0)
        def _():
            dd_ref[...] = part

        @pl.when(i > 0)
        def _():
            dd_ref[...] += part

    return _row_call(body, [(y, 'rows'), (dz, 'rows'), (u, 'rows')], [((S, D), F32, 'rows'), ((1, D), F32, 'acc')],
                     name="ssm_gelu_bwd", tr=_tile(S, 256, SUBLANES), n_rows=S)


def _glu_fwd(zz, x):
    S, D = x.shape

    def body(a_ref, g_ref, x_ref, o_ref):
        o_ref[...] = x_ref[...] + a_ref[...] * jax.nn.sigmoid(g_ref[...])

    return _row_call(body, [(zz, ('cols', D, 0)), (zz, ('cols', D, 1)), (x, 'rows')], [((S, D), F32, 'rows')],
                     name="ssm_glu_fwd", tr=_tile(S, 256, SUBLANES), n_rows=S)[0]


def _glu_bwd(zz, dm):
    S, D = dm.shape

    def body(a_ref, g_ref, dm_ref, o_ref):
        s = jax.nn.sigmoid(g_ref[...])
        d = dm_ref[...]
        o_ref[:, 0:D] = (d * s).astype(BF16)
        o_ref[:, D:] = (d * a_ref[...] * s * (1.0 - s)).astype(BF16)

    return _row_call(body, [(zz, ('cols', D, 0)), (zz, ('cols', D, 1)), (dm, 'rows')], [((S, 2 * D), BF16, 'rows')],
                     name="ssm_glu_bwd", tr=_tile(S, 256, SUBLANES), n_rows=S)[0]


def _ssm_mixer_fwd(x, gain, tabs, d_skip, w_glu):
    _, u = _rms_fwd(x, gain, name="ssm_norm", with_f32=True)
    yv, xs = _ssm_scan_fwd(u, tabs, d_skip)
    z = _gelu_fwd(yv)
    zz = _mm(z, w_glu, name="ssm_glu_in")
    y = _glu_fwd(zz, x)
    return y, (u, xs, yv, z, zz)


def _ssm_mixer_bwd(x, gain, tabs, d_skip, w_glu, b_re, b_im, saved, dy, dyb):
    u, xs, yv, z, zz = saved
    dzz = _glu_bwd(zz, dy)
    dz = _mm(dzz, w_glu, tb=True, name="ssm_dz")
    dw_glu = _mm(z, dzz, ta=True, out_dtypes=(BF16,), name="ssm_dwglu")
    dyv, dd = _gelu_bwd(yv, dz, u)
    du, gb, gc, q = _ssm_scan_bwd(u, xs, dyv, tabs, d_skip)
    small = _ssm_param_grads(tabs, b_re, b_im, gb, gc, q)
    dx, dxb, dgain = _rms_bwd(x, gain, du, dy, name="ssm_dnorm")
    return dx, dxb, dgain, small, dd, dw_glu


def _local_step(x, target, p):
    depth = p['norm_mix'].shape[0]
    tabs = _ssm_tables(p['ssm_a_re'], p['ssm_a_im'], p['ssm_log_dt'], p['ssm_b_re'], p['ssm_b_im'], p['ssm_c_re'],
                       p['ssm_c_im'])
    xs_in, saved_mix, saved_mlp = [], [], []
    for i in range(depth):
        gm = p['norm_mix'][i:i + 1]
        xs_in.append(x)
        if i % 4 == 0:
            x, sv = _conv_mixer_fwd(x, gm, p['conv_w_in'], p['conv_w'], p['conv_w_out'])
        elif i % 4 == 1:
            x, sv = _pool_mixer_fwd(x, gm, p['pool_w_in'], p['pool_w_group'], p['pool_scale'])
        elif i % 4 == 2:
            x, sv = _att_mixer_fwd(x, gm, p['att_w_qkv'], p['att_q_norm'], p['att_k_norm'], p['att_rel_bias'],
                                   p['att_w_out'])
        else:
            x, sv = _ssm_mixer_fwd(x, gm, tabs, p['ssm_d'], p['ssm_w_glu'])
        saved_mix.append(sv)
        xs_in.append(x)
        x, sv = _mlp_fwd(x, p['norm_mlp'][i:i + 1], p['mlp_w1'][i], p['mlp_w2'][i], tag=str(i))
        saved_mlp.append(sv)
    dx, dxb, loss_cols = _loss_head(x, target, name="loss_head")
    g = {'norm_mix': [None] * depth, 'norm_mlp': [None] * depth, 'mlp_w1': [None] * depth, 'mlp_w2': [None] * depth}
    for i in reversed(range(depth)):
        dx, dxb, g['norm_mlp'][i], g['mlp_w1'][i], g['mlp_w2'][i] = _mlp_bwd(
            xs_in[2 * i + 1], p['norm_mlp'][i:i + 1], p['mlp_w1'][i], p['mlp_w2'][i], saved_mlp[i], dx, dxb, tag=str(i))
        gm = p['norm_mix'][i:i + 1]
        xin, sv = xs_in[2 * i], saved_mix[i]
        if i % 4 == 0:
            dx, dxb, g['norm_mix'][i], g['conv_w_in'], g['conv_w'], g['conv_w_out'] = _conv_mixer_bwd(
                xin, gm, p['conv_w_in'], p['conv_w'], p['conv_w_out'], sv, dx, dxb)
        elif i % 4 == 1:
            dx, dxb, g['norm_mix'][i], g['pool_w_in'], g['pool_w_group'], g['pool_scale'] = _pool_mixer_bwd(
                xin, gm, p['pool_w_in'], p['pool_w_group'], p['pool_scale'], sv, dx, dxb)
        elif i % 4 == 2:
            (dx, dxb, g['norm_mix'][i], g['att_w_qkv'], g['att_q_norm'], g['att_k_norm'], g['att_rel_bias'],
             g['att_w_out']) = _att_mixer_bwd(xin, gm, p['att_w_qkv'], p['att_q_norm'], p['att_k_norm'],
                                              p['att_w_out'], sv, dx, dxb)
        else:
            dx, dxb, g['norm_mix'][i], small, g['ssm_d'], g['ssm_w_glu'] = _ssm_mixer_bwd(
                xin, gm, tabs, p['ssm_d'], p['ssm_w_glu'], p['ssm_b_re'], p['ssm_b_im'], sv, dx, dxb)
            (g['ssm_a_re'], g['ssm_a_im'], g['ssm_log_dt'], g['ssm_b_re'], g['ssm_b_im'], g['ssm_c_re'],
             g['ssm_c_im']) = small
    g['norm_mix'] = jnp.concatenate(g['norm_mix'], axis=0)
    g['norm_mlp'] = jnp.concatenate(g['norm_mlp'], axis=0)
    return loss_cols, dx, g


_ANY = pl.BlockSpec(memory_space=pl.ANY)
_VM = pl.BlockSpec(memory_space=pltpu.VMEM)
_REL_ALL = [(0, 0, 1), (0, 1, 0), (0, 1, 1), (1, 0, 0), (1, 0, 1), (1, 1, 0), (1, 1, 1)]
_REL_CHIPS = [(1, 0, 0), (0, 1, 0), (1, 1, 0)]


def _me():
    return lax.axis_index("x"), lax.axis_index("y"), lax.axis_index("c")


def _flip(pos, rel):
    return tuple(1 - p if r else p for p, r in zip(pos, rel))


def _chip_of(pos):
    return 2 * pos[0] + pos[1]


def _dev_of(pos):
    return 4 * pos[0] + 2 * pos[1] + pos[2]


def _gather_small(buf, *, reduce, name):
    rows = buf.shape[0]

    def body(in_ref, out_ref, *rest):
        if reduce:
            gath, send_sems, recv_sems = rest
        else:
            gath = out_ref
            send_sems, recv_sems = rest
        me = _me()
        gath[_dev_of(me)] = in_ref[...]
        copies = []
        for k, rel in enumerate(_REL_ALL):
            peer = _flip(me, rel)
            cp = pltpu.make_async_remote_copy(src_ref=in_ref, dst_ref=gath.at[_dev_of(me)], send_sem=send_sems.at[k],
                                              recv_sem=recv_sems.at[k], device_id=peer, device_id_type=MESH)
            cp.start()
            copies.append(cp)
        for k, rel in enumerate(_REL_ALL):
            peer = _flip(me, rel)
            pltpu.make_async_remote_copy(src_ref=in_ref, dst_ref=gath.at[_dev_of(peer)], send_sem=send_sems.at[k],
                                         recv_sem=recv_sems.at[k], device_id=peer, device_id_type=MESH).wait_recv()
        for cp in copies:
            cp.wait_send()
        if reduce:
            acc = gath[0]
            for s in range(1, N_DEV):
                acc = acc + gath[s]
            out_ref[...] = acc

    out_shape = (rows, LANES) if reduce else (N_DEV, rows, LANES)
    scratch = ([pltpu.VMEM((N_DEV, rows, LANES), F32)] if reduce else []) + [
        pltpu.SemaphoreType.DMA((len(_REL_ALL),)), pltpu.SemaphoreType.DMA((len(_REL_ALL),))]
    return pl.pallas_call(
        body, name=name, in_specs=[_VM], out_specs=_VM, out_shape=jax.ShapeDtypeStruct(out_shape, F32),
        scratch_shapes=scratch, compiler_params=pltpu.CompilerParams(vmem_limit_bytes=VMEM_LIMIT),
    )(buf)


def _region(ref, kind, k, shard_shape, half=None):
    r, c = shard_shape
    hr = r // 2
    if kind == 'col':
        rows = slice(None) if half is None else pl.ds(pl.multiple_of(half * hr, 16), hr)
        return ref.at[rows, pl.ds(pl.multiple_of(k * c, LANES), c)]
    if half is None:
        return ref.at[pl.ds(pl.multiple_of(k * r, 16), r), :]
    return ref.at[pl.ds(pl.multiple_of(k * r + half * hr, 16), hr), :]


def _all_gather_matrices(shards, kinds, *, name):
    n = len(shards)
    fulls = [(s.shape[0], s.shape[1] * N_CHIPS) if kd == 'col' else (s.shape[0] * N_CHIPS, s.shape[1])
             for s, kd in zip(shards, kinds)]
    nj = len(_REL_CHIPS)

    def body(*refs):
        in_refs, out_refs = refs[:n], refs[n:2 * n]
        ici_send, ici_recv, d2d_send, d2d_recv, local_sems = refs[2 * n:]
        me = _me()
        core = me[2]
        sib = _flip(me, (0, 0, 1))
        pending, locs = [], []
        for m in range(n):
            shp = shards[m].shape
            hr = shp[0] // 2
            lc = pltpu.make_async_copy(in_refs[m], _region(out_refs[m], kinds[m], _chip_of(me), shp), local_sems.at[m])
            lc.start()
            locs.append(lc)
            src = in_refs[m].at[pl.ds(pl.multiple_of(core * hr, 16), hr), :]
            dst = _region(out_refs[m], kinds[m], _chip_of(me), shp, half=core)
            for j, rel in enumerate(_REL_CHIPS):
                cp = pltpu.make_async_remote_copy(src_ref=src, dst_ref=dst, send_sem=ici_send.at[m, j],
                                                  recv_sem=ici_recv.at[m, j], device_id=_flip(me, rel),
                                                  device_id_type=MESH)
                cp.start()
                pending.append(cp)
        for m in range(n):
            shp = shards[m].shape
            for j, rel in enumerate(_REL_CHIPS):
                peer = _flip(me, rel)
                landed = _region(out_refs[m], kinds[m], _chip_of(peer), shp, half=core)
                pltpu.make_async_remote_copy(src_ref=landed, dst_ref=landed, send_sem=ici_send.at[m, j],
                                             recv_sem=ici_recv.at[m, j], device_id=peer,
                                             device_id_type=MESH).wait_recv()
                fw = pltpu.make_async_remote_copy(src_ref=landed, dst_ref=landed, send_sem=d2d_send.at[m, j],
                                                  recv_sem=d2d_recv.at[m, j], device_id=sib, device_id_type=MESH)
                fw.start()
                pending.append(fw)
        for m in range(n):
            shp = shards[m].shape
            for j, rel in enumerate(_REL_CHIPS):
                other = _region(out_refs[m], kinds[m], _chip_of(_flip(me, rel)), shp, half=1 - core)
                pltpu.make_async_remote_copy(src_ref=other, dst_ref=other, send_sem=d2d_send.at[m, j],
                                             recv_sem=d2d_recv.at[m, j], device_id=sib,
                                             device_id_type=MESH).wait_recv()
        for cp in pending:
            cp.wait_send()
        for lc in locs:
            lc.wait()

    return pl.pallas_call(
        body, name=name, in_specs=[_ANY] * n, out_specs=[_ANY] * n,
        out_shape=[jax.ShapeDtypeStruct(f, BF16) for f in fulls],
        scratch_shapes=[pltpu.SemaphoreType.DMA((n, nj))] * 4 + [pltpu.SemaphoreType.DMA((n,))],
    )(*shards)


def _piece_shape(full, kind):
    R, C = full
    return (R // 2, C // N_CHIPS) if kind == 'col' else (R // N_DEV, C)


def _piece(ref, kind, k, c, full):
    pr, pc = _piece_shape(full, kind)
    if kind == 'col':
        return ref.at[pl.ds(pl.multiple_of(c * pr, 16), pr), pl.ds(pl.multiple_of(k * pc, LANES), pc)]
    return ref.at[pl.ds(pl.multiple_of((2 * k + c) * pr, 16), pr), :]


def _scatter_pieces(grads, kinds, *, name):
    n = len(grads)
    pieces = [_piece_shape(g.shape, kd) for g, kd in zip(grads, kinds)]

    def body(*refs):
        in_refs, out_refs = refs[:n], refs[n:2 * n]
        send_sems, recv_sems, local_sems = refs[2 * n:]
        me = _me()
        sends, locs = [], []
        for m in range(n):
            full = grads[m].shape
            lc = pltpu.make_async_copy(_piece(in_refs[m], kinds[m], _chip_of(me), me[2], full),
                                       out_refs[m].at[_dev_of(me)], local_sems.at[m])
            lc.start()
            locs.append(lc)
            for j, rel in enumerate(_REL_ALL):
                peer = _flip(me, rel)
                cp = pltpu.make_async_remote_copy(
                    src_ref=_piece(in_refs[m], kinds[m], _chip_of(peer), peer[2], full),
                    dst_ref=out_refs[m].at[_dev_of(me)], send_sem=send_sems.at[m, j], recv_sem=recv_sems.at[m, j],
                    device_id=peer, device_id_type=MESH)
                cp.start()
                sends.append(cp)
        for m in range(n):
            full = grads[m].shape
            for j, rel in enumerate(_REL_ALL):
                peer = _flip(me, rel)
                pltpu.make_async_remote_copy(
                    src_ref=_piece(in_refs[m], kinds[m], _chip_of(me), me[2], full),
                    dst_ref=out_refs[m].at[_dev_of(peer)], send_sem=send_sems.at[m, j], recv_sem=recv_sems.at[m, j],
                    device_id=peer, device_id_type=MESH).wait_recv()
        for cp in sends:
            cp.wait_send()
        for lc in locs:
            lc.wait()

    return pl.pallas_call(
        body, name=name, in_specs=[_ANY] * n, out_specs=[_ANY] * n,
        out_shape=[jax.ShapeDtypeStruct((N_DEV,) + p, BF16) for p in pieces],
        scratch_shapes=[pltpu.SemaphoreType.DMA((n, len(_REL_ALL))), pltpu.SemaphoreType.DMA((n, len(_REL_ALL))),
                        pltpu.SemaphoreType.DMA((n,))],
    )(*grads)


def _sum_into(parts, dest, layer, core, *, name):
    _, pr, pc = parts.shape
    tr = _tile(pr, max(16, (1 << 19) // pc // 16 * 16), 16)
    nb = pr // tr

    def body(core_ref, p_ref, d_ref, o_ref):
        acc = p_ref[0].astype(F32)
        for s in range(1, N_DEV):
            acc = acc + p_ref[s].astype(F32)
        o_ref[0] = acc

    grid_spec = pltpu.PrefetchScalarGridSpec(
        num_scalar_prefetch=1, grid=(nb,),
        in_specs=[pl.BlockSpec((N_DEV, tr, pc), lambda i, c: (0, i, 0)), _ANY],
        out_specs=pl.BlockSpec((1, tr, pc), lambda i, c: (layer, c[0] * nb + i, 0)))
    return pl.pallas_call(
        body, name=name, grid_spec=grid_spec, out_shape=jax.ShapeDtypeStruct(dest.shape, F32),
        input_output_aliases={2: 0}, compiler_params=_cparams(("arbitrary",)),
    )(core, parts, dest)


def _exchange_halves(blocks, *, name):
    n_out = len(blocks)
    n = sum(b.shape[0] for b in blocks)

    def body(*refs):
        in_refs, out_refs = refs[:n_out], refs[n_out:2 * n_out]
        send_sems, recv_sems = refs[2 * n_out:]
        me = _me()
        sib = _flip(me, (0, 0, 1))
        sends, m = [], 0
        for o in range(n_out):
            L, r2, _ = blocks[o].shape
            hr = r2 // 2
            for l in range(L):
                rows = pl.ds(pl.multiple_of(me[2] * hr, SUBLANES), hr)
                cp = pltpu.make_async_remote_copy(src_ref=in_refs[o].at[l, rows, :], dst_ref=out_refs[o].at[l, rows, :],
                                                  send_sem=send_sems.at[m], recv_sem=recv_sems.at[m], device_id=sib,
                                                  device_id_type=MESH)
                cp.start()
                sends.append(cp)
                m += 1
        m = 0
        for o in range(n_out):
            L, r2, _ = blocks[o].shape
            hr = r2 // 2
            for l in range(L):
                rows = pl.ds(pl.multiple_of(sib[2] * hr, SUBLANES), hr)
                theirs = out_refs[o].at[l, rows, :]
                pltpu.make_async_remote_copy(src_ref=theirs, dst_ref=theirs, send_sem=send_sems.at[m],
                                             recv_sem=recv_sems.at[m], device_id=sib, device_id_type=MESH).wait_recv()
                m += 1
        for cp in sends:
            cp.wait_send()

    return pl.pallas_call(
        body, name=name, in_specs=[_ANY] * n_out, out_specs=[_ANY] * n_out,
        out_shape=[jax.ShapeDtypeStruct(b.shape, F32) for b in blocks],
        input_output_aliases={o: o for o in range(n_out)},
        scratch_shapes=[pltpu.SemaphoreType.DMA((n,)), pltpu.SemaphoreType.DMA((n,))],
    )(*blocks)


def _adamw(w, g, m, v, *, name):
    R, C = w.shape
    tr = _tile(R, max(SUBLANES, (1 << 19) // C // SUBLANES * SUBLANES), SUBLANES)
    c1 = 1.0 / (1.0 - ADAM_B1 ** ADAM_STEP)
    c2 = 1.0 / (1.0 - ADAM_B2 ** ADAM_STEP)

    def body(w_ref, g_ref, m_ref, v_ref, d_ref, nm_ref, nv_ref):
        gv = g_ref[...]
        nm = ADAM_B1 * m_ref[...] + (1.0 - ADAM_B1) * gv
        nv = ADAM_B2 * v_ref[...] + (1.0 - ADAM_B2) * (gv * gv)
        nm_ref[...] = nm
        nv_ref[...] = nv
        d_ref[...] = -ADAM_LR * ((nm * c1) / (jnp.sqrt(nv * c2) + ADAM_EPS) + ADAM_WD * w_ref[...])

    blk = pl.BlockSpec((tr, C), lambda i: (i, 0))
    return pl.pallas_call(
        body, name=name, grid=(R // tr,), in_specs=[blk] * 4, out_specs=[blk] * 3,
        out_shape=[jax.ShapeDtypeStruct((R, C), F32)] * 3, compiler_params=_cparams(("parallel",)),
    )(w, g, m, v)


_BIG = ['mlp_w1', 'mlp_w2', 'conv_w_in', 'conv_w_out', 'pool_w_in', 'pool_w_group', 'att_w_qkv', 'att_w_out',
        'ssm_w_glu']
_KIND = {'mlp_w1': 'col', 'mlp_w2': 'row', 'conv_w_in': 'col', 'conv_w_out': 'row', 'pool_w_in': 'row',
         'pool_w_group': 'row', 'att_w_qkv': 'col', 'att_w_out': 'row', 'ssm_w_glu': 'col'}
_TINY_SHARDED = ['conv_w', 'pool_scale', 'ssm_d']
_REPLICATED = ['norm_mix', 'norm_mlp', 'att_q_norm', 'att_k_norm', 'att_rel_bias', 'ssm_a_re', 'ssm_a_im',
               'ssm_log_dt', 'ssm_b_re', 'ssm_b_im', 'ssm_c_re', 'ssm_c_im']
_SMALL = _REPLICATED + _TINY_SHARDED
_ORDER = ['norm_mix', 'norm_mlp', 'mlp_w1', 'mlp_w2', 'conv_w_in', 'conv_w', 'conv_w_out', 'pool_w_in',
          'pool_w_group', 'pool_scale', 'att_w_qkv', 'att_q_norm', 'att_k_norm', 'att_rel_bias', 'att_w_out',
          'ssm_a_re', 'ssm_a_im', 'ssm_log_dt', 'ssm_b_re', 'ssm_b_im', 'ssm_c_re', 'ssm_c_im', 'ssm_d', 'ssm_w_glu']
_LAYER_OF = {'conv_w_in': 0, 'conv_w_out': 0, 'pool_w_in': 1, 'pool_w_group': 1, 'att_w_qkv': 2, 'att_w_out': 2,
             'ssm_w_glu': 3}


def _pack(arrays):
    flat = jnp.concatenate([a.reshape(-1).astype(F32) for a in arrays])
    n = flat.shape[0]
    total = -(-n // (SUBLANES * LANES)) * (SUBLANES * LANES)
    return jnp.pad(flat, (0, total - n)).reshape(total // LANES, LANES)


def _unpack(buf, shapes):
    flat = buf.reshape(-1)
    out, off = [], 0
    for s in shapes:
        n = int(np.prod(s))
        out.append(flat[off:off + n].reshape(s))
        off += n
    return out


def _matrices(w, name):
    t = w.reshape((-1,) + w.shape[-2:])
    return [t[l] for l in range(t.shape[0])]


def kernel(x, norm_mix, norm_mlp, mlp_w1, mlp_w2, conv_w_in, conv_w, conv_w_out, pool_w_in, pool_w_group, pool_scale, att_w_qkv, att_q_norm, att_k_norm, att_rel_bias, att_w_out, ssm_a_re, ssm_a_im, ssm_log_dt, ssm_b_re, ssm_b_im, ssm_c_re, ssm_c_im, ssm_d, ssm_w_glu, loss_target, m_norm_mix, m_norm_mlp, m_mlp_w1, m_mlp_w2, m_conv_w_in, m_conv_w, m_conv_w_out, m_pool_w_in, m_pool_w_group, m_pool_scale, m_att_w_qkv, m_att_q_norm, m_att_k_norm, m_att_rel_bias, m_att_w_out, m_ssm_a_re, m_ssm_a_im, m_ssm_log_dt, m_ssm_b_re, m_ssm_b_im, m_ssm_c_re, m_ssm_c_im, m_ssm_d, m_ssm_w_glu, v_norm_mix, v_norm_mlp, v_mlp_w1, v_mlp_w2, v_conv_w_in, v_conv_w, v_conv_w_out, v_pool_w_in, v_pool_w_group, v_pool_scale, v_att_w_qkv, v_att_q_norm, v_att_k_norm, v_att_rel_bias, v_att_w_out, v_ssm_a_re, v_ssm_a_im, v_ssm_log_dt, v_ssm_b_re, v_ssm_b_im, v_ssm_c_re, v_ssm_c_im, v_ssm_d, v_ssm_w_glu):
    args = dict(locals())
    W = {n: args[n] for n in _ORDER}
    M = {n: args['m_' + n] for n in _ORDER}
    V = {n: args['v_' + n] for n in _ORDER}
    depth = norm_mix.shape[0]
    d_model = x.shape[-1]
    chip = 2 * lax.axis_index("x") + lax.axis_index("y")

    tiny = _gather_small(_pack([W[n] for n in _TINY_SHARDED]), reduce=False, name="gather_vectors")
    tiny_shapes = [W[n].shape for n in _TINY_SHARDED]
    per_chip = [_unpack(tiny[2 * k], tiny_shapes) for k in range(N_CHIPS)]
    full_tiny = {n: jnp.concatenate([per_chip[k][i] for k in range(N_CHIPS)], axis=-1)
                 for i, n in enumerate(_TINY_SHARDED)}
    groups = [[] for _ in range(depth)]
    for n in _BIG:
        mats = _matrices(W[n], n)
        for l in range(len(mats)):
            groups[l if n.startswith('mlp') else _LAYER_OF[n]].append((n, l))
    shard_bf = {n: _matrices(W[n].astype(BF16), n) for n in _BIG}
    full = {n: [None] * len(shard_bf[n]) for n in _BIG}
    for i in range(depth):
        got = _all_gather_matrices([shard_bf[n][l] for n, l in groups[i]], [_KIND[n] for n, _ in groups[i]],
                                   name=f"gather_weights_{i}")
        for (n, l), t in zip(groups[i], got):
            full[n][l] = t

    p = dict(
        norm_mix=norm_mix, norm_mlp=norm_mlp, mlp_w1=full['mlp_w1'], mlp_w2=full['mlp_w2'],
        conv_w_in=full['conv_w_in'][0], conv_w=full_tiny['conv_w'][0], conv_w_out=full['conv_w_out'][0],
        pool_w_in=full['pool_w_in'][0], pool_w_group=jnp.stack(full['pool_w_group']),
        pool_scale=full_tiny['pool_scale'], att_w_qkv=full['att_w_qkv'][0], att_q_norm=att_q_norm,
        att_k_norm=att_k_norm, att_rel_bias=att_rel_bias[0], att_w_out=full['att_w_out'][0],
        ssm_a_re=ssm_a_re[0], ssm_a_im=ssm_a_im[0], ssm_log_dt=ssm_log_dt[0], ssm_b_re=ssm_b_re[0],
        ssm_b_im=ssm_b_im[0], ssm_c_re=ssm_c_re[0], ssm_c_im=ssm_c_im[0], ssm_d=full_tiny['ssm_d'],
        ssm_w_glu=full['ssm_w_glu'][0])

    loss_cols, dx, g = _local_step(x[0], loss_target[0], p)
    loss = lax.psum(0.5 * jnp.sum(loss_cols) / d_model, ("x", "y", "c"))

    gmats = {n: (g[n] if isinstance(g[n], list) else _matrices(g[n], n)) for n in _BIG}
    core = lax.axis_index("c").astype(jnp.int32).reshape(1)
    blocks = {n: lax.empty((len(gmats[n]),) + W[n].shape[-2:], F32) for n in _BIG}
    for i in reversed(range(depth)):
        parts = _scatter_pieces([gmats[n][l] for n, l in groups[i]], [_KIND[n] for n, _ in groups[i]],
                                name=f"scatter_grads_{i}")
        for (n, l), t in zip(groups[i], parts):
            blocks[n] = _sum_into(t, blocks[n], l, core, name=f"sum_grads_{n}_{l}")
    reduced = _exchange_halves([blocks[n] for n in _BIG], name="exchange_halves")
    grads = {n: t.reshape(W[n].shape) for n, t in zip(_BIG, reduced)}

    small_full_shapes = [W[n].shape for n in _REPLICATED] + [full_tiny[n].shape for n in _TINY_SHARDED]
    gsmall = _gather_small(_pack([g[n] for n in _SMALL]), reduce=True, name="reduce_small_grads")
    for n, t in zip(_SMALL, _unpack(gsmall, small_full_shapes)):
        if n in _TINY_SHARDED:
            width = W[n].shape[-1]
            t = lax.dynamic_slice_in_dim(t, chip * width, width, axis=t.ndim - 1)
        grads[n] = t.reshape(W[n].shape)

    delta, new_m, new_v = {}, {}, {}
    for n in _BIG:
        shp = W[n].shape
        two = (int(np.prod(shp[:-1])), shp[-1])
        d, nm, nv = _adamw(W[n].reshape(two), grads[n].reshape(two), M[n].reshape(two), V[n].reshape(two),
                           name=f"adamw_{n}")
        delta[n], new_m[n], new_v[n] = d.reshape(shp), nm.reshape(shp), nv.reshape(shp)
    shapes = [W[n].shape for n in _SMALL]
    d, nm, nv = _adamw(_pack([W[n] for n in _SMALL]), _pack([grads[n] for n in _SMALL]),
                       _pack([M[n] for n in _SMALL]), _pack([V[n] for n in _SMALL]), name="adamw_small")
    for n, a, b, c in zip(_SMALL, _unpack(d, shapes), _unpack(nm, shapes), _unpack(nv, shapes)):
        delta[n], new_m[n], new_v[n] = a, b, c

    return (loss, dx.reshape(x.shape), *[grads[n] for n in _ORDER], *[delta[n] for n in _ORDER],
            *[new_m[n] for n in _ORDER], *[new_v[n] for n in _ORDER])
```

```python
import functools
import math

import numpy as np
import jax
import jax.numpy as jnp
from jax import lax
from jax.experimental import pallas as pl
from jax.experimental.pallas import tpu as pltpu

F32 = jnp.float32
BF16 = jnp.bfloat16
MESH = pl.DeviceIdType.MESH

V7X_VMEM_BYTES = 64 * 1024 * 1024
VMEM_LIMIT = V7X_VMEM_BYTES - 12 * 1024 * 1024
LANES = 128
SUBLANES = 8

CHUNK = 64
ATT_HEAD_DIM = 128
ATT_PAD = 8 * CHUNK
REL_CLIP = 256
MASK_VALUE = -1e30
POOL_WINDOWS = (2, 4, 8, 16)
POOL_HALO = 16
SSM_GROUP = 16
SSM_STATE = 64
SSM_SLAB_GROUPS = LANES // SSM_GROUP
SSM_SLAB_STATE = SSM_SLAB_GROUPS * SSM_STATE
RMS_EPS = 1e-6
ADAM_LR, ADAM_B1, ADAM_B2, ADAM_EPS, ADAM_WD, ADAM_STEP = 0.001, 0.9, 0.999, 1e-08, 0.01, 10
ATT_TQ = 256
N_CHIPS = 4
N_DEV = 8


def _cparams(sem=None, **kw):
    return pltpu.CompilerParams(dimension_semantics=sem, vmem_limit_bytes=VMEM_LIMIT, **kw)


def _tile(n, target, mult):
    if n <= target:
        return n
    t = (target // mult) * mult
    while t > mult and n % t:
        t -= mult
    assert n % t == 0, (n, target, mult)
    return t


def _mm(a, b, *, ta=False, tb=False, extras=(), epilogue=None, out_dtypes=(F32,), name,
        tm=1024, tn=1024, tk=2048):
    M, K = (a.shape[1], a.shape[0]) if ta else a.shape
    N = b.shape[0] if tb else b.shape[1]
    assert (b.shape[1] if tb else b.shape[0]) == K, (a.shape, b.shape, ta, tb)
    tm, tn, tk = _tile(M, tm, LANES), _tile(N, tn, LANES), _tile(K, tk, LANES)
    nk = K // tk
    n_ex, n_out = len(extras), len(out_dtypes)
    dn = (((0 if ta else 1,), (1 if tb else 0,)), ((), ()))

    def body(*refs):
        a_ref, b_ref = refs[0], refs[1]
        ex_refs = refs[2:2 + n_ex]
        o_refs = refs[2 + n_ex:2 + n_ex + n_out]
        p = lax.dot_general(a_ref[...], b_ref[...], dn, preferred_element_type=F32)

        def finish(acc):
            outs = (acc,) if epilogue is None else epilogue(acc, *[r[...] for r in ex_refs])
            for o_ref, o in zip(o_refs, outs):
                o_ref[...] = o.astype(o_ref.dtype)

        if nk == 1:
            finish(p)
        else:
            acc_ref = refs[-1]
            k = pl.program_id(2)

            @pl.when(k == 0)
            def _():
                acc_ref[...] = p

            @pl.when(k > 0)
            def _():
                acc_ref[...] += p

            @pl.when(k == nk - 1)
            def _():
                finish(acc_ref[...])

    a_spec = pl.BlockSpec((tk, tm), lambda i, j, k: (k, i)) if ta else pl.BlockSpec((tm, tk), lambda i, j, k: (i, k))
    b_spec = pl.BlockSpec((tn, tk), lambda i, j, k: (j, k)) if tb else pl.BlockSpec((tk, tn), lambda i, j, k: (k, j))
    mn_spec = pl.BlockSpec((tm, tn), lambda i, j, k: (i, j))
    outs = pl.pallas_call(
        body, name=name, grid=(M // tm, N // tn, nk),
        in_specs=[a_spec, b_spec] + [mn_spec] * n_ex,
        out_specs=[mn_spec] * n_out,
        out_shape=[jax.ShapeDtypeStruct((M, N), d) for d in out_dtypes],
        scratch_shapes=[pltpu.VMEM((tm, tn), F32)] if nk > 1 else [],
        compiler_params=_cparams(("parallel", "parallel", "arbitrary")),
    )(a, b, *extras)
    return outs[0] if n_out == 1 else tuple(outs)


def _row_call(body, ins, outs, *, name, tr, n_rows, acc_outs=(), scratch=(), halo=None):
    nb = n_rows // tr
    hb = halo or SUBLANES
    per = tr // hb
    last = n_rows // hb - 1

    def spec(arr_shape, kind):
        if kind == 'rows':
            return pl.BlockSpec((tr,) + tuple(arr_shape[1:]), lambda i: (i,) + (0,) * (len(arr_shape) - 1))
        if kind == 'full' or kind == 'acc':
            return pl.BlockSpec(tuple(arr_shape), lambda i: (0,) * len(arr_shape))
        tag, w, j = kind
        if tag == 'cols':
            return pl.BlockSpec((tr, w), lambda i: (i, j))
        if tag == 'prev':
            return pl.BlockSpec((hb, w), lambda i: (jnp.maximum(i * per - 1, 0), j))
        if tag == 'next':
            return pl.BlockSpec((hb, w), lambda i: (jnp.minimum((i + 1) * per, last), j))
        raise ValueError(kind)

    return pl.pallas_call(
        body, name=name, grid=(nb,),
        in_specs=[spec(a.shape, k) for a, k in ins],
        out_specs=[spec(s, k) for s, _, k in outs],
        out_shape=[jax.ShapeDtypeStruct(s, d) for s, d, _ in outs],
        scratch_shapes=list(scratch),
        compiler_params=_cparams(("arbitrary",)),
    )(*[a for a, _ in ins])


def _rms_fwd(x, gain, *, name, with_f32=False):
    S, D = x.shape
    tr = _tile(S, 512, SUBLANES)

    def body(x_ref, g_ref, *o_refs):
        xv = x_ref[...]
        r = lax.rsqrt(jnp.mean(xv * xv, axis=-1, keepdims=True) + RMS_EPS)
        h = xv * r * g_ref[...]
        o_refs[0][...] = h.astype(BF16)
        if with_f32:
            o_refs[1][...] = h

    outs = [((S, D), BF16, 'rows')] + ([((S, D), F32, 'rows')] if with_f32 else [])
    res = _row_call(body, [(x, 'rows'), (gain, 'full')], outs, name=name, tr=tr, n_rows=S)
    return tuple(res) if with_f32 else res[0]


def _rms_bwd(x, gain, dh, dres, *, name):
    S, D = x.shape
    tr = _tile(S, 256, SUBLANES)

    def body(x_ref, g_ref, dh_ref, dr_ref, dx_ref, dxb_ref, dg_ref):
        i = pl.program_id(0)
        xv = x_ref[...]
        r = lax.rsqrt(jnp.mean(xv * xv, axis=-1, keepdims=True) + RMS_EPS)
        xn = xv * r
        dhv = dh_ref[...]
        dxn = dhv * g_ref[...]
        dx = r * (dxn - xn * jnp.mean(dxn * xn, axis=-1, keepdims=True)) + dr_ref[...]
        dx_ref[...] = dx
        dxb_ref[...] = dx.astype(BF16)
        part = jnp.sum(dhv * xn, axis=0, keepdims=True)

        @pl.when(i == 0)
        def _():
            dg_ref[...] = part

        @pl.when(i > 0)
        def _():
            dg_ref[...] += part

    return _row_call(body, [(x, 'rows'), (gain, 'full'), (dh, 'rows'), (dres, 'rows')],
                     [((S, D), F32, 'rows'), ((S, D), BF16, 'rows'), ((1, D), F32, 'acc')],
                     name=name, tr=tr, n_rows=S)


def _loss_head(y, target, *, name):
    S, D = y.shape
    tr = _tile(S, 512, SUBLANES)

    def body(y_ref, t_ref, d_ref, db_ref, l_ref):
        i = pl.program_id(0)
        e = y_ref[...] - t_ref[...]
        d = e * (1.0 / D)
        d_ref[...] = d
        db_ref[...] = d.astype(BF16)
        part = jnp.sum(e * e, axis=0, keepdims=True)

        @pl.when(i == 0)
        def _():
            l_ref[...] = part

        @pl.when(i > 0)
        def _():
            l_ref[...] += part

    return _row_call(body, [(y, 'rows'), (target, 'rows')],
                     [((S, D), F32, 'rows'), ((S, D), BF16, 'rows'), ((1, D), F32, 'acc')],
                     name=name, tr=tr, n_rows=S)


def _relu2_epilogue(acc):
    r = jnp.maximum(acc, 0.0)
    return r, r * r


def _mlp_fwd(x, gain, w1, w2, *, tag):
    h = _rms_fwd(x, gain, name=f"mlp_norm_{tag}")
    r, act = _mm(h, w1, epilogue=_relu2_epilogue, out_dtypes=(BF16, BF16), name=f"mlp_up_{tag}")
    y = _mm(act, w2, extras=(x,), epilogue=lambda acc, res: (acc + res,), name=f"mlp_down_{tag}")
    return y, (h, r, act)


def _mlp_bwd(x, gain, w1, w2, saved, dy, dyb, *, tag):
    h, r, act = saved
    da = _mm(dyb, w2, tb=True, extras=(r,), epilogue=lambda acc, rr: (acc * (2.0 * rr.astype(F32)),),
             out_dtypes=(BF16,), name=f"mlp_dact_{tag}")
    dw2 = _mm(act, dyb, ta=True, out_dtypes=(BF16,), name=f"mlp_dw2_{tag}")
    dw1 = _mm(h, da, ta=True, out_dtypes=(BF16,), name=f"mlp_dw1_{tag}")
    dh = _mm(da, w1, tb=True, name=f"mlp_dh_{tag}")
    dx, dxb, dgain = _rms_bwd(x, gain, dh, dy, name=f"mlp_dnorm_{tag}")
    return dx, dxb, dgain, dw1, dw2


def _conv_gate_fwd(z, conv_w):
    S, D3 = z.shape
    D = D3 // 3
    tr = _tile(S, 256, SUBLANES)

    def body(b_ref, c_ref, v_ref, cp_ref, vp_ref, w_ref, g_ref, scr):
        i = pl.program_id(0)
        u = c_ref[...] * v_ref[...]
        scr[0:SUBLANES, :] = cp_ref[...] * vp_ref[...] * (i > 0).astype(F32)
        scr[SUBLANES:, :] = u
        conv = (w_ref[0:1, :] * scr[pl.ds(SUBLANES - 2, tr), :] + w_ref[1:2, :] * scr[pl.ds(SUBLANES - 1, tr), :]
                + w_ref[2:3, :] * u)
        g_ref[...] = (b_ref[...] * conv).astype(BF16)

    ins = [(z, ('cols', D, 0)), (z, ('cols', D, 1)), (z, ('cols', D, 2)), (z, ('prev', D, 1)), (z, ('prev', D, 2)),
           (conv_w, 'full')]
    return _row_call(body, ins, [((S, D), BF16, 'rows')], name="conv_gate_fwd", tr=tr, n_rows=S,
                     scratch=[pltpu.VMEM((tr + SUBLANES, D), F32)])[0]


def _conv_gate_bwd(z, conv_w, dg):
    S, D3 = z.shape
    D = D3 // 3
    tr = _tile(S, 128, SUBLANES)
    nb = S // tr

    def body(b_ref, c_ref, v_ref, cp_ref, vp_ref, bn_ref, dg_ref, dgn_ref, w_ref, dz_ref, dw_ref, scr, scr2):
        i = pl.program_id(0)
        c, v, b, dgv = c_ref[...], v_ref[...], b_ref[...], dg_ref[...]
        u = c * v
        scr[0:SUBLANES, :] = cp_ref[...] * vp_ref[...] * (i > 0).astype(F32)
        scr[SUBLANES:, :] = u
        u1 = scr[pl.ds(SUBLANES - 1, tr), :]
        u2 = scr[pl.ds(SUBLANES - 2, tr), :]
        conv = w_ref[0:1, :] * u2 + w_ref[1:2, :] * u1 + w_ref[2:3, :] * u
        dconv = dgv * b
        scr2[0:tr, :] = dconv
        scr2[tr:, :] = dgn_ref[...] * bn_ref[...] * (i < nb - 1).astype(F32)
        du = (w_ref[2:3, :] * dconv + w_ref[1:2, :] * scr2[pl.ds(1, tr), :] + w_ref[0:1, :] * scr2[pl.ds(2, tr), :])
        dz_ref[:, 0:D] = (dgv * conv).astype(BF16)
        dz_ref[:, D:2 * D] = (du * v).astype(BF16)
        dz_ref[:, 2 * D:] = (du * c).astype(BF16)
        parts = [jnp.sum(dconv * t, axis=0, keepdims=True) for t in (u2, u1, u)]

        @pl.when(i == 0)
        def _():
            for k in range(3):
                dw_ref[k:k + 1, :] = parts[k]

        @pl.when(i > 0)
        def _():
            for k in range(3):
                dw_ref[k:k + 1, :] += parts[k]

    ins = [(z, ('cols', D, 0)), (z, ('cols', D, 1)), (z, ('cols', D, 2)), (z, ('prev', D, 1)), (z, ('prev', D, 2)),
           (z, ('next', D, 0)), (dg, 'rows'), (dg, ('next', D, 0)), (conv_w, 'full')]
    return _row_call(body, ins, [((S, D3), BF16, 'rows'), ((3, D), F32, 'acc')], name="conv_gate_bwd", tr=tr,
                     n_rows=S, scratch=[pltpu.VMEM((tr + SUBLANES, D), F32), pltpu.VMEM((tr + SUBLANES, D), F32)])


def _conv_mixer_fwd(x, gain, w_in, conv_w, w_out):
    h = _rms_fwd(x, gain, name="conv_norm")
    z = _mm(h, w_in, name="conv_in")
    g = _conv_gate_fwd(z, conv_w)
    y = _mm(g, w_out, extras=(x,), epilogue=lambda acc, res: (acc + res,), name="conv_out")
    return y, (h, z, g)


def _conv_mixer_bwd(x, gain, w_in, conv_w, w_out, saved, dy, dyb):
    h, z, g = saved
    dg = _mm(dyb, w_out, tb=True, name="conv_dg")
    dw_out = _mm(g, dyb, ta=True, out_dtypes=(BF16,), name="conv_dwout")
    dz, dconv_w = _conv_gate_bwd(z, conv_w, dg)
    dh = _mm(dz, w_in, tb=True, name="conv_dh")
    dw_in = _mm(h, dz, ta=True, out_dtypes=(BF16,), name="conv_dwin")
    dx, dxb, dgain = _rms_bwd(x, gain, dh, dy, name="conv_dnorm")
    return dx, dxb, dgain, dw_in, dconv_w, dw_out


def _pool_fwd(u):
    S, D = u.shape
    G = D // len(POOL_WINDOWS)
    tr = _tile(S, 256, SUBLANES)
    H = POOL_HALO

    def body(u_ref, up_ref, p_ref, scr):
        i = pl.program_id(0)
        uv = u_ref[...]
        scr[0:H, :] = up_ref[...] * (i > 0).astype(F32)
        scr[H:, :] = uv
        t = (lax.broadcasted_iota(jnp.int32, (tr, 1), 0) + i * tr + 1).astype(F32)
        for gi, w in enumerate(POOL_WINDOWS):
            cols = slice(gi * G, (gi + 1) * G)
            acc = uv[:, cols]
            for j in range(1, w):
                acc = acc + scr[pl.ds(H - j, tr), cols]
            p_ref[:, cols] = (acc / jnp.minimum(t, float(w)) - uv[:, cols]).astype(BF16)

    return _row_call(body, [(u, 'rows'), (u, ('prev', D, 0))], [((S, D), BF16, 'rows')], name="pool_fwd", tr=tr,
                     n_rows=S, halo=H, scratch=[pltpu.VMEM((tr + H, D), F32)])[0]


def _pool_bwd(dp):
    S, D = dp.shape
    G = D // len(POOL_WINDOWS)
    tr = _tile(S, 256, SUBLANES)
    H = POOL_HALO
    nb = S // tr

    def body(d_ref, dn_ref, o_ref, scr):
        i = pl.program_id(0)
        dv = d_ref[...]
        t = (lax.broadcasted_iota(jnp.int32, (tr, 1), 0) + i * tr + 1).astype(F32)
        tn = (lax.broadcasted_iota(jnp.int32, (H, 1), 0) + (i + 1) * tr + 1).astype(F32)
        for gi, w in enumerate(POOL_WINDOWS):
            cols = slice(gi * G, (gi + 1) * G)
            scr[0:tr, cols] = dv[:, cols] / jnp.minimum(t, float(w))
            scr[tr:, cols] = dn_ref[:, cols] / jnp.minimum(tn, float(w)) * (i < nb - 1).astype(F32)
        for gi, w in enumerate(POOL_WINDOWS):
            cols = slice(gi * G, (gi + 1) * G)
            acc = scr[0:tr, cols]
            for j in range(1, w):
                acc = acc + scr[pl.ds(j, tr), cols]
            o_ref[:, cols] = (acc - dv[:, cols]).astype(BF16)

    return _row_call(body, [(dp, 'rows'), (dp, ('next', D, 0))], [((S, D), BF16, 'rows')], name="pool_bwd", tr=tr,
                     n_rows=S, halo=H, scratch=[pltpu.VMEM((tr + H, D), F32)])[0]


def _pool_group_fwd(p, wg, scale, x):
    S, D = p.shape
    NG, G, _ = wg.shape
    tm = _tile(S, 1024, SUBLANES)

    def body(p_ref, w_ref, s_ref, x_ref, o_ref, y_ref):
        y = jnp.dot(p_ref[...], w_ref[0], preferred_element_type=F32)
        y_ref[...] = y
        o_ref[...] = x_ref[...] + y * s_ref[...]

    blk = pl.BlockSpec((tm, G), lambda i, g: (i, g))
    return pl.pallas_call(
        body, name="pool_group_fwd", grid=(S // tm, NG),
        in_specs=[blk, pl.BlockSpec((1, G, G), lambda i, g: (g, 0, 0)), pl.BlockSpec((1, G), lambda i, g: (0, g)), blk],
        out_specs=[blk, blk],
        out_shape=[jax.ShapeDtypeStruct((S, D), F32), jax.ShapeDtypeStruct((S, D), F32)],
        compiler_params=_cparams(("parallel", "arbitrary")),
    )(p, wg, scale, x)


def _pool_group_bwd(p, wg, scale, y, dm):
    S, D = p.shape
    NG, G, _ = wg.shape
    tm = _tile(S, 1024, SUBLANES)
    nb = S // tm

    def body(p_ref, w_ref, s_ref, y_ref, dm_ref, dp_ref, dw_ref, ds_ref, acc_ref):
        i = pl.program_id(1)
        dmv = dm_ref[...]
        dy = (dmv * s_ref[...]).astype(BF16)
        dp_ref[...] = lax.dot_general(dy, w_ref[0], (((1,), (1,)), ((), ())), preferred_element_type=F32)
        dw = lax.dot_general(p_ref[...], dy, (((0,), (0,)), ((), ())), preferred_element_type=F32)
        dsp = jnp.sum(dmv * y_ref[...], axis=0, keepdims=True)

        @pl.when(i == 0)
        def _():
            acc_ref[...] = dw
            ds_ref[...] = dsp

        @pl.when(i > 0)
        def _():
            acc_ref[...] += dw
            ds_ref[...] += dsp

        @pl.when(i == nb - 1)
        def _():
            dw_ref[0] = acc_ref[...].astype(BF16)

    blk = pl.BlockSpec((tm, G), lambda g, i: (i, g))
    wspec = pl.BlockSpec((1, G, G), lambda g, i: (g, 0, 0))
    sspec = pl.BlockSpec((1, G), lambda g, i: (0, g))
    return pl.pallas_call(
        body, name="pool_group_bwd", grid=(NG, nb),
        in_specs=[blk, wspec, sspec, blk, blk],
        out_specs=[blk, wspec, sspec],
        out_shape=[jax.ShapeDtypeStruct((S, D), F32), jax.ShapeDtypeStruct((NG, G, G), BF16),
                   jax.ShapeDtypeStruct((1, D), F32)],
        scratch_shapes=[pltpu.VMEM((G, G), F32)],
        compiler_params=_cparams(("parallel", "arbitrary")),
    )(p, wg, scale, y, dm)


def _pool_mixer_fwd(x, gain, w_in, wg, scale):
    h = _rms_fwd(x, gain, name="pool_norm")
    u = _mm(h, w_in, name="pool_in")
    p = _pool_fwd(u)
    y, yg = _pool_group_fwd(p, wg, scale, x)
    return y, (h, p, yg)


def _pool_mixer_bwd(x, gain, w_in, wg, scale, saved, dy, dyb):
    h, p, yg = saved
    dp, dwg, dscale = _pool_group_bwd(p, wg, scale, yg, dy)
    du = _pool_bwd(dp)
    dh = _mm(du, w_in, tb=True, name="pool_dh")
    dw_in = _mm(h, du, ta=True, out_dtypes=(BF16,), name="pool_dwin")
    dx, dxb, dgain = _rms_bwd(x, gain, dh, dy, name="pool_dnorm")
    return dx, dxb, dgain, dw_in, dwg, dscale


def _qk_norm_fwd(qkv, qg, kg):
    S, D3 = qkv.shape
    D = D3 // 3
    NH = D // ATT_HEAD_DIM
    tr = _tile(S, 256, SUBLANES)

    def body(q_ref, k_ref, v_ref, qg_ref, kg_ref, qo_ref, ko_ref, vo_ref):
        for src, g_ref, dst in ((q_ref, qg_ref, qo_ref), (k_ref, kg_ref, ko_ref)):
            for hd in range(NH):
                cols = slice(hd * ATT_HEAD_DIM, (hd + 1) * ATT_HEAD_DIM)
                t = src[:, cols]
                r = lax.rsqrt(jnp.mean(t * t, axis=-1, keepdims=True) + RMS_EPS)
                dst[:, cols] = (t * r * g_ref[...]).astype(BF16)
        vo_ref[...] = v_ref[...].astype(BF16)

    ins = [(qkv, ('cols', D, 0)), (qkv, ('cols', D, 1)), (qkv, ('cols', D, 2)), (qg, 'full'), (kg, 'full')]
    return _row_call(body, ins, [((S, D), BF16, 'rows')] * 3, name="att_qknorm_fwd", tr=tr, n_rows=S)


def _qk_norm_bwd(qkv, qg, kg, dqn, dkn, dv):
    S, D3 = qkv.shape
    D = D3 // 3
    NH = D // ATT_HEAD_DIM
    tr = _tile(S, 128, SUBLANES)

    def body(q_ref, k_ref, qg_ref, kg_ref, dq_ref, dk_ref, dv_ref, o_ref, dqg_ref, dkg_ref):
        i = pl.program_id(0)
        for sec, (src, g_ref, d_ref, dg_ref) in enumerate(((q_ref, qg_ref, dq_ref, dqg_ref),
                                                            (k_ref, kg_ref, dk_ref, dkg_ref))):
            part = jnp.zeros((1, ATT_HEAD_DIM), F32)
            for hd in range(NH):
                cols = slice(hd * ATT_HEAD_DIM, (hd + 1) * ATT_HEAD_DIM)
                t = src[:, cols]
                r = lax.rsqrt(jnp.mean(t * t, axis=-1, keepdims=True) + RMS_EPS)
                tn = t * r
                d = d_ref[:, cols]
                dn = d * g_ref[...]
                dt = r * (dn - tn * jnp.mean(dn * tn, axis=-1, keepdims=True))
                o_ref[:, sec * D + hd * ATT_HEAD_DIM:sec * D + (hd + 1) * ATT_HEAD_DIM] = dt.astype(BF16)
                part = part + jnp.sum(d * tn, axis=0, keepdims=True)

            @pl.when(i == 0)
            def _():
                dg_ref[...] = part

            @pl.when(i > 0)
            def _():
                dg_ref[...] += part

        o_ref[:, 2 * D:] = dv_ref[...].astype(BF16)

    ins = [(qkv, ('cols', D, 0)), (qkv, ('cols', D, 1)), (qg, 'full'), (kg, 'full'), (dqn, 'rows'), (dkn, 'rows'),
           (dv, 'rows')]
    return _row_call(body, ins, [((S, D3), BF16, 'rows'), ((1, ATT_HEAD_DIM), F32, 'acc'),
                                 ((1, ATT_HEAD_DIM), F32, 'acc')], name="att_qknorm_bwd", tr=tr, n_rows=S)


def _att_band_mask():
    r = np.arange(ATT_TQ)[:, None]
    c = np.arange(ATT_TQ + ATT_PAD)[None, :]
    lo = (r // CHUNK) * CHUNK
    return (c >= lo) & (c < lo + ATT_PAD + CHUNK)


def _att_bias_toeplitz(rel_bias):
    H = rel_bias.shape[0]
    R, C = ATT_TQ, ATT_TQ + ATT_PAD
    L = C + R - 1
    assert R - 1 < REL_CLIP
    near = rel_bias[:, REL_CLIP - (R - 1):2 * REL_CLIP][:, ::-1]
    far = jnp.broadcast_to(rel_bias[:, 2 * REL_CLIP:], (H, L - near.shape[1]))
    v = jnp.concatenate([far, near, jnp.zeros((H, 1), rel_bias.dtype)], axis=1)
    skew = jnp.broadcast_to(v[:, None, :], (H, R, L + 1)).reshape(H, R * (L + 1))[:, :R * L].reshape(H, R, L)
    return skew[:, :, R - 1:R - 1 + C]


def _att_bias_tile(rel_bias):
    return jnp.where(_att_band_mask()[None], _att_bias_toeplitz(rel_bias), MASK_VALUE).astype(F32)


def _att_bias_grad(dtile, rel_bias):
    _, pull = jax.vjp(_att_bias_toeplitz, rel_bias)
    return pull(jnp.where(_att_band_mask()[None], dtile, 0.0))[0]


def _att_core_fwd(qn, kp, vp, bias):
    S, D = qn.shape
    NH = D // ATT_HEAD_DIM
    KW = ATT_TQ + ATT_PAD
    scale = ATT_HEAD_DIM ** -0.5

    def body(q_ref, k_ref, v_ref, b_ref, o_ref):
        qb = pl.program_id(1)
        start = pl.multiple_of(qb * ATT_TQ, ATT_TQ)
        ks = k_ref[pl.ds(start, KW), :]
        vs = v_ref[pl.ds(start, KW), :]
        s = lax.dot_general(q_ref[...], ks, (((1,), (1,)), ((), ())), preferred_element_type=F32) * scale + b_ref[0]
        kpos = lax.broadcasted_iota(jnp.int32, (1, KW), 1) + (qb * ATT_TQ - ATT_PAD)
        s = jnp.where(kpos >= 0, s, MASK_VALUE)
        m = jnp.max(s, axis=-1, keepdims=True)
        p = jnp.exp(s - m)
        l = jnp.sum(p, axis=-1, keepdims=True)
        o = jnp.dot(p.astype(BF16), vs, preferred_element_type=F32) / l
        o_ref[...] = o.astype(BF16)

    qspec = pl.BlockSpec((ATT_TQ, ATT_HEAD_DIM), lambda h, qb: (qb, h))
    kvspec = pl.BlockSpec((S + ATT_PAD, ATT_HEAD_DIM), lambda h, qb: (0, h))
    return pl.pallas_call(
        body, name="att_core_fwd", grid=(NH, S // ATT_TQ),
        in_specs=[qspec, kvspec, kvspec, pl.BlockSpec((1, ATT_TQ, KW), lambda h, qb: (h, 0, 0))],
        out_specs=qspec, out_shape=jax.ShapeDtypeStruct((S, D), BF16),
        compiler_params=_cparams(("parallel", "arbitrary")),
    )(qn, kp, vp, bias)


def _att_core_bwd(qn, kp, vp, bias, do):
    S, D = qn.shape
    NH = D // ATT_HEAD_DIM
    KW = ATT_TQ + ATT_PAD
    scale = ATT_HEAD_DIM ** -0.5

    def body(q_ref, k_ref, v_ref, b_ref, do_ref, dq_ref, dk_ref, dv_ref, db_ref):
        qb = pl.program_id(1)
        start = pl.multiple_of(qb * ATT_TQ, ATT_TQ)
        q = q_ref[...]
        dov = do_ref[...]
        ks = k_ref[pl.ds(start, KW), :]
        vs = v_ref[pl.ds(start, KW), :]
        s = lax.dot_general(q, ks, (((1,), (1,)), ((), ())), preferred_element_type=F32) * scale + b_ref[0]
        kpos = lax.broadcasted_iota(jnp.int32, (1, KW), 1) + (qb * ATT_TQ - ATT_PAD)
        s = jnp.where(kpos >= 0, s, MASK_VALUE)
        m = jnp.max(s, axis=-1, keepdims=True)
        e = jnp.exp(s - m)
        p = e / jnp.sum(e, axis=-1, keepdims=True)
        dp = lax.dot_general(dov, vs, (((1,), (1,)), ((), ())), preferred_element_type=F32)
        ds = p * (dp - jnp.sum(p * dp, axis=-1, keepdims=True))
        dsb = ds.astype(BF16)
        dq_ref[...] = jnp.dot(dsb, ks, preferred_element_type=F32) * scale
        dk = lax.dot_general(dsb, q, (((0,), (0,)), ((), ())), preferred_element_type=F32) * scale
        dv = lax.dot_general(p.astype(BF16), dov, (((0,), (0,)), ((), ())), preferred_element_type=F32)

        @pl.when(qb == 0)
        def _():
            dk_ref[...] = jnp.zeros_like(dk_ref)
            dv_ref[...] = jnp.zeros_like(dv_ref)
            db_ref[0] = ds

        @pl.when(qb > 0)
        def _():
            db_ref[0] += ds

        dk_ref[pl.ds(start, KW), :] += dk
        dv_ref[pl.ds(start, KW), :] += dv

    qspec = pl.BlockSpec((ATT_TQ, ATT_HEAD_DIM), lambda h, qb: (qb, h))
    kvspec = pl.BlockSpec((S + ATT_PAD, ATT_HEAD_DIM), lambda h, qb: (0, h))
    bspec = pl.BlockSpec((1, ATT_TQ, KW), lambda h, qb: (h, 0, 0))
    return pl.pallas_call(
        body, name="att_core_bwd", grid=(NH, S // ATT_TQ),
        in_specs=[qspec, kvspec, kvspec, bspec, qspec],
        out_specs=[qspec, kvspec, kvspec, bspec],
        out_shape=[jax.ShapeDtypeStruct((S, D), F32), jax.ShapeDtypeStruct((S + ATT_PAD, D), F32),
                   jax.ShapeDtypeStruct((S + ATT_PAD, D), F32), jax.ShapeDtypeStruct((NH, ATT_TQ, KW), F32)],
        compiler_params=_cparams(("parallel", "arbitrary")),
    )(qn, kp, vp, bias, do)


def _att_mixer_fwd(x, gain, w_qkv, qg, kg, rel_bias, w_out):
    h = _rms_fwd(x, gain, name="att_norm")
    qkv = _mm(h, w_qkv, name="att_qkv")
    qn, kn, v = _qk_norm_fwd(qkv, qg, kg)
    kp = jnp.pad(kn, ((ATT_PAD, 0), (0, 0)))
    vp = jnp.pad(v, ((ATT_PAD, 0), (0, 0)))
    bias = _att_bias_tile(rel_bias)
    o = _att_core_fwd(qn, kp, vp, bias)
    y = _mm(o, w_out, extras=(x,), epilogue=lambda acc, res: (acc + res,), name="att_out")
    return y, (h, qkv, qn, kp, vp, bias, o, rel_bias)


def _att_mixer_bwd(x, gain, w_qkv, qg, kg, w_out, saved, dy, dyb):
    h, qkv, qn, kp, vp, bias, o, rel_bias = saved
    do = _mm(dyb, w_out, tb=True, out_dtypes=(BF16,), name="att_do")
    dw_out = _mm(o, dyb, ta=True, out_dtypes=(BF16,), name="att_dwout")
    dqn, dkp, dvp, dbt = _att_core_bwd(qn, kp, vp, bias, do)
    drel = _att_bias_grad(dbt, rel_bias)
    dqkv, dqg, dkg = _qk_norm_bwd(qkv, qg, kg, dqn, dkp[ATT_PAD:], dvp[ATT_PAD:])
    dh = _mm(dqkv, w_qkv, tb=True, name="att_dh")
    dw_qkv = _mm(h, dqkv, ta=True, out_dtypes=(BF16,), name="att_dwqkv")
    dx, dxb, dgain = _rms_bwd(x, gain, dh, dy, name="att_dnorm")
    return dx, dxb, dgain, dw_qkv, dqg, dkg, drel, dw_out


def _ssm_tables(a_re, a_im, log_dt, b_re, b_im, c_re, c_im):
    G, N = a_re.shape
    NS = G // SSM_SLAB_GROUPS
    lam = lax.complex(a_re, a_im)
    dt = jnp.exp(log_dt)[:, None]
    abar = jnp.exp(lam * dt)
    coef = (abar - 1.0) / lam
    bbar = coef[..., None] * lax.complex(b_re, b_im)
    eye = jnp.eye(SSM_SLAB_GROUPS, dtype=F32)

    def blockdiag(t):
        P, Q = t.shape[1:]
        t = t.reshape(NS, SSM_SLAB_GROUPS, P, Q)
        return jnp.einsum('sgpq,gh->sgphq', t, eye).reshape(NS, SSM_SLAB_GROUPS * P, SSM_SLAB_GROUPS * Q)

    bt = jnp.swapaxes(bbar, 1, 2)
    bmat = jnp.concatenate([blockdiag(jnp.real(bt)), blockdiag(jnp.imag(bt))], axis=2)
    ct = jnp.swapaxes(lax.complex(c_re, c_im), 1, 2)
    cmat = jnp.concatenate([blockdiag(jnp.real(ct)), -blockdiag(jnp.imag(ct))], axis=1)
    al = abar.reshape(NS, 1, SSM_SLAB_STATE)
    rows = jnp.arange(SUBLANES)[None, :, None]
    fwd, bwd = [], []
    for k in (1, 2, 4):
        ak = al ** k
        f = jnp.where(rows >= k, ak, 0.0)
        b = jnp.where(rows < SUBLANES - k, ak, 0.0)
        fwd += [jnp.real(f), jnp.imag(f)]
        bwd += [jnp.real(b), jnp.imag(b)]
    pf = al ** (rows + 1)
    pb = al ** (SUBLANES - rows)
    fwd += [jnp.real(pf), jnp.imag(pf)]
    bwd += [jnp.real(pb), jnp.imag(pb)]
    coef_f = jnp.concatenate(fwd, axis=1).astype(F32)
    coef_b = jnp.concatenate(bwd, axis=1).astype(F32)
    return dict(lam=lam, dt=dt, abar=abar, coef=coef, bmat=bmat.astype(BF16), cmat=cmat.astype(BF16),
                bmat_t=jnp.swapaxes(bmat, 1, 2).astype(BF16), cmat_t=jnp.swapaxes(cmat, 1, 2).astype(BF16),
                coef_f=coef_f, coef_b=coef_b)


def _ssm_scan_fwd(u, tabs, d_skip):
    S, D = u.shape
    NS = D // LANES
    W = 2 * SSM_SLAB_STATE
    T = _tile(S, 512, SUBLANES)
    HS = SSM_SLAB_STATE

    def body(u_ref, bm_ref, cm_ref, cf_ref, d_ref, y_ref, xs_ref, bu_scr, carry_scr):
        i = pl.program_id(1)

        @pl.when(i == 0)
        def _():
            carry_scr[...] = jnp.zeros_like(carry_scr)

        uv = u_ref[...]
        bu_scr[...] = jnp.dot(uv.astype(BF16), bm_ref[0], preferred_element_type=F32)

        def step(r, carry):
            rows = pl.ds(pl.multiple_of(r * SUBLANES, SUBLANES), SUBLANES)
            xr = bu_scr[rows, 0:HS]
            xi = bu_scr[rows, HS:W]
            for n, k in enumerate((1, 2, 4)):
                ar = cf_ref[0, 16 * n:16 * n + 8, :]
                ai = cf_ref[0, 16 * n + 8:16 * n + 16, :]
                sr = pltpu.roll(xr, k, 0)
                si = pltpu.roll(xi, k, 0)
                xr, xi = xr + ar * sr - ai * si, xi + ar * si + ai * sr
            pr = cf_ref[0, 48:56, :]
            pi_ = cf_ref[0, 56:64, :]
            cr, ci = carry
            xr, xi = xr + pr * cr - pi_ * ci, xi + pr * ci + pi_ * cr
            xs_ref[rows, 0:HS] = xr
            xs_ref[rows, HS:W] = xi
            return xr[SUBLANES - 1:SUBLANES, :], xi[SUBLANES - 1:SUBLANES, :]

        cr, ci = lax.fori_loop(0, T // SUBLANES, step, (carry_scr[0:1, 0:HS], carry_scr[0:1, HS:W]))
        carry_scr[0:1, 0:HS] = cr
        carry_scr[0:1, HS:W] = ci
        y_ref[...] = jnp.dot(xs_ref[...].astype(BF16), cm_ref[0], preferred_element_type=F32) + d_ref[...] * uv

    return pl.pallas_call(
        body, name="ssm_scan_fwd", grid=(NS, S // T),
        in_specs=[pl.BlockSpec((T, LANES), lambda j, i: (i, j)),
                  pl.BlockSpec((1, LANES, W), lambda j, i: (j, 0, 0)),
                  pl.BlockSpec((1, W, LANES), lambda j, i: (j, 0, 0)),
                  pl.BlockSpec((1, 8 * SUBLANES, HS), lambda j, i: (j, 0, 0)),
                  pl.BlockSpec((1, LANES), lambda j, i: (0, j))],
        out_specs=[pl.BlockSpec((T, LANES), lambda j, i: (i, j)), pl.BlockSpec((T, W), lambda j, i: (i, j))],
        out_shape=[jax.ShapeDtypeStruct((S, D), F32), jax.ShapeDtypeStruct((S, NS * W), F32)],
        scratch_shapes=[pltpu.VMEM((T, W), F32), pltpu.VMEM((SUBLANES, W), F32)],
        compiler_params=_cparams(("parallel", "arbitrary")),
    )(u, tabs['bmat'], tabs['cmat'], tabs['coef_f'], d_skip)


def _ssm_scan_bwd(u, xs, dy, tabs, d_skip):
    S, D = u.shape
    NS = D // LANES
    W = 2 * SSM_SLAB_STATE
    T = _tile(S, 512, SUBLANES)
    HS = SSM_SLAB_STATE
    nb = S // T

    def body(u_ref, xs_ref, dy_ref, bt_ref, ct_ref, cf_ref, d_ref, du_ref, gb_ref, gc_ref, q_ref,
             cy_scr, lam_scr, carry_scr):
        i = pl.program_id(1)

        @pl.when(i == 0)
        def _():
            carry_scr[...] = jnp.zeros_like(carry_scr)
            gb_ref[...] = jnp.zeros_like(gb_ref)
            gc_ref[...] = jnp.zeros_like(gc_ref)
            q_ref[...] = jnp.zeros_like(q_ref)

        dyv = dy_ref[...]
        dyb = dyv.astype(BF16)
        cy_scr[...] = jnp.dot(dyb, ct_ref[0], preferred_element_type=F32)

        def step(n, carry):
            r = T // SUBLANES - 1 - n
            rows = pl.ds(pl.multiple_of(r * SUBLANES, SUBLANES), SUBLANES)
            cyr = cy_scr[rows, 0:HS]
            cyi = cy_scr[rows, HS:W]
            lr, li = cyr, cyi
            for m, k in enumerate((1, 2, 4)):
                br = cf_ref[0, 16 * m:16 * m + 8, :]
                bi = cf_ref[0, 16 * m + 8:16 * m + 16, :]
                sr = pltpu.roll(lr, SUBLANES - k, 0)
                si = pltpu.roll(li, SUBLANES - k, 0)
                lr, li = lr + br * sr + bi * si, li + br * si - bi * sr
            pr = cf_ref[0, 48:56, :]
            pi_ = cf_ref[0, 56:64, :]
            cr, ci, qr, qi = carry
            lr, li = lr + pr * cr + pi_ * ci, li + pr * ci - pi_ * cr
            lam_scr[rows, 0:HS] = lr
            lam_scr[rows, HS:W] = li
            mr, mi = lr - cyr, li - cyi
            xr = xs_ref[rows, 0:HS]
            xi = xs_ref[rows, HS:W]
            return lr[0:1, :], li[0:1, :], qr + mr * xr + mi * xi, qi + mi * xr - mr * xi

        zero = jnp.zeros((SUBLANES, HS), F32)
        cr, ci, qr, qi = lax.fori_loop(0, T // SUBLANES, step,
                                       (carry_scr[0:1, 0:HS], carry_scr[0:1, HS:W], zero, zero))
        carry_scr[0:1, 0:HS] = cr
        carry_scr[0:1, HS:W] = ci
        q_ref[0, :, 0:HS] += qr
        q_ref[0, :, HS:W] += qi
        lamb = lam_scr[...].astype(BF16)
        uv = u_ref[...]
        du_ref[...] = jnp.dot(lamb, bt_ref[0], preferred_element_type=F32) + d_ref[...] * dyv
        gb_ref[0] += lax.dot_general(lamb, uv.astype(BF16), (((0,), (0,)), ((), ())), preferred_element_type=F32)
        gc_ref[0] += lax.dot_general(xs_ref[...].astype(BF16), dyb, (((0,), (0,)), ((), ())),
                                     preferred_element_type=F32)

    rev = lambda j, i: (nb - 1 - i, j)
    slab3 = lambda j, i: (j, 0, 0)
    return pl.pallas_call(
        body, name="ssm_scan_bwd", grid=(NS, nb),
        in_specs=[pl.BlockSpec((T, LANES), rev), pl.BlockSpec((T, W), rev), pl.BlockSpec((T, LANES), rev),
                  pl.BlockSpec((1, W, LANES), slab3), pl.BlockSpec((1, LANES, W), slab3),
                  pl.BlockSpec((1, 8 * SUBLANES, HS), slab3), pl.BlockSpec((1, LANES), lambda j, i: (0, j))],
        out_specs=[pl.BlockSpec((T, LANES), rev), pl.BlockSpec((1, W, LANES), slab3),
                   pl.BlockSpec((1, W, LANES), slab3), pl.BlockSpec((1, SUBLANES, W), slab3)],
        out_shape=[jax.ShapeDtypeStruct((S, D), F32), jax.ShapeDtypeStruct((NS, W, LANES), F32),
                   jax.ShapeDtypeStruct((NS, W, LANES), F32), jax.ShapeDtypeStruct((NS, SUBLANES, W), F32)],
        scratch_shapes=[pltpu.VMEM((T, W), F32), pltpu.VMEM((T, W), F32), pltpu.VMEM((SUBLANES, W), F32)],
        compiler_params=_cparams(("parallel", "arbitrary")),
    )(u, xs, dy, tabs['bmat_t'], tabs['cmat_t'], tabs['coef_b'], d_skip)


def _ssm_param_grads(tabs, b_re, b_im, gb, gc, q):
    NS = gb.shape[0]
    G = NS * SSM_SLAB_GROUPS
    N, C = SSM_STATE, SSM_GROUP

    def diag_blocks(t):
        t = t.reshape(NS, SSM_SLAB_GROUPS, N, SSM_SLAB_GROUPS, C)
        t = jnp.einsum('sgnhc,gh->sgnc', t, jnp.eye(SSM_SLAB_GROUPS, dtype=F32))
        return t.reshape(G, N, C)

    HS = SSM_SLAB_STATE
    g_bbar = lax.complex(diag_blocks(gb[:, :HS]), diag_blocks(gb[:, HS:]))
    g_c = lax.complex(diag_blocks(gc[:, :HS]), -diag_blocks(gc[:, HS:]))
    qs = jnp.sum(q, axis=1)
    qc = lax.complex(qs[:, :HS], qs[:, HS:]).reshape(G, N)
    lam, dt, abar, coef = tabs['lam'], tabs['dt'], tabs['abar'], tabs['coef']
    bmat = lax.complex(b_re, b_im)
    g_b = g_bbar * jnp.conj(coef)[..., None]
    g_coef = jnp.sum(g_bbar * jnp.conj(bmat), axis=-1)
    g_abar_coef = g_coef * jnp.conj(1.0 / lam)
    g_lam = g_coef * jnp.conj(-(abar - 1.0) / (lam * lam))
    g_ld = qc + jnp.conj(abar) * g_abar_coef
    g_lam = g_lam + g_ld * dt
    g_dt = jnp.sum(jnp.real(g_ld * jnp.conj(lam)), axis=-1)
    g_logdt = g_dt * dt[:, 0]
    g_ct = jnp.swapaxes(g_c, 1, 2)
    return (jnp.real(g_lam), jnp.imag(g_lam), g_logdt, jnp.real(g_b), jnp.imag(g_b), jnp.real(g_ct), jnp.imag(g_ct))


_GELU_C = math.sqrt(2.0 / math.pi)


def _gelu_fwd(y):
    S, D = y.shape

    def body(y_ref, z_ref):
        v = y_ref[...]
        z_ref[...] = (0.5 * v * (1.0 + jnp.tanh(_GELU_C * (v + 0.044715 * v * v * v)))).astype(BF16)

    return _row_call(body, [(y, 'rows')], [((S, D), BF16, 'rows')], name="ssm_gelu_fwd",
                     tr=_tile(S, 512, SUBLANES), n_rows=S)[0]


def _gelu_bwd(y, dz, u):
    S, D = y.shape

    def body(y_ref, dz_ref, u_ref, dy_ref, dd_ref):
        i = pl.program_id(0)
        v = y_ref[...]
        t = jnp.tanh(_GELU_C * (v + 0.044715 * v * v * v))
        g = 0.5 * (1.0 + t) + 0.5 * v * (1.0 - t * t) * _GELU_C * (1.0 + 3 * 0.044715 * v * v)
        dy = dz_ref[...] * g
        dy_ref[...] = dy
        part = jnp.sum(dy * u_ref[...], axis=0, keepdims=True)

        @pl.when(i == 0)
        def _():
            dd_ref[...] = part

        @pl.when(i > 0)
        def _():
            dd_ref[...] += part

    return _row_call(body, [(y, 'rows'), (dz, 'rows'), (u, 'rows')], [((S, D), F32, 'rows'), ((1, D), F32, 'acc')],
                     name="ssm_gelu_bwd", tr=_tile(S, 256, SUBLANES), n_rows=S)


def _glu_fwd(zz, x):
    S, D = x.shape

    def body(a_ref, g_ref, x_ref, o_ref):
        o_ref[...] = x_ref[...] + a_ref[...] * jax.nn.sigmoid(g_ref[...])

    return _row_call(body, [(zz, ('cols', D, 0)), (zz, ('cols', D, 1)), (x, 'rows')], [((S, D), F32, 'rows')],
                     name="ssm_glu_fwd", tr=_tile(S, 256, SUBLANES), n_rows=S)[0]


def _glu_bwd(zz, dm):
    S, D = dm.shape

    def body(a_ref, g_ref, dm_ref, o_ref):
        s = jax.nn.sigmoid(g_ref[...])
        d = dm_ref[...]
        o_ref[:, 0:D] = (d * s).astype(BF16)
        o_ref[:, D:] = (d * a_ref[...] * s * (1.0 - s)).astype(BF16)

    return _row_call(body, [(zz, ('cols', D, 0)), (zz, ('cols', D, 1)), (dm, 'rows')], [((S, 2 * D), BF16, 'rows')],
                     name="ssm_glu_bwd", tr=_tile(S, 256, SUBLANES), n_rows=S)[0]


def _ssm_mixer_fwd(x, gain, tabs, d_skip, w_glu):
    _, u = _rms_fwd(x, gain, name="ssm_norm", with_f32=True)
    yv, xs = _ssm_scan_fwd(u, tabs, d_skip)
    z = _gelu_fwd(yv)
    zz = _mm(z, w_glu, name="ssm_glu_in")
    y = _glu_fwd(zz, x)
    return y, (u, xs, yv, z, zz)


def _ssm_mixer_bwd(x, gain, tabs, d_skip, w_glu, b_re, b_im, saved, dy, dyb):
    u, xs, yv, z, zz = saved
    dzz = _glu_bwd(zz, dy)
    dz = _mm(dzz, w_glu, tb=True, name="ssm_dz")
    dw_glu = _mm(z, dzz, ta=True, out_dtypes=(BF16,), name="ssm_dwglu")
    dyv, dd = _gelu_bwd(yv, dz, u)
    du, gb, gc, q = _ssm_scan_bwd(u, xs, dyv, tabs, d_skip)
    small = _ssm_param_grads(tabs, b_re, b_im, gb, gc, q)
    dx, dxb, dgain = _rms_bwd(x, gain, du, dy, name="ssm_dnorm")
    return dx, dxb, dgain, small, dd, dw_glu


def _local_step(x, target, p):
    depth = p['norm_mix'].shape[0]
    tabs = _ssm_tables(p['ssm_a_re'], p['ssm_a_im'], p['ssm_log_dt'], p['ssm_b_re'], p['ssm_b_im'], p['ssm_c_re'],
                       p['ssm_c_im'])
    xs_in, saved_mix, saved_mlp = [], [], []
    for i in range(depth):
        gm = p['norm_mix'][i:i + 1]
        xs_in.append(x)
        if i % 4 == 0:
            x, sv = _conv_mixer_fwd(x, gm, p['conv_w_in'], p['conv_w'], p['conv_w_out'])
        elif i % 4 == 1:
            x, sv = _pool_mixer_fwd(x, gm, p['pool_w_in'], p['pool_w_group'], p['pool_scale'])
        elif i % 4 == 2:
            x, sv = _att_mixer_fwd(x, gm, p['att_w_qkv'], p['att_q_norm'], p['att_k_norm'], p['att_rel_bias'],
                                   p['att_w_out'])
        else:
            x, sv = _ssm_mixer_fwd(x, gm, tabs, p['ssm_d'], p['ssm_w_glu'])
        saved_mix.append(sv)
        xs_in.append(x)
        x, sv = _mlp_fwd(x, p['norm_mlp'][i:i + 1], p['mlp_w1'][i], p['mlp_w2'][i], tag=str(i))
        saved_mlp.append(sv)
    dx, dxb, loss_cols = _loss_head(x, target, name="loss_head")
    g = {'norm_mix': [None] * depth, 'norm_mlp': [None] * depth, 'mlp_w1': [None] * depth, 'mlp_w2': [None] * depth}
    for i in reversed(range(depth)):
        dx, dxb, g['norm_mlp'][i], g['mlp_w1'][i], g['mlp_w2'][i] = _mlp_bwd(
            xs_in[2 * i + 1], p['norm_mlp'][i:i + 1], p['mlp_w1'][i], p['mlp_w2'][i], saved_mlp[i], dx, dxb, tag=str(i))
        gm = p['norm_mix'][i:i + 1]
        xin, sv = xs_in[2 * i], saved_mix[i]
        if i % 4 == 0:
            dx, dxb, g['norm_mix'][i], g['conv_w_in'], g['conv_w'], g['conv_w_out'] = _conv_mixer_bwd(
                xin, gm, p['conv_w_in'], p['conv_w'], p['conv_w_out'], sv, dx, dxb)
        elif i % 4 == 1:
            dx, dxb, g['norm_mix'][i], g['pool_w_in'], g['pool_w_group'], g['pool_scale'] = _pool_mixer_bwd(
                xin, gm, p['pool_w_in'], p['pool_w_group'], p['pool_scale'], sv, dx, dxb)
        elif i % 4 == 2:
            (dx, dxb, g['norm_mix'][i], g['att_w_qkv'], g['att_q_norm'], g['att_k_norm'], g['att_rel_bias'],
             g['att_w_out']) = _att_mixer_bwd(xin, gm, p['att_w_qkv'], p['att_q_norm'], p['att_k_norm'],
                                              p['att_w_out'], sv, dx, dxb)
        else:
            dx, dxb, g['norm_mix'][i], small, g['ssm_d'], g['ssm_w_glu'] = _ssm_mixer_bwd(
                xin, gm, tabs, p['ssm_d'], p['ssm_w_glu'], p['ssm_b_re'], p['ssm_b_im'], sv, dx, dxb)
            (g['ssm_a_re'], g['ssm_a_im'], g['ssm_log_dt'], g['ssm_b_re'], g['ssm_b_im'], g['ssm_c_re'],
             g['ssm_c_im']) = small
    g['norm_mix'] = jnp.concatenate(g['norm_mix'], axis=0)
    g['norm_mlp'] = jnp.concatenate(g['norm_mlp'], axis=0)
    return loss_cols, dx, g


_ANY = pl.BlockSpec(memory_space=pl.ANY)
_VM = pl.BlockSpec(memory_space=pltpu.VMEM)
_REL_ALL = [(0, 0, 1), (0, 1, 0), (0, 1, 1), (1, 0, 0), (1, 0, 1), (1, 1, 0), (1, 1, 1)]
_REL_CHIPS = [(1, 0, 0), (0, 1, 0), (1, 1, 0)]


def _me():
    return lax.axis_index("x"), lax.axis_index("y"), lax.axis_index("c")


def _flip(pos, rel):
    return tuple(1 - p if r else p for p, r in zip(pos, rel))


def _chip_of(pos):
    return 2 * pos[0] + pos[1]


def _dev_of(pos):
    return 4 * pos[0] + 2 * pos[1] + pos[2]


def _gather_small(buf, *, reduce, name):
    rows = buf.shape[0]

    def body(in_ref, out_ref, *rest):
        if reduce:
            gath, send_sems, recv_sems = rest
        else:
            gath = out_ref
            send_sems, recv_sems = rest
        me = _me()
        gath[_dev_of(me)] = in_ref[...]
        copies = []
        for k, rel in enumerate(_REL_ALL):
            peer = _flip(me, rel)
            cp = pltpu.make_async_remote_copy(src_ref=in_ref, dst_ref=gath.at[_dev_of(me)], send_sem=send_sems.at[k],
                                              recv_sem=recv_sems.at[k], device_id=peer, device_id_type=MESH)
            cp.start()
            copies.append(cp)
        for k, rel in enumerate(_REL_ALL):
            peer = _flip(me, rel)
            pltpu.make_async_remote_copy(src_ref=in_ref, dst_ref=gath.at[_dev_of(peer)], send_sem=send_sems.at[k],
                                         recv_sem=recv_sems.at[k], device_id=peer, device_id_type=MESH).wait_recv()
        for cp in copies:
            cp.wait_send()
        if reduce:
            acc = gath[0]
            for s in range(1, N_DEV):
                acc = acc + gath[s]
            out_ref[...] = acc

    out_shape = (rows, LANES) if reduce else (N_DEV, rows, LANES)
    scratch = ([pltpu.VMEM((N_DEV, rows, LANES), F32)] if reduce else []) + [
        pltpu.SemaphoreType.DMA((len(_REL_ALL),)), pltpu.SemaphoreType.DMA((len(_REL_ALL),))]
    return pl.pallas_call(
        body, name=name, in_specs=[_VM], out_specs=_VM, out_shape=jax.ShapeDtypeStruct(out_shape, F32),
        scratch_shapes=scratch, compiler_params=pltpu.CompilerParams(vmem_limit_bytes=VMEM_LIMIT),
    )(buf)


def _region(ref, kind, k, shard_shape, half=None):
    r, c = shard_shape
    hr = r // 2
    if kind == 'col':
        rows = slice(None) if half is None else pl.ds(pl.multiple_of(half * hr, 16), hr)
        return ref.at[rows, pl.ds(pl.multiple_of(k * c, LANES), c)]
    if half is None:
        return ref.at[pl.ds(pl.multiple_of(k * r, 16), r), :]
    return ref.at[pl.ds(pl.multiple_of(k * r + half * hr, 16), hr), :]


def _place_block(shard, dest, layer, chip, kind, *, name):
    _, r, c = shard.shape
    tr = _tile(r, max(16, (1 << 20) // c // 16 * 16), 16)
    nb = r // tr

    def body(chip_ref, s_ref, d_ref, o_ref):
        o_ref[...] = s_ref[0].astype(BF16)

    out_map = (lambda i, k: (i, k[0])) if kind == 'col' else (lambda i, k: (k[0] * nb + i, 0))
    grid_spec = pltpu.PrefetchScalarGridSpec(
        num_scalar_prefetch=1, grid=(nb,),
        in_specs=[pl.BlockSpec((1, tr, c), lambda i, k: (layer, i, 0)), _ANY],
        out_specs=pl.BlockSpec((tr, c), out_map))
    return pl.pallas_call(
        body, name=name, grid_spec=grid_spec, out_shape=jax.ShapeDtypeStruct(dest.shape, BF16),
        input_output_aliases={2: 0}, compiler_params=_cparams(("arbitrary",)),
    )(chip, shard, dest)


def _all_gather_matrices(fulls, kinds, *, name):
    n = len(fulls)
    nj = len(_REL_CHIPS)
    shards = [(f.shape[0], f.shape[1] // N_CHIPS) if kd == 'col' else (f.shape[0] // N_CHIPS, f.shape[1])
              for f, kd in zip(fulls, kinds)]

    def body(*refs):
        in_refs, out_refs = refs[:n], refs[n:2 * n]
        ici_send, ici_recv, d2d_send, d2d_recv = refs[2 * n:]
        me = _me()
        core = me[2]
        sib = _flip(me, (0, 0, 1))
        pending = []
        for m in range(n):
            shp = shards[m]
            src = _region(in_refs[m], kinds[m], _chip_of(me), shp, half=core)
            dst = _region(out_refs[m], kinds[m], _chip_of(me), shp, half=core)
            for j, rel in enumerate(_REL_CHIPS):
                cp = pltpu.make_async_remote_copy(src_ref=src, dst_ref=dst, send_sem=ici_send.at[m, j],
                                                  recv_sem=ici_recv.at[m, j], device_id=_flip(me, rel),
                                                  device_id_type=MESH)
                cp.start()
                pending.append(cp)
        for m in range(n):
            shp = shards[m]
            for j, rel in enumerate(_REL_CHIPS):
                peer = _flip(me, rel)
                landed = _region(out_refs[m], kinds[m], _chip_of(peer), shp, half=core)
                pltpu.make_async_remote_copy(src_ref=landed, dst_ref=landed, send_sem=ici_send.at[m, j],
                                             recv_sem=ici_recv.at[m, j], device_id=peer,
                                             device_id_type=MESH).wait_recv()
                fw = pltpu.make_async_remote_copy(src_ref=landed, dst_ref=landed, send_sem=d2d_send.at[m, j],
                                                  recv_sem=d2d_recv.at[m, j], device_id=sib, device_id_type=MESH)
                fw.start()
                pending.append(fw)
        for m in range(n):
            shp = shards[m]
            for j, rel in enumerate(_REL_CHIPS):
                other = _region(out_refs[m], kinds[m], _chip_of(_flip(me, rel)), shp, half=1 - core)
                pltpu.make_async_remote_copy(src_ref=other, dst_ref=other, send_sem=d2d_send.at[m, j],
                                             recv_sem=d2d_recv.at[m, j], device_id=sib,
                                             device_id_type=MESH).wait_recv()
        for cp in pending:
            cp.wait_send()

    return pl.pallas_call(
        body, name=name, in_specs=[_ANY] * n, out_specs=[_ANY] * n,
        out_shape=[jax.ShapeDtypeStruct(f.shape, BF16) for f in fulls],
        input_output_aliases={m: m for m in range(n)},
        scratch_shapes=[pltpu.SemaphoreType.DMA((n, nj))] * 4,
    )(*fulls)


def _piece_shape(full, kind):
    R, C = full
    return (R // 2, C // N_CHIPS) if kind == 'col' else (R // N_DEV, C)


def _piece(ref, kind, k, c, full):
    pr, pc = _piece_shape(full, kind)
    if kind == 'col':
        return ref.at[pl.ds(pl.multiple_of(c * pr, 16), pr), pl.ds(pl.multiple_of(k * pc, LANES), pc)]
    return ref.at[pl.ds(pl.multiple_of((2 * k + c) * pr, 16), pr), :]


def _pair_exchange(grads, kinds, *, name):
    n = len(grads)
    pieces = [_piece_shape(g.shape, kd) for g, kd in zip(grads, kinds)]

    def body(*refs):
        in_refs, out_refs = refs[:n], refs[n:2 * n]
        send_sems, recv_sems = refs[2 * n:]
        me = _me()
        core = me[2]
        sib = _flip(me, (0, 0, 1))
        sends = []
        for m in range(n):
            for k in range(N_CHIPS):
                cp = pltpu.make_async_remote_copy(
                    src_ref=_piece(in_refs[m], kinds[m], k, 1 - core, grads[m].shape), dst_ref=out_refs[m].at[k],
                    send_sem=send_sems.at[m, k], recv_sem=recv_sems.at[m, k], device_id=sib, device_id_type=MESH)
                cp.start()
                sends.append(cp)
        for m in range(n):
            for k in range(N_CHIPS):
                pltpu.make_async_remote_copy(
                    src_ref=_piece(in_refs[m], kinds[m], k, core, grads[m].shape), dst_ref=out_refs[m].at[k],
                    send_sem=send_sems.at[m, k], recv_sem=recv_sems.at[m, k], device_id=sib,
                    device_id_type=MESH).wait_recv()
        for cp in sends:
            cp.wait_send()

    return pl.pallas_call(
        body, name=name, in_specs=[_ANY] * n, out_specs=[_ANY] * n,
        out_shape=[jax.ShapeDtypeStruct((N_CHIPS,) + p, BF16) for p in pieces],
        scratch_shapes=[pltpu.SemaphoreType.DMA((n, N_CHIPS)), pltpu.SemaphoreType.DMA((n, N_CHIPS))],
    )(*grads)


def _pair_sum(g, recv, core, kind, *, name):
    pr, pc = _piece_shape(g.shape, kind)
    tr = _tile(pr, max(16, (1 << 19) // pc // 16 * 16), 16)
    nb = pr // tr

    def body(core_ref, g_ref, r_ref, o_ref):
        o_ref[0] = (g_ref[...].astype(F32) + r_ref[0].astype(F32)).astype(BF16)

    g_map = (lambda k, i, c: (c[0] * nb + i, k)) if kind == 'col' else (lambda k, i, c: ((2 * k + c[0]) * nb + i, 0))
    slot = pl.BlockSpec((1, tr, pc), lambda k, i, c: (k, i, 0))
    grid_spec = pltpu.PrefetchScalarGridSpec(
        num_scalar_prefetch=1, grid=(N_CHIPS, nb), in_specs=[pl.BlockSpec((tr, pc), g_map), slot], out_specs=slot)
    return pl.pallas_call(
        body, name=name, grid_spec=grid_spec, out_shape=jax.ShapeDtypeStruct((N_CHIPS, pr, pc), BF16),
        compiler_params=_cparams(("arbitrary", "arbitrary")),
    )(core, g, recv)


def _chip_scatter(sums, *, name):
    n = len(sums)
    nj = len(_REL_CHIPS)

    def body(*refs):
        in_refs, out_refs = refs[:n], refs[n:2 * n]
        send_sems, recv_sems = refs[2 * n:]
        me = _me()
        sends = []
        for m in range(n):
            for j, rel in enumerate(_REL_CHIPS):
                peer = _flip(me, rel)
                cp = pltpu.make_async_remote_copy(
                    src_ref=in_refs[m].at[_chip_of(peer)], dst_ref=out_refs[m].at[j], send_sem=send_sems.at[m, j],
                    recv_sem=recv_sems.at[m, j], device_id=peer, device_id_type=MESH)
                cp.start()
                sends.append(cp)
        for m in range(n):
            for j, rel in enumerate(_REL_CHIPS):
                pltpu.make_async_remote_copy(
                    src_ref=in_refs[m].at[0], dst_ref=out_refs[m].at[j], send_sem=send_sems.at[m, j],
                    recv_sem=recv_sems.at[m, j], device_id=_flip(me, rel), device_id_type=MESH).wait_recv()
        for cp in sends:
            cp.wait_send()

    return pl.pallas_call(
        body, name=name, in_specs=[_ANY] * n, out_specs=[_ANY] * n,
        out_shape=[jax.ShapeDtypeStruct((nj,) + s.shape[1:], BF16) for s in sums],
        scratch_shapes=[pltpu.SemaphoreType.DMA((n, nj)), pltpu.SemaphoreType.DMA((n, nj))],
    )(*sums)


def _sum_into(own, recv, dest, layer, core, chip, *, name):
    _, pr, pc = own.shape
    tr = _tile(pr, max(16, (1 << 19) // pc // 16 * 16), 16)
    nb = pr // tr

    def body(core_ref, chip_ref, own_ref, r_ref, d_ref, o_ref):
        acc = own_ref[0].astype(F32)
        for s in range(len(_REL_CHIPS)):
            acc = acc + r_ref[s].astype(F32)
        o_ref[0] = acc

    grid_spec = pltpu.PrefetchScalarGridSpec(
        num_scalar_prefetch=2, grid=(nb,),
        in_specs=[pl.BlockSpec((1, tr, pc), lambda i, c, k: (k[0], i, 0)),
                  pl.BlockSpec((len(_REL_CHIPS), tr, pc), lambda i, c, k: (0, i, 0)), _ANY],
        out_specs=pl.BlockSpec((1, tr, pc), lambda i, c, k: (layer, c[0] * nb + i, 0)))
    return pl.pallas_call(
        body, name=name, grid_spec=grid_spec, out_shape=jax.ShapeDtypeStruct(dest.shape, F32),
        input_output_aliases={4: 0}, compiler_params=_cparams(("arbitrary",)),
    )(core, chip, own, recv, dest)


def _exchange_halves(blocks, *, name):
    n_out = len(blocks)
    n = sum(b.shape[0] for b in blocks)

    def body(*refs):
        in_refs, out_refs = refs[:n_out], refs[n_out:2 * n_out]
        send_sems, recv_sems = refs[2 * n_out:]
        me = _me()
        sib = _flip(me, (0, 0, 1))
        sends, m = [], 0
        for o in range(n_out):
            L, r2, _ = blocks[o].shape
            hr = r2 // 2
            for l in range(L):
                rows = pl.ds(pl.multiple_of(me[2] * hr, SUBLANES), hr)
                cp = pltpu.make_async_remote_copy(src_ref=in_refs[o].at[l, rows, :], dst_ref=out_refs[o].at[l, rows, :],
                                                  send_sem=send_sems.at[m], recv_sem=recv_sems.at[m], device_id=sib,
                                                  device_id_type=MESH)
                cp.start()
                sends.append(cp)
                m += 1
        m = 0
        for o in range(n_out):
            L, r2, _ = blocks[o].shape
            hr = r2 // 2
            for l in range(L):
                rows = pl.ds(pl.multiple_of(sib[2] * hr, SUBLANES), hr)
                theirs = out_refs[o].at[l, rows, :]
                pltpu.make_async_remote_copy(src_ref=theirs, dst_ref=theirs, send_sem=send_sems.at[m],
                                             recv_sem=recv_sems.at[m], device_id=sib, device_id_type=MESH).wait_recv()
                m += 1
        for cp in sends:
            cp.wait_send()

    return pl.pallas_call(
        body, name=name, in_specs=[_ANY] * n_out, out_specs=[_ANY] * n_out,
        out_shape=[jax.ShapeDtypeStruct(b.shape, F32) for b in blocks],
        input_output_aliases={o: o for o in range(n_out)},
        scratch_shapes=[pltpu.SemaphoreType.DMA((n,)), pltpu.SemaphoreType.DMA((n,))],
    )(*blocks)


def _adamw(w, g, m, v, *, name):
    R, C = w.shape
    tr = _tile(R, max(SUBLANES, (1 << 19) // C // SUBLANES * SUBLANES), SUBLANES)
    c1 = 1.0 / (1.0 - ADAM_B1 ** ADAM_STEP)
    c2 = 1.0 / (1.0 - ADAM_B2 ** ADAM_STEP)

    def body(w_ref, g_ref, m_ref, v_ref, d_ref, nm_ref, nv_ref):
        gv = g_ref[...]
        nm = ADAM_B1 * m_ref[...] + (1.0 - ADAM_B1) * gv
        nv = ADAM_B2 * v_ref[...] + (1.0 - ADAM_B2) * (gv * gv)
        nm_ref[...] = nm
        nv_ref[...] = nv
        d_ref[...] = -ADAM_LR * ((nm * c1) / (jnp.sqrt(nv * c2) + ADAM_EPS) + ADAM_WD * w_ref[...])

    blk = pl.BlockSpec((tr, C), lambda i: (i, 0))
    return pl.pallas_call(
        body, name=name, grid=(R // tr,), in_specs=[blk] * 4, out_specs=[blk] * 3,
        out_shape=[jax.ShapeDtypeStruct((R, C), F32)] * 3, compiler_params=_cparams(("parallel",)),
    )(w, g, m, v)


_BIG = ['mlp_w1', 'mlp_w2', 'conv_w_in', 'conv_w_out', 'pool_w_in', 'pool_w_group', 'att_w_qkv', 'att_w_out',
        'ssm_w_glu']
_KIND = {'mlp_w1': 'col', 'mlp_w2': 'row', 'conv_w_in': 'col', 'conv_w_out': 'row', 'pool_w_in': 'row',
         'pool_w_group': 'row', 'att_w_qkv': 'col', 'att_w_out': 'row', 'ssm_w_glu': 'col'}
_TINY_SHARDED = ['conv_w', 'pool_scale', 'ssm_d']
_REPLICATED = ['norm_mix', 'norm_mlp', 'att_q_norm', 'att_k_norm', 'att_rel_bias', 'ssm_a_re', 'ssm_a_im',
               'ssm_log_dt', 'ssm_b_re', 'ssm_b_im', 'ssm_c_re', 'ssm_c_im']
_SMALL = _REPLICATED + _TINY_SHARDED
_ORDER = ['norm_mix', 'norm_mlp', 'mlp_w1', 'mlp_w2', 'conv_w_in', 'conv_w', 'conv_w_out', 'pool_w_in',
          'pool_w_group', 'pool_scale', 'att_w_qkv', 'att_q_norm', 'att_k_norm', 'att_rel_bias', 'att_w_out',
          'ssm_a_re', 'ssm_a_im', 'ssm_log_dt', 'ssm_b_re', 'ssm_b_im', 'ssm_c_re', 'ssm_c_im', 'ssm_d', 'ssm_w_glu']
_LAYER_OF = {'conv_w_in': 0, 'conv_w_out': 0, 'pool_w_in': 1, 'pool_w_group': 1, 'att_w_qkv': 2, 'att_w_out': 2,
             'ssm_w_glu': 3}


def _pack(arrays):
    flat = jnp.concatenate([a.reshape(-1).astype(F32) for a in arrays])
    n = flat.shape[0]
    total = -(-n // (SUBLANES * LANES)) * (SUBLANES * LANES)
    return jnp.pad(flat, (0, total - n)).reshape(total // LANES, LANES)


def _unpack(buf, shapes):
    flat = buf.reshape(-1)
    out, off = [], 0
    for s in shapes:
        n = int(np.prod(s))
        out.append(flat[off:off + n].reshape(s))
        off += n
    return out


def _matrices(w, name):
    t = w.reshape((-1,) + w.shape[-2:])
    return [t[l] for l in range(t.shape[0])]


def kernel(x, norm_mix, norm_mlp, mlp_w1, mlp_w2, conv_w_in, conv_w, conv_w_out, pool_w_in, pool_w_group, pool_scale, att_w_qkv, att_q_norm, att_k_norm, att_rel_bias, att_w_out, ssm_a_re, ssm_a_im, ssm_log_dt, ssm_b_re, ssm_b_im, ssm_c_re, ssm_c_im, ssm_d, ssm_w_glu, loss_target, m_norm_mix, m_norm_mlp, m_mlp_w1, m_mlp_w2, m_conv_w_in, m_conv_w, m_conv_w_out, m_pool_w_in, m_pool_w_group, m_pool_scale, m_att_w_qkv, m_att_q_norm, m_att_k_norm, m_att_rel_bias, m_att_w_out, m_ssm_a_re, m_ssm_a_im, m_ssm_log_dt, m_ssm_b_re, m_ssm_b_im, m_ssm_c_re, m_ssm_c_im, m_ssm_d, m_ssm_w_glu, v_norm_mix, v_norm_mlp, v_mlp_w1, v_mlp_w2, v_conv_w_in, v_conv_w, v_conv_w_out, v_pool_w_in, v_pool_w_group, v_pool_scale, v_att_w_qkv, v_att_q_norm, v_att_k_norm, v_att_rel_bias, v_att_w_out, v_ssm_a_re, v_ssm_a_im, v_ssm_log_dt, v_ssm_b_re, v_ssm_b_im, v_ssm_c_re, v_ssm_c_im, v_ssm_d, v_ssm_w_glu):
    args = dict(locals())
    W = {n: args[n] for n in _ORDER}
    M = {n: args['m_' + n] for n in _ORDER}
    V = {n: args['v_' + n] for n in _ORDER}
    depth = norm_mix.shape[0]
    d_model = x.shape[-1]
    chip = 2 * lax.axis_index("x") + lax.axis_index("y")

    tiny = _gather_small(_pack([W[n] for n in _TINY_SHARDED]), reduce=False, name="gather_vectors")
    tiny_shapes = [W[n].shape for n in _TINY_SHARDED]
    per_chip = [_unpack(tiny[2 * k], tiny_shapes) for k in range(N_CHIPS)]
    full_tiny = {n: jnp.concatenate([per_chip[k][i] for k in range(N_CHIPS)], axis=-1)
                 for i, n in enumerate(_TINY_SHARDED)}
    chip_arr = chip.astype(jnp.int32).reshape(1)
    groups = [[] for _ in range(depth)]
    shard3 = {n: W[n].reshape((-1,) + W[n].shape[-2:]) for n in _BIG}
    for n in _BIG:
        for l in range(shard3[n].shape[0]):
            groups[l if n.startswith('mlp') else _LAYER_OF[n]].append((n, l))
    full = {n: [None] * shard3[n].shape[0] for n in _BIG}
    for i in range(depth):
        placed = []
        for n, l in groups[i]:
            _, r, c = shard3[n].shape
            shape = (r, c * N_CHIPS) if _KIND[n] == 'col' else (r * N_CHIPS, c)
            placed.append(_place_block(shard3[n], lax.empty(shape, BF16), l, chip_arr, _KIND[n],
                                       name=f"place_{n}_{l}"))
        got = _all_gather_matrices(placed, [_KIND[n] for n, _ in groups[i]], name=f"gather_weights_{i}")
        for (n, l), t in zip(groups[i], got):
            full[n][l] = t

    p = dict(
        norm_mix=norm_mix, norm_mlp=norm_mlp, mlp_w1=full['mlp_w1'], mlp_w2=full['mlp_w2'],
        conv_w_in=full['conv_w_in'][0], conv_w=full_tiny['conv_w'][0], conv_w_out=full['conv_w_out'][0],
        pool_w_in=full['pool_w_in'][0], pool_w_group=jnp.stack(full['pool_w_group']),
        pool_scale=full_tiny['pool_scale'], att_w_qkv=full['att_w_qkv'][0], att_q_norm=att_q_norm,
        att_k_norm=att_k_norm, att_rel_bias=att_rel_bias[0], att_w_out=full['att_w_out'][0],
        ssm_a_re=ssm_a_re[0], ssm_a_im=ssm_a_im[0], ssm_log_dt=ssm_log_dt[0], ssm_b_re=ssm_b_re[0],
        ssm_b_im=ssm_b_im[0], ssm_c_re=ssm_c_re[0], ssm_c_im=ssm_c_im[0], ssm_d=full_tiny['ssm_d'],
        ssm_w_glu=full['ssm_w_glu'][0])

    loss_cols, dx, g = _local_step(x[0], loss_target[0], p)
    loss = lax.psum(0.5 * jnp.sum(loss_cols) / d_model, ("x", "y", "c"))

    gmats = {n: (g[n] if isinstance(g[n], list) else _matrices(g[n], n)) for n in _BIG}
    core = lax.axis_index("c").astype(jnp.int32).reshape(1)
    blocks = {n: lax.empty((len(gmats[n]),) + W[n].shape[-2:], F32) for n in _BIG}
    for i in reversed(range(depth)):
        kinds = [_KIND[n] for n, _ in groups[i]]
        mats = [gmats[n][l] for n, l in groups[i]]
        from_sibling = _pair_exchange(mats, kinds, name=f"pair_exchange_{i}")
        pair_sums = [_pair_sum(gm, t, core, kd, name=f"pair_sum_{n}_{l}")
                     for (n, l), gm, t, kd in zip(groups[i], mats, from_sibling, kinds)]
        from_chips = _chip_scatter(pair_sums, name=f"chip_scatter_{i}")
        for (n, l), own, t in zip(groups[i], pair_sums, from_chips):
            blocks[n] = _sum_into(own, t, blocks[n], l, core, chip_arr, name=f"sum_grads_{n}_{l}")
    reduced = _exchange_halves([blocks[n] for n in _BIG], name="exchange_halves")
    grads = {n: t.reshape(W[n].shape) for n, t in zip(_BIG, reduced)}

    small_full_shapes = [W[n].shape for n in _REPLICATED] + [full_tiny[n].shape for n in _TINY_SHARDED]
    gsmall = _gather_small(_pack([g[n] for n in _SMALL]), reduce=True, name="reduce_small_grads")
    for n, t in zip(_SMALL, _unpack(gsmall, small_full_shapes)):
        if n in _TINY_SHARDED:
            width = W[n].shape[-1]
            t = lax.dynamic_slice_in_dim(t, chip * width, width, axis=t.ndim - 1)
        grads[n] = t.reshape(W[n].shape)

    delta, new_m, new_v = {}, {}, {}
    for n in _BIG:
        shp = W[n].shape
        two = (int(np.prod(shp[:-1])), shp[-1])
        d, nm, nv = _adamw(W[n].reshape(two), grads[n].reshape(two), M[n].reshape(two), V[n].reshape(two),
                           name=f"adamw_{n}")
        delta[n], new_m[n], new_v[n] = d.reshape(shp), nm.reshape(shp), nv.reshape(shp)
    shapes = [W[n].shape for n in _SMALL]
    d, nm, nv = _adamw(_pack([W[n] for n in _SMALL]), _pack([grads[n] for n in _SMALL]),
                       _pack([M[n] for n in _SMALL]), _pack([V[n] for n in _SMALL]), name="adamw_small")
    for n, a, b, c in zip(_SMALL, _unpack(d, shapes), _unpack(nm, shapes), _unpack(nv, shapes)):
        delta[n], new_m[n], new_v[n] = a, b, c

    return (loss, dx.reshape(x.shape), *[grads[n] for n in _ORDER], *[delta[n] for n in _ORDER],
            *[new_m[n] for n in _ORDER], *[new_v[n] for n in _ORDER])
```

```python
import functools
import math

import numpy as np
import jax
import jax.numpy as jnp
from jax import lax
from jax.experimental import pallas as pl
from jax.experimental.pallas import tpu as pltpu

F32 = jnp.float32
BF16 = jnp.bfloat16
MESH = pl.DeviceIdType.MESH

V7X_VMEM_BYTES = 64 * 1024 * 1024
VMEM_LIMIT = V7X_VMEM_BYTES - 12 * 1024 * 1024
LANES = 128
SUBLANES = 8

CHUNK = 64
ATT_HEAD_DIM = 128
ATT_PAD = 8 * CHUNK
REL_CLIP = 256
MASK_VALUE = -1e30
POOL_WINDOWS = (2, 4, 8, 16)
POOL_HALO = 16
SSM_GROUP = 16
SSM_STATE = 64
SSM_SLAB_GROUPS = LANES // SSM_GROUP
SSM_SLAB_STATE = SSM_SLAB_GROUPS * SSM_STATE
RMS_EPS = 1e-6
ADAM_LR, ADAM_B1, ADAM_B2, ADAM_EPS, ADAM_WD, ADAM_STEP = 0.001, 0.9, 0.999, 1e-08, 0.01, 10
ATT_TQ = 256
N_CHIPS = 4
N_DEV = 8


def _cparams(sem=None, **kw):
    return pltpu.CompilerParams(dimension_semantics=sem, vmem_limit_bytes=VMEM_LIMIT, **kw)


def _tile(n, target, mult):
    if n <= target:
        return n
    t = (target // mult) * mult
    while t > mult and n % t:
        t -= mult
    assert n % t == 0, (n, target, mult)
    return t


def _mm(a, b, *, ta=False, tb=False, extras=(), epilogue=None, out_dtypes=(F32,), name,
        tm=1024, tn=1024, tk=2048):
    M, K = (a.shape[1], a.shape[0]) if ta else a.shape
    N = b.shape[0] if tb else b.shape[1]
    assert (b.shape[1] if tb else b.shape[0]) == K, (a.shape, b.shape, ta, tb)
    tm, tn, tk = _tile(M, tm, LANES), _tile(N, tn, LANES), _tile(K, tk, LANES)
    nk = K // tk
    n_ex, n_out = len(extras), len(out_dtypes)
    dn = (((0 if ta else 1,), (1 if tb else 0,)), ((), ()))

    def body(*refs):
        a_ref, b_ref = refs[0], refs[1]
        ex_refs = refs[2:2 + n_ex]
        o_refs = refs[2 + n_ex:2 + n_ex + n_out]
        p = lax.dot_general(a_ref[...], b_ref[...], dn, preferred_element_type=F32)

        def finish(acc):
            outs = (acc,) if epilogue is None else epilogue(acc, *[r[...] for r in ex_refs])
            for o_ref, o in zip(o_refs, outs):
                o_ref[...] = o.astype(o_ref.dtype)

        if nk == 1:
            finish(p)
        else:
            acc_ref = refs[-1]
            k = pl.program_id(2)

            @pl.when(k == 0)
            def _():
                acc_ref[...] = p

            @pl.when(k > 0)
            def _():
                acc_ref[...] += p

            @pl.when(k == nk - 1)
            def _():
                finish(acc_ref[...])

    a_spec = pl.BlockSpec((tk, tm), lambda i, j, k: (k, i)) if ta else pl.BlockSpec((tm, tk), lambda i, j, k: (i, k))
    b_spec = pl.BlockSpec((tn, tk), lambda i, j, k: (j, k)) if tb else pl.BlockSpec((tk, tn), lambda i, j, k: (k, j))
    mn_spec = pl.BlockSpec((tm, tn), lambda i, j, k: (i, j))
    outs = pl.pallas_call(
        body, name=name, grid=(M // tm, N // tn, nk),
        in_specs=[a_spec, b_spec] + [mn_spec] * n_ex,
        out_specs=[mn_spec] * n_out,
        out_shape=[jax.ShapeDtypeStruct((M, N), d) for d in out_dtypes],
        scratch_shapes=[pltpu.VMEM((tm, tn), F32)] if nk > 1 else [],
        compiler_params=_cparams(("parallel", "parallel", "arbitrary")),
    )(a, b, *extras)
    return outs[0] if n_out == 1 else tuple(outs)


def _row_call(body, ins, outs, *, name, tr, n_rows, acc_outs=(), scratch=(), halo=None):
    nb = n_rows // tr
    hb = halo or SUBLANES
    per = tr // hb
    last = n_rows // hb - 1

    def spec(arr_shape, kind):
        if kind == 'rows':
            return pl.BlockSpec((tr,) + tuple(arr_shape[1:]), lambda i: (i,) + (0,) * (len(arr_shape) - 1))
        if kind == 'full' or kind == 'acc':
            return pl.BlockSpec(tuple(arr_shape), lambda i: (0,) * len(arr_shape))
        tag, w, j = kind
        if tag == 'cols':
            return pl.BlockSpec((tr, w), lambda i: (i, j))
        if tag == 'rows_from':
            return pl.BlockSpec((tr, w), lambda i: (i + j // tr, 0))
        if tag == 'prev':
            return pl.BlockSpec((hb, w), lambda i: (jnp.maximum(i * per - 1, 0), j))
        if tag == 'next':
            return pl.BlockSpec((hb, w), lambda i: (jnp.minimum((i + 1) * per, last), j))
        raise ValueError(kind)

    return pl.pallas_call(
        body, name=name, grid=(nb,),
        in_specs=[spec(a.shape, k) for a, k in ins],
        out_specs=[spec(s, k) for s, _, k in outs],
        out_shape=[jax.ShapeDtypeStruct(s, d) for s, d, _ in outs],
        scratch_shapes=list(scratch),
        compiler_params=_cparams(("arbitrary",)),
    )(*[a for a, _ in ins])


def _rms_fwd(x, gain, *, name, with_f32=False):
    S, D = x.shape
    tr = _tile(S, 512, SUBLANES)

    def body(x_ref, g_ref, *o_refs):
        xv = x_ref[...]
        r = lax.rsqrt(jnp.mean(xv * xv, axis=-1, keepdims=True) + RMS_EPS)
        h = xv * r * g_ref[...]
        o_refs[0][...] = h.astype(BF16)
        if with_f32:
            o_refs[1][...] = h

    outs = [((S, D), BF16, 'rows')] + ([((S, D), F32, 'rows')] if with_f32 else [])
    res = _row_call(body, [(x, 'rows'), (gain, 'full')], outs, name=name, tr=tr, n_rows=S)
    return tuple(res) if with_f32 else res[0]


def _rms_bwd(x, gain, dh, dres, *, name):
    S, D = x.shape
    tr = _tile(S, 256, SUBLANES)

    def body(x_ref, g_ref, dh_ref, dr_ref, dx_ref, dxb_ref, dg_ref):
        i = pl.program_id(0)
        xv = x_ref[...]
        r = lax.rsqrt(jnp.mean(xv * xv, axis=-1, keepdims=True) + RMS_EPS)
        xn = xv * r
        dhv = dh_ref[...]
        dxn = dhv * g_ref[...]
        dx = r * (dxn - xn * jnp.mean(dxn * xn, axis=-1, keepdims=True)) + dr_ref[...]
        dx_ref[...] = dx
        dxb_ref[...] = dx.astype(BF16)
        part = jnp.sum(dhv * xn, axis=0, keepdims=True)

        @pl.when(i == 0)
        def _():
            dg_ref[...] = part

        @pl.when(i > 0)
        def _():
            dg_ref[...] += part

    return _row_call(body, [(x, 'rows'), (gain, 'full'), (dh, 'rows'), (dres, 'rows')],
                     [((S, D), F32, 'rows'), ((S, D), BF16, 'rows'), ((1, D), F32, 'acc')],
                     name=name, tr=tr, n_rows=S)


def _loss_head(y, target, *, name):
    S, D = y.shape
    tr = _tile(S, 512, SUBLANES)

    def body(y_ref, t_ref, d_ref, db_ref, l_ref):
        i = pl.program_id(0)
        e = y_ref[...] - t_ref[...]
        d = e * (1.0 / D)
        d_ref[...] = d
        db_ref[...] = d.astype(BF16)
        part = jnp.sum(e * e, axis=0, keepdims=True)

        @pl.when(i == 0)
        def _():
            l_ref[...] = part

        @pl.when(i > 0)
        def _():
            l_ref[...] += part

    return _row_call(body, [(y, 'rows'), (target, 'rows')],
                     [((S, D), F32, 'rows'), ((S, D), BF16, 'rows'), ((1, D), F32, 'acc')],
                     name=name, tr=tr, n_rows=S)


def _relu2_epilogue(acc):
    r = jnp.maximum(acc, 0.0)
    return r, r * r


def _mlp_fwd(x, gain, w1, w2, *, tag):
    h = _rms_fwd(x, gain, name=f"mlp_norm_{tag}")
    r, act = _mm(h, w1, epilogue=_relu2_epilogue, out_dtypes=(BF16, BF16), name=f"mlp_up_{tag}")
    y = _mm(act, w2, extras=(x,), epilogue=lambda acc, res: (acc + res,), name=f"mlp_down_{tag}")
    return y, (h, r, act)


def _mlp_bwd(x, gain, w1, w2, saved, dy, dyb, *, tag):
    h, r, act = saved
    da = _mm(dyb, w2, tb=True, extras=(r,), epilogue=lambda acc, rr: (acc * (2.0 * rr.astype(F32)),),
             out_dtypes=(BF16,), name=f"mlp_dact_{tag}")
    dw2 = _mm(act, dyb, ta=True, out_dtypes=(BF16,), name=f"mlp_dw2_{tag}")
    dw1 = _mm(h, da, ta=True, out_dtypes=(BF16,), name=f"mlp_dw1_{tag}")
    dh = _mm(da, w1, tb=True, name=f"mlp_dh_{tag}")
    dx, dxb, dgain = _rms_bwd(x, gain, dh, dy, name=f"mlp_dnorm_{tag}")
    return dx, dxb, dgain, dw1, dw2


def _conv_gate_fwd(z, conv_w):
    S, D3 = z.shape
    D = D3 // 3
    tr = _tile(S, 256, SUBLANES)

    def body(b_ref, c_ref, v_ref, cp_ref, vp_ref, w_ref, g_ref, scr):
        i = pl.program_id(0)
        u = c_ref[...] * v_ref[...]
        scr[0:SUBLANES, :] = cp_ref[...] * vp_ref[...] * (i > 0).astype(F32)
        scr[SUBLANES:, :] = u
        conv = (w_ref[0:1, :] * scr[pl.ds(SUBLANES - 2, tr), :] + w_ref[1:2, :] * scr[pl.ds(SUBLANES - 1, tr), :]
                + w_ref[2:3, :] * u)
        g_ref[...] = (b_ref[...] * conv).astype(BF16)

    ins = [(z, ('cols', D, 0)), (z, ('cols', D, 1)), (z, ('cols', D, 2)), (z, ('prev', D, 1)), (z, ('prev', D, 2)),
           (conv_w, 'full')]
    return _row_call(body, ins, [((S, D), BF16, 'rows')], name="conv_gate_fwd", tr=tr, n_rows=S,
                     scratch=[pltpu.VMEM((tr + SUBLANES, D), F32)])[0]


def _conv_gate_bwd(z, conv_w, dg):
    S, D3 = z.shape
    D = D3 // 3
    tr = _tile(S, 128, SUBLANES)
    nb = S // tr

    def body(b_ref, c_ref, v_ref, cp_ref, vp_ref, bn_ref, dg_ref, dgn_ref, w_ref, dz_ref, dw_ref, scr, scr2):
        i = pl.program_id(0)
        c, v, b, dgv = c_ref[...], v_ref[...], b_ref[...], dg_ref[...]
        u = c * v
        scr[0:SUBLANES, :] = cp_ref[...] * vp_ref[...] * (i > 0).astype(F32)
        scr[SUBLANES:, :] = u
        u1 = scr[pl.ds(SUBLANES - 1, tr), :]
        u2 = scr[pl.ds(SUBLANES - 2, tr), :]
        conv = w_ref[0:1, :] * u2 + w_ref[1:2, :] * u1 + w_ref[2:3, :] * u
        dconv = dgv * b
        scr2[0:tr, :] = dconv
        scr2[tr:, :] = dgn_ref[...] * bn_ref[...] * (i < nb - 1).astype(F32)
        du = (w_ref[2:3, :] * dconv + w_ref[1:2, :] * scr2[pl.ds(1, tr), :] + w_ref[0:1, :] * scr2[pl.ds(2, tr), :])
        dz_ref[:, 0:D] = (dgv * conv).astype(BF16)
        dz_ref[:, D:2 * D] = (du * v).astype(BF16)
        dz_ref[:, 2 * D:] = (du * c).astype(BF16)
        parts = [jnp.sum(dconv * t, axis=0, keepdims=True) for t in (u2, u1, u)]

        @pl.when(i == 0)
        def _():
            for k in range(3):
                dw_ref[k:k + 1, :] = parts[k]

        @pl.when(i > 0)
        def _():
            for k in range(3):
                dw_ref[k:k + 1, :] += parts[k]

    ins = [(z, ('cols', D, 0)), (z, ('cols', D, 1)), (z, ('cols', D, 2)), (z, ('prev', D, 1)), (z, ('prev', D, 2)),
           (z, ('next', D, 0)), (dg, 'rows'), (dg, ('next', D, 0)), (conv_w, 'full')]
    return _row_call(body, ins, [((S, D3), BF16, 'rows'), ((3, D), F32, 'acc')], name="conv_gate_bwd", tr=tr,
                     n_rows=S, scratch=[pltpu.VMEM((tr + SUBLANES, D), F32), pltpu.VMEM((tr + SUBLANES, D), F32)])


def _conv_mixer_fwd(x, gain, w_in, conv_w, w_out):
    h = _rms_fwd(x, gain, name="conv_norm")
    z = _mm(h, w_in, name="conv_in")
    g = _conv_gate_fwd(z, conv_w)
    y = _mm(g, w_out, extras=(x,), epilogue=lambda acc, res: (acc + res,), name="conv_out")
    return y, (h, z, g)


def _conv_mixer_bwd(x, gain, w_in, conv_w, w_out, saved, dy, dyb):
    h, z, g = saved
    dg = _mm(dyb, w_out, tb=True, name="conv_dg")
    dw_out = _mm(g, dyb, ta=True, out_dtypes=(BF16,), name="conv_dwout")
    dz, dconv_w = _conv_gate_bwd(z, conv_w, dg)
    dh = _mm(dz, w_in, tb=True, name="conv_dh")
    dw_in = _mm(h, dz, ta=True, out_dtypes=(BF16,), name="conv_dwin")
    dx, dxb, dgain = _rms_bwd(x, gain, dh, dy, name="conv_dnorm")
    return dx, dxb, dgain, dw_in, dconv_w, dw_out


def _pool_fwd(u):
    S, D = u.shape
    G = D // len(POOL_WINDOWS)
    tr = _tile(S, 256, SUBLANES)
    H = POOL_HALO

    def body(u_ref, up_ref, p_ref, scr):
        i = pl.program_id(0)
        uv = u_ref[...]
        scr[0:H, :] = up_ref[...] * (i > 0).astype(F32)
        scr[H:, :] = uv
        t = (lax.broadcasted_iota(jnp.int32, (tr, 1), 0) + i * tr + 1).astype(F32)
        for gi, w in enumerate(POOL_WINDOWS):
            cols = slice(gi * G, (gi + 1) * G)
            acc = uv[:, cols]
            for j in range(1, w):
                acc = acc + scr[pl.ds(H - j, tr), cols]
            p_ref[:, cols] = (acc / jnp.minimum(t, float(w)) - uv[:, cols]).astype(BF16)

    return _row_call(body, [(u, 'rows'), (u, ('prev', D, 0))], [((S, D), BF16, 'rows')], name="pool_fwd", tr=tr,
                     n_rows=S, halo=H, scratch=[pltpu.VMEM((tr + H, D), F32)])[0]


def _pool_bwd(dp):
    S, D = dp.shape
    G = D // len(POOL_WINDOWS)
    tr = _tile(S, 256, SUBLANES)
    H = POOL_HALO
    nb = S // tr

    def body(d_ref, dn_ref, o_ref, scr):
        i = pl.program_id(0)
        dv = d_ref[...]
        t = (lax.broadcasted_iota(jnp.int32, (tr, 1), 0) + i * tr + 1).astype(F32)
        tn = (lax.broadcasted_iota(jnp.int32, (H, 1), 0) + (i + 1) * tr + 1).astype(F32)
        for gi, w in enumerate(POOL_WINDOWS):
            cols = slice(gi * G, (gi + 1) * G)
            scr[0:tr, cols] = dv[:, cols] / jnp.minimum(t, float(w))
            scr[tr:, cols] = dn_ref[:, cols] / jnp.minimum(tn, float(w)) * (i < nb - 1).astype(F32)
        for gi, w in enumerate(POOL_WINDOWS):
            cols = slice(gi * G, (gi + 1) * G)
            acc = scr[0:tr, cols]
            for j in range(1, w):
                acc = acc + scr[pl.ds(j, tr), cols]
            o_ref[:, cols] = (acc - dv[:, cols]).astype(BF16)

    return _row_call(body, [(dp, 'rows'), (dp, ('next', D, 0))], [((S, D), BF16, 'rows')], name="pool_bwd", tr=tr,
                     n_rows=S, halo=H, scratch=[pltpu.VMEM((tr + H, D), F32)])[0]


def _pool_group_fwd(p, wg, scale, x):
    S, D = p.shape
    NG, G, _ = wg.shape
    tm = _tile(S, 1024, SUBLANES)

    def body(p_ref, w_ref, s_ref, x_ref, o_ref, y_ref):
        y = jnp.dot(p_ref[...], w_ref[0], preferred_element_type=F32)
        y_ref[...] = y
        o_ref[...] = x_ref[...] + y * s_ref[...]

    blk = pl.BlockSpec((tm, G), lambda i, g: (i, g))
    return pl.pallas_call(
        body, name="pool_group_fwd", grid=(S // tm, NG),
        in_specs=[blk, pl.BlockSpec((1, G, G), lambda i, g: (g, 0, 0)), pl.BlockSpec((1, G), lambda i, g: (0, g)), blk],
        out_specs=[blk, blk],
        out_shape=[jax.ShapeDtypeStruct((S, D), F32), jax.ShapeDtypeStruct((S, D), F32)],
        compiler_params=_cparams(("parallel", "arbitrary")),
    )(p, wg, scale, x)


def _pool_group_bwd(p, wg, scale, y, dm):
    S, D = p.shape
    NG, G, _ = wg.shape
    tm = _tile(S, 1024, SUBLANES)
    nb = S // tm

    def body(p_ref, w_ref, s_ref, y_ref, dm_ref, dp_ref, dw_ref, ds_ref, acc_ref):
        i = pl.program_id(1)
        dmv = dm_ref[...]
        dy = (dmv * s_ref[...]).astype(BF16)
        dp_ref[...] = lax.dot_general(dy, w_ref[0], (((1,), (1,)), ((), ())), preferred_element_type=F32)
        dw = lax.dot_general(p_ref[...], dy, (((0,), (0,)), ((), ())), preferred_element_type=F32)
        dsp = jnp.sum(dmv * y_ref[...], axis=0, keepdims=True)

        @pl.when(i == 0)
        def _():
            acc_ref[...] = dw
            ds_ref[...] = dsp

        @pl.when(i > 0)
        def _():
            acc_ref[...] += dw
            ds_ref[...] += dsp

        @pl.when(i == nb - 1)
        def _():
            dw_ref[0] = acc_ref[...].astype(BF16)

    blk = pl.BlockSpec((tm, G), lambda g, i: (i, g))
    wspec = pl.BlockSpec((1, G, G), lambda g, i: (g, 0, 0))
    sspec = pl.BlockSpec((1, G), lambda g, i: (0, g))
    return pl.pallas_call(
        body, name="pool_group_bwd", grid=(NG, nb),
        in_specs=[blk, wspec, sspec, blk, blk],
        out_specs=[blk, wspec, sspec],
        out_shape=[jax.ShapeDtypeStruct((S, D), F32), jax.ShapeDtypeStruct((NG, G, G), BF16),
                   jax.ShapeDtypeStruct((1, D), F32)],
        scratch_shapes=[pltpu.VMEM((G, G), F32)],
        compiler_params=_cparams(("parallel", "arbitrary")),
    )(p, wg, scale, y, dm)


def _pool_mixer_fwd(x, gain, w_in, wg, scale):
    h = _rms_fwd(x, gain, name="pool_norm")
    u = _mm(h, w_in, name="pool_in")
    p = _pool_fwd(u)
    y, yg = _pool_group_fwd(p, wg, scale, x)
    return y, (h, p, yg)


def _pool_mixer_bwd(x, gain, w_in, wg, scale, saved, dy, dyb):
    h, p, yg = saved
    dp, dwg, dscale = _pool_group_bwd(p, wg, scale, yg, dy)
    du = _pool_bwd(dp)
    dh = _mm(du, w_in, tb=True, name="pool_dh")
    dw_in = _mm(h, du, ta=True, out_dtypes=(BF16,), name="pool_dwin")
    dx, dxb, dgain = _rms_bwd(x, gain, dh, dy, name="pool_dnorm")
    return dx, dxb, dgain, dw_in, dwg, dscale


def _qk_norm_fwd(qkv, qg, kg):
    S, D3 = qkv.shape
    D = D3 // 3
    NH = D // ATT_HEAD_DIM
    tr = _tile(S, 256, SUBLANES)

    def body(q_ref, k_ref, v_ref, qg_ref, kg_ref, qo_ref, ko_ref, vo_ref):
        for src, g_ref, dst in ((q_ref, qg_ref, qo_ref), (k_ref, kg_ref, ko_ref)):
            for hd in range(NH):
                cols = slice(hd * ATT_HEAD_DIM, (hd + 1) * ATT_HEAD_DIM)
                t = src[:, cols]
                r = lax.rsqrt(jnp.mean(t * t, axis=-1, keepdims=True) + RMS_EPS)
                dst[:, cols] = (t * r * g_ref[...]).astype(BF16)
        vo_ref[...] = v_ref[...].astype(BF16)

    ins = [(qkv, ('cols', D, 0)), (qkv, ('cols', D, 1)), (qkv, ('cols', D, 2)), (qg, 'full'), (kg, 'full')]
    return _row_call(body, ins, [((S, D), BF16, 'rows')] * 3, name="att_qknorm_fwd", tr=tr, n_rows=S)


def _qk_norm_bwd(qkv, qg, kg, dqn, dkn, dv):
    S, D3 = qkv.shape
    D = D3 // 3
    NH = D // ATT_HEAD_DIM
    tr = _tile(S, 128, SUBLANES)

    def body(q_ref, k_ref, qg_ref, kg_ref, dq_ref, dk_ref, dv_ref, o_ref, dqg_ref, dkg_ref):
        i = pl.program_id(0)
        for sec, (src, g_ref, d_ref, dg_ref) in enumerate(((q_ref, qg_ref, dq_ref, dqg_ref),
                                                            (k_ref, kg_ref, dk_ref, dkg_ref))):
            part = jnp.zeros((1, ATT_HEAD_DIM), F32)
            for hd in range(NH):
                cols = slice(hd * ATT_HEAD_DIM, (hd + 1) * ATT_HEAD_DIM)
                t = src[:, cols]
                r = lax.rsqrt(jnp.mean(t * t, axis=-1, keepdims=True) + RMS_EPS)
                tn = t * r
                d = d_ref[:, cols]
                dn = d * g_ref[...]
                dt = r * (dn - tn * jnp.mean(dn * tn, axis=-1, keepdims=True))
                o_ref[:, sec * D + hd * ATT_HEAD_DIM:sec * D + (hd + 1) * ATT_HEAD_DIM] = dt.astype(BF16)
                part = part + jnp.sum(d * tn, axis=0, keepdims=True)

            @pl.when(i == 0)
            def _():
                dg_ref[...] = part

            @pl.when(i > 0)
            def _():
                dg_ref[...] += part

        o_ref[:, 2 * D:] = dv_ref[...].astype(BF16)

    assert ATT_PAD % tr == 0 and dkn.shape[0] == S + ATT_PAD
    ins = [(qkv, ('cols', D, 0)), (qkv, ('cols', D, 1)), (qg, 'full'), (kg, 'full'), (dqn, 'rows'),
           (dkn, ('rows_from', D, ATT_PAD)), (dv, ('rows_from', D, ATT_PAD))]
    return _row_call(body, ins, [((S, D3), BF16, 'rows'), ((1, ATT_HEAD_DIM), F32, 'acc'),
                                 ((1, ATT_HEAD_DIM), F32, 'acc')], name="att_qknorm_bwd", tr=tr, n_rows=S)


def _att_band_mask():
    r = np.arange(ATT_TQ)[:, None]
    c = np.arange(ATT_TQ + ATT_PAD)[None, :]
    lo = (r // CHUNK) * CHUNK
    return (c >= lo) & (c < lo + ATT_PAD + CHUNK)


def _att_bias_toeplitz(rel_bias):
    H = rel_bias.shape[0]
    R, C = ATT_TQ, ATT_TQ + ATT_PAD
    L = C + R - 1
    assert R - 1 < REL_CLIP
    near = rel_bias[:, REL_CLIP - (R - 1):2 * REL_CLIP][:, ::-1]
    far = jnp.broadcast_to(rel_bias[:, 2 * REL_CLIP:], (H, L - near.shape[1]))
    v = jnp.concatenate([far, near, jnp.zeros((H, 1), rel_bias.dtype)], axis=1)
    skew = jnp.broadcast_to(v[:, None, :], (H, R, L + 1)).reshape(H, R * (L + 1))[:, :R * L].reshape(H, R, L)
    return skew[:, :, R - 1:R - 1 + C]


def _att_bias_tile(rel_bias):
    return jnp.where(_att_band_mask()[None], _att_bias_toeplitz(rel_bias), MASK_VALUE).astype(F32)


def _att_bias_grad(dtile, rel_bias):
    _, pull = jax.vjp(_att_bias_toeplitz, rel_bias)
    return pull(jnp.where(_att_band_mask()[None], dtile, 0.0))[0]


def _att_core_fwd(qn, kp, vp, bias):
    S, D = qn.shape
    NH = D // ATT_HEAD_DIM
    KW = ATT_TQ + ATT_PAD
    scale = ATT_HEAD_DIM ** -0.5

    def body(q_ref, k_ref, v_ref, b_ref, o_ref):
        qb = pl.program_id(1)
        start = pl.multiple_of(qb * ATT_TQ, ATT_TQ)
        ks = k_ref[pl.ds(start, KW), :]
        vs = v_ref[pl.ds(start, KW), :]
        s = lax.dot_general(q_ref[...], ks, (((1,), (1,)), ((), ())), preferred_element_type=F32) * scale + b_ref[0]
        kpos = lax.broadcasted_iota(jnp.int32, (1, KW), 1) + (qb * ATT_TQ - ATT_PAD)
        s = jnp.where(kpos >= 0, s, MASK_VALUE)
        m = jnp.max(s, axis=-1, keepdims=True)
        p = jnp.exp(s - m)
        l = jnp.sum(p, axis=-1, keepdims=True)
        o = jnp.dot(p.astype(BF16), vs, preferred_element_type=F32) / l
        o_ref[...] = o.astype(BF16)

    qspec = pl.BlockSpec((ATT_TQ, ATT_HEAD_DIM), lambda h, qb: (qb, h))
    kvspec = pl.BlockSpec((S + ATT_PAD, ATT_HEAD_DIM), lambda h, qb: (0, h))
    return pl.pallas_call(
        body, name="att_core_fwd", grid=(NH, S // ATT_TQ),
        in_specs=[qspec, kvspec, kvspec, pl.BlockSpec((1, ATT_TQ, KW), lambda h, qb: (h, 0, 0))],
        out_specs=qspec, out_shape=jax.ShapeDtypeStruct((S, D), BF16),
        compiler_params=_cparams(("parallel", "arbitrary")),
    )(qn, kp, vp, bias)


def _att_core_bwd(qn, kp, vp, bias, do):
    S, D = qn.shape
    NH = D // ATT_HEAD_DIM
    KW = ATT_TQ + ATT_PAD
    scale = ATT_HEAD_DIM ** -0.5

    def body(q_ref, k_ref, v_ref, b_ref, do_ref, dq_ref, dk_ref, dv_ref, db_ref):
        qb = pl.program_id(1)
        start = pl.multiple_of(qb * ATT_TQ, ATT_TQ)
        q = q_ref[...]
        dov = do_ref[...]
        ks = k_ref[pl.ds(start, KW), :]
        vs = v_ref[pl.ds(start, KW), :]
        s = lax.dot_general(q, ks, (((1,), (1,)), ((), ())), preferred_element_type=F32) * scale + b_ref[0]
        kpos = lax.broadcasted_iota(jnp.int32, (1, KW), 1) + (qb * ATT_TQ - ATT_PAD)
        s = jnp.where(kpos >= 0, s, MASK_VALUE)
        m = jnp.max(s, axis=-1, keepdims=True)
        e = jnp.exp(s - m)
        p = e / jnp.sum(e, axis=-1, keepdims=True)
        dp = lax.dot_general(dov, vs, (((1,), (1,)), ((), ())), preferred_element_type=F32)
        ds = p * (dp - jnp.sum(p * dp, axis=-1, keepdims=True))
        dsb = ds.astype(BF16)
        dq_ref[...] = jnp.dot(dsb, ks, preferred_element_type=F32) * scale
        dk = lax.dot_general(dsb, q, (((0,), (0,)), ((), ())), preferred_element_type=F32) * scale
        dv = lax.dot_general(p.astype(BF16), dov, (((0,), (0,)), ((), ())), preferred_element_type=F32)

        @pl.when(qb == 0)
        def _():
            dk_ref[...] = jnp.zeros_like(dk_ref)
            dv_ref[...] = jnp.zeros_like(dv_ref)
            db_ref[0] = ds

        @pl.when(qb > 0)
        def _():
            db_ref[0] += ds

        dk_ref[pl.ds(start, KW), :] += dk
        dv_ref[pl.ds(start, KW), :] += dv

    qspec = pl.BlockSpec((ATT_TQ, ATT_HEAD_DIM), lambda h, qb: (qb, h))
    kvspec = pl.BlockSpec((S + ATT_PAD, ATT_HEAD_DIM), lambda h, qb: (0, h))
    bspec = pl.BlockSpec((1, ATT_TQ, KW), lambda h, qb: (h, 0, 0))
    return pl.pallas_call(
        body, name="att_core_bwd", grid=(NH, S // ATT_TQ),
        in_specs=[qspec, kvspec, kvspec, bspec, qspec],
        out_specs=[qspec, kvspec, kvspec, bspec],
        out_shape=[jax.ShapeDtypeStruct((S, D), F32), jax.ShapeDtypeStruct((S + ATT_PAD, D), F32),
                   jax.ShapeDtypeStruct((S + ATT_PAD, D), F32), jax.ShapeDtypeStruct((NH, ATT_TQ, KW), F32)],
        compiler_params=_cparams(("parallel", "arbitrary")),
    )(qn, kp, vp, bias, do)


def _att_mixer_fwd(x, gain, w_qkv, qg, kg, rel_bias, w_out):
    h = _rms_fwd(x, gain, name="att_norm")
    qkv = _mm(h, w_qkv, name="att_qkv")
    qn, kn, v = _qk_norm_fwd(qkv, qg, kg)
    kp = jnp.pad(kn, ((ATT_PAD, 0), (0, 0)))
    vp = jnp.pad(v, ((ATT_PAD, 0), (0, 0)))
    bias = _att_bias_tile(rel_bias)
    o = _att_core_fwd(qn, kp, vp, bias)
    y = _mm(o, w_out, extras=(x,), epilogue=lambda acc, res: (acc + res,), name="att_out")
    return y, (h, qkv, qn, kp, vp, bias, o, rel_bias)


def _att_mixer_bwd(x, gain, w_qkv, qg, kg, w_out, saved, dy, dyb):
    h, qkv, qn, kp, vp, bias, o, rel_bias = saved
    do = _mm(dyb, w_out, tb=True, out_dtypes=(BF16,), name="att_do")
    dw_out = _mm(o, dyb, ta=True, out_dtypes=(BF16,), name="att_dwout")
    dqn, dkp, dvp, dbt = _att_core_bwd(qn, kp, vp, bias, do)
    drel = _att_bias_grad(dbt, rel_bias)
    dqkv, dqg, dkg = _qk_norm_bwd(qkv, qg, kg, dqn, dkp, dvp)
    dh = _mm(dqkv, w_qkv, tb=True, name="att_dh")
    dw_qkv = _mm(h, dqkv, ta=True, out_dtypes=(BF16,), name="att_dwqkv")
    dx, dxb, dgain = _rms_bwd(x, gain, dh, dy, name="att_dnorm")
    return dx, dxb, dgain, dw_qkv, dqg, dkg, drel, dw_out


def _ssm_tables(a_re, a_im, log_dt, b_re, b_im, c_re, c_im):
    G, N = a_re.shape
    NS = G // SSM_SLAB_GROUPS
    lam = lax.complex(a_re, a_im)
    dt = jnp.exp(log_dt)[:, None]
    abar = jnp.exp(lam * dt)
    coef = (abar - 1.0) / lam
    bbar = coef[..., None] * lax.complex(b_re, b_im)
    eye = jnp.eye(SSM_SLAB_GROUPS, dtype=F32)

    def blockdiag(t):
        P, Q = t.shape[1:]
        t = t.reshape(NS, SSM_SLAB_GROUPS, P, Q)
        return jnp.einsum('sgpq,gh->sgphq', t, eye).reshape(NS, SSM_SLAB_GROUPS * P, SSM_SLAB_GROUPS * Q)

    bt = jnp.swapaxes(bbar, 1, 2)
    bmat = jnp.concatenate([blockdiag(jnp.real(bt)), blockdiag(jnp.imag(bt))], axis=2)
    ct = jnp.swapaxes(lax.complex(c_re, c_im), 1, 2)
    cmat = jnp.concatenate([blockdiag(jnp.real(ct)), -blockdiag(jnp.imag(ct))], axis=1)
    al = abar.reshape(NS, 1, SSM_SLAB_STATE)
    rows = jnp.arange(SUBLANES)[None, :, None]
    fwd, bwd = [], []
    for k in (1, 2, 4):
        ak = al ** k
        f = jnp.where(rows >= k, ak, 0.0)
        b = jnp.where(rows < SUBLANES - k, ak, 0.0)
        fwd += [jnp.real(f), jnp.imag(f)]
        bwd += [jnp.real(b), jnp.imag(b)]
    pf = al ** (rows + 1)
    pb = al ** (SUBLANES - rows)
    fwd += [jnp.real(pf), jnp.imag(pf)]
    bwd += [jnp.real(pb), jnp.imag(pb)]
    coef_f = jnp.concatenate(fwd, axis=1).astype(F32)
    coef_b = jnp.concatenate(bwd, axis=1).astype(F32)
    return dict(lam=lam, dt=dt, abar=abar, coef=coef, bmat=bmat.astype(BF16), cmat=cmat.astype(BF16),
                bmat_t=jnp.swapaxes(bmat, 1, 2).astype(BF16), cmat_t=jnp.swapaxes(cmat, 1, 2).astype(BF16),
                coef_f=coef_f, coef_b=coef_b)


def _ssm_scan_fwd(u, tabs, d_skip):
    S, D = u.shape
    NS = D // LANES
    W = 2 * SSM_SLAB_STATE
    T = _tile(S, 512, SUBLANES)
    HS = SSM_SLAB_STATE

    def body(u_ref, bm_ref, cm_ref, cf_ref, d_ref, y_ref, xs_ref, bu_scr, carry_scr):
        i = pl.program_id(1)

        @pl.when(i == 0)
        def _():
            carry_scr[...] = jnp.zeros_like(carry_scr)

        uv = u_ref[...]
        bu_scr[...] = jnp.dot(uv.astype(BF16), bm_ref[0], preferred_element_type=F32)

        def step(r, carry):
            rows = pl.ds(pl.multiple_of(r * SUBLANES, SUBLANES), SUBLANES)
            xr = bu_scr[rows, 0:HS]
            xi = bu_scr[rows, HS:W]
            for n, k in enumerate((1, 2, 4)):
                ar = cf_ref[0, 16 * n:16 * n + 8, :]
                ai = cf_ref[0, 16 * n + 8:16 * n + 16, :]
                sr = pltpu.roll(xr, k, 0)
                si = pltpu.roll(xi, k, 0)
                xr, xi = xr + ar * sr - ai * si, xi + ar * si + ai * sr
            pr = cf_ref[0, 48:56, :]
            pi_ = cf_ref[0, 56:64, :]
            cr, ci = carry
            xr, xi = xr + pr * cr - pi_ * ci, xi + pr * ci + pi_ * cr
            xs_ref[rows, 0:HS] = xr
            xs_ref[rows, HS:W] = xi
            return xr[SUBLANES - 1:SUBLANES, :], xi[SUBLANES - 1:SUBLANES, :]

        cr, ci = lax.fori_loop(0, T // SUBLANES, step, (carry_scr[0:1, 0:HS], carry_scr[0:1, HS:W]), unroll=2)
        carry_scr[0:1, 0:HS] = cr
        carry_scr[0:1, HS:W] = ci
        y_ref[...] = jnp.dot(xs_ref[...].astype(BF16), cm_ref[0], preferred_element_type=F32) + d_ref[...] * uv

    return pl.pallas_call(
        body, name="ssm_scan_fwd", grid=(NS, S // T),
        in_specs=[pl.BlockSpec((T, LANES), lambda j, i: (i, j)),
                  pl.BlockSpec((1, LANES, W), lambda j, i: (j, 0, 0)),
                  pl.BlockSpec((1, W, LANES), lambda j, i: (j, 0, 0)),
                  pl.BlockSpec((1, 8 * SUBLANES, HS), lambda j, i: (j, 0, 0)),
                  pl.BlockSpec((1, LANES), lambda j, i: (0, j))],
        out_specs=[pl.BlockSpec((T, LANES), lambda j, i: (i, j)), pl.BlockSpec((T, W), lambda j, i: (i, j))],
        out_shape=[jax.ShapeDtypeStruct((S, D), F32), jax.ShapeDtypeStruct((S, NS * W), F32)],
        scratch_shapes=[pltpu.VMEM((T, W), F32), pltpu.VMEM((SUBLANES, W), F32)],
        compiler_params=_cparams(("parallel", "arbitrary")),
    )(u, tabs['bmat'], tabs['cmat'], tabs['coef_f'], d_skip)


def _ssm_scan_bwd(u, xs, dy, tabs, d_skip):
    S, D = u.shape
    NS = D // LANES
    W = 2 * SSM_SLAB_STATE
    T = _tile(S, 512, SUBLANES)
    HS = SSM_SLAB_STATE
    nb = S // T

    def body(u_ref, xs_ref, dy_ref, bt_ref, ct_ref, cf_ref, d_ref, du_ref, gb_ref, gc_ref, q_ref,
             cy_scr, lam_scr, carry_scr):
        i = pl.program_id(1)

        @pl.when(i == 0)
        def _():
            carry_scr[...] = jnp.zeros_like(carry_scr)
            gb_ref[...] = jnp.zeros_like(gb_ref)
            gc_ref[...] = jnp.zeros_like(gc_ref)
            q_ref[...] = jnp.zeros_like(q_ref)

        dyv = dy_ref[...]
        dyb = dyv.astype(BF16)
        cy_scr[...] = jnp.dot(dyb, ct_ref[0], preferred_element_type=F32)

        def step(n, carry):
            r = T // SUBLANES - 1 - n
            rows = pl.ds(pl.multiple_of(r * SUBLANES, SUBLANES), SUBLANES)
            cyr = cy_scr[rows, 0:HS]
            cyi = cy_scr[rows, HS:W]
            lr, li = cyr, cyi
            for m, k in enumerate((1, 2, 4)):
                br = cf_ref[0, 16 * m:16 * m + 8, :]
                bi = cf_ref[0, 16 * m + 8:16 * m + 16, :]
                sr = pltpu.roll(lr, SUBLANES - k, 0)
                si = pltpu.roll(li, SUBLANES - k, 0)
                lr, li = lr + br * sr + bi * si, li + br * si - bi * sr
            pr = cf_ref[0, 48:56, :]
            pi_ = cf_ref[0, 56:64, :]
            cr, ci, qr, qi = carry
            lr, li = lr + pr * cr + pi_ * ci, li + pr * ci - pi_ * cr
            lam_scr[rows, 0:HS] = lr
            lam_scr[rows, HS:W] = li
            mr, mi = lr - cyr, li - cyi
            xr = xs_ref[rows, 0:HS]
            xi = xs_ref[rows, HS:W]
            return lr[0:1, :], li[0:1, :], qr + mr * xr + mi * xi, qi + mi * xr - mr * xi

        zero = jnp.zeros((SUBLANES, HS), F32)
        cr, ci, qr, qi = lax.fori_loop(0, T // SUBLANES, step,
                                       (carry_scr[0:1, 0:HS], carry_scr[0:1, HS:W], zero, zero), unroll=2)
        carry_scr[0:1, 0:HS] = cr
        carry_scr[0:1, HS:W] = ci
        q_ref[0, :, 0:HS] += qr
        q_ref[0, :, HS:W] += qi
        lamb = lam_scr[...].astype(BF16)
        uv = u_ref[...]
        du_ref[...] = jnp.dot(lamb, bt_ref[0], preferred_element_type=F32) + d_ref[...] * dyv
        gb_ref[0] += lax.dot_general(lamb, uv.astype(BF16), (((0,), (0,)), ((), ())), preferred_element_type=F32)
        gc_ref[0] += lax.dot_general(xs_ref[...].astype(BF16), dyb, (((0,), (0,)), ((), ())),
                                     preferred_element_type=F32)

    rev = lambda j, i: (nb - 1 - i, j)
    slab3 = lambda j, i: (j, 0, 0)
    return pl.pallas_call(
        body, name="ssm_scan_bwd", grid=(NS, nb),
        in_specs=[pl.BlockSpec((T, LANES), rev), pl.BlockSpec((T, W), rev), pl.BlockSpec((T, LANES), rev),
                  pl.BlockSpec((1, W, LANES), slab3), pl.BlockSpec((1, LANES, W), slab3),
                  pl.BlockSpec((1, 8 * SUBLANES, HS), slab3), pl.BlockSpec((1, LANES), lambda j, i: (0, j))],
        out_specs=[pl.BlockSpec((T, LANES), rev), pl.BlockSpec((1, W, LANES), slab3),
                   pl.BlockSpec((1, W, LANES), slab3), pl.BlockSpec((1, SUBLANES, W), slab3)],
        out_shape=[jax.ShapeDtypeStruct((S, D), F32), jax.ShapeDtypeStruct((NS, W, LANES), F32),
                   jax.ShapeDtypeStruct((NS, W, LANES), F32), jax.ShapeDtypeStruct((NS, SUBLANES, W), F32)],
        scratch_shapes=[pltpu.VMEM((T, W), F32), pltpu.VMEM((T, W), F32), pltpu.VMEM((SUBLANES, W), F32)],
        compiler_params=_cparams(("parallel", "arbitrary")),
    )(u, xs, dy, tabs['bmat_t'], tabs['cmat_t'], tabs['coef_b'], d_skip)


def _ssm_param_grads(tabs, b_re, b_im, gb, gc, q):
    NS = gb.shape[0]
    G = NS * SSM_SLAB_GROUPS
    N, C = SSM_STATE, SSM_GROUP

    def diag_blocks(t):
        t = t.reshape(NS, SSM_SLAB_GROUPS, N, SSM_SLAB_GROUPS, C)
        t = jnp.einsum('sgnhc,gh->sgnc', t, jnp.eye(SSM_SLAB_GROUPS, dtype=F32))
        return t.reshape(G, N, C)

    HS = SSM_SLAB_STATE
    g_bbar = lax.complex(diag_blocks(gb[:, :HS]), diag_blocks(gb[:, HS:]))
    g_c = lax.complex(diag_blocks(gc[:, :HS]), -diag_blocks(gc[:, HS:]))
    qs = jnp.sum(q, axis=1)
    qc = lax.complex(qs[:, :HS], qs[:, HS:]).reshape(G, N)
    lam, dt, abar, coef = tabs['lam'], tabs['dt'], tabs['abar'], tabs['coef']
    bmat = lax.complex(b_re, b_im)
    g_b = g_bbar * jnp.conj(coef)[..., None]
    g_coef = jnp.sum(g_bbar * jnp.conj(bmat), axis=-1)
    g_abar_coef = g_coef * jnp.conj(1.0 / lam)
    g_lam = g_coef * jnp.conj(-(abar - 1.0) / (lam * lam))
    g_ld = qc + jnp.conj(abar) * g_abar_coef
    g_lam = g_lam + g_ld * dt
    g_dt = jnp.sum(jnp.real(g_ld * jnp.conj(lam)), axis=-1)
    g_logdt = g_dt * dt[:, 0]
    g_ct = jnp.swapaxes(g_c, 1, 2)
    return (jnp.real(g_lam), jnp.imag(g_lam), g_logdt, jnp.real(g_b), jnp.imag(g_b), jnp.real(g_ct), jnp.imag(g_ct))


_GELU_C = math.sqrt(2.0 / math.pi)


def _gelu_fwd(y):
    S, D = y.shape

    def body(y_ref, z_ref):
        v = y_ref[...]
        z_ref[...] = (0.5 * v * (1.0 + jnp.tanh(_GELU_C * (v + 0.044715 * v * v * v)))).astype(BF16)

    return _row_call(body, [(y, 'rows')], [((S, D), BF16, 'rows')], name="ssm_gelu_fwd",
                     tr=_tile(S, 512, SUBLANES), n_rows=S)[0]


def _gelu_bwd(y, dz, u):
    S, D = y.shape

    def body(y_ref, dz_ref, u_ref, dy_ref, dd_ref):
        i = pl.program_id(0)
        v = y_ref[...]
        t = jnp.tanh(_GELU_C * (v + 0.044715 * v * v * v))
        g = 0.5 * (1.0 + t) + 0.5 * v * (1.0 - t * t) * _GELU_C * (1.0 + 3 * 0.044715 * v * v)
        dy = dz_ref[...] * g
        dy_ref[...] = dy
        part = jnp.sum(dy * u_ref[...], axis=0, keepdims=True)

        @pl.when(i == 0)
        def _():
            dd_ref[...] = part

        @pl.when(i > 0)
        def _():
            dd_ref[...] += part

    return _row_call(body, [(y, 'rows'), (dz, 'rows'), (u, 'rows')], [((S, D), F32, 'rows'), ((1, D), F32, 'acc')],
                     name="ssm_gelu_bwd", tr=_tile(S, 256, SUBLANES), n_rows=S)


def _glu_fwd(zz, x):
    S, D = x.shape

    def body(a_ref, g_ref, x_ref, o_ref):
        o_ref[...] = x_ref[...] + a_ref[...] * jax.nn.sigmoid(g_ref[...])

    return _row_call(body, [(zz, ('cols', D, 0)), (zz, ('cols', D, 1)), (x, 'rows')], [((S, D), F32, 'rows')],
                     name="ssm_glu_fwd", tr=_tile(S, 256, SUBLANES), n_rows=S)[0]


def _glu_bwd(zz, dm):
    S, D = dm.shape

    def body(a_ref, g_ref, dm_ref, o_ref):
        s = jax.nn.sigmoid(g_ref[...])
        d = dm_ref[...]
        o_ref[:, 0:D] = (d * s).astype(BF16)
        o_ref[:, D:] = (d * a_ref[...] * s * (1.0 - s)).astype(BF16)

    return _row_call(body, [(zz, ('cols', D, 0)), (zz, ('cols', D, 1)), (dm, 'rows')], [((S, 2 * D), BF16, 'rows')],
                     name="ssm_glu_bwd", tr=_tile(S, 256, SUBLANES), n_rows=S)[0]


def _ssm_mixer_fwd(x, gain, tabs, d_skip, w_glu):
    _, u = _rms_fwd(x, gain, name="ssm_norm", with_f32=True)
    yv, xs = _ssm_scan_fwd(u, tabs, d_skip)
    z = _gelu_fwd(yv)
    zz = _mm(z, w_glu, name="ssm_glu_in")
    y = _glu_fwd(zz, x)
    return y, (u, xs, yv, z, zz)


def _ssm_mixer_bwd(x, gain, tabs, d_skip, w_glu, b_re, b_im, saved, dy, dyb):
    u, xs, yv, z, zz = saved
    dzz = _glu_bwd(zz, dy)
    dz = _mm(dzz, w_glu, tb=True, name="ssm_dz")
    dw_glu = _mm(z, dzz, ta=True, out_dtypes=(BF16,), name="ssm_dwglu")
    dyv, dd = _gelu_bwd(yv, dz, u)
    du, gb, gc, q = _ssm_scan_bwd(u, xs, dyv, tabs, d_skip)
    small = _ssm_param_grads(tabs, b_re, b_im, gb, gc, q)
    dx, dxb, dgain = _rms_bwd(x, gain, du, dy, name="ssm_dnorm")
    return dx, dxb, dgain, small, dd, dw_glu


def _local_step(x, target, p):
    depth = p['norm_mix'].shape[0]
    tabs = _ssm_tables(p['ssm_a_re'], p['ssm_a_im'], p['ssm_log_dt'], p['ssm_b_re'], p['ssm_b_im'], p['ssm_c_re'],
                       p['ssm_c_im'])
    xs_in, saved_mix, saved_mlp = [], [], []
    for i in range(depth):
        gm = p['norm_mix'][i:i + 1]
        xs_in.append(x)
        if i % 4 == 0:
            x, sv = _conv_mixer_fwd(x, gm, p['conv_w_in'], p['conv_w'], p['conv_w_out'])
        elif i % 4 == 1:
            x, sv = _pool_mixer_fwd(x, gm, p['pool_w_in'], p['pool_w_group'], p['pool_scale'])
        elif i % 4 == 2:
            x, sv = _att_mixer_fwd(x, gm, p['att_w_qkv'], p['att_q_norm'], p['att_k_norm'], p['att_rel_bias'],
                                   p['att_w_out'])
        else:
            x, sv = _ssm_mixer_fwd(x, gm, tabs, p['ssm_d'], p['ssm_w_glu'])
        saved_mix.append(sv)
        xs_in.append(x)
        x, sv = _mlp_fwd(x, p['norm_mlp'][i:i + 1], p['mlp_w1'][i], p['mlp_w2'][i], tag=str(i))
        saved_mlp.append(sv)
    dx, dxb, loss_cols = _loss_head(x, target, name="loss_head")
    g = {'norm_mix': [None] * depth, 'norm_mlp': [None] * depth, 'mlp_w1': [None] * depth, 'mlp_w2': [None] * depth}
    for i in reversed(range(depth)):
        dx, dxb, g['norm_mlp'][i], g['mlp_w1'][i], g['mlp_w2'][i] = _mlp_bwd(
            xs_in[2 * i + 1], p['norm_mlp'][i:i + 1], p['mlp_w1'][i], p['mlp_w2'][i], saved_mlp[i], dx, dxb, tag=str(i))
        gm = p['norm_mix'][i:i + 1]
        xin, sv = xs_in[2 * i], saved_mix[i]
        if i % 4 == 0:
            dx, dxb, g['norm_mix'][i], g['conv_w_in'], g['conv_w'], g['conv_w_out'] = _conv_mixer_bwd(
                xin, gm, p['conv_w_in'], p['conv_w'], p['conv_w_out'], sv, dx, dxb)
        elif i % 4 == 1:
            dx, dxb, g['norm_mix'][i], g['pool_w_in'], g['pool_w_group'], g['pool_scale'] = _pool_mixer_bwd(
                xin, gm, p['pool_w_in'], p['pool_w_group'], p['pool_scale'], sv, dx, dxb)
        elif i % 4 == 2:
            (dx, dxb, g['norm_mix'][i], g['att_w_qkv'], g['att_q_norm'], g['att_k_norm'], g['att_rel_bias'],
             g['att_w_out']) = _att_mixer_bwd(xin, gm, p['att_w_qkv'], p['att_q_norm'], p['att_k_norm'],
                                              p['att_w_out'], sv, dx, dxb)
        else:
            dx, dxb, g['norm_mix'][i], small, g['ssm_d'], g['ssm_w_glu'] = _ssm_mixer_bwd(
                xin, gm, tabs, p['ssm_d'], p['ssm_w_glu'], p['ssm_b_re'], p['ssm_b_im'], sv, dx, dxb)
            (g['ssm_a_re'], g['ssm_a_im'], g['ssm_log_dt'], g['ssm_b_re'], g['ssm_b_im'], g['ssm_c_re'],
             g['ssm_c_im']) = small
    g['norm_mix'] = jnp.concatenate(g['norm_mix'], axis=0)
    g['norm_mlp'] = jnp.concatenate(g['norm_mlp'], axis=0)
    return loss_cols, dx, g


_ANY = pl.BlockSpec(memory_space=pl.ANY)
_VM = pl.BlockSpec(memory_space=pltpu.VMEM)
_REL_ALL = [(0, 0, 1), (0, 1, 0), (0, 1, 1), (1, 0, 0), (1, 0, 1), (1, 1, 0), (1, 1, 1)]
_REL_CHIPS = [(1, 0, 0), (0, 1, 0), (1, 1, 0)]


def _me():
    return lax.axis_index("x"), lax.axis_index("y"), lax.axis_index("c")


def _flip(pos, rel):
    return tuple(1 - p if r else p for p, r in zip(pos, rel))


def _chip_of(pos):
    return 2 * pos[0] + pos[1]


def _dev_of(pos):
    return 4 * pos[0] + 2 * pos[1] + pos[2]


def _gather_small(buf, *, reduce, name):
    rows = buf.shape[0]

    def body(in_ref, out_ref, *rest):
        if reduce:
            gath, send_sems, recv_sems = rest
        else:
            gath = out_ref
            send_sems, recv_sems = rest
        me = _me()
        gath[_dev_of(me)] = in_ref[...]
        copies = []
        for k, rel in enumerate(_REL_ALL):
            peer = _flip(me, rel)
            cp = pltpu.make_async_remote_copy(src_ref=in_ref, dst_ref=gath.at[_dev_of(me)], send_sem=send_sems.at[k],
                                              recv_sem=recv_sems.at[k], device_id=peer, device_id_type=MESH)
            cp.start()
            copies.append(cp)
        for k, rel in enumerate(_REL_ALL):
            peer = _flip(me, rel)
            pltpu.make_async_remote_copy(src_ref=in_ref, dst_ref=gath.at[_dev_of(peer)], send_sem=send_sems.at[k],
                                         recv_sem=recv_sems.at[k], device_id=peer, device_id_type=MESH).wait_recv()
        for cp in copies:
            cp.wait_send()
        if reduce:
            acc = gath[0]
            for s in range(1, N_DEV):
                acc = acc + gath[s]
            out_ref[...] = acc

    out_shape = (rows, LANES) if reduce else (N_DEV, rows, LANES)
    scratch = ([pltpu.VMEM((N_DEV, rows, LANES), F32)] if reduce else []) + [
        pltpu.SemaphoreType.DMA((len(_REL_ALL),)), pltpu.SemaphoreType.DMA((len(_REL_ALL),))]
    return pl.pallas_call(
        body, name=name, in_specs=[_VM], out_specs=_VM, out_shape=jax.ShapeDtypeStruct(out_shape, F32),
        scratch_shapes=scratch, compiler_params=pltpu.CompilerParams(vmem_limit_bytes=VMEM_LIMIT),
    )(buf)


def _region(ref, kind, k, shard_shape, half, quarter=None):
    r, c = shard_shape
    n = r // 2 if quarter is None else r // 4
    start = half * (r // 2) + (0 if quarter is None else quarter * n)
    if kind == 'col':
        return ref.at[pl.ds(pl.multiple_of(start, 16), n), pl.ds(pl.multiple_of(k * c, LANES), c)]
    return ref.at[pl.ds(pl.multiple_of(k * r + start, 16), n), :]


def _place_block(shard, dest, layer, chip, kind, *, name):
    _, r, c = shard.shape
    tr = _tile(r, max(16, (1 << 20) // c // 16 * 16), 16)
    nb = r // tr

    def body(chip_ref, s_ref, d_ref, o_ref):
        o_ref[...] = s_ref[0].astype(BF16)

    out_map = (lambda i, k: (i, k[0])) if kind == 'col' else (lambda i, k: (k[0] * nb + i, 0))
    grid_spec = pltpu.PrefetchScalarGridSpec(
        num_scalar_prefetch=1, grid=(nb,),
        in_specs=[pl.BlockSpec((1, tr, c), lambda i, k: (layer, i, 0)), _ANY],
        out_specs=pl.BlockSpec((tr, c), out_map))
    return pl.pallas_call(
        body, name=name, grid_spec=grid_spec, out_shape=jax.ShapeDtypeStruct(dest.shape, BF16),
        input_output_aliases={2: 0}, compiler_params=_cparams(("arbitrary",)),
    )(chip, shard, dest)


def _all_gather_matrices(fulls, kinds, *, name):
    n = len(fulls)
    shards = [(f.shape[0], f.shape[1] // N_CHIPS) if kd == 'col' else (f.shape[0] // N_CHIPS, f.shape[1])
              for f, kd in zip(fulls, kinds)]

    def body(*refs):
        in_refs, out_refs = refs[:n], refs[n:2 * n]
        ici_send, ici_recv, fwd_send, fwd_recv, d2d_send, d2d_recv = refs[2 * n:]
        me = _me()
        core = me[2]
        sib = _flip(me, (0, 0, 1))
        nbr = [_flip(me, (1, 0, 0)), _flip(me, (0, 1, 0))]
        diag = _chip_of(_flip(me, (1, 1, 0)))
        pending = []

        def send(src, dst, sems_s, sems_r, idx, to):
            cp = pltpu.make_async_remote_copy(src_ref=src, dst_ref=dst, send_sem=sems_s.at[idx], recv_sem=sems_r.at[idx],
                                              device_id=to, device_id_type=MESH)
            cp.start()
            pending.append(cp)

        def arrived(where, sems_s, sems_r, idx, frm):
            pltpu.make_async_remote_copy(src_ref=where, dst_ref=where, send_sem=sems_s.at[idx], recv_sem=sems_r.at[idx],
                                         device_id=frm, device_id_type=MESH).wait_recv()

        for m in range(n):
            src = _region(in_refs[m], kinds[m], _chip_of(me), shards[m], core)
            dst = _region(out_refs[m], kinds[m], _chip_of(me), shards[m], core)
            for j in range(2):
                send(src, dst, ici_send, ici_recv, (m, j), nbr[j])
        for m in range(n):
            for j in range(2):
                landed = _region(out_refs[m], kinds[m], _chip_of(nbr[j]), shards[m], core)
                arrived(landed, ici_send, ici_recv, (m, j), nbr[j])
                part = _region(out_refs[m], kinds[m], _chip_of(nbr[j]), shards[m], core, quarter=j)
                send(part, part, fwd_send, fwd_recv, (m, j), nbr[1 - j])
                send(landed, landed, d2d_send, d2d_recv, (m, j), sib)
        for m in range(n):
            for j in range(2):
                part = _region(out_refs[m], kinds[m], diag, shards[m], core, quarter=j)
                arrived(part, fwd_send, fwd_recv, (m, j), nbr[1 - j])
                send(part, part, d2d_send, d2d_recv, (m, 2 + j), sib)
        for m in range(n):
            for j in range(2):
                arrived(_region(out_refs[m], kinds[m], _chip_of(nbr[j]), shards[m], 1 - core),
                        d2d_send, d2d_recv, (m, j), sib)
                arrived(_region(out_refs[m], kinds[m], diag, shards[m], 1 - core, quarter=j),
                        d2d_send, d2d_recv, (m, 2 + j), sib)
        for cp in pending:
            cp.wait_send()

    return pl.pallas_call(
        body, name=name, in_specs=[_ANY] * n, out_specs=[_ANY] * n,
        out_shape=[jax.ShapeDtypeStruct(f.shape, BF16) for f in fulls],
        input_output_aliases={m: m for m in range(n)},
        scratch_shapes=[pltpu.SemaphoreType.DMA((n, 2))] * 4 + [pltpu.SemaphoreType.DMA((n, 4))] * 2,
    )(*fulls)


def _piece_shape(full, kind):
    R, C = full
    return (R // 2, C // N_CHIPS) if kind == 'col' else (R // N_DEV, C)


def _piece(ref, kind, k, c, full):
    pr, pc = _piece_shape(full, kind)
    if kind == 'col':
        return ref.at[pl.ds(pl.multiple_of(c * pr, 16), pr), pl.ds(pl.multiple_of(k * pc, LANES), pc)]
    return ref.at[pl.ds(pl.multiple_of((2 * k + c) * pr, 16), pr), :]


def _pair_exchange(grads, kinds, *, name):
    n = len(grads)
    pieces = [_piece_shape(g.shape, kd) for g, kd in zip(grads, kinds)]

    def body(*refs):
        in_refs, out_refs = refs[:n], refs[n:2 * n]
        send_sems, recv_sems = refs[2 * n:]
        me = _me()
        core = me[2]
        sib = _flip(me, (0, 0, 1))
        sends = []
        for m in range(n):
            for k in range(N_CHIPS):
                cp = pltpu.make_async_remote_copy(
                    src_ref=_piece(in_refs[m], kinds[m], k, 1 - core, grads[m].shape), dst_ref=out_refs[m].at[k],
                    send_sem=send_sems.at[m, k], recv_sem=recv_sems.at[m, k], device_id=sib, device_id_type=MESH)
                cp.start()
                sends.append(cp)
        for m in range(n):
            for k in range(N_CHIPS):
                pltpu.make_async_remote_copy(
                    src_ref=_piece(in_refs[m], kinds[m], k, core, grads[m].shape), dst_ref=out_refs[m].at[k],
                    send_sem=send_sems.at[m, k], recv_sem=recv_sems.at[m, k], device_id=sib,
                    device_id_type=MESH).wait_recv()
        for cp in sends:
            cp.wait_send()

    return pl.pallas_call(
        body, name=name, in_specs=[_ANY] * n, out_specs=[_ANY] * n,
        out_shape=[jax.ShapeDtypeStruct((N_CHIPS,) + p, BF16) for p in pieces],
        scratch_shapes=[pltpu.SemaphoreType.DMA((n, N_CHIPS)), pltpu.SemaphoreType.DMA((n, N_CHIPS))],
    )(*grads)


def _pair_sum(g, recv, core, kind, *, name):
    pr, pc = _piece_shape(g.shape, kind)
    tr = _tile(pr, max(16, (1 << 19) // pc // 16 * 16), 16)
    nb = pr // tr

    def body(core_ref, g_ref, r_ref, o_ref):
        o_ref[0] = (g_ref[...].astype(F32) + r_ref[0].astype(F32)).astype(BF16)

    g_map = (lambda k, i, c: (c[0] * nb + i, k)) if kind == 'col' else (lambda k, i, c: ((2 * k + c[0]) * nb + i, 0))
    slot = pl.BlockSpec((1, tr, pc), lambda k, i, c: (k, i, 0))
    grid_spec = pltpu.PrefetchScalarGridSpec(
        num_scalar_prefetch=1, grid=(N_CHIPS, nb), in_specs=[pl.BlockSpec((tr, pc), g_map), slot], out_specs=slot)
    return pl.pallas_call(
        body, name=name, grid_spec=grid_spec, out_shape=jax.ShapeDtypeStruct((N_CHIPS, pr, pc), BF16),
        compiler_params=_cparams(("arbitrary", "arbitrary")),
    )(core, g, recv)


def _chip_scatter(sums, *, name):
    n = len(sums)
    nj = len(_REL_CHIPS)

    def body(*refs):
        in_refs, out_refs, transit = refs[:n], refs[n:2 * n], refs[2 * n:3 * n]
        ici_send, ici_recv, hop_send, hop_recv, fwd_send, fwd_recv = refs[3 * n:]
        me = _me()
        nbr = [_flip(me, (1, 0, 0)), _flip(me, (0, 1, 0))]
        diag = _chip_of(_flip(me, (1, 1, 0)))
        pending = []

        def send(src, dst, sems_s, sems_r, idx, to):
            cp = pltpu.make_async_remote_copy(src_ref=src, dst_ref=dst, send_sem=sems_s.at[idx], recv_sem=sems_r.at[idx],
                                              device_id=to, device_id_type=MESH)
            cp.start()
            pending.append(cp)

        def arrived(where, sems_s, sems_r, idx, frm):
            pltpu.make_async_remote_copy(src_ref=where, dst_ref=where, send_sem=sems_s.at[idx], recv_sem=sems_r.at[idx],
                                         device_id=frm, device_id_type=MESH).wait_recv()

        def half_rows(j, m):
            h = sums[m].shape[1] // 2
            return pl.ds(j * h, h)

        for m in range(n):
            for j in range(2):
                send(in_refs[m].at[_chip_of(nbr[j])], out_refs[m].at[j], ici_send, ici_recv, (m, j), nbr[j])
                send(in_refs[m].at[diag, half_rows(j, m), :], transit[m].at[j], hop_send, hop_recv, (m, j), nbr[j])
        for m in range(n):
            for j in range(2):
                arrived(transit[m].at[j], hop_send, hop_recv, (m, j), nbr[j])
                send(transit[m].at[j], out_refs[m].at[2, half_rows(j, m), :], fwd_send, fwd_recv, (m, j), nbr[1 - j])
        for m in range(n):
            for j in range(2):
                arrived(out_refs[m].at[j], ici_send, ici_recv, (m, j), nbr[j])
                arrived(out_refs[m].at[2, half_rows(j, m), :], fwd_send, fwd_recv, (m, j), nbr[1 - j])
        for cp in pending:
            cp.wait_send()

    res = pl.pallas_call(
        body, name=name, in_specs=[_ANY] * n, out_specs=[_ANY] * (2 * n),
        out_shape=[jax.ShapeDtypeStruct((nj,) + s.shape[1:], BF16) for s in sums]
        + [jax.ShapeDtypeStruct((2, s.shape[1] // 2, s.shape[2]), BF16) for s in sums],
        scratch_shapes=[pltpu.SemaphoreType.DMA((n, 2))] * 6,
    )(*sums)
    return res[:n]


def _sum_into(own, recv, dest, layer, core, chip, *, name):
    _, pr, pc = own.shape
    tr = _tile(pr, max(16, (1 << 19) // pc // 16 * 16), 16)
    nb = pr // tr

    def body(core_ref, chip_ref, own_ref, r_ref, d_ref, o_ref):
        acc = own_ref[0].astype(F32)
        for s in range(len(_REL_CHIPS)):
            acc = acc + r_ref[s].astype(F32)
        o_ref[0] = acc

    grid_spec = pltpu.PrefetchScalarGridSpec(
        num_scalar_prefetch=2, grid=(nb,),
        in_specs=[pl.BlockSpec((1, tr, pc), lambda i, c, k: (k[0], i, 0)),
                  pl.BlockSpec((len(_REL_CHIPS), tr, pc), lambda i, c, k: (0, i, 0)), _ANY],
        out_specs=pl.BlockSpec((1, tr, pc), lambda i, c, k: (layer, c[0] * nb + i, 0)))
    return pl.pallas_call(
        body, name=name, grid_spec=grid_spec, out_shape=jax.ShapeDtypeStruct(dest.shape, F32),
        input_output_aliases={4: 0}, compiler_params=_cparams(("arbitrary",)),
    )(core, chip, own, recv, dest)


def _exchange_halves(blocks, *, name):
    n_out = len(blocks)
    n = sum(b.shape[0] for b in blocks)

    def body(*refs):
        in_refs, out_refs = refs[:n_out], refs[n_out:2 * n_out]
        send_sems, recv_sems = refs[2 * n_out:]
        me = _me()
        sib = _flip(me, (0, 0, 1))
        sends, m = [], 0
        for o in range(n_out):
            L, r2, _ = blocks[o].shape
            hr = r2 // 2
            for l in range(L):
                rows = pl.ds(pl.multiple_of(me[2] * hr, SUBLANES), hr)
                cp = pltpu.make_async_remote_copy(src_ref=in_refs[o].at[l, rows, :], dst_ref=out_refs[o].at[l, rows, :],
                                                  send_sem=send_sems.at[m], recv_sem=recv_sems.at[m], device_id=sib,
                                                  device_id_type=MESH)
                cp.start()
                sends.append(cp)
                m += 1
        m = 0
        for o in range(n_out):
            L, r2, _ = blocks[o].shape
            hr = r2 // 2
            for l in range(L):
                rows = pl.ds(pl.multiple_of(sib[2] * hr, SUBLANES), hr)
                theirs = out_refs[o].at[l, rows, :]
                pltpu.make_async_remote_copy(src_ref=theirs, dst_ref=theirs, send_sem=send_sems.at[m],
                                             recv_sem=recv_sems.at[m], device_id=sib, device_id_type=MESH).wait_recv()
                m += 1
        for cp in sends:
            cp.wait_send()

    return pl.pallas_call(
        body, name=name, in_specs=[_ANY] * n_out, out_specs=[_ANY] * n_out,
        out_shape=[jax.ShapeDtypeStruct(b.shape, F32) for b in blocks],
        input_output_aliases={o: o for o in range(n_out)},
        scratch_shapes=[pltpu.SemaphoreType.DMA((n,)), pltpu.SemaphoreType.DMA((n,))],
    )(*blocks)


def _adamw(w, g, m, v, *, name):
    R, C = w.shape
    tr = _tile(R, max(SUBLANES, (1 << 19) // C // SUBLANES * SUBLANES), SUBLANES)
    c1 = 1.0 / (1.0 - ADAM_B1 ** ADAM_STEP)
    c2 = 1.0 / (1.0 - ADAM_B2 ** ADAM_STEP)

    def body(w_ref, g_ref, m_ref, v_ref, go_ref, d_ref, nm_ref, nv_ref):
        gv = g_ref[...]
        go_ref[...] = gv
        nm = ADAM_B1 * m_ref[...] + (1.0 - ADAM_B1) * gv
        nv = ADAM_B2 * v_ref[...] + (1.0 - ADAM_B2) * (gv * gv)
        nm_ref[...] = nm
        nv_ref[...] = nv
        d_ref[...] = -ADAM_LR * ((nm * c1) / (jnp.sqrt(nv * c2) + ADAM_EPS) + ADAM_WD * w_ref[...])

    blk = pl.BlockSpec((tr, C), lambda i: (i, 0))
    return pl.pallas_call(
        body, name=name, grid=(R // tr,), in_specs=[blk] * 4, out_specs=[blk] * 4,
        out_shape=[jax.ShapeDtypeStruct((R, C), F32)] * 4, compiler_params=_cparams(("parallel",)),
    )(w, g, m, v)


_BIG = ['mlp_w1', 'mlp_w2', 'conv_w_in', 'conv_w_out', 'pool_w_in', 'pool_w_group', 'att_w_qkv', 'att_w_out',
        'ssm_w_glu']
_KIND = {'mlp_w1': 'col', 'mlp_w2': 'row', 'conv_w_in': 'col', 'conv_w_out': 'row', 'pool_w_in': 'row',
         'pool_w_group': 'row', 'att_w_qkv': 'col', 'att_w_out': 'row', 'ssm_w_glu': 'col'}
_TINY_SHARDED = ['conv_w', 'pool_scale', 'ssm_d']
_REPLICATED = ['norm_mix', 'norm_mlp', 'att_q_norm', 'att_k_norm', 'att_rel_bias', 'ssm_a_re', 'ssm_a_im',
               'ssm_log_dt', 'ssm_b_re', 'ssm_b_im', 'ssm_c_re', 'ssm_c_im']
_SMALL = _REPLICATED + _TINY_SHARDED
_ORDER = ['norm_mix', 'norm_mlp', 'mlp_w1', 'mlp_w2', 'conv_w_in', 'conv_w', 'conv_w_out', 'pool_w_in',
          'pool_w_group', 'pool_scale', 'att_w_qkv', 'att_q_norm', 'att_k_norm', 'att_rel_bias', 'att_w_out',
          'ssm_a_re', 'ssm_a_im', 'ssm_log_dt', 'ssm_b_re', 'ssm_b_im', 'ssm_c_re', 'ssm_c_im', 'ssm_d', 'ssm_w_glu']
_LAYER_OF = {'conv_w_in': 0, 'conv_w_out': 0, 'pool_w_in': 1, 'pool_w_group': 1, 'att_w_qkv': 2, 'att_w_out': 2,
             'ssm_w_glu': 3}


def _pack(arrays):
    flat = jnp.concatenate([a.reshape(-1).astype(F32) for a in arrays])
    n = flat.shape[0]
    total = -(-n // (SUBLANES * LANES)) * (SUBLANES * LANES)
    return jnp.pad(flat, (0, total - n)).reshape(total // LANES, LANES)


def _unpack(buf, shapes):
    flat = buf.reshape(-1)
    out, off = [], 0
    for s in shapes:
        n = int(np.prod(s))
        out.append(flat[off:off + n].reshape(s))
        off += n
    return out


def _matrices(w, name):
    t = w.reshape((-1,) + w.shape[-2:])
    return [t[l] for l in range(t.shape[0])]


def kernel(x, norm_mix, norm_mlp, mlp_w1, mlp_w2, conv_w_in, conv_w, conv_w_out, pool_w_in, pool_w_group, pool_scale, att_w_qkv, att_q_norm, att_k_norm, att_rel_bias, att_w_out, ssm_a_re, ssm_a_im, ssm_log_dt, ssm_b_re, ssm_b_im, ssm_c_re, ssm_c_im, ssm_d, ssm_w_glu, loss_target, m_norm_mix, m_norm_mlp, m_mlp_w1, m_mlp_w2, m_conv_w_in, m_conv_w, m_conv_w_out, m_pool_w_in, m_pool_w_group, m_pool_scale, m_att_w_qkv, m_att_q_norm, m_att_k_norm, m_att_rel_bias, m_att_w_out, m_ssm_a_re, m_ssm_a_im, m_ssm_log_dt, m_ssm_b_re, m_ssm_b_im, m_ssm_c_re, m_ssm_c_im, m_ssm_d, m_ssm_w_glu, v_norm_mix, v_norm_mlp, v_mlp_w1, v_mlp_w2, v_conv_w_in, v_conv_w, v_conv_w_out, v_pool_w_in, v_pool_w_group, v_pool_scale, v_att_w_qkv, v_att_q_norm, v_att_k_norm, v_att_rel_bias, v_att_w_out, v_ssm_a_re, v_ssm_a_im, v_ssm_log_dt, v_ssm_b_re, v_ssm_b_im, v_ssm_c_re, v_ssm_c_im, v_ssm_d, v_ssm_w_glu):
    args = dict(locals())
    W = {n: args[n] for n in _ORDER}
    M = {n: args['m_' + n] for n in _ORDER}
    V = {n: args['v_' + n] for n in _ORDER}
    depth = norm_mix.shape[0]
    d_model = x.shape[-1]
    chip = 2 * lax.axis_index("x") + lax.axis_index("y")

    tiny = _gather_small(_pack([W[n] for n in _TINY_SHARDED]), reduce=False, name="gather_vectors")
    tiny_shapes = [W[n].shape for n in _TINY_SHARDED]
    per_chip = [_unpack(tiny[2 * k], tiny_shapes) for k in range(N_CHIPS)]
    full_tiny = {n: jnp.concatenate([per_chip[k][i] for k in range(N_CHIPS)], axis=-1)
                 for i, n in enumerate(_TINY_SHARDED)}
    chip_arr = chip.astype(jnp.int32).reshape(1)
    groups = [[] for _ in range(depth)]
    shard3 = {n: W[n].reshape((-1,) + W[n].shape[-2:]) for n in _BIG}
    for n in _BIG:
        for l in range(shard3[n].shape[0]):
            groups[l if n.startswith('mlp') else _LAYER_OF[n]].append((n, l))
    full = {n: [None] * shard3[n].shape[0] for n in _BIG}
    for i in range(depth):
        placed = []
        for n, l in groups[i]:
            _, r, c = shard3[n].shape
            shape = (r, c * N_CHIPS) if _KIND[n] == 'col' else (r * N_CHIPS, c)
            placed.append(_place_block(shard3[n], lax.empty(shape, BF16), l, chip_arr, _KIND[n],
                                       name=f"place_{n}_{l}"))
        got = _all_gather_matrices(placed, [_KIND[n] for n, _ in groups[i]], name=f"gather_weights_{i}")
        for (n, l), t in zip(groups[i], got):
            full[n][l] = t

    p = dict(
        norm_mix=norm_mix, norm_mlp=norm_mlp, mlp_w1=full['mlp_w1'], mlp_w2=full['mlp_w2'],
        conv_w_in=full['conv_w_in'][0], conv_w=full_tiny['conv_w'][0], conv_w_out=full['conv_w_out'][0],
        pool_w_in=full['pool_w_in'][0], pool_w_group=jnp.stack(full['pool_w_group']),
        pool_scale=full_tiny['pool_scale'], att_w_qkv=full['att_w_qkv'][0], att_q_norm=att_q_norm,
        att_k_norm=att_k_norm, att_rel_bias=att_rel_bias[0], att_w_out=full['att_w_out'][0],
        ssm_a_re=ssm_a_re[0], ssm_a_im=ssm_a_im[0], ssm_log_dt=ssm_log_dt[0], ssm_b_re=ssm_b_re[0],
        ssm_b_im=ssm_b_im[0], ssm_c_re=ssm_c_re[0], ssm_c_im=ssm_c_im[0], ssm_d=full_tiny['ssm_d'],
        ssm_w_glu=full['ssm_w_glu'][0])

    loss_cols, dx, g = _local_step(x[0], loss_target[0], p)
    loss = lax.psum(0.5 * jnp.sum(loss_cols) / d_model, ("x", "y", "c"))

    gmats = {n: (g[n] if isinstance(g[n], list) else _matrices(g[n], n)) for n in _BIG}
    core = lax.axis_index("c").astype(jnp.int32).reshape(1)
    blocks = {n: lax.empty((len(gmats[n]),) + W[n].shape[-2:], F32) for n in _BIG}
    for i in reversed(range(depth)):
        kinds = [_KIND[n] for n, _ in groups[i]]
        mats = [gmats[n][l] for n, l in groups[i]]
        from_sibling = _pair_exchange(mats, kinds, name=f"pair_exchange_{i}")
        pair_sums = [_pair_sum(gm, t, core, kd, name=f"pair_sum_{n}_{l}")
                     for (n, l), gm, t, kd in zip(groups[i], mats, from_sibling, kinds)]
        from_chips = _chip_scatter(pair_sums, name=f"chip_scatter_{i}")
        for (n, l), own, t in zip(groups[i], pair_sums, from_chips):
            blocks[n] = _sum_into(own, t, blocks[n], l, core, chip_arr, name=f"sum_grads_{n}_{l}")
    reduced = _exchange_halves([blocks[n] for n in _BIG], name="exchange_halves")
    grads = {n: t.reshape(W[n].shape) for n, t in zip(_BIG, reduced)}

    small_full_shapes = [W[n].shape for n in _REPLICATED] + [full_tiny[n].shape for n in _TINY_SHARDED]
    gsmall = _gather_small(_pack([g[n] for n in _SMALL]), reduce=True, name="reduce_small_grads")
    for n, t in zip(_SMALL, _unpack(gsmall, small_full_shapes)):
        if n in _TINY_SHARDED:
            width = W[n].shape[-1]
            t = lax.dynamic_slice_in_dim(t, chip * width, width, axis=t.ndim - 1)
        grads[n] = t.reshape(W[n].shape)

    delta, new_m, new_v = {}, {}, {}
    for n in _BIG:
        shp = W[n].shape
        two = (int(np.prod(shp[:-1])), shp[-1])
        gout, d, nm, nv = _adamw(W[n].reshape(two), grads[n].reshape(two), M[n].reshape(two), V[n].reshape(two),
                                 name=f"adamw_{n}")
        grads[n], delta[n], new_m[n], new_v[n] = gout.reshape(shp), d.reshape(shp), nm.reshape(shp), nv.reshape(shp)
    shapes = [W[n].shape for n in _SMALL]
    _, d, nm, nv = _adamw(_pack([W[n] for n in _SMALL]), _pack([grads[n] for n in _SMALL]),
                          _pack([M[n] for n in _SMALL]), _pack([V[n] for n in _SMALL]), name="adamw_small")
    for n, a, b, c in zip(_SMALL, _unpack(d, shapes), _unpack(nm, shapes), _unpack(nv, shapes)):
        delta[n], new_m[n], new_v[n] = a, b, c

    return (loss, dx.reshape(x.shape), *[grads[n] for n in _ORDER], *[delta[n] for n in _ORDER],
            *[new_m[n] for n in _ORDER], *[new_v[n] for n in _ORDER])
```

```python
import functools
import math

import numpy as np
import jax
import jax.numpy as jnp
from jax import lax
from jax.experimental import pallas as pl
from jax.experimental.pallas import tpu as pltpu

F32 = jnp.float32
BF16 = jnp.bfloat16
MESH = pl.DeviceIdType.MESH

V7X_VMEM_BYTES = 64 * 1024 * 1024
VMEM_LIMIT = V7X_VMEM_BYTES - 12 * 1024 * 1024
LANES = 128
SUBLANES = 8

CHUNK = 64
ATT_HEAD_DIM = 128
ATT_PAD = 8 * CHUNK
REL_CLIP = 256
MASK_VALUE = -1e30
POOL_WINDOWS = (2, 4, 8, 16)
POOL_HALO = 16
SSM_GROUP = 16
SSM_STATE = 64
SSM_SLAB_GROUPS = LANES // SSM_GROUP
SSM_SLAB_STATE = SSM_SLAB_GROUPS * SSM_STATE
RMS_EPS = 1e-6
ADAM_LR, ADAM_B1, ADAM_B2, ADAM_EPS, ADAM_WD, ADAM_STEP = 0.001, 0.9, 0.999, 1e-08, 0.01, 10
ATT_TQ = 256
N_CHIPS = 4
N_DEV = 8


def _cparams(sem=None, **kw):
    return pltpu.CompilerParams(dimension_semantics=sem, vmem_limit_bytes=VMEM_LIMIT, **kw)


def _tile(n, target, mult):
    if n <= target:
        return n
    t = (target // mult) * mult
    while t > mult and n % t:
        t -= mult
    assert n % t == 0, (n, target, mult)
    return t


class _Side:
    def __init__(self, arrays, outs, aliases, sems, phases):
        self.arrays, self.outs, self.aliases, self.sems, self.phases = arrays, outs, aliases, sems, phases


def _run_side(side, *, name):
    n_in, n_out = len(side.arrays), len(side.outs)

    def body(*refs):
        for phase in side.phases:
            phase(refs[:n_in], refs[n_in:n_in + n_out], refs[n_in + n_out:])

    return pl.pallas_call(
        body, name=name, in_specs=[_ANY] * n_in, out_specs=[_ANY] * n_out, out_shape=list(side.outs),
        input_output_aliases=dict(side.aliases), scratch_shapes=list(side.sems),
    )(*side.arrays)


def _mm(a, b, *, ta=False, tb=False, extras=(), epilogue=None, out_dtypes=(F32,), name,
        tm=1024, tn=1024, tk=2048, side=None):
    M, K = (a.shape[1], a.shape[0]) if ta else a.shape
    N = b.shape[0] if tb else b.shape[1]
    assert (b.shape[1] if tb else b.shape[0]) == K, (a.shape, b.shape, ta, tb)
    tm, tn, tk = _tile(M, tm, LANES), _tile(N, tn, LANES), _tile(K, tk, LANES)
    nk = K // tk
    gm, gn = M // tm, N // tn
    n_ex, n_out = len(extras), len(out_dtypes)
    n_sin = len(side.arrays) if side else 0
    n_sout = len(side.outs) if side else 0
    n_sems = len(side.sems) if side else 0
    dn = (((0 if ta else 1,), (1 if tb else 0,)), ((), ()))
    if side:
        steps = gm * gn * nk
        n_ph = len(side.phases)
        at = [p * (steps - 1) // max(n_ph - 1, 1) for p in range(n_ph)]

    def body(*refs):
        a_ref, b_ref = refs[0], refs[1]
        ex_refs = refs[2:2 + n_ex]
        n_in = 2 + n_ex + n_sin
        o_refs = refs[n_in:n_in + n_out]
        if side:
            s_in = refs[2 + n_ex:n_in]
            s_out = refs[n_in + n_out:n_in + n_out + n_sout]
            s_sems = refs[len(refs) - n_sems:]
            t = (pl.program_id(0) * gn + pl.program_id(1)) * nk + pl.program_id(2)
            for phase, when in zip(side.phases, at):
                @pl.when(t == when)
                def _(phase=phase):
                    phase(s_in, s_out, s_sems)

        p = lax.dot_general(a_ref[...], b_ref[...], dn, preferred_element_type=F32)

        def finish(acc):
            outs = (acc,) if epilogue is None else epilogue(acc, *[r[...] for r in ex_refs])
            for o_ref, o in zip(o_refs, outs):
                o_ref[...] = o.astype(o_ref.dtype)

        if nk == 1:
            finish(p)
        else:
            acc_ref = refs[n_in + n_out + n_sout]
            k = pl.program_id(2)

            @pl.when(k == 0)
            def _():
                acc_ref[...] = p

            @pl.when(k > 0)
            def _():
                acc_ref[...] += p

            @pl.when(k == nk - 1)
            def _():
                finish(acc_ref[...])

    a_spec = pl.BlockSpec((tk, tm), lambda i, j, k: (k, i)) if ta else pl.BlockSpec((tm, tk), lambda i, j, k: (i, k))
    b_spec = pl.BlockSpec((tn, tk), lambda i, j, k: (j, k)) if tb else pl.BlockSpec((tk, tn), lambda i, j, k: (k, j))
    mn_spec = pl.BlockSpec((tm, tn), lambda i, j, k: (i, j))
    outs = pl.pallas_call(
        body, name=name, grid=(gm, gn, nk),
        in_specs=[a_spec, b_spec] + [mn_spec] * n_ex + [_ANY] * n_sin,
        out_specs=[mn_spec] * n_out + [_ANY] * n_sout,
        out_shape=[jax.ShapeDtypeStruct((M, N), d) for d in out_dtypes] + (list(side.outs) if side else []),
        input_output_aliases={2 + n_ex + i: n_out + o for i, o in side.aliases.items()} if side else {},
        scratch_shapes=([pltpu.VMEM((tm, tn), F32)] if nk > 1 else []) + (list(side.sems) if side else []),
        compiler_params=_cparams(("arbitrary",) * 3 if side else ("parallel", "parallel", "arbitrary")),
    )(a, b, *extras, *(side.arrays if side else ()))
    res = outs[0] if n_out == 1 else tuple(outs[:n_out])
    return (res, list(outs[n_out:])) if side else res


def _row_call(body, ins, outs, *, name, tr, n_rows, acc_outs=(), scratch=(), halo=None):
    nb = n_rows // tr
    hb = halo or SUBLANES
    per = tr // hb
    last = n_rows // hb - 1

    def spec(arr_shape, kind):
        if kind == 'rows':
            return pl.BlockSpec((tr,) + tuple(arr_shape[1:]), lambda i: (i,) + (0,) * (len(arr_shape) - 1))
        if kind == 'full' or kind == 'acc':
            return pl.BlockSpec(tuple(arr_shape), lambda i: (0,) * len(arr_shape))
        tag, w, j = kind
        if tag == 'cols':
            return pl.BlockSpec((tr, w), lambda i: (i, j))
        if tag == 'rows_from':
            return pl.BlockSpec((tr, w), lambda i: (i + j // tr, 0))
        if tag == 'prev':
            return pl.BlockSpec((hb, w), lambda i: (jnp.maximum(i * per - 1, 0), j))
        if tag == 'next':
            return pl.BlockSpec((hb, w), lambda i: (jnp.minimum((i + 1) * per, last), j))
        raise ValueError(kind)

    return pl.pallas_call(
        body, name=name, grid=(nb,),
        in_specs=[spec(a.shape, k) for a, k in ins],
        out_specs=[spec(s, k) for s, _, k in outs],
        out_shape=[jax.ShapeDtypeStruct(s, d) for s, d, _ in outs],
        scratch_shapes=list(scratch),
        compiler_params=_cparams(("arbitrary",)),
    )(*[a for a, _ in ins])


def _rms_fwd(x, gain, *, name, with_f32=False):
    S, D = x.shape
    tr = _tile(S, 512, SUBLANES)

    def body(x_ref, g_ref, *o_refs):
        xv = x_ref[...]
        r = lax.rsqrt(jnp.mean(xv * xv, axis=-1, keepdims=True) + RMS_EPS)
        h = xv * r * g_ref[...]
        o_refs[0][...] = h.astype(BF16)
        if with_f32:
            o_refs[1][...] = h

    outs = [((S, D), BF16, 'rows')] + ([((S, D), F32, 'rows')] if with_f32 else [])
    res = _row_call(body, [(x, 'rows'), (gain, 'full')], outs, name=name, tr=tr, n_rows=S)
    return tuple(res) if with_f32 else res[0]


def _rms_bwd(x, gain, dh, dres, *, name):
    S, D = x.shape
    tr = _tile(S, 256, SUBLANES)

    def body(x_ref, g_ref, dh_ref, dr_ref, dx_ref, dxb_ref, dg_ref):
        i = pl.program_id(0)
        xv = x_ref[...]
        r = lax.rsqrt(jnp.mean(xv * xv, axis=-1, keepdims=True) + RMS_EPS)
        xn = xv * r
        dhv = dh_ref[...]
        dxn = dhv * g_ref[...]
        dx = r * (dxn - xn * jnp.mean(dxn * xn, axis=-1, keepdims=True)) + dr_ref[...]
        dx_ref[...] = dx
        dxb_ref[...] = dx.astype(BF16)
        part = jnp.sum(dhv * xn, axis=0, keepdims=True)

        @pl.when(i == 0)
        def _():
            dg_ref[...] = part

        @pl.when(i > 0)
        def _():
            dg_ref[...] += part

    return _row_call(body, [(x, 'rows'), (gain, 'full'), (dh, 'rows'), (dres, 'rows')],
                     [((S, D), F32, 'rows'), ((S, D), BF16, 'rows'), ((1, D), F32, 'acc')],
                     name=name, tr=tr, n_rows=S)


def _loss_head(y, target, *, name):
    S, D = y.shape
    tr = _tile(S, 512, SUBLANES)

    def body(y_ref, t_ref, d_ref, db_ref, l_ref):
        i = pl.program_id(0)
        e = y_ref[...] - t_ref[...]
        d = e * (1.0 / D)
        d_ref[...] = d
        db_ref[...] = d.astype(BF16)
        part = jnp.sum(e * e, axis=0, keepdims=True)

        @pl.when(i == 0)
        def _():
            l_ref[...] = part

        @pl.when(i > 0)
        def _():
            l_ref[...] += part

    return _row_call(body, [(y, 'rows'), (target, 'rows')],
                     [((S, D), F32, 'rows'), ((S, D), BF16, 'rows'), ((1, D), F32, 'acc')],
                     name=name, tr=tr, n_rows=S)


def _relu2_epilogue(acc):
    r = jnp.maximum(acc, 0.0)
    return r, r * r


def _mlp_fwd(x, gain, w1, w2, *, tag, side=None):
    h = _rms_fwd(x, gain, name=f"mlp_norm_{tag}")
    up = _mm(h, w1, epilogue=_relu2_epilogue, out_dtypes=(BF16, BF16), name=f"mlp_up_{tag}", side=side)
    (r, act), side_out = up if side else (up, None)
    y = _mm(act, w2, extras=(x,), epilogue=lambda acc, res: (acc + res,), name=f"mlp_down_{tag}")
    return y, (h, r, act), side_out


def _mlp_bwd(x, gain, w1, w2, saved, dy, dyb, *, tag, sides=None):
    h, r, act = saved
    da = _mm(dyb, w2, tb=True, extras=(r,), epilogue=lambda acc, rr: (acc * (2.0 * rr.astype(F32)),),
             out_dtypes=(BF16,), name=f"mlp_dact_{tag}", side=sides[0] if sides else None)
    dw2 = _mm(act, dyb, ta=True, out_dtypes=(BF16,), name=f"mlp_dw2_{tag}", side=sides[1] if sides else None)
    side_out = None
    if sides:
        (da, out0), (dw2, out1) = da, dw2
        side_out = (out0, out1)
    dw1 = _mm(h, da, ta=True, out_dtypes=(BF16,), name=f"mlp_dw1_{tag}")
    dh = _mm(da, w1, tb=True, name=f"mlp_dh_{tag}")
    dx, dxb, dgain = _rms_bwd(x, gain, dh, dy, name=f"mlp_dnorm_{tag}")
    return dx, dxb, dgain, dw1, dw2, side_out


def _conv_gate_fwd(z, conv_w):
    S, D3 = z.shape
    D = D3 // 3
    tr = _tile(S, 256, SUBLANES)

    def body(b_ref, c_ref, v_ref, cp_ref, vp_ref, w_ref, g_ref, scr):
        i = pl.program_id(0)
        u = c_ref[...] * v_ref[...]
        scr[0:SUBLANES, :] = cp_ref[...] * vp_ref[...] * (i > 0).astype(F32)
        scr[SUBLANES:, :] = u
        conv = (w_ref[0:1, :] * scr[pl.ds(SUBLANES - 2, tr), :] + w_ref[1:2, :] * scr[pl.ds(SUBLANES - 1, tr), :]
                + w_ref[2:3, :] * u)
        g_ref[...] = (b_ref[...] * conv).astype(BF16)

    ins = [(z, ('cols', D, 0)), (z, ('cols', D, 1)), (z, ('cols', D, 2)), (z, ('prev', D, 1)), (z, ('prev', D, 2)),
           (conv_w, 'full')]
    return _row_call(body, ins, [((S, D), BF16, 'rows')], name="conv_gate_fwd", tr=tr, n_rows=S,
                     scratch=[pltpu.VMEM((tr + SUBLANES, D), F32)])[0]


def _conv_gate_bwd(z, conv_w, dg):
    S, D3 = z.shape
    D = D3 // 3
    tr = _tile(S, 128, SUBLANES)
    nb = S // tr

    def body(b_ref, c_ref, v_ref, cp_ref, vp_ref, bn_ref, dg_ref, dgn_ref, w_ref, dz_ref, dw_ref, scr, scr2):
        i = pl.program_id(0)
        c, v, b, dgv = c_ref[...], v_ref[...], b_ref[...], dg_ref[...]
        u = c * v
        scr[0:SUBLANES, :] = cp_ref[...] * vp_ref[...] * (i > 0).astype(F32)
        scr[SUBLANES:, :] = u
        u1 = scr[pl.ds(SUBLANES - 1, tr), :]
        u2 = scr[pl.ds(SUBLANES - 2, tr), :]
        conv = w_ref[0:1, :] * u2 + w_ref[1:2, :] * u1 + w_ref[2:3, :] * u
        dconv = dgv * b
        scr2[0:tr, :] = dconv
        scr2[tr:, :] = dgn_ref[...] * bn_ref[...] * (i < nb - 1).astype(F32)
        du = (w_ref[2:3, :] * dconv + w_ref[1:2, :] * scr2[pl.ds(1, tr), :] + w_ref[0:1, :] * scr2[pl.ds(2, tr), :])
        dz_ref[:, 0:D] = (dgv * conv).astype(BF16)
        dz_ref[:, D:2 * D] = (du * v).astype(BF16)
        dz_ref[:, 2 * D:] = (du * c).astype(BF16)
        parts = [jnp.sum(dconv * t, axis=0, keepdims=True) for t in (u2, u1, u)]

        @pl.when(i == 0)
        def _():
            for k in range(3):
                dw_ref[k:k + 1, :] = parts[k]

        @pl.when(i > 0)
        def _():
            for k in range(3):
                dw_ref[k:k + 1, :] += parts[k]

    ins = [(z, ('cols', D, 0)), (z, ('cols', D, 1)), (z, ('cols', D, 2)), (z, ('prev', D, 1)), (z, ('prev', D, 2)),
           (z, ('next', D, 0)), (dg, 'rows'), (dg, ('next', D, 0)), (conv_w, 'full')]
    return _row_call(body, ins, [((S, D3), BF16, 'rows'), ((3, D), F32, 'acc')], name="conv_gate_bwd", tr=tr,
                     n_rows=S, scratch=[pltpu.VMEM((tr + SUBLANES, D), F32), pltpu.VMEM((tr + SUBLANES, D), F32)])


def _conv_mixer_fwd(x, gain, w_in, conv_w, w_out):
    h = _rms_fwd(x, gain, name="conv_norm")
    z = _mm(h, w_in, name="conv_in")
    g = _conv_gate_fwd(z, conv_w)
    y = _mm(g, w_out, extras=(x,), epilogue=lambda acc, res: (acc + res,), name="conv_out")
    return y, (h, z, g)


def _conv_mixer_bwd(x, gain, w_in, conv_w, w_out, saved, dy, dyb):
    h, z, g = saved
    dg = _mm(dyb, w_out, tb=True, name="conv_dg")
    dw_out = _mm(g, dyb, ta=True, out_dtypes=(BF16,), name="conv_dwout")
    dz, dconv_w = _conv_gate_bwd(z, conv_w, dg)
    dh = _mm(dz, w_in, tb=True, name="conv_dh")
    dw_in = _mm(h, dz, ta=True, out_dtypes=(BF16,), name="conv_dwin")
    dx, dxb, dgain = _rms_bwd(x, gain, dh, dy, name="conv_dnorm")
    return dx, dxb, dgain, dw_in, dconv_w, dw_out


def _pool_fwd(u):
    S, D = u.shape
    G = D // len(POOL_WINDOWS)
    tr = _tile(S, 256, SUBLANES)
    H = POOL_HALO

    def body(u_ref, up_ref, p_ref, scr):
        i = pl.program_id(0)
        uv = u_ref[...]
        scr[0:H, :] = up_ref[...] * (i > 0).astype(F32)
        scr[H:, :] = uv
        t = (lax.broadcasted_iota(jnp.int32, (tr, 1), 0) + i * tr + 1).astype(F32)
        for gi, w in enumerate(POOL_WINDOWS):
            cols = slice(gi * G, (gi + 1) * G)
            acc = uv[:, cols]
            for j in range(1, w):
                acc = acc + scr[pl.ds(H - j, tr), cols]
            p_ref[:, cols] = (acc / jnp.minimum(t, float(w)) - uv[:, cols]).astype(BF16)

    return _row_call(body, [(u, 'rows'), (u, ('prev', D, 0))], [((S, D), BF16, 'rows')], name="pool_fwd", tr=tr,
                     n_rows=S, halo=H, scratch=[pltpu.VMEM((tr + H, D), F32)])[0]


def _pool_bwd(dp):
    S, D = dp.shape
    G = D // len(POOL_WINDOWS)
    tr = _tile(S, 256, SUBLANES)
    H = POOL_HALO
    nb = S // tr

    def body(d_ref, dn_ref, o_ref, scr):
        i = pl.program_id(0)
        dv = d_ref[...]
        t = (lax.broadcasted_iota(jnp.int32, (tr, 1), 0) + i * tr + 1).astype(F32)
        tn = (lax.broadcasted_iota(jnp.int32, (H, 1), 0) + (i + 1) * tr + 1).astype(F32)
        for gi, w in enumerate(POOL_WINDOWS):
            cols = slice(gi * G, (gi + 1) * G)
            scr[0:tr, cols] = dv[:, cols] / jnp.minimum(t, float(w))
            scr[tr:, cols] = dn_ref[:, cols] / jnp.minimum(tn, float(w)) * (i < nb - 1).astype(F32)
        for gi, w in enumerate(POOL_WINDOWS):
            cols = slice(gi * G, (gi + 1) * G)
            acc = scr[0:tr, cols]
            for j in range(1, w):
                acc = acc + scr[pl.ds(j, tr), cols]
            o_ref[:, cols] = (acc - dv[:, cols]).astype(BF16)

    return _row_call(body, [(dp, 'rows'), (dp, ('next', D, 0))], [((S, D), BF16, 'rows')], name="pool_bwd", tr=tr,
                     n_rows=S, halo=H, scratch=[pltpu.VMEM((tr + H, D), F32)])[0]


def _pool_group_fwd(p, wg, scale, x):
    S, D = p.shape
    NG, G, _ = wg.shape
    tm = _tile(S, 1024, SUBLANES)

    def body(p_ref, w_ref, s_ref, x_ref, o_ref, y_ref):
        y = jnp.dot(p_ref[...], w_ref[0], preferred_element_type=F32)
        y_ref[...] = y
        o_ref[...] = x_ref[...] + y * s_ref[...]

    blk = pl.BlockSpec((tm, G), lambda i, g: (i, g))
    return pl.pallas_call(
        body, name="pool_group_fwd", grid=(S // tm, NG),
        in_specs=[blk, pl.BlockSpec((1, G, G), lambda i, g: (g, 0, 0)), pl.BlockSpec((1, G), lambda i, g: (0, g)), blk],
        out_specs=[blk, blk],
        out_shape=[jax.ShapeDtypeStruct((S, D), F32), jax.ShapeDtypeStruct((S, D), F32)],
        compiler_params=_cparams(("parallel", "arbitrary")),
    )(p, wg, scale, x)


def _pool_group_bwd(p, wg, scale, y, dm):
    S, D = p.shape
    NG, G, _ = wg.shape
    tm = _tile(S, 1024, SUBLANES)
    nb = S // tm

    def body(p_ref, w_ref, s_ref, y_ref, dm_ref, dp_ref, dw_ref, ds_ref, acc_ref):
        i = pl.program_id(1)
        dmv = dm_ref[...]
        dy = (dmv * s_ref[...]).astype(BF16)
        dp_ref[...] = lax.dot_general(dy, w_ref[0], (((1,), (1,)), ((), ())), preferred_element_type=F32)
        dw = lax.dot_general(p_ref[...], dy, (((0,), (0,)), ((), ())), preferred_element_type=F32)
        dsp = jnp.sum(dmv * y_ref[...], axis=0, keepdims=True)

        @pl.when(i == 0)
        def _():
            acc_ref[...] = dw
            ds_ref[...] = dsp

        @pl.when(i > 0)
        def _():
            acc_ref[...] += dw
            ds_ref[...] += dsp

        @pl.when(i == nb - 1)
        def _():
            dw_ref[0] = acc_ref[...].astype(BF16)

    blk = pl.BlockSpec((tm, G), lambda g, i: (i, g))
    wspec = pl.BlockSpec((1, G, G), lambda g, i: (g, 0, 0))
    sspec = pl.BlockSpec((1, G), lambda g, i: (0, g))
    return pl.pallas_call(
        body, name="pool_group_bwd", grid=(NG, nb),
        in_specs=[blk, wspec, sspec, blk, blk],
        out_specs=[blk, wspec, sspec],
        out_shape=[jax.ShapeDtypeStruct((S, D), F32), jax.ShapeDtypeStruct((NG, G, G), BF16),
                   jax.ShapeDtypeStruct((1, D), F32)],
        scratch_shapes=[pltpu.VMEM((G, G), F32)],
        compiler_params=_cparams(("parallel", "arbitrary")),
    )(p, wg, scale, y, dm)


def _pool_mixer_fwd(x, gain, w_in, wg, scale):
    h = _rms_fwd(x, gain, name="pool_norm")
    u = _mm(h, w_in, name="pool_in")
    p = _pool_fwd(u)
    y, yg = _pool_group_fwd(p, wg, scale, x)
    return y, (h, p, yg)


def _pool_mixer_bwd(x, gain, w_in, wg, scale, saved, dy, dyb):
    h, p, yg = saved
    dp, dwg, dscale = _pool_group_bwd(p, wg, scale, yg, dy)
    du = _pool_bwd(dp)
    dh = _mm(du, w_in, tb=True, name="pool_dh")
    dw_in = _mm(h, du, ta=True, out_dtypes=(BF16,), name="pool_dwin")
    dx, dxb, dgain = _rms_bwd(x, gain, dh, dy, name="pool_dnorm")
    return dx, dxb, dgain, dw_in, dwg, dscale


def _qk_norm_fwd(qkv, qg, kg):
    S, D3 = qkv.shape
    D = D3 // 3
    NH = D // ATT_HEAD_DIM
    tr = _tile(S, 256, SUBLANES)

    def body(q_ref, k_ref, v_ref, qg_ref, kg_ref, qo_ref, ko_ref, vo_ref):
        for src, g_ref, dst in ((q_ref, qg_ref, qo_ref), (k_ref, kg_ref, ko_ref)):
            for hd in range(NH):
                cols = slice(hd * ATT_HEAD_DIM, (hd + 1) * ATT_HEAD_DIM)
                t = src[:, cols]
                r = lax.rsqrt(jnp.mean(t * t, axis=-1, keepdims=True) + RMS_EPS)
                dst[:, cols] = (t * r * g_ref[...]).astype(BF16)
        vo_ref[...] = v_ref[...].astype(BF16)

    ins = [(qkv, ('cols', D, 0)), (qkv, ('cols', D, 1)), (qkv, ('cols', D, 2)), (qg, 'full'), (kg, 'full')]
    return _row_call(body, ins, [((S, D), BF16, 'rows')] * 3, name="att_qknorm_fwd", tr=tr, n_rows=S)


def _qk_norm_bwd(qkv, qg, kg, dqn, dkn, dv):
    S, D3 = qkv.shape
    D = D3 // 3
    NH = D // ATT_HEAD_DIM
    tr = _tile(S, 128, SUBLANES)

    def body(q_ref, k_ref, qg_ref, kg_ref, dq_ref, dk_ref, dv_ref, o_ref, dqg_ref, dkg_ref):
        i = pl.program_id(0)
        for sec, (src, g_ref, d_ref, dg_ref) in enumerate(((q_ref, qg_ref, dq_ref, dqg_ref),
                                                            (k_ref, kg_ref, dk_ref, dkg_ref))):
            part = jnp.zeros((1, ATT_HEAD_DIM), F32)
            for hd in range(NH):
                cols = slice(hd * ATT_HEAD_DIM, (hd + 1) * ATT_HEAD_DIM)
                t = src[:, cols]
                r = lax.rsqrt(jnp.mean(t * t, axis=-1, keepdims=True) + RMS_EPS)
                tn = t * r
                d = d_ref[:, cols]
                dn = d * g_ref[...]
                dt = r * (dn - tn * jnp.mean(dn * tn, axis=-1, keepdims=True))
                o_ref[:, sec * D + hd * ATT_HEAD_DIM:sec * D + (hd + 1) * ATT_HEAD_DIM] = dt.astype(BF16)
                part = part + jnp.sum(d * tn, axis=0, keepdims=True)

            @pl.when(i == 0)
            def _():
                dg_ref[...] = part

            @pl.when(i > 0)
            def _():
                dg_ref[...] += part

        o_ref[:, 2 * D:] = dv_ref[...].astype(BF16)

    assert ATT_PAD % tr == 0 and dkn.shape[0] == S + ATT_PAD
    ins = [(qkv, ('cols', D, 0)), (qkv, ('cols', D, 1)), (qg, 'full'), (kg, 'full'), (dqn, 'rows'),
           (dkn, ('rows_from', D, ATT_PAD)), (dv, ('rows_from', D, ATT_PAD))]
    return _row_call(body, ins, [((S, D3), BF16, 'rows'), ((1, ATT_HEAD_DIM), F32, 'acc'),
                                 ((1, ATT_HEAD_DIM), F32, 'acc')], name="att_qknorm_bwd", tr=tr, n_rows=S)


def _att_band_mask():
    r = np.arange(ATT_TQ)[:, None]
    c = np.arange(ATT_TQ + ATT_PAD)[None, :]
    lo = (r // CHUNK) * CHUNK
    return (c >= lo) & (c < lo + ATT_PAD + CHUNK)


def _att_bias_toeplitz(rel_bias):
    H = rel_bias.shape[0]
    R, C = ATT_TQ, ATT_TQ + ATT_PAD
    L = C + R - 1
    assert R - 1 < REL_CLIP
    near = rel_bias[:, REL_CLIP - (R - 1):2 * REL_CLIP][:, ::-1]
    far = jnp.broadcast_to(rel_bias[:, 2 * REL_CLIP:], (H, L - near.shape[1]))
    v = jnp.concatenate([far, near, jnp.zeros((H, 1), rel_bias.dtype)], axis=1)
    skew = jnp.broadcast_to(v[:, None, :], (H, R, L + 1)).reshape(H, R * (L + 1))[:, :R * L].reshape(H, R, L)
    return skew[:, :, R - 1:R - 1 + C]


def _att_bias_tile(rel_bias):
    return jnp.where(_att_band_mask()[None], _att_bias_toeplitz(rel_bias), MASK_VALUE).astype(F32)


def _att_bias_grad(dtile, rel_bias):
    _, pull = jax.vjp(_att_bias_toeplitz, rel_bias)
    return pull(jnp.where(_att_band_mask()[None], dtile, 0.0))[0]


def _att_core_fwd(qn, kp, vp, bias):
    S, D = qn.shape
    NH = D // ATT_HEAD_DIM
    KW = ATT_TQ + ATT_PAD
    scale = ATT_HEAD_DIM ** -0.5

    def body(q_ref, k_ref, v_ref, b_ref, o_ref):
        qb = pl.program_id(1)
        start = pl.multiple_of(qb * ATT_TQ, ATT_TQ)
        ks = k_ref[pl.ds(start, KW), :]
        vs = v_ref[pl.ds(start, KW), :]
        s = lax.dot_general(q_ref[...], ks, (((1,), (1,)), ((), ())), preferred_element_type=F32) * scale + b_ref[0]
        kpos = lax.broadcasted_iota(jnp.int32, (1, KW), 1) + (qb * ATT_TQ - ATT_PAD)
        s = jnp.where(kpos >= 0, s, MASK_VALUE)
        m = jnp.max(s, axis=-1, keepdims=True)
        p = jnp.exp(s - m)
        l = jnp.sum(p, axis=-1, keepdims=True)
        o = jnp.dot(p.astype(BF16), vs, preferred_element_type=F32) / l
        o_ref[...] = o.astype(BF16)

    qspec = pl.BlockSpec((ATT_TQ, ATT_HEAD_DIM), lambda h, qb: (qb, h))
    kvspec = pl.BlockSpec((S + ATT_PAD, ATT_HEAD_DIM), lambda h, qb: (0, h))
    return pl.pallas_call(
        body, name="att_core_fwd", grid=(NH, S // ATT_TQ),
        in_specs=[qspec, kvspec, kvspec, pl.BlockSpec((1, ATT_TQ, KW), lambda h, qb: (h, 0, 0))],
        out_specs=qspec, out_shape=jax.ShapeDtypeStruct((S, D), BF16),
        compiler_params=_cparams(("parallel", "arbitrary")),
    )(qn, kp, vp, bias)


def _att_core_bwd(qn, kp, vp, bias, do):
    S, D = qn.shape
    NH = D // ATT_HEAD_DIM
    KW = ATT_TQ + ATT_PAD
    scale = ATT_HEAD_DIM ** -0.5

    def body(q_ref, k_ref, v_ref, b_ref, do_ref, dq_ref, dk_ref, dv_ref, db_ref):
        qb = pl.program_id(1)
        start = pl.multiple_of(qb * ATT_TQ, ATT_TQ)
        q = q_ref[...]
        dov = do_ref[...]
        ks = k_ref[pl.ds(start, KW), :]
        vs = v_ref[pl.ds(start, KW), :]
        s = lax.dot_general(q, ks, (((1,), (1,)), ((), ())), preferred_element_type=F32) * scale + b_ref[0]
        kpos = lax.broadcasted_iota(jnp.int32, (1, KW), 1) + (qb * ATT_TQ - ATT_PAD)
        s = jnp.where(kpos >= 0, s, MASK_VALUE)
        m = jnp.max(s, axis=-1, keepdims=True)
        e = jnp.exp(s - m)
        p = e / jnp.sum(e, axis=-1, keepdims=True)
        dp = lax.dot_general(dov, vs, (((1,), (1,)), ((), ())), preferred_element_type=F32)
        ds = p * (dp - jnp.sum(p * dp, axis=-1, keepdims=True))
        dsb = ds.astype(BF16)
        dq_ref[...] = jnp.dot(dsb, ks, preferred_element_type=F32) * scale
        dk = lax.dot_general(dsb, q, (((0,), (0,)), ((), ())), preferred_element_type=F32) * scale
        dv = lax.dot_general(p.astype(BF16), dov, (((0,), (0,)), ((), ())), preferred_element_type=F32)

        @pl.when(qb == 0)
        def _():
            dk_ref[...] = jnp.zeros_like(dk_ref)
            dv_ref[...] = jnp.zeros_like(dv_ref)
            db_ref[0] = ds

        @pl.when(qb > 0)
        def _():
            db_ref[0] += ds

        dk_ref[pl.ds(start, KW), :] += dk
        dv_ref[pl.ds(start, KW), :] += dv

    qspec = pl.BlockSpec((ATT_TQ, ATT_HEAD_DIM), lambda h, qb: (qb, h))
    kvspec = pl.BlockSpec((S + ATT_PAD, ATT_HEAD_DIM), lambda h, qb: (0, h))
    bspec = pl.BlockSpec((1, ATT_TQ, KW), lambda h, qb: (h, 0, 0))
    return pl.pallas_call(
        body, name="att_core_bwd", grid=(NH, S // ATT_TQ),
        in_specs=[qspec, kvspec, kvspec, bspec, qspec],
        out_specs=[qspec, kvspec, kvspec, bspec],
        out_shape=[jax.ShapeDtypeStruct((S, D), F32), jax.ShapeDtypeStruct((S + ATT_PAD, D), F32),
                   jax.ShapeDtypeStruct((S + ATT_PAD, D), F32), jax.ShapeDtypeStruct((NH, ATT_TQ, KW), F32)],
        compiler_params=_cparams(("parallel", "arbitrary")),
    )(qn, kp, vp, bias, do)


def _att_mixer_fwd(x, gain, w_qkv, qg, kg, rel_bias, w_out):
    h = _rms_fwd(x, gain, name="att_norm")
    qkv = _mm(h, w_qkv, name="att_qkv")
    qn, kn, v = _qk_norm_fwd(qkv, qg, kg)
    kp = jnp.pad(kn, ((ATT_PAD, 0), (0, 0)))
    vp = jnp.pad(v, ((ATT_PAD, 0), (0, 0)))
    bias = _att_bias_tile(rel_bias)
    o = _att_core_fwd(qn, kp, vp, bias)
    y = _mm(o, w_out, extras=(x,), epilogue=lambda acc, res: (acc + res,), name="att_out")
    return y, (h, qkv, qn, kp, vp, bias, o, rel_bias)


def _att_mixer_bwd(x, gain, w_qkv, qg, kg, w_out, saved, dy, dyb):
    h, qkv, qn, kp, vp, bias, o, rel_bias = saved
    do = _mm(dyb, w_out, tb=True, out_dtypes=(BF16,), name="att_do")
    dw_out = _mm(o, dyb, ta=True, out_dtypes=(BF16,), name="att_dwout")
    dqn, dkp, dvp, dbt = _att_core_bwd(qn, kp, vp, bias, do)
    drel = _att_bias_grad(dbt, rel_bias)
    dqkv, dqg, dkg = _qk_norm_bwd(qkv, qg, kg, dqn, dkp, dvp)
    dh = _mm(dqkv, w_qkv, tb=True, name="att_dh")
    dw_qkv = _mm(h, dqkv, ta=True, out_dtypes=(BF16,), name="att_dwqkv")
    dx, dxb, dgain = _rms_bwd(x, gain, dh, dy, name="att_dnorm")
    return dx, dxb, dgain, dw_qkv, dqg, dkg, drel, dw_out


def _ssm_tables(a_re, a_im, log_dt, b_re, b_im, c_re, c_im):
    G, N = a_re.shape
    NS = G // SSM_SLAB_GROUPS
    lam = lax.complex(a_re, a_im)
    dt = jnp.exp(log_dt)[:, None]
    abar = jnp.exp(lam * dt)
    coef = (abar - 1.0) / lam
    bbar = coef[..., None] * lax.complex(b_re, b_im)
    eye = jnp.eye(SSM_SLAB_GROUPS, dtype=F32)

    def blockdiag(t):
        P, Q = t.shape[1:]
        t = t.reshape(NS, SSM_SLAB_GROUPS, P, Q)
        return jnp.einsum('sgpq,gh->sgphq', t, eye).reshape(NS, SSM_SLAB_GROUPS * P, SSM_SLAB_GROUPS * Q)

    bt = jnp.swapaxes(bbar, 1, 2)
    bmat = jnp.concatenate([blockdiag(jnp.real(bt)), blockdiag(jnp.imag(bt))], axis=2)
    ct = jnp.swapaxes(lax.complex(c_re, c_im), 1, 2)
    cmat = jnp.concatenate([blockdiag(jnp.real(ct)), -blockdiag(jnp.imag(ct))], axis=1)
    al = abar.reshape(NS, 1, SSM_SLAB_STATE)
    rows = jnp.arange(SUBLANES)[None, :, None]
    fwd, bwd = [], []
    for k in (1, 2, 4):
        ak = al ** k
        f = jnp.where(rows >= k, ak, 0.0)
        b = jnp.where(rows < SUBLANES - k, ak, 0.0)
        fwd += [jnp.real(f), jnp.imag(f)]
        bwd += [jnp.real(b), jnp.imag(b)]
    pf = al ** (rows + 1)
    pb = al ** (SUBLANES - rows)
    fwd += [jnp.real(pf), jnp.imag(pf)]
    bwd += [jnp.real(pb), jnp.imag(pb)]
    coef_f = jnp.concatenate(fwd, axis=1).astype(F32)
    coef_b = jnp.concatenate(bwd, axis=1).astype(F32)
    return dict(lam=lam, dt=dt, abar=abar, coef=coef, bmat=bmat.astype(BF16), cmat=cmat.astype(BF16),
                bmat_t=jnp.swapaxes(bmat, 1, 2).astype(BF16), cmat_t=jnp.swapaxes(cmat, 1, 2).astype(BF16),
                coef_f=coef_f, coef_b=coef_b)


def _ssm_scan_fwd(u, tabs, d_skip):
    S, D = u.shape
    NS = D // LANES
    W = 2 * SSM_SLAB_STATE
    T = _tile(S, 512, SUBLANES)
    HS = SSM_SLAB_STATE

    def body(u_ref, bm_ref, cm_ref, cf_ref, d_ref, y_ref, xs_ref, bu_scr, carry_scr):
        i = pl.program_id(1)

        @pl.when(i == 0)
        def _():
            carry_scr[...] = jnp.zeros_like(carry_scr)

        uv = u_ref[...]
        bu_scr[...] = jnp.dot(uv.astype(BF16), bm_ref[0], preferred_element_type=F32)

        def step(r, carry):
            rows = pl.ds(pl.multiple_of(r * SUBLANES, SUBLANES), SUBLANES)
            xr = bu_scr[rows, 0:HS]
            xi = bu_scr[rows, HS:W]
            for n, k in enumerate((1, 2, 4)):
                ar = cf_ref[0, 16 * n:16 * n + 8, :]
                ai = cf_ref[0, 16 * n + 8:16 * n + 16, :]
                sr = pltpu.roll(xr, k, 0)
                si = pltpu.roll(xi, k, 0)
                xr, xi = xr + ar * sr - ai * si, xi + ar * si + ai * sr
            pr = cf_ref[0, 48:56, :]
            pi_ = cf_ref[0, 56:64, :]
            cr, ci = carry
            xr, xi = xr + pr * cr - pi_ * ci, xi + pr * ci + pi_ * cr
            xs_ref[rows, 0:HS] = xr
            xs_ref[rows, HS:W] = xi
            return xr[SUBLANES - 1:SUBLANES, :], xi[SUBLANES - 1:SUBLANES, :]

        cr, ci = lax.fori_loop(0, T // SUBLANES, step, (carry_scr[0:1, 0:HS], carry_scr[0:1, HS:W]), unroll=2)
        carry_scr[0:1, 0:HS] = cr
        carry_scr[0:1, HS:W] = ci
        y_ref[...] = jnp.dot(xs_ref[...].astype(BF16), cm_ref[0], preferred_element_type=F32) + d_ref[...] * uv

    return pl.pallas_call(
        body, name="ssm_scan_fwd", grid=(NS, S // T),
        in_specs=[pl.BlockSpec((T, LANES), lambda j, i: (i, j)),
                  pl.BlockSpec((1, LANES, W), lambda j, i: (j, 0, 0)),
                  pl.BlockSpec((1, W, LANES), lambda j, i: (j, 0, 0)),
                  pl.BlockSpec((1, 8 * SUBLANES, HS), lambda j, i: (j, 0, 0)),
                  pl.BlockSpec((1, LANES), lambda j, i: (0, j))],
        out_specs=[pl.BlockSpec((T, LANES), lambda j, i: (i, j)), pl.BlockSpec((T, W), lambda j, i: (i, j))],
        out_shape=[jax.ShapeDtypeStruct((S, D), F32), jax.ShapeDtypeStruct((S, NS * W), F32)],
        scratch_shapes=[pltpu.VMEM((T, W), F32), pltpu.VMEM((SUBLANES, W), F32)],
        compiler_params=_cparams(("parallel", "arbitrary")),
    )(u, tabs['bmat'], tabs['cmat'], tabs['coef_f'], d_skip)


def _ssm_scan_bwd(u, xs, dy, tabs, d_skip):
    S, D = u.shape
    NS = D // LANES
    W = 2 * SSM_SLAB_STATE
    T = _tile(S, 512, SUBLANES)
    HS = SSM_SLAB_STATE
    nb = S // T

    def body(u_ref, xs_ref, dy_ref, bt_ref, ct_ref, cf_ref, d_ref, du_ref, gb_ref, gc_ref, q_ref,
             cy_scr, lam_scr, carry_scr):
        i = pl.program_id(1)

        @pl.when(i == 0)
        def _():
            carry_scr[...] = jnp.zeros_like(carry_scr)
            gb_ref[...] = jnp.zeros_like(gb_ref)
            gc_ref[...] = jnp.zeros_like(gc_ref)
            q_ref[...] = jnp.zeros_like(q_ref)

        dyv = dy_ref[...]
        dyb = dyv.astype(BF16)
        cy_scr[...] = jnp.dot(dyb, ct_ref[0], preferred_element_type=F32)

        def step(n, carry):
            r = T // SUBLANES - 1 - n
            rows = pl.ds(pl.multiple_of(r * SUBLANES, SUBLANES), SUBLANES)
            cyr = cy_scr[rows, 0:HS]
            cyi = cy_scr[rows, HS:W]
            lr, li = cyr, cyi
            for m, k in enumerate((1, 2, 4)):
                br = cf_ref[0, 16 * m:16 * m + 8, :]
                bi = cf_ref[0, 16 * m + 8:16 * m + 16, :]
                sr = pltpu.roll(lr, SUBLANES - k, 0)
                si = pltpu.roll(li, SUBLANES - k, 0)
                lr, li = lr + br * sr + bi * si, li + br * si - bi * sr
            pr = cf_ref[0, 48:56, :]
            pi_ = cf_ref[0, 56:64, :]
            cr, ci, qr, qi = carry
            lr, li = lr + pr * cr + pi_ * ci, li + pr * ci - pi_ * cr
            lam_scr[rows, 0:HS] = lr
            lam_scr[rows, HS:W] = li
            mr, mi = lr - cyr, li - cyi
            xr = xs_ref[rows, 0:HS]
            xi = xs_ref[rows, HS:W]
            return lr[0:1, :], li[0:1, :], qr + mr * xr + mi * xi, qi + mi * xr - mr * xi

        zero = jnp.zeros((SUBLANES, HS), F32)
        cr, ci, qr, qi = lax.fori_loop(0, T // SUBLANES, step,
                                       (carry_scr[0:1, 0:HS], carry_scr[0:1, HS:W], zero, zero), unroll=2)
        carry_scr[0:1, 0:HS] = cr
        carry_scr[0:1, HS:W] = ci
        q_ref[0, :, 0:HS] += qr
        q_ref[0, :, HS:W] += qi
        lamb = lam_scr[...].astype(BF16)
        uv = u_ref[...]
        du_ref[...] = jnp.dot(lamb, bt_ref[0], preferred_element_type=F32) + d_ref[...] * dyv
        gb_ref[0] += lax.dot_general(lamb, uv.astype(BF16), (((0,), (0,)), ((), ())), preferred_element_type=F32)
        gc_ref[0] += lax.dot_general(xs_ref[...].astype(BF16), dyb, (((0,), (0,)), ((), ())),
                                     preferred_element_type=F32)

    rev = lambda j, i: (nb - 1 - i, j)
    slab3 = lambda j, i: (j, 0, 0)
    return pl.pallas_call(
        body, name="ssm_scan_bwd", grid=(NS, nb),
        in_specs=[pl.BlockSpec((T, LANES), rev), pl.BlockSpec((T, W), rev), pl.BlockSpec((T, LANES), rev),
                  pl.BlockSpec((1, W, LANES), slab3), pl.BlockSpec((1, LANES, W), slab3),
                  pl.BlockSpec((1, 8 * SUBLANES, HS), slab3), pl.BlockSpec((1, LANES), lambda j, i: (0, j))],
        out_specs=[pl.BlockSpec((T, LANES), rev), pl.BlockSpec((1, W, LANES), slab3),
                   pl.BlockSpec((1, W, LANES), slab3), pl.BlockSpec((1, SUBLANES, W), slab3)],
        out_shape=[jax.ShapeDtypeStruct((S, D), F32), jax.ShapeDtypeStruct((NS, W, LANES), F32),
                   jax.ShapeDtypeStruct((NS, W, LANES), F32), jax.ShapeDtypeStruct((NS, SUBLANES, W), F32)],
        scratch_shapes=[pltpu.VMEM((T, W), F32), pltpu.VMEM((T, W), F32), pltpu.VMEM((SUBLANES, W), F32)],
        compiler_params=_cparams(("parallel", "arbitrary")),
    )(u, xs, dy, tabs['bmat_t'], tabs['cmat_t'], tabs['coef_b'], d_skip)


def _ssm_param_grads(tabs, b_re, b_im, gb, gc, q):
    NS = gb.shape[0]
    G = NS * SSM_SLAB_GROUPS
    N, C = SSM_STATE, SSM_GROUP

    def diag_blocks(t):
        t = t.reshape(NS, SSM_SLAB_GROUPS, N, SSM_SLAB_GROUPS, C)
        t = jnp.einsum('sgnhc,gh->sgnc', t, jnp.eye(SSM_SLAB_GROUPS, dtype=F32))
        return t.reshape(G, N, C)

    HS = SSM_SLAB_STATE
    g_bbar = lax.complex(diag_blocks(gb[:, :HS]), diag_blocks(gb[:, HS:]))
    g_c = lax.complex(diag_blocks(gc[:, :HS]), -diag_blocks(gc[:, HS:]))
    qs = jnp.sum(q, axis=1)
    qc = lax.complex(qs[:, :HS], qs[:, HS:]).reshape(G, N)
    lam, dt, abar, coef = tabs['lam'], tabs['dt'], tabs['abar'], tabs['coef']
    bmat = lax.complex(b_re, b_im)
    g_b = g_bbar * jnp.conj(coef)[..., None]
    g_coef = jnp.sum(g_bbar * jnp.conj(bmat), axis=-1)
    g_abar_coef = g_coef * jnp.conj(1.0 / lam)
    g_lam = g_coef * jnp.conj(-(abar - 1.0) / (lam * lam))
    g_ld = qc + jnp.conj(abar) * g_abar_coef
    g_lam = g_lam + g_ld * dt
    g_dt = jnp.sum(jnp.real(g_ld * jnp.conj(lam)), axis=-1)
    g_logdt = g_dt * dt[:, 0]
    g_ct = jnp.swapaxes(g_c, 1, 2)
    return (jnp.real(g_lam), jnp.imag(g_lam), g_logdt, jnp.real(g_b), jnp.imag(g_b), jnp.real(g_ct), jnp.imag(g_ct))


_GELU_C = math.sqrt(2.0 / math.pi)


def _gelu_fwd(y):
    S, D = y.shape

    def body(y_ref, z_ref):
        v = y_ref[...]
        z_ref[...] = (0.5 * v * (1.0 + jnp.tanh(_GELU_C * (v + 0.044715 * v * v * v)))).astype(BF16)

    return _row_call(body, [(y, 'rows')], [((S, D), BF16, 'rows')], name="ssm_gelu_fwd",
                     tr=_tile(S, 512, SUBLANES), n_rows=S)[0]


def _gelu_bwd(y, dz, u):
    S, D = y.shape

    def body(y_ref, dz_ref, u_ref, dy_ref, dd_ref):
        i = pl.program_id(0)
        v = y_ref[...]
        t = jnp.tanh(_GELU_C * (v + 0.044715 * v * v * v))
        g = 0.5 * (1.0 + t) + 0.5 * v * (1.0 - t * t) * _GELU_C * (1.0 + 3 * 0.044715 * v * v)
        dy = dz_ref[...] * g
        dy_ref[...] = dy
        part = jnp.sum(dy * u_ref[...], axis=0, keepdims=True)

        @pl.when(i == 0)
        def _():
            dd_ref[...] = part

        @pl.when(i > 0)
        def _():
            dd_ref[...] += part

    return _row_call(body, [(y, 'rows'), (dz, 'rows'), (u, 'rows')], [((S, D), F32, 'rows'), ((1, D), F32, 'acc')],
                     name="ssm_gelu_bwd", tr=_tile(S, 256, SUBLANES), n_rows=S)


def _glu_fwd(zz, x):
    S, D = x.shape

    def body(a_ref, g_ref, x_ref, o_ref):
        o_ref[...] = x_ref[...] + a_ref[...] * jax.nn.sigmoid(g_ref[...])

    return _row_call(body, [(zz, ('cols', D, 0)), (zz, ('cols', D, 1)), (x, 'rows')], [((S, D), F32, 'rows')],
                     name="ssm_glu_fwd", tr=_tile(S, 256, SUBLANES), n_rows=S)[0]


def _glu_bwd(zz, dm):
    S, D = dm.shape

    def body(a_ref, g_ref, dm_ref, o_ref):
        s = jax.nn.sigmoid(g_ref[...])
        d = dm_ref[...]
        o_ref[:, 0:D] = (d * s).astype(BF16)
        o_ref[:, D:] = (d * a_ref[...] * s * (1.0 - s)).astype(BF16)

    return _row_call(body, [(zz, ('cols', D, 0)), (zz, ('cols', D, 1)), (dm, 'rows')], [((S, 2 * D), BF16, 'rows')],
                     name="ssm_glu_bwd", tr=_tile(S, 256, SUBLANES), n_rows=S)[0]


def _ssm_mixer_fwd(x, gain, tabs, d_skip, w_glu):
    _, u = _rms_fwd(x, gain, name="ssm_norm", with_f32=True)
    yv, xs = _ssm_scan_fwd(u, tabs, d_skip)
    z = _gelu_fwd(yv)
    zz = _mm(z, w_glu, name="ssm_glu_in")
    y = _glu_fwd(zz, x)
    return y, (u, xs, yv, z, zz)


def _ssm_mixer_bwd(x, gain, tabs, d_skip, w_glu, b_re, b_im, saved, dy, dyb):
    u, xs, yv, z, zz = saved
    dzz = _glu_bwd(zz, dy)
    dz = _mm(dzz, w_glu, tb=True, name="ssm_dz")
    dw_glu = _mm(z, dzz, ta=True, out_dtypes=(BF16,), name="ssm_dwglu")
    dyv, dd = _gelu_bwd(yv, dz, u)
    du, gb, gc, q = _ssm_scan_bwd(u, xs, dyv, tabs, d_skip)
    small = _ssm_param_grads(tabs, b_re, b_im, gb, gc, q)
    dx, dxb, dgain = _rms_bwd(x, gain, du, dy, name="ssm_dnorm")
    return dx, dxb, dgain, small, dd, dw_glu


class _LocalWeights:
    def __init__(self, p):
        self.p = p

    def layer(self, i):
        return {n: (v[i] if n.startswith('mlp') else v) for n, v in self.p.items()}

    def side(self, i):
        return None

    def deliver(self, i, outs):
        pass


class _LocalGrads:
    def __init__(self):
        self.g = {}

    def reduce(self, i, mats):
        for n, v in mats.items():
            if n.startswith('mlp'):
                self.g.setdefault(n, {})[i] = v
            else:
                self.g[n] = v
        return None

    def deliver(self, i, outs):
        pass


def _local_step(x, target, p, weights=None, sink=None):
    depth = p['norm_mix'].shape[0]
    local = sink is None
    weights = weights or _LocalWeights({n: p[n] for n in _BIG})
    sink = sink or _LocalGrads()
    tabs = _ssm_tables(p['ssm_a_re'], p['ssm_a_im'], p['ssm_log_dt'], p['ssm_b_re'], p['ssm_b_im'], p['ssm_c_re'],
                       p['ssm_c_im'])
    xs_in, saved_mix, saved_mlp, lw = [], [], [], []
    for i in range(depth):
        w = weights.layer(i)
        lw.append(w)
        gm = p['norm_mix'][i:i + 1]
        xs_in.append(x)
        if i % 4 == 0:
            x, sv = _conv_mixer_fwd(x, gm, w['conv_w_in'], p['conv_w'], w['conv_w_out'])
        elif i % 4 == 1:
            x, sv = _pool_mixer_fwd(x, gm, w['pool_w_in'], w['pool_w_group'], p['pool_scale'])
        elif i % 4 == 2:
            x, sv = _att_mixer_fwd(x, gm, w['att_w_qkv'], p['att_q_norm'], p['att_k_norm'], p['att_rel_bias'],
                                   w['att_w_out'])
        else:
            x, sv = _ssm_mixer_fwd(x, gm, tabs, p['ssm_d'], w['ssm_w_glu'])
        saved_mix.append(sv)
        xs_in.append(x)
        side = weights.side(i + 1) if i + 1 < depth else None
        x, sv, side_out = _mlp_fwd(x, p['norm_mlp'][i:i + 1], w['mlp_w1'], w['mlp_w2'], tag=str(i), side=side)
        if side:
            weights.deliver(i + 1, side_out)
        saved_mlp.append(sv)
    dx, dxb, loss_cols = _loss_head(x, target, name="loss_head")
    g = {'norm_mix': [None] * depth, 'norm_mlp': [None] * depth}
    sides = None
    for i in reversed(range(depth)):
        w = lw[i]
        big = {}
        dx, dxb, g['norm_mlp'][i], big['mlp_w1'], big['mlp_w2'], side_out = _mlp_bwd(
            xs_in[2 * i + 1], p['norm_mlp'][i:i + 1], w['mlp_w1'], w['mlp_w2'], saved_mlp[i], dx, dxb, tag=str(i),
            sides=sides)
        if sides:
            sink.deliver(i + 1, side_out)
        gm = p['norm_mix'][i:i + 1]
        xin, sv = xs_in[2 * i], saved_mix[i]
        if i % 4 == 0:
            dx, dxb, g['norm_mix'][i], big['conv_w_in'], g['conv_w'], big['conv_w_out'] = _conv_mixer_bwd(
                xin, gm, w['conv_w_in'], p['conv_w'], w['conv_w_out'], sv, dx, dxb)
        elif i % 4 == 1:
            dx, dxb, g['norm_mix'][i], big['pool_w_in'], big['pool_w_group'], g['pool_scale'] = _pool_mixer_bwd(
                xin, gm, w['pool_w_in'], w['pool_w_group'], p['pool_scale'], sv, dx, dxb)
        elif i % 4 == 2:
            (dx, dxb, g['norm_mix'][i], big['att_w_qkv'], g['att_q_norm'], g['att_k_norm'], g['att_rel_bias'],
             big['att_w_out']) = _att_mixer_bwd(xin, gm, w['att_w_qkv'], p['att_q_norm'], p['att_k_norm'],
                                                w['att_w_out'], sv, dx, dxb)
        else:
            dx, dxb, g['norm_mix'][i], small, g['ssm_d'], big['ssm_w_glu'] = _ssm_mixer_bwd(
                xin, gm, tabs, p['ssm_d'], w['ssm_w_glu'], p['ssm_b_re'], p['ssm_b_im'], sv, dx, dxb)
            (g['ssm_a_re'], g['ssm_a_im'], g['ssm_log_dt'], g['ssm_b_re'], g['ssm_b_im'], g['ssm_c_re'],
             g['ssm_c_im']) = small
        sides = sink.reduce(i, big)
    g['norm_mix'] = jnp.concatenate(g['norm_mix'], axis=0)
    g['norm_mlp'] = jnp.concatenate(g['norm_mlp'], axis=0)
    if local:
        for n, v in sink.g.items():
            g[n] = [v[i] for i in range(depth)] if n.startswith('mlp') else v
    return loss_cols, dx, g


_ANY = pl.BlockSpec(memory_space=pl.ANY)
_VM = pl.BlockSpec(memory_space=pltpu.VMEM)
_REL_ALL = [(0, 0, 1), (0, 1, 0), (0, 1, 1), (1, 0, 0), (1, 0, 1), (1, 1, 0), (1, 1, 1)]
_REL_CHIPS = [(1, 0, 0), (0, 1, 0), (1, 1, 0)]


def _me():
    return lax.axis_index("x"), lax.axis_index("y"), lax.axis_index("c")


def _flip(pos, rel):
    return tuple(1 - p if r else p for p, r in zip(pos, rel))


def _chip_of(pos):
    return 2 * pos[0] + pos[1]


def _dev_of(pos):
    return 4 * pos[0] + 2 * pos[1] + pos[2]


def _gather_small(buf, *, reduce, name):
    rows = buf.shape[0]

    def body(in_ref, out_ref, *rest):
        if reduce:
            gath, send_sems, recv_sems = rest
        else:
            gath = out_ref
            send_sems, recv_sems = rest
        me = _me()
        gath[_dev_of(me)] = in_ref[...]
        copies = []
        for k, rel in enumerate(_REL_ALL):
            peer = _flip(me, rel)
            cp = pltpu.make_async_remote_copy(src_ref=in_ref, dst_ref=gath.at[_dev_of(me)], send_sem=send_sems.at[k],
                                              recv_sem=recv_sems.at[k], device_id=peer, device_id_type=MESH)
            cp.start()
            copies.append(cp)
        for k, rel in enumerate(_REL_ALL):
            peer = _flip(me, rel)
            pltpu.make_async_remote_copy(src_ref=in_ref, dst_ref=gath.at[_dev_of(peer)], send_sem=send_sems.at[k],
                                         recv_sem=recv_sems.at[k], device_id=peer, device_id_type=MESH).wait_recv()
        for cp in copies:
            cp.wait_send()
        if reduce:
            acc = gath[0]
            for s in range(1, N_DEV):
                acc = acc + gath[s]
            out_ref[...] = acc

    out_shape = (rows, LANES) if reduce else (N_DEV, rows, LANES)
    scratch = ([pltpu.VMEM((N_DEV, rows, LANES), F32)] if reduce else []) + [
        pltpu.SemaphoreType.DMA((len(_REL_ALL),)), pltpu.SemaphoreType.DMA((len(_REL_ALL),))]
    return pl.pallas_call(
        body, name=name, in_specs=[_VM], out_specs=_VM, out_shape=jax.ShapeDtypeStruct(out_shape, F32),
        scratch_shapes=scratch, compiler_params=pltpu.CompilerParams(vmem_limit_bytes=VMEM_LIMIT),
    )(buf)


def _region(ref, kind, k, shard_shape, half, quarter=None):
    r, c = shard_shape
    n = r // 2 if quarter is None else r // 4
    start = half * (r // 2) + (0 if quarter is None else quarter * n)
    if kind == 'col':
        return ref.at[pl.ds(pl.multiple_of(start, 16), n), pl.ds(pl.multiple_of(k * c, LANES), c)]
    return ref.at[pl.ds(pl.multiple_of(k * r + start, 16), n), :]


def _place_block(shard, dest, layer, chip, kind, *, name):
    _, r, c = shard.shape
    tr = _tile(r, max(16, (1 << 20) // c // 16 * 16), 16)
    nb = r // tr

    def body(chip_ref, s_ref, d_ref, o_ref):
        o_ref[...] = s_ref[0].astype(BF16)

    out_map = (lambda i, k: (i, k[0])) if kind == 'col' else (lambda i, k: (k[0] * nb + i, 0))
    grid_spec = pltpu.PrefetchScalarGridSpec(
        num_scalar_prefetch=1, grid=(nb,),
        in_specs=[pl.BlockSpec((1, tr, c), lambda i, k: (layer, i, 0)), _ANY],
        out_specs=pl.BlockSpec((tr, c), out_map))
    return pl.pallas_call(
        body, name=name, grid_spec=grid_spec, out_shape=jax.ShapeDtypeStruct(dest.shape, BF16),
        input_output_aliases={2: 0}, compiler_params=_cparams(("arbitrary",)),
    )(chip, shard, dest)


def _all_gather_matrices(fulls, kinds, *, name):
    return _run_side(_gather_side(fulls, kinds), name=name)


class _Ops:
    def __init__(self, active):
        self.active, self.cur, self.sends = active, 0, []

    def phase(self, p):
        self.cur = p

    def send(self, src, dst, sems_s, sems_r, idx, to):
        cp = pltpu.make_async_remote_copy(src_ref=src, dst_ref=dst, send_sem=sems_s.at[idx], recv_sem=sems_r.at[idx],
                                          device_id=to, device_id_type=MESH)
        if self.cur == self.active:
            cp.start()
        self.sends.append(cp)

    def arrived(self, where, sems_s, sems_r, idx, frm):
        if self.cur == self.active:
            pltpu.make_async_remote_copy(src_ref=where, dst_ref=where, send_sem=sems_s.at[idx], recv_sem=sems_r.at[idx],
                                         device_id=frm, device_id_type=MESH).wait_recv()


def _phases(plan, n_phases):
    def make(p):
        def run(ins, outs, sems):
            op = _Ops(p)
            plan(ins, outs, sems, op)
            if p == n_phases - 1:
                for cp in op.sends:
                    cp.wait_send()
        return run
    return [make(p) for p in range(n_phases)]


def _gather_side(fulls, kinds):
    n = len(fulls)
    shards = [(f.shape[0], f.shape[1] // N_CHIPS) if kd == 'col' else (f.shape[0] // N_CHIPS, f.shape[1])
              for f, kd in zip(fulls, kinds)]

    def plan(in_refs, out_refs, sems, op):
        ici_send, ici_recv, fwd_send, fwd_recv, d2d_send, d2d_recv = sems
        me = _me()
        core = me[2]
        sib = _flip(me, (0, 0, 1))
        nbr = [_flip(me, (1, 0, 0)), _flip(me, (0, 1, 0))]
        diag = _chip_of(_flip(me, (1, 1, 0)))
        op.phase(0)
        for m in range(n):
            src = _region(in_refs[m], kinds[m], _chip_of(me), shards[m], core)
            dst = _region(out_refs[m], kinds[m], _chip_of(me), shards[m], core)
            for j in range(2):
                op.send(src, dst, ici_send, ici_recv, (m, j), nbr[j])
        op.phase(1)
        for m in range(n):
            for j in range(2):
                landed = _region(out_refs[m], kinds[m], _chip_of(nbr[j]), shards[m], core)
                op.arrived(landed, ici_send, ici_recv, (m, j), nbr[j])
                part = _region(out_refs[m], kinds[m], _chip_of(nbr[j]), shards[m], core, quarter=j)
                op.send(part, part, fwd_send, fwd_recv, (m, j), nbr[1 - j])
                op.send(landed, landed, d2d_send, d2d_recv, (m, j), sib)
        op.phase(2)
        for m in range(n):
            for j in range(2):
                part = _region(out_refs[m], kinds[m], diag, shards[m], core, quarter=j)
                op.arrived(part, fwd_send, fwd_recv, (m, j), nbr[1 - j])
                op.send(part, part, d2d_send, d2d_recv, (m, 2 + j), sib)
        op.phase(3)
        for m in range(n):
            for j in range(2):
                op.arrived(_region(out_refs[m], kinds[m], _chip_of(nbr[j]), shards[m], 1 - core),
                           d2d_send, d2d_recv, (m, j), sib)
                op.arrived(_region(out_refs[m], kinds[m], diag, shards[m], 1 - core, quarter=j),
                           d2d_send, d2d_recv, (m, 2 + j), sib)

    return _Side(list(fulls), [jax.ShapeDtypeStruct(f.shape, BF16) for f in fulls], {m: m for m in range(n)},
                 [pltpu.SemaphoreType.DMA((n, 2))] * 4 + [pltpu.SemaphoreType.DMA((n, 4))] * 2, _phases(plan, 4))


def _piece_shape(full, kind):
    R, C = full
    return (R // 2, C // N_CHIPS) if kind == 'col' else (R // N_DEV, C)


def _piece(ref, kind, k, c, full):
    pr, pc = _piece_shape(full, kind)
    if kind == 'col':
        return ref.at[pl.ds(pl.multiple_of(c * pr, 16), pr), pl.ds(pl.multiple_of(k * pc, LANES), pc)]
    return ref.at[pl.ds(pl.multiple_of((2 * k + c) * pr, 16), pr), :]


def _pair_exchange(grads, kinds, *, name):
    n = len(grads)
    pieces = [_piece_shape(g.shape, kd) for g, kd in zip(grads, kinds)]

    def body(*refs):
        in_refs, out_refs = refs[:n], refs[n:2 * n]
        send_sems, recv_sems = refs[2 * n:]
        me = _me()
        core = me[2]
        sib = _flip(me, (0, 0, 1))
        sends = []
        for m in range(n):
            for k in range(N_CHIPS):
                cp = pltpu.make_async_remote_copy(
                    src_ref=_piece(in_refs[m], kinds[m], k, 1 - core, grads[m].shape), dst_ref=out_refs[m].at[k],
                    send_sem=send_sems.at[m, k], recv_sem=recv_sems.at[m, k], device_id=sib, device_id_type=MESH)
                cp.start()
                sends.append(cp)
        for m in range(n):
            for k in range(N_CHIPS):
                pltpu.make_async_remote_copy(
                    src_ref=_piece(in_refs[m], kinds[m], k, core, grads[m].shape), dst_ref=out_refs[m].at[k],
                    send_sem=send_sems.at[m, k], recv_sem=recv_sems.at[m, k], device_id=sib,
                    device_id_type=MESH).wait_recv()
        for cp in sends:
            cp.wait_send()

    return pl.pallas_call(
        body, name=name, in_specs=[_ANY] * n, out_specs=[_ANY] * n,
        out_shape=[jax.ShapeDtypeStruct((N_CHIPS,) + p, BF16) for p in pieces],
        scratch_shapes=[pltpu.SemaphoreType.DMA((n, N_CHIPS)), pltpu.SemaphoreType.DMA((n, N_CHIPS))],
    )(*grads)


def _pair_sum(g, recv, core, kind, *, name):
    pr, pc = _piece_shape(g.shape, kind)
    tr = _tile(pr, max(16, (1 << 19) // pc // 16 * 16), 16)
    nb = pr // tr

    def body(core_ref, g_ref, r_ref, o_ref):
        o_ref[0] = (g_ref[...].astype(F32) + r_ref[0].astype(F32)).astype(BF16)

    g_map = (lambda k, i, c: (c[0] * nb + i, k)) if kind == 'col' else (lambda k, i, c: ((2 * k + c[0]) * nb + i, 0))
    slot = pl.BlockSpec((1, tr, pc), lambda k, i, c: (k, i, 0))
    grid_spec = pltpu.PrefetchScalarGridSpec(
        num_scalar_prefetch=1, grid=(N_CHIPS, nb), in_specs=[pl.BlockSpec((tr, pc), g_map), slot], out_specs=slot)
    return pl.pallas_call(
        body, name=name, grid_spec=grid_spec, out_shape=jax.ShapeDtypeStruct((N_CHIPS, pr, pc), BF16),
        compiler_params=_cparams(("arbitrary", "arbitrary")),
    )(core, g, recv)


def _chip_scatter(sums, *, name):
    return _run_side(_chip_scatter_side(sums), name=name)


def _chip_scatter_side(sums):
    n = len(sums)
    nj = len(_REL_CHIPS)

    def plan(in_refs, out_refs, sems, op):
        send_sems, recv_sems = sems
        me = _me()
        op.phase(0)
        for m in range(n):
            for j, rel in enumerate(_REL_CHIPS):
                peer = _flip(me, rel)
                op.send(in_refs[m].at[_chip_of(peer)], out_refs[m].at[j], send_sems, recv_sems, (m, j), peer)
        op.phase(1)
        for m in range(n):
            for j, rel in enumerate(_REL_CHIPS):
                op.arrived(out_refs[m].at[j], send_sems, recv_sems, (m, j), _flip(me, rel))

    return _Side(list(sums), [jax.ShapeDtypeStruct((nj,) + s.shape[1:], BF16) for s in sums], {},
                 [pltpu.SemaphoreType.DMA((n, nj)), pltpu.SemaphoreType.DMA((n, nj))], _phases(plan, 2))


def _sum_into(own, recv, dest, layer, core, chip, *, name):
    _, pr, pc = own.shape
    tr = _tile(pr, max(16, (1 << 19) // pc // 16 * 16), 16)
    nb = pr // tr

    def body(core_ref, chip_ref, own_ref, r_ref, d_ref, o_ref):
        acc = own_ref[0].astype(F32)
        for s in range(len(_REL_CHIPS)):
            acc = acc + r_ref[s].astype(F32)
        o_ref[0] = acc

    grid_spec = pltpu.PrefetchScalarGridSpec(
        num_scalar_prefetch=2, grid=(nb,),
        in_specs=[pl.BlockSpec((1, tr, pc), lambda i, c, k: (k[0], i, 0)),
                  pl.BlockSpec((len(_REL_CHIPS), tr, pc), lambda i, c, k: (0, i, 0)), _ANY],
        out_specs=pl.BlockSpec((1, tr, pc), lambda i, c, k: (layer, c[0] * nb + i, 0)))
    return pl.pallas_call(
        body, name=name, grid_spec=grid_spec, out_shape=jax.ShapeDtypeStruct(dest.shape, F32),
        input_output_aliases={4: 0}, compiler_params=_cparams(("arbitrary",)),
    )(core, chip, own, recv, dest)


def _exchange_halves(blocks, *, name):
    n_out = len(blocks)
    n = sum(b.shape[0] for b in blocks)

    def body(*refs):
        in_refs, out_refs = refs[:n_out], refs[n_out:2 * n_out]
        send_sems, recv_sems = refs[2 * n_out:]
        me = _me()
        sib = _flip(me, (0, 0, 1))
        sends, m = [], 0
        for o in range(n_out):
            L, r2, _ = blocks[o].shape
            hr = r2 // 2
            for l in range(L):
                rows = pl.ds(pl.multiple_of(me[2] * hr, SUBLANES), hr)
                cp = pltpu.make_async_remote_copy(src_ref=in_refs[o].at[l, rows, :], dst_ref=out_refs[o].at[l, rows, :],
                                                  send_sem=send_sems.at[m], recv_sem=recv_sems.at[m], device_id=sib,
                                                  device_id_type=MESH)
                cp.start()
                sends.append(cp)
                m += 1
        m = 0
        for o in range(n_out):
            L, r2, _ = blocks[o].shape
            hr = r2 // 2
            for l in range(L):
                rows = pl.ds(pl.multiple_of(sib[2] * hr, SUBLANES), hr)
                theirs = out_refs[o].at[l, rows, :]
                pltpu.make_async_remote_copy(src_ref=theirs, dst_ref=theirs, send_sem=send_sems.at[m],
                                             recv_sem=recv_sems.at[m], device_id=sib, device_id_type=MESH).wait_recv()
                m += 1
        for cp in sends:
            cp.wait_send()

    return pl.pallas_call(
        body, name=name, in_specs=[_ANY] * n_out, out_specs=[_ANY] * n_out,
        out_shape=[jax.ShapeDtypeStruct(b.shape, F32) for b in blocks],
        input_output_aliases={o: o for o in range(n_out)},
        scratch_shapes=[pltpu.SemaphoreType.DMA((n,)), pltpu.SemaphoreType.DMA((n,))],
    )(*blocks)


def _adamw(w, g, m, v, *, name):
    R, C = w.shape
    tr = _tile(R, max(SUBLANES, (1 << 19) // C // SUBLANES * SUBLANES), SUBLANES)
    c1 = 1.0 / (1.0 - ADAM_B1 ** ADAM_STEP)
    c2 = 1.0 / (1.0 - ADAM_B2 ** ADAM_STEP)

    def body(w_ref, g_ref, m_ref, v_ref, go_ref, d_ref, nm_ref, nv_ref):
        gv = g_ref[...]
        go_ref[...] = gv
        nm = ADAM_B1 * m_ref[...] + (1.0 - ADAM_B1) * gv
        nv = ADAM_B2 * v_ref[...] + (1.0 - ADAM_B2) * (gv * gv)
        nm_ref[...] = nm
        nv_ref[...] = nv
        d_ref[...] = -ADAM_LR * ((nm * c1) / (jnp.sqrt(nv * c2) + ADAM_EPS) + ADAM_WD * w_ref[...])

    blk = pl.BlockSpec((tr, C), lambda i: (i, 0))
    return pl.pallas_call(
        body, name=name, grid=(R // tr,), in_specs=[blk] * 4, out_specs=[blk] * 4,
        out_shape=[jax.ShapeDtypeStruct((R, C), F32)] * 4, compiler_params=_cparams(("parallel",)),
    )(w, g, m, v)


_BIG = ['mlp_w1', 'mlp_w2', 'conv_w_in', 'conv_w_out', 'pool_w_in', 'pool_w_group', 'att_w_qkv', 'att_w_out',
        'ssm_w_glu']
_KIND = {'mlp_w1': 'col', 'mlp_w2': 'row', 'conv_w_in': 'col', 'conv_w_out': 'row', 'pool_w_in': 'row',
         'pool_w_group': 'row', 'att_w_qkv': 'col', 'att_w_out': 'row', 'ssm_w_glu': 'col'}
_TINY_SHARDED = ['conv_w', 'pool_scale', 'ssm_d']
_REPLICATED = ['norm_mix', 'norm_mlp', 'att_q_norm', 'att_k_norm', 'att_rel_bias', 'ssm_a_re', 'ssm_a_im',
               'ssm_log_dt', 'ssm_b_re', 'ssm_b_im', 'ssm_c_re', 'ssm_c_im']
_SMALL = _REPLICATED + _TINY_SHARDED
_ORDER = ['norm_mix', 'norm_mlp', 'mlp_w1', 'mlp_w2', 'conv_w_in', 'conv_w', 'conv_w_out', 'pool_w_in',
          'pool_w_group', 'pool_scale', 'att_w_qkv', 'att_q_norm', 'att_k_norm', 'att_rel_bias', 'att_w_out',
          'ssm_a_re', 'ssm_a_im', 'ssm_log_dt', 'ssm_b_re', 'ssm_b_im', 'ssm_c_re', 'ssm_c_im', 'ssm_d', 'ssm_w_glu']
_LAYER_OF = {'conv_w_in': 0, 'conv_w_out': 0, 'pool_w_in': 1, 'pool_w_group': 1, 'att_w_qkv': 2, 'att_w_out': 2,
             'ssm_w_glu': 3}


def _pack(arrays):
    flat = jnp.concatenate([a.reshape(-1).astype(F32) for a in arrays])
    n = flat.shape[0]
    total = -(-n // (SUBLANES * LANES)) * (SUBLANES * LANES)
    return jnp.pad(flat, (0, total - n)).reshape(total // LANES, LANES)


def _unpack(buf, shapes):
    flat = buf.reshape(-1)
    out, off = [], 0
    for s in shapes:
        n = int(np.prod(s))
        out.append(flat[off:off + n].reshape(s))
        off += n
    return out


def _matrices(w, name):
    t = w.reshape((-1,) + w.shape[-2:])
    return [t[l] for l in range(t.shape[0])]


def kernel(x, norm_mix, norm_mlp, mlp_w1, mlp_w2, conv_w_in, conv_w, conv_w_out, pool_w_in, pool_w_group, pool_scale, att_w_qkv, att_q_norm, att_k_norm, att_rel_bias, att_w_out, ssm_a_re, ssm_a_im, ssm_log_dt, ssm_b_re, ssm_b_im, ssm_c_re, ssm_c_im, ssm_d, ssm_w_glu, loss_target, m_norm_mix, m_norm_mlp, m_mlp_w1, m_mlp_w2, m_conv_w_in, m_conv_w, m_conv_w_out, m_pool_w_in, m_pool_w_group, m_pool_scale, m_att_w_qkv, m_att_q_norm, m_att_k_norm, m_att_rel_bias, m_att_w_out, m_ssm_a_re, m_ssm_a_im, m_ssm_log_dt, m_ssm_b_re, m_ssm_b_im, m_ssm_c_re, m_ssm_c_im, m_ssm_d, m_ssm_w_glu, v_norm_mix, v_norm_mlp, v_mlp_w1, v_mlp_w2, v_conv_w_in, v_conv_w, v_conv_w_out, v_pool_w_in, v_pool_w_group, v_pool_scale, v_att_w_qkv, v_att_q_norm, v_att_k_norm, v_att_rel_bias, v_att_w_out, v_ssm_a_re, v_ssm_a_im, v_ssm_log_dt, v_ssm_b_re, v_ssm_b_im, v_ssm_c_re, v_ssm_c_im, v_ssm_d, v_ssm_w_glu):
    args = dict(locals())
    W = {n: args[n] for n in _ORDER}
    M = {n: args['m_' + n] for n in _ORDER}
    V = {n: args['v_' + n] for n in _ORDER}
    depth = norm_mix.shape[0]
    d_model = x.shape[-1]
    chip = 2 * lax.axis_index("x") + lax.axis_index("y")

    tiny = _gather_small(_pack([W[n] for n in _TINY_SHARDED]), reduce=False, name="gather_vectors")
    tiny_shapes = [W[n].shape for n in _TINY_SHARDED]
    per_chip = [_unpack(tiny[2 * k], tiny_shapes) for k in range(N_CHIPS)]
    full_tiny = {n: jnp.concatenate([per_chip[k][i] for k in range(N_CHIPS)], axis=-1)
                 for i, n in enumerate(_TINY_SHARDED)}
    chip_arr = chip.astype(jnp.int32).reshape(1)
    groups = [[] for _ in range(depth)]
    shard3 = {n: W[n].reshape((-1,) + W[n].shape[-2:]) for n in _BIG}
    for n in _BIG:
        for l in range(shard3[n].shape[0]):
            groups[l if n.startswith('mlp') else _LAYER_OF[n]].append((n, l))
    core = lax.axis_index("c").astype(jnp.int32).reshape(1)
    kinds_of = [[_KIND[n] for n, _ in grp] for grp in groups]

    class Weights:
        def __init__(self):
            self.full = {}

        def _placed(self, i):
            placed = []
            for n, l in groups[i]:
                _, r, c = shard3[n].shape
                shape = (r, c * N_CHIPS) if _KIND[n] == 'col' else (r * N_CHIPS, c)
                placed.append(_place_block(shard3[n], lax.empty(shape, BF16), l, chip_arr, _KIND[n],
                                           name=f"place_{n}_{l}"))
            return placed

        def side(self, i):
            return _gather_side(self._placed(i), kinds_of[i])

        def deliver(self, i, outs):
            mats = {}
            for (n, l), t in zip(groups[i], outs):
                mats.setdefault(n, []).append(t)
            self.full[i] = {n: (jnp.stack(v) if n == 'pool_w_group' else v[0]) for n, v in mats.items()}

        def layer(self, i):
            if i not in self.full:
                self.deliver(i, _all_gather_matrices(self._placed(i), kinds_of[i], name=f"gather_weights_{i}"))
            return self.full[i]

    class Sink:
        def __init__(self):
            self.blocks = {n: lax.empty((shard3[n].shape[0],) + shard3[n].shape[1:], F32) for n in _BIG}
            self.pair_sums = {}

        def reduce(self, i, big):
            mats = [(_matrices(big[n], n)[l] if n == 'pool_w_group' else big[n]) for n, l in groups[i]]
            from_sibling = _pair_exchange(mats, kinds_of[i], name=f"pair_exchange_{i}")
            sums = [_pair_sum(gm, t, core, kd, name=f"pair_sum_{n}_{l}")
                    for (n, l), gm, t, kd in zip(groups[i], mats, from_sibling, kinds_of[i])]
            self.pair_sums[i] = sums
            if i == 0:
                self._finish(i, list(range(len(sums))), _chip_scatter(sums, name=f"chip_scatter_{i}"))
                return None
            self.split = ([m for m, (n, _) in enumerate(groups[i]) if n.startswith('mlp')],
                          [m for m, (n, _) in enumerate(groups[i]) if not n.startswith('mlp')])
            return [_chip_scatter_side([sums[m] for m in part]) for part in self.split]

        def deliver(self, i, outs):
            for part, got in zip(self.split, outs):
                self._finish(i, part, got)

        def _finish(self, i, which, got):
            for m, t in zip(which, got):
                n, l = groups[i][m]
                self.blocks[n] = _sum_into(self.pair_sums[i][m], t, self.blocks[n], l, core, chip_arr,
                                           name=f"sum_grads_{n}_{l}")

    p = dict(
        norm_mix=norm_mix, norm_mlp=norm_mlp, conv_w=full_tiny['conv_w'][0], pool_scale=full_tiny['pool_scale'],
        att_q_norm=att_q_norm, att_k_norm=att_k_norm, att_rel_bias=att_rel_bias[0],
        ssm_a_re=ssm_a_re[0], ssm_a_im=ssm_a_im[0], ssm_log_dt=ssm_log_dt[0], ssm_b_re=ssm_b_re[0],
        ssm_b_im=ssm_b_im[0], ssm_c_re=ssm_c_re[0], ssm_c_im=ssm_c_im[0], ssm_d=full_tiny['ssm_d'])

    sink = Sink()
    loss_cols, dx, g = _local_step(x[0], loss_target[0], p, Weights(), sink)
    loss = lax.psum(0.5 * jnp.sum(loss_cols) / d_model, ("x", "y", "c"))
    reduced = _exchange_halves([sink.blocks[n] for n in _BIG], name="exchange_halves")
    grads = {n: t.reshape(W[n].shape) for n, t in zip(_BIG, reduced)}

    small_full_shapes = [W[n].shape for n in _REPLICATED] + [full_tiny[n].shape for n in _TINY_SHARDED]
    gsmall = _gather_small(_pack([g[n] for n in _SMALL]), reduce=True, name="reduce_small_grads")
    for n, t in zip(_SMALL, _unpack(gsmall, small_full_shapes)):
        if n in _TINY_SHARDED:
            width = W[n].shape[-1]
            t = lax.dynamic_slice_in_dim(t, chip * width, width, axis=t.ndim - 1)
        grads[n] = t.reshape(W[n].shape)

    delta, new_m, new_v = {}, {}, {}
    for n in _BIG:
        shp = W[n].shape
        two = (int(np.prod(shp[:-1])), shp[-1])
        gout, d, nm, nv = _adamw(W[n].reshape(two), grads[n].reshape(two), M[n].reshape(two), V[n].reshape(two),
                                 name=f"adamw_{n}")
        grads[n], delta[n], new_m[n], new_v[n] = gout.reshape(shp), d.reshape(shp), nm.reshape(shp), nv.reshape(shp)
    shapes = [W[n].shape for n in _SMALL]
    _, d, nm, nv = _adamw(_pack([W[n] for n in _SMALL]), _pack([grads[n] for n in _SMALL]),
                          _pack([M[n] for n in _SMALL]), _pack([V[n] for n in _SMALL]), name="adamw_small")
    for n, a, b, c in zip(_SMALL, _unpack(d, shapes), _unpack(nm, shapes), _unpack(nv, shapes)):
        delta[n], new_m[n], new_v[n] = a, b, c

    return (loss, dx.reshape(x.shape), *[grads[n] for n in _ORDER], *[delta[n] for n in _ORDER],
            *[new_m[n] for n in _ORDER], *[new_v[n] for n in _ORDER])
```

```python
import functools
import math

import numpy as np
import jax
import jax.numpy as jnp
from jax import lax
from jax.experimental import pallas as pl
from jax.experimental.pallas import tpu as pltpu

F32 = jnp.float32
BF16 = jnp.bfloat16
MESH = pl.DeviceIdType.MESH

V7X_VMEM_BYTES = 64 * 1024 * 1024
VMEM_LIMIT = V7X_VMEM_BYTES - 12 * 1024 * 1024
LANES = 128
SUBLANES = 8

CHUNK = 64
ATT_HEAD_DIM = 128
ATT_PAD = 8 * CHUNK
REL_CLIP = 256
MASK_VALUE = -1e30
POOL_WINDOWS = (2, 4, 8, 16)
POOL_HALO = 16
SSM_GROUP = 16
SSM_STATE = 64
SSM_SLAB_GROUPS = LANES // SSM_GROUP
SSM_SLAB_STATE = SSM_SLAB_GROUPS * SSM_STATE
RMS_EPS = 1e-6
ADAM_LR, ADAM_B1, ADAM_B2, ADAM_EPS, ADAM_WD, ADAM_STEP = 0.001, 0.9, 0.999, 1e-08, 0.01, 10
ATT_TQ = 256
N_CHIPS = 4
N_DEV = 8


def _cparams(sem=None, **kw):
    return pltpu.CompilerParams(dimension_semantics=sem, vmem_limit_bytes=VMEM_LIMIT, **kw)


def _tile(n, target, mult):
    if n <= target:
        return n
    t = (target // mult) * mult
    while t > mult and n % t:
        t -= mult
    assert n % t == 0, (n, target, mult)
    return t


class _Side:
    def __init__(self, arrays, outs, aliases, sems, phases):
        self.arrays, self.outs, self.aliases, self.sems, self.phases = arrays, outs, aliases, sems, phases


def _run_side(side, *, name):
    n_in, n_out = len(side.arrays), len(side.outs)

    def body(*refs):
        for phase in side.phases:
            phase(refs[:n_in], refs[n_in:n_in + n_out], refs[n_in + n_out:])

    return pl.pallas_call(
        body, name=name, in_specs=[_ANY] * n_in, out_specs=[_ANY] * n_out, out_shape=list(side.outs),
        input_output_aliases=dict(side.aliases), scratch_shapes=list(side.sems),
    )(*side.arrays)


def _mm(a, b, *, ta=False, tb=False, extras=(), epilogue=None, out_dtypes=(F32,), name,
        tm=1024, tn=1024, tk=2048, side=None):
    M, K = (a.shape[1], a.shape[0]) if ta else a.shape
    N = b.shape[0] if tb else b.shape[1]
    assert (b.shape[1] if tb else b.shape[0]) == K, (a.shape, b.shape, ta, tb)
    tm, tn, tk = _tile(M, tm, LANES), _tile(N, tn, LANES), _tile(K, tk, LANES)
    nk = K // tk
    gm, gn = M // tm, N // tn
    n_ex, n_out = len(extras), len(out_dtypes)
    n_sin = len(side.arrays) if side else 0
    n_sout = len(side.outs) if side else 0
    n_sems = len(side.sems) if side else 0
    dn = (((0 if ta else 1,), (1 if tb else 0,)), ((), ()))
    if side:
        steps = gm * gn * nk
        n_ph = len(side.phases)
        at = [p * (steps - 1) // max(n_ph - 1, 1) for p in range(n_ph)]

    def body(*refs):
        a_ref, b_ref = refs[0], refs[1]
        ex_refs = refs[2:2 + n_ex]
        n_in = 2 + n_ex + n_sin
        o_refs = refs[n_in:n_in + n_out]
        if side:
            s_in = refs[2 + n_ex:n_in]
            s_out = refs[n_in + n_out:n_in + n_out + n_sout]
            s_sems = refs[len(refs) - n_sems:]
            t = (pl.program_id(0) * gn + pl.program_id(1)) * nk + pl.program_id(2)
            for phase, when in zip(side.phases, at):
                @pl.when(t == when)
                def _(phase=phase):
                    phase(s_in, s_out, s_sems)

        p = lax.dot_general(a_ref[...], b_ref[...], dn, preferred_element_type=F32)

        def finish(acc):
            outs = (acc,) if epilogue is None else epilogue(acc, *[r[...] for r in ex_refs])
            for o_ref, o in zip(o_refs, outs):
                o_ref[...] = o.astype(o_ref.dtype)

        if nk == 1:
            finish(p)
        else:
            acc_ref = refs[n_in + n_out + n_sout]
            k = pl.program_id(2)

            @pl.when(k == 0)
            def _():
                acc_ref[...] = p

            @pl.when(k > 0)
            def _():
                acc_ref[...] += p

            @pl.when(k == nk - 1)
            def _():
                finish(acc_ref[...])

    a_spec = pl.BlockSpec((tk, tm), lambda i, j, k: (k, i)) if ta else pl.BlockSpec((tm, tk), lambda i, j, k: (i, k))
    b_spec = pl.BlockSpec((tn, tk), lambda i, j, k: (j, k)) if tb else pl.BlockSpec((tk, tn), lambda i, j, k: (k, j))
    mn_spec = pl.BlockSpec((tm, tn), lambda i, j, k: (i, j))
    outs = pl.pallas_call(
        body, name=name, grid=(gm, gn, nk),
        in_specs=[a_spec, b_spec] + [mn_spec] * n_ex + [_ANY] * n_sin,
        out_specs=[mn_spec] * n_out + [_ANY] * n_sout,
        out_shape=[jax.ShapeDtypeStruct((M, N), d) for d in out_dtypes] + (list(side.outs) if side else []),
        input_output_aliases={2 + n_ex + i: n_out + o for i, o in side.aliases.items()} if side else {},
        scratch_shapes=([pltpu.VMEM((tm, tn), F32)] if nk > 1 else []) + (list(side.sems) if side else []),
        compiler_params=_cparams(("arbitrary",) * 3 if side else ("parallel", "parallel", "arbitrary")),
    )(a, b, *extras, *(side.arrays if side else ()))
    res = outs[0] if n_out == 1 else tuple(outs[:n_out])
    return (res, list(outs[n_out:])) if side else res


def _row_call(body, ins, outs, *, name, tr, n_rows, acc_outs=(), scratch=(), halo=None):
    nb = n_rows // tr
    hb = halo or SUBLANES
    per = tr // hb
    last = n_rows // hb - 1

    def spec(arr_shape, kind):
        if kind == 'rows':
            return pl.BlockSpec((tr,) + tuple(arr_shape[1:]), lambda i: (i,) + (0,) * (len(arr_shape) - 1))
        if kind == 'full' or kind == 'acc':
            return pl.BlockSpec(tuple(arr_shape), lambda i: (0,) * len(arr_shape))
        tag, w, j = kind
        if tag == 'cols':
            return pl.BlockSpec((tr, w), lambda i: (i, j))
        if tag == 'rows_from':
            return pl.BlockSpec((tr, w), lambda i: (i + j // tr, 0))
        if tag == 'prev':
            return pl.BlockSpec((hb, w), lambda i: (jnp.maximum(i * per - 1, 0), j))
        if tag == 'next':
            return pl.BlockSpec((hb, w), lambda i: (jnp.minimum((i + 1) * per, last), j))
        raise ValueError(kind)

    return pl.pallas_call(
        body, name=name, grid=(nb,),
        in_specs=[spec(a.shape, k) for a, k in ins],
        out_specs=[spec(s, k) for s, _, k in outs],
        out_shape=[jax.ShapeDtypeStruct(s, d) for s, d, _ in outs],
        scratch_shapes=list(scratch),
        compiler_params=_cparams(("arbitrary",)),
    )(*[a for a, _ in ins])


def _rms_fwd(x, gain, *, name, with_f32=False):
    S, D = x.shape
    tr = _tile(S, 512, SUBLANES)

    def body(x_ref, g_ref, *o_refs):
        xv = x_ref[...]
        r = lax.rsqrt(jnp.mean(xv * xv, axis=-1, keepdims=True) + RMS_EPS)
        h = xv * r * g_ref[...]
        o_refs[0][...] = h.astype(BF16)
        if with_f32:
            o_refs[1][...] = h

    outs = [((S, D), BF16, 'rows')] + ([((S, D), F32, 'rows')] if with_f32 else [])
    res = _row_call(body, [(x, 'rows'), (gain, 'full')], outs, name=name, tr=tr, n_rows=S)
    return tuple(res) if with_f32 else res[0]


def _rms_bwd(x, gain, dh, dres, *, name):
    S, D = x.shape
    tr = _tile(S, 256, SUBLANES)

    def body(x_ref, g_ref, dh_ref, dr_ref, dx_ref, dxb_ref, dg_ref):
        i = pl.program_id(0)
        xv = x_ref[...]
        r = lax.rsqrt(jnp.mean(xv * xv, axis=-1, keepdims=True) + RMS_EPS)
        xn = xv * r
        dhv = dh_ref[...]
        dxn = dhv * g_ref[...]
        dx = r * (dxn - xn * jnp.mean(dxn * xn, axis=-1, keepdims=True)) + dr_ref[...]
        dx_ref[...] = dx
        dxb_ref[...] = dx.astype(BF16)
        part = jnp.sum(dhv * xn, axis=0, keepdims=True)

        @pl.when(i == 0)
        def _():
            dg_ref[...] = part

        @pl.when(i > 0)
        def _():
            dg_ref[...] += part

    return _row_call(body, [(x, 'rows'), (gain, 'full'), (dh, 'rows'), (dres, 'rows')],
                     [((S, D), F32, 'rows'), ((S, D), BF16, 'rows'), ((1, D), F32, 'acc')],
                     name=name, tr=tr, n_rows=S)


def _loss_head(y, target, *, name):
    S, D = y.shape
    tr = _tile(S, 512, SUBLANES)

    def body(y_ref, t_ref, d_ref, db_ref, l_ref):
        i = pl.program_id(0)
        e = y_ref[...] - t_ref[...]
        d = e * (1.0 / D)
        d_ref[...] = d
        db_ref[...] = d.astype(BF16)
        part = jnp.sum(e * e, axis=0, keepdims=True)

        @pl.when(i == 0)
        def _():
            l_ref[...] = part

        @pl.when(i > 0)
        def _():
            l_ref[...] += part

    return _row_call(body, [(y, 'rows'), (target, 'rows')],
                     [((S, D), F32, 'rows'), ((S, D), BF16, 'rows'), ((1, D), F32, 'acc')],
                     name=name, tr=tr, n_rows=S)


def _relu2_epilogue(acc):
    r = jnp.maximum(acc, 0.0)
    return r, r * r


def _with_side(res, side):
    return res if side else (res, None)


def _mlp_fwd(x, gain, w1, w2, *, tag, sides=(None, None)):
    h = _rms_fwd(x, gain, name=f"mlp_norm_{tag}")
    (r, act), out0 = _with_side(_mm(h, w1, epilogue=_relu2_epilogue, out_dtypes=(BF16, BF16), name=f"mlp_up_{tag}",
                                    side=sides[0]), sides[0])
    y, out1 = _with_side(_mm(act, w2, extras=(x,), epilogue=lambda acc, res: (acc + res,), name=f"mlp_down_{tag}",
                             side=sides[1]), sides[1])
    return y, (h, r, act), (out0, out1)


def _mlp_bwd(x, gain, w1, w2, saved, dy, dyb, *, tag, sides=(None, None, None)):
    h, r, act = saved
    da, out0 = _with_side(_mm(dyb, w2, tb=True, extras=(r,), epilogue=lambda acc, rr: (acc * (2.0 * rr.astype(F32)),),
                              out_dtypes=(BF16,), name=f"mlp_dact_{tag}", side=sides[0]), sides[0])
    dw2, out1 = _with_side(_mm(act, dyb, ta=True, out_dtypes=(BF16,), name=f"mlp_dw2_{tag}", side=sides[1]), sides[1])
    dw1, out2 = _with_side(_mm(h, da, ta=True, out_dtypes=(BF16,), name=f"mlp_dw1_{tag}", side=sides[2]), sides[2])
    dh = _mm(da, w1, tb=True, name=f"mlp_dh_{tag}")
    dx, dxb, dgain = _rms_bwd(x, gain, dh, dy, name=f"mlp_dnorm_{tag}")
    return dx, dxb, dgain, dw1, dw2, (out0, out1, out2)


def _conv_gate_fwd(z, conv_w):
    S, D3 = z.shape
    D = D3 // 3
    tr = _tile(S, 256, SUBLANES)

    def body(b_ref, c_ref, v_ref, cp_ref, vp_ref, w_ref, g_ref, scr):
        i = pl.program_id(0)
        u = c_ref[...] * v_ref[...]
        scr[0:SUBLANES, :] = cp_ref[...] * vp_ref[...] * (i > 0).astype(F32)
        scr[SUBLANES:, :] = u
        conv = (w_ref[0:1, :] * scr[pl.ds(SUBLANES - 2, tr), :] + w_ref[1:2, :] * scr[pl.ds(SUBLANES - 1, tr), :]
                + w_ref[2:3, :] * u)
        g_ref[...] = (b_ref[...] * conv).astype(BF16)

    ins = [(z, ('cols', D, 0)), (z, ('cols', D, 1)), (z, ('cols', D, 2)), (z, ('prev', D, 1)), (z, ('prev', D, 2)),
           (conv_w, 'full')]
    return _row_call(body, ins, [((S, D), BF16, 'rows')], name="conv_gate_fwd", tr=tr, n_rows=S,
                     scratch=[pltpu.VMEM((tr + SUBLANES, D), F32)])[0]


def _conv_gate_bwd(z, conv_w, dg):
    S, D3 = z.shape
    D = D3 // 3
    tr = _tile(S, 128, SUBLANES)
    nb = S // tr

    def body(b_ref, c_ref, v_ref, cp_ref, vp_ref, bn_ref, dg_ref, dgn_ref, w_ref, dz_ref, dw_ref, scr, scr2):
        i = pl.program_id(0)
        c, v, b, dgv = c_ref[...], v_ref[...], b_ref[...], dg_ref[...]
        u = c * v
        scr[0:SUBLANES, :] = cp_ref[...] * vp_ref[...] * (i > 0).astype(F32)
        scr[SUBLANES:, :] = u
        u1 = scr[pl.ds(SUBLANES - 1, tr), :]
        u2 = scr[pl.ds(SUBLANES - 2, tr), :]
        conv = w_ref[0:1, :] * u2 + w_ref[1:2, :] * u1 + w_ref[2:3, :] * u
        dconv = dgv * b
        scr2[0:tr, :] = dconv
        scr2[tr:, :] = dgn_ref[...] * bn_ref[...] * (i < nb - 1).astype(F32)
        du = (w_ref[2:3, :] * dconv + w_ref[1:2, :] * scr2[pl.ds(1, tr), :] + w_ref[0:1, :] * scr2[pl.ds(2, tr), :])
        dz_ref[:, 0:D] = (dgv * conv).astype(BF16)
        dz_ref[:, D:2 * D] = (du * v).astype(BF16)
        dz_ref[:, 2 * D:] = (du * c).astype(BF16)
        parts = [jnp.sum(dconv * t, axis=0, keepdims=True) for t in (u2, u1, u)]

        @pl.when(i == 0)
        def _():
            for k in range(3):
                dw_ref[k:k + 1, :] = parts[k]

        @pl.when(i > 0)
        def _():
            for k in range(3):
                dw_ref[k:k + 1, :] += parts[k]

    ins = [(z, ('cols', D, 0)), (z, ('cols', D, 1)), (z, ('cols', D, 2)), (z, ('prev', D, 1)), (z, ('prev', D, 2)),
           (z, ('next', D, 0)), (dg, 'rows'), (dg, ('next', D, 0)), (conv_w, 'full')]
    return _row_call(body, ins, [((S, D3), BF16, 'rows'), ((3, D), F32, 'acc')], name="conv_gate_bwd", tr=tr,
                     n_rows=S, scratch=[pltpu.VMEM((tr + SUBLANES, D), F32), pltpu.VMEM((tr + SUBLANES, D), F32)])


def _conv_mixer_fwd(x, gain, w_in, conv_w, w_out, side=None):
    h = _rms_fwd(x, gain, name="conv_norm")
    z, side_out = _with_side(_mm(h, w_in, name="conv_in", side=side), side)
    g = _conv_gate_fwd(z, conv_w)
    y = _mm(g, w_out, extras=(x,), epilogue=lambda acc, res: (acc + res,), name="conv_out")
    return y, (h, z, g), side_out


def _conv_mixer_bwd(x, gain, w_in, conv_w, w_out, saved, dy, dyb, sides=(None, None)):
    h, z, g = saved
    dg = _mm(dyb, w_out, tb=True, name="conv_dg")
    dw_out = _mm(g, dyb, ta=True, out_dtypes=(BF16,), name="conv_dwout")
    dz, dconv_w = _conv_gate_bwd(z, conv_w, dg)
    dh, out0 = _with_side(_mm(dz, w_in, tb=True, name="conv_dh", side=sides[0]), sides[0])
    dw_in, out1 = _with_side(_mm(h, dz, ta=True, out_dtypes=(BF16,), name="conv_dwin", side=sides[1]), sides[1])
    dx, dxb, dgain = _rms_bwd(x, gain, dh, dy, name="conv_dnorm")
    return dx, dxb, dgain, dw_in, dconv_w, dw_out, (out0, out1)


def _pool_fwd(u):
    S, D = u.shape
    G = D // len(POOL_WINDOWS)
    tr = _tile(S, 256, SUBLANES)
    H = POOL_HALO

    def body(u_ref, up_ref, p_ref, scr):
        i = pl.program_id(0)
        uv = u_ref[...]
        scr[0:H, :] = up_ref[...] * (i > 0).astype(F32)
        scr[H:, :] = uv
        t = (lax.broadcasted_iota(jnp.int32, (tr, 1), 0) + i * tr + 1).astype(F32)
        for gi, w in enumerate(POOL_WINDOWS):
            cols = slice(gi * G, (gi + 1) * G)
            acc = uv[:, cols]
            for j in range(1, w):
                acc = acc + scr[pl.ds(H - j, tr), cols]
            p_ref[:, cols] = (acc / jnp.minimum(t, float(w)) - uv[:, cols]).astype(BF16)

    return _row_call(body, [(u, 'rows'), (u, ('prev', D, 0))], [((S, D), BF16, 'rows')], name="pool_fwd", tr=tr,
                     n_rows=S, halo=H, scratch=[pltpu.VMEM((tr + H, D), F32)])[0]


def _pool_bwd(dp):
    S, D = dp.shape
    G = D // len(POOL_WINDOWS)
    tr = _tile(S, 256, SUBLANES)
    H = POOL_HALO
    nb = S // tr

    def body(d_ref, dn_ref, o_ref, scr):
        i = pl.program_id(0)
        dv = d_ref[...]
        t = (lax.broadcasted_iota(jnp.int32, (tr, 1), 0) + i * tr + 1).astype(F32)
        tn = (lax.broadcasted_iota(jnp.int32, (H, 1), 0) + (i + 1) * tr + 1).astype(F32)
        for gi, w in enumerate(POOL_WINDOWS):
            cols = slice(gi * G, (gi + 1) * G)
            scr[0:tr, cols] = dv[:, cols] / jnp.minimum(t, float(w))
            scr[tr:, cols] = dn_ref[:, cols] / jnp.minimum(tn, float(w)) * (i < nb - 1).astype(F32)
        for gi, w in enumerate(POOL_WINDOWS):
            cols = slice(gi * G, (gi + 1) * G)
            acc = scr[0:tr, cols]
            for j in range(1, w):
                acc = acc + scr[pl.ds(j, tr), cols]
            o_ref[:, cols] = (acc - dv[:, cols]).astype(BF16)

    return _row_call(body, [(dp, 'rows'), (dp, ('next', D, 0))], [((S, D), BF16, 'rows')], name="pool_bwd", tr=tr,
                     n_rows=S, halo=H, scratch=[pltpu.VMEM((tr + H, D), F32)])[0]


def _pool_group_fwd(p, wg, scale, x):
    S, D = p.shape
    NG, G, _ = wg.shape
    tm = _tile(S, 1024, SUBLANES)

    def body(p_ref, w_ref, s_ref, x_ref, o_ref, y_ref):
        y = jnp.dot(p_ref[...], w_ref[0], preferred_element_type=F32)
        y_ref[...] = y
        o_ref[...] = x_ref[...] + y * s_ref[...]

    blk = pl.BlockSpec((tm, G), lambda i, g: (i, g))
    return pl.pallas_call(
        body, name="pool_group_fwd", grid=(S // tm, NG),
        in_specs=[blk, pl.BlockSpec((1, G, G), lambda i, g: (g, 0, 0)), pl.BlockSpec((1, G), lambda i, g: (0, g)), blk],
        out_specs=[blk, blk],
        out_shape=[jax.ShapeDtypeStruct((S, D), F32), jax.ShapeDtypeStruct((S, D), F32)],
        compiler_params=_cparams(("parallel", "arbitrary")),
    )(p, wg, scale, x)


def _pool_group_bwd(p, wg, scale, y, dm):
    S, D = p.shape
    NG, G, _ = wg.shape
    tm = _tile(S, 1024, SUBLANES)
    nb = S // tm

    def body(p_ref, w_ref, s_ref, y_ref, dm_ref, dp_ref, dw_ref, ds_ref, acc_ref):
        i = pl.program_id(1)
        dmv = dm_ref[...]
        dy = (dmv * s_ref[...]).astype(BF16)
        dp_ref[...] = lax.dot_general(dy, w_ref[0], (((1,), (1,)), ((), ())), preferred_element_type=F32)
        dw = lax.dot_general(p_ref[...], dy, (((0,), (0,)), ((), ())), preferred_element_type=F32)
        dsp = jnp.sum(dmv * y_ref[...], axis=0, keepdims=True)

        @pl.when(i == 0)
        def _():
            acc_ref[...] = dw
            ds_ref[...] = dsp

        @pl.when(i > 0)
        def _():
            acc_ref[...] += dw
            ds_ref[...] += dsp

        @pl.when(i == nb - 1)
        def _():
            dw_ref[0] = acc_ref[...].astype(BF16)

    blk = pl.BlockSpec((tm, G), lambda g, i: (i, g))
    wspec = pl.BlockSpec((1, G, G), lambda g, i: (g, 0, 0))
    sspec = pl.BlockSpec((1, G), lambda g, i: (0, g))
    return pl.pallas_call(
        body, name="pool_group_bwd", grid=(NG, nb),
        in_specs=[blk, wspec, sspec, blk, blk],
        out_specs=[blk, wspec, sspec],
        out_shape=[jax.ShapeDtypeStruct((S, D), F32), jax.ShapeDtypeStruct((NG, G, G), BF16),
                   jax.ShapeDtypeStruct((1, D), F32)],
        scratch_shapes=[pltpu.VMEM((G, G), F32)],
        compiler_params=_cparams(("parallel", "arbitrary")),
    )(p, wg, scale, y, dm)


def _pool_mixer_fwd(x, gain, w_in, wg, scale):
    h = _rms_fwd(x, gain, name="pool_norm")
    u = _mm(h, w_in, name="pool_in")
    p = _pool_fwd(u)
    y, yg = _pool_group_fwd(p, wg, scale, x)
    return y, (h, p, yg)


def _pool_mixer_bwd(x, gain, w_in, wg, scale, saved, dy, dyb):
    h, p, yg = saved
    dp, dwg, dscale = _pool_group_bwd(p, wg, scale, yg, dy)
    du = _pool_bwd(dp)
    dh = _mm(du, w_in, tb=True, name="pool_dh")
    dw_in = _mm(h, du, ta=True, out_dtypes=(BF16,), name="pool_dwin")
    dx, dxb, dgain = _rms_bwd(x, gain, dh, dy, name="pool_dnorm")
    return dx, dxb, dgain, dw_in, dwg, dscale


def _qk_norm_fwd(qkv, qg, kg):
    S, D3 = qkv.shape
    D = D3 // 3
    NH = D // ATT_HEAD_DIM
    tr = _tile(S, 256, SUBLANES)

    def body(q_ref, k_ref, v_ref, qg_ref, kg_ref, qo_ref, ko_ref, vo_ref):
        for src, g_ref, dst in ((q_ref, qg_ref, qo_ref), (k_ref, kg_ref, ko_ref)):
            for hd in range(NH):
                cols = slice(hd * ATT_HEAD_DIM, (hd + 1) * ATT_HEAD_DIM)
                t = src[:, cols]
                r = lax.rsqrt(jnp.mean(t * t, axis=-1, keepdims=True) + RMS_EPS)
                dst[:, cols] = (t * r * g_ref[...]).astype(BF16)
        vo_ref[...] = v_ref[...].astype(BF16)

    ins = [(qkv, ('cols', D, 0)), (qkv, ('cols', D, 1)), (qkv, ('cols', D, 2)), (qg, 'full'), (kg, 'full')]
    return _row_call(body, ins, [((S, D), BF16, 'rows')] * 3, name="att_qknorm_fwd", tr=tr, n_rows=S)


def _qk_norm_bwd(qkv, qg, kg, dqn, dkn, dv):
    S, D3 = qkv.shape
    D = D3 // 3
    NH = D // ATT_HEAD_DIM
    tr = _tile(S, 128, SUBLANES)

    def body(q_ref, k_ref, qg_ref, kg_ref, dq_ref, dk_ref, dv_ref, o_ref, dqg_ref, dkg_ref):
        i = pl.program_id(0)
        for sec, (src, g_ref, d_ref, dg_ref) in enumerate(((q_ref, qg_ref, dq_ref, dqg_ref),
                                                            (k_ref, kg_ref, dk_ref, dkg_ref))):
            part = jnp.zeros((1, ATT_HEAD_DIM), F32)
            for hd in range(NH):
                cols = slice(hd * ATT_HEAD_DIM, (hd + 1) * ATT_HEAD_DIM)
                t = src[:, cols]
                r = lax.rsqrt(jnp.mean(t * t, axis=-1, keepdims=True) + RMS_EPS)
                tn = t * r
                d = d_ref[:, cols]
                dn = d * g_ref[...]
                dt = r * (dn - tn * jnp.mean(dn * tn, axis=-1, keepdims=True))
                o_ref[:, sec * D + hd * ATT_HEAD_DIM:sec * D + (hd + 1) * ATT_HEAD_DIM] = dt.astype(BF16)
                part = part + jnp.sum(d * tn, axis=0, keepdims=True)

            @pl.when(i == 0)
            def _():
                dg_ref[...] = part

            @pl.when(i > 0)
            def _():
                dg_ref[...] += part

        o_ref[:, 2 * D:] = dv_ref[...].astype(BF16)

    assert ATT_PAD % tr == 0 and dkn.shape[0] == S + ATT_PAD
    ins = [(qkv, ('cols', D, 0)), (qkv, ('cols', D, 1)), (qg, 'full'), (kg, 'full'), (dqn, 'rows'),
           (dkn, ('rows_from', D, ATT_PAD)), (dv, ('rows_from', D, ATT_PAD))]
    return _row_call(body, ins, [((S, D3), BF16, 'rows'), ((1, ATT_HEAD_DIM), F32, 'acc'),
                                 ((1, ATT_HEAD_DIM), F32, 'acc')], name="att_qknorm_bwd", tr=tr, n_rows=S)


def _att_band_mask():
    r = np.arange(ATT_TQ)[:, None]
    c = np.arange(ATT_TQ + ATT_PAD)[None, :]
    lo = (r // CHUNK) * CHUNK
    return (c >= lo) & (c < lo + ATT_PAD + CHUNK)


def _att_bias_toeplitz(rel_bias):
    H = rel_bias.shape[0]
    R, C = ATT_TQ, ATT_TQ + ATT_PAD
    L = C + R - 1
    assert R - 1 < REL_CLIP
    near = rel_bias[:, REL_CLIP - (R - 1):2 * REL_CLIP][:, ::-1]
    far = jnp.broadcast_to(rel_bias[:, 2 * REL_CLIP:], (H, L - near.shape[1]))
    v = jnp.concatenate([far, near, jnp.zeros((H, 1), rel_bias.dtype)], axis=1)
    skew = jnp.broadcast_to(v[:, None, :], (H, R, L + 1)).reshape(H, R * (L + 1))[:, :R * L].reshape(H, R, L)
    return skew[:, :, R - 1:R - 1 + C]


def _att_bias_tile(rel_bias):
    return jnp.where(_att_band_mask()[None], _att_bias_toeplitz(rel_bias), MASK_VALUE).astype(F32)


def _att_bias_grad(dtile, rel_bias):
    _, pull = jax.vjp(_att_bias_toeplitz, rel_bias)
    return pull(jnp.where(_att_band_mask()[None], dtile, 0.0))[0]


def _att_core_fwd(qn, kp, vp, bias):
    S, D = qn.shape
    NH = D // ATT_HEAD_DIM
    KW = ATT_TQ + ATT_PAD
    scale = ATT_HEAD_DIM ** -0.5

    def body(q_ref, k_ref, v_ref, b_ref, o_ref):
        qb = pl.program_id(1)
        start = pl.multiple_of(qb * ATT_TQ, ATT_TQ)
        ks = k_ref[pl.ds(start, KW), :]
        vs = v_ref[pl.ds(start, KW), :]
        s = lax.dot_general(q_ref[...], ks, (((1,), (1,)), ((), ())), preferred_element_type=F32) * scale + b_ref[0]
        kpos = lax.broadcasted_iota(jnp.int32, (1, KW), 1) + (qb * ATT_TQ - ATT_PAD)
        s = jnp.where(kpos >= 0, s, MASK_VALUE)
        m = jnp.max(s, axis=-1, keepdims=True)
        p = jnp.exp(s - m)
        l = jnp.sum(p, axis=-1, keepdims=True)
        o = jnp.dot(p.astype(BF16), vs, preferred_element_type=F32) / l
        o_ref[...] = o.astype(BF16)

    qspec = pl.BlockSpec((ATT_TQ, ATT_HEAD_DIM), lambda h, qb: (qb, h))
    kvspec = pl.BlockSpec((S + ATT_PAD, ATT_HEAD_DIM), lambda h, qb: (0, h))
    return pl.pallas_call(
        body, name="att_core_fwd", grid=(NH, S // ATT_TQ),
        in_specs=[qspec, kvspec, kvspec, pl.BlockSpec((1, ATT_TQ, KW), lambda h, qb: (h, 0, 0))],
        out_specs=qspec, out_shape=jax.ShapeDtypeStruct((S, D), BF16),
        compiler_params=_cparams(("parallel", "arbitrary")),
    )(qn, kp, vp, bias)


def _att_core_bwd(qn, kp, vp, bias, do):
    S, D = qn.shape
    NH = D // ATT_HEAD_DIM
    KW = ATT_TQ + ATT_PAD
    scale = ATT_HEAD_DIM ** -0.5

    def body(q_ref, k_ref, v_ref, b_ref, do_ref, dq_ref, dk_ref, dv_ref, db_ref):
        qb = pl.program_id(1)
        start = pl.multiple_of(qb * ATT_TQ, ATT_TQ)
        q = q_ref[...]
        dov = do_ref[...]
        ks = k_ref[pl.ds(start, KW), :]
        vs = v_ref[pl.ds(start, KW), :]
        s = lax.dot_general(q, ks, (((1,), (1,)), ((), ())), preferred_element_type=F32) * scale + b_ref[0]
        kpos = lax.broadcasted_iota(jnp.int32, (1, KW), 1) + (qb * ATT_TQ - ATT_PAD)
        s = jnp.where(kpos >= 0, s, MASK_VALUE)
        m = jnp.max(s, axis=-1, keepdims=True)
        e = jnp.exp(s - m)
        p = e / jnp.sum(e, axis=-1, keepdims=True)
        dp = lax.dot_general(dov, vs, (((1,), (1,)), ((), ())), preferred_element_type=F32)
        ds = p * (dp - jnp.sum(p * dp, axis=-1, keepdims=True))
        dsb = ds.astype(BF16)
        dq_ref[...] = jnp.dot(dsb, ks, preferred_element_type=F32) * scale
        dk = lax.dot_general(dsb, q, (((0,), (0,)), ((), ())), preferred_element_type=F32) * scale
        dv = lax.dot_general(p.astype(BF16), dov, (((0,), (0,)), ((), ())), preferred_element_type=F32)

        @pl.when(qb == 0)
        def _():
            dk_ref[...] = jnp.zeros_like(dk_ref)
            dv_ref[...] = jnp.zeros_like(dv_ref)
            db_ref[0] = ds

        @pl.when(qb > 0)
        def _():
            db_ref[0] += ds

        dk_ref[pl.ds(start, KW), :] += dk
        dv_ref[pl.ds(start, KW), :] += dv

    qspec = pl.BlockSpec((ATT_TQ, ATT_HEAD_DIM), lambda h, qb: (qb, h))
    kvspec = pl.BlockSpec((S + ATT_PAD, ATT_HEAD_DIM), lambda h, qb: (0, h))
    bspec = pl.BlockSpec((1, ATT_TQ, KW), lambda h, qb: (h, 0, 0))
    return pl.pallas_call(
        body, name="att_core_bwd", grid=(NH, S // ATT_TQ),
        in_specs=[qspec, kvspec, kvspec, bspec, qspec],
        out_specs=[qspec, kvspec, kvspec, bspec],
        out_shape=[jax.ShapeDtypeStruct((S, D), F32), jax.ShapeDtypeStruct((S + ATT_PAD, D), F32),
                   jax.ShapeDtypeStruct((S + ATT_PAD, D), F32), jax.ShapeDtypeStruct((NH, ATT_TQ, KW), F32)],
        compiler_params=_cparams(("parallel", "arbitrary")),
    )(qn, kp, vp, bias, do)


def _att_mixer_fwd(x, gain, w_qkv, qg, kg, rel_bias, w_out):
    h = _rms_fwd(x, gain, name="att_norm")
    qkv = _mm(h, w_qkv, name="att_qkv")
    qn, kn, v = _qk_norm_fwd(qkv, qg, kg)
    kp = jnp.pad(kn, ((ATT_PAD, 0), (0, 0)))
    vp = jnp.pad(v, ((ATT_PAD, 0), (0, 0)))
    bias = _att_bias_tile(rel_bias)
    o = _att_core_fwd(qn, kp, vp, bias)
    y = _mm(o, w_out, extras=(x,), epilogue=lambda acc, res: (acc + res,), name="att_out")
    return y, (h, qkv, qn, kp, vp, bias, o, rel_bias)


def _att_mixer_bwd(x, gain, w_qkv, qg, kg, w_out, saved, dy, dyb):
    h, qkv, qn, kp, vp, bias, o, rel_bias = saved
    do = _mm(dyb, w_out, tb=True, out_dtypes=(BF16,), name="att_do")
    dw_out = _mm(o, dyb, ta=True, out_dtypes=(BF16,), name="att_dwout")
    dqn, dkp, dvp, dbt = _att_core_bwd(qn, kp, vp, bias, do)
    drel = _att_bias_grad(dbt, rel_bias)
    dqkv, dqg, dkg = _qk_norm_bwd(qkv, qg, kg, dqn, dkp, dvp)
    dh = _mm(dqkv, w_qkv, tb=True, name="att_dh")
    dw_qkv = _mm(h, dqkv, ta=True, out_dtypes=(BF16,), name="att_dwqkv")
    dx, dxb, dgain = _rms_bwd(x, gain, dh, dy, name="att_dnorm")
    return dx, dxb, dgain, dw_qkv, dqg, dkg, drel, dw_out


def _ssm_tables(a_re, a_im, log_dt, b_re, b_im, c_re, c_im):
    G, N = a_re.shape
    NS = G // SSM_SLAB_GROUPS
    lam = lax.complex(a_re, a_im)
    dt = jnp.exp(log_dt)[:, None]
    abar = jnp.exp(lam * dt)
    coef = (abar - 1.0) / lam
    bbar = coef[..., None] * lax.complex(b_re, b_im)
    eye = jnp.eye(SSM_SLAB_GROUPS, dtype=F32)

    def blockdiag(t):
        P, Q = t.shape[1:]
        t = t.reshape(NS, SSM_SLAB_GROUPS, P, Q)
        return jnp.einsum('sgpq,gh->sgphq', t, eye).reshape(NS, SSM_SLAB_GROUPS * P, SSM_SLAB_GROUPS * Q)

    bt = jnp.swapaxes(bbar, 1, 2)
    bmat = jnp.concatenate([blockdiag(jnp.real(bt)), blockdiag(jnp.imag(bt))], axis=2)
    ct = jnp.swapaxes(lax.complex(c_re, c_im), 1, 2)
    cmat = jnp.concatenate([blockdiag(jnp.real(ct)), -blockdiag(jnp.imag(ct))], axis=1)
    al = abar.reshape(NS, 1, SSM_SLAB_STATE)
    rows = jnp.arange(SUBLANES)[None, :, None]
    fwd, bwd = [], []
    for k in (1, 2, 4):
        ak = al ** k
        f = jnp.where(rows >= k, ak, 0.0)
        b = jnp.where(rows < SUBLANES - k, ak, 0.0)
        fwd += [jnp.real(f), jnp.imag(f)]
        bwd += [jnp.real(b), jnp.imag(b)]
    pf = al ** (rows + 1)
    pb = al ** (SUBLANES - rows)
    fwd += [jnp.real(pf), jnp.imag(pf)]
    bwd += [jnp.real(pb), jnp.imag(pb)]
    coef_f = jnp.concatenate(fwd, axis=1).astype(F32)
    coef_b = jnp.concatenate(bwd, axis=1).astype(F32)
    return dict(lam=lam, dt=dt, abar=abar, coef=coef, bmat=bmat.astype(BF16), cmat=cmat.astype(BF16),
                bmat_t=jnp.swapaxes(bmat, 1, 2).astype(BF16), cmat_t=jnp.swapaxes(cmat, 1, 2).astype(BF16),
                coef_f=coef_f, coef_b=coef_b)


def _ssm_scan_fwd(u, tabs, d_skip):
    S, D = u.shape
    NS = D // LANES
    W = 2 * SSM_SLAB_STATE
    T = _tile(S, 512, SUBLANES)
    HS = SSM_SLAB_STATE

    def body(u_ref, bm_ref, cm_ref, cf_ref, d_ref, y_ref, xs_ref, bu_scr, carry_scr):
        i = pl.program_id(1)

        @pl.when(i == 0)
        def _():
            carry_scr[...] = jnp.zeros_like(carry_scr)

        uv = u_ref[...]
        bu_scr[...] = jnp.dot(uv.astype(BF16), bm_ref[0], preferred_element_type=F32)

        def step(r, carry):
            rows = pl.ds(pl.multiple_of(r * SUBLANES, SUBLANES), SUBLANES)
            xr = bu_scr[rows, 0:HS]
            xi = bu_scr[rows, HS:W]
            for n, k in enumerate((1, 2, 4)):
                ar = cf_ref[0, 16 * n:16 * n + 8, :]
                ai = cf_ref[0, 16 * n + 8:16 * n + 16, :]
                sr = pltpu.roll(xr, k, 0)
                si = pltpu.roll(xi, k, 0)
                xr, xi = xr + ar * sr - ai * si, xi + ar * si + ai * sr
            pr = cf_ref[0, 48:56, :]
            pi_ = cf_ref[0, 56:64, :]
            cr, ci = carry
            xr, xi = xr + pr * cr - pi_ * ci, xi + pr * ci + pi_ * cr
            xs_ref[rows, 0:HS] = xr
            xs_ref[rows, HS:W] = xi
            return xr[SUBLANES - 1:SUBLANES, :], xi[SUBLANES - 1:SUBLANES, :]

        cr, ci = lax.fori_loop(0, T // SUBLANES, step, (carry_scr[0:1, 0:HS], carry_scr[0:1, HS:W]), unroll=2)
        carry_scr[0:1, 0:HS] = cr
        carry_scr[0:1, HS:W] = ci
        y_ref[...] = jnp.dot(xs_ref[...].astype(BF16), cm_ref[0], preferred_element_type=F32) + d_ref[...] * uv

    return pl.pallas_call(
        body, name="ssm_scan_fwd", grid=(NS, S // T),
        in_specs=[pl.BlockSpec((T, LANES), lambda j, i: (i, j)),
                  pl.BlockSpec((1, LANES, W), lambda j, i: (j, 0, 0)),
                  pl.BlockSpec((1, W, LANES), lambda j, i: (j, 0, 0)),
                  pl.BlockSpec((1, 8 * SUBLANES, HS), lambda j, i: (j, 0, 0)),
                  pl.BlockSpec((1, LANES), lambda j, i: (0, j))],
        out_specs=[pl.BlockSpec((T, LANES), lambda j, i: (i, j)), pl.BlockSpec((T, W), lambda j, i: (i, j))],
        out_shape=[jax.ShapeDtypeStruct((S, D), F32), jax.ShapeDtypeStruct((S, NS * W), F32)],
        scratch_shapes=[pltpu.VMEM((T, W), F32), pltpu.VMEM((SUBLANES, W), F32)],
        compiler_params=_cparams(("parallel", "arbitrary")),
    )(u, tabs['bmat'], tabs['cmat'], tabs['coef_f'], d_skip)


def _ssm_scan_bwd(u, xs, dy, tabs, d_skip):
    S, D = u.shape
    NS = D // LANES
    W = 2 * SSM_SLAB_STATE
    T = _tile(S, 512, SUBLANES)
    HS = SSM_SLAB_STATE
    nb = S // T

    def body(u_ref, xs_ref, dy_ref, bt_ref, ct_ref, cf_ref, d_ref, du_ref, gb_ref, gc_ref, q_ref,
             cy_scr, lam_scr, carry_scr):
        i = pl.program_id(1)

        @pl.when(i == 0)
        def _():
            carry_scr[...] = jnp.zeros_like(carry_scr)
            gb_ref[...] = jnp.zeros_like(gb_ref)
            gc_ref[...] = jnp.zeros_like(gc_ref)
            q_ref[...] = jnp.zeros_like(q_ref)

        dyv = dy_ref[...]
        dyb = dyv.astype(BF16)
        cy_scr[...] = jnp.dot(dyb, ct_ref[0], preferred_element_type=F32)

        def step(n, carry):
            r = T // SUBLANES - 1 - n
            rows = pl.ds(pl.multiple_of(r * SUBLANES, SUBLANES), SUBLANES)
            cyr = cy_scr[rows, 0:HS]
            cyi = cy_scr[rows, HS:W]
            lr, li = cyr, cyi
            for m, k in enumerate((1, 2, 4)):
                br = cf_ref[0, 16 * m:16 * m + 8, :]
                bi = cf_ref[0, 16 * m + 8:16 * m + 16, :]
                sr = pltpu.roll(lr, SUBLANES - k, 0)
                si = pltpu.roll(li, SUBLANES - k, 0)
                lr, li = lr + br * sr + bi * si, li + br * si - bi * sr
            pr = cf_ref[0, 48:56, :]
            pi_ = cf_ref[0, 56:64, :]
            cr, ci, qr, qi = carry
            lr, li = lr + pr * cr + pi_ * ci, li + pr * ci - pi_ * cr
            lam_scr[rows, 0:HS] = lr
            lam_scr[rows, HS:W] = li
            mr, mi = lr - cyr, li - cyi
            xr = xs_ref[rows, 0:HS]
            xi = xs_ref[rows, HS:W]
            return lr[0:1, :], li[0:1, :], qr + mr * xr + mi * xi, qi + mi * xr - mr * xi

        zero = jnp.zeros((SUBLANES, HS), F32)
        cr, ci, qr, qi = lax.fori_loop(0, T // SUBLANES, step,
                                       (carry_scr[0:1, 0:HS], carry_scr[0:1, HS:W], zero, zero), unroll=2)
        carry_scr[0:1, 0:HS] = cr
        carry_scr[0:1, HS:W] = ci
        q_ref[0, :, 0:HS] += qr
        q_ref[0, :, HS:W] += qi
        lamb = lam_scr[...].astype(BF16)
        uv = u_ref[...]
        du_ref[...] = jnp.dot(lamb, bt_ref[0], preferred_element_type=F32) + d_ref[...] * dyv
        gb_ref[0] += lax.dot_general(lamb, uv.astype(BF16), (((0,), (0,)), ((), ())), preferred_element_type=F32)
        gc_ref[0] += lax.dot_general(xs_ref[...].astype(BF16), dyb, (((0,), (0,)), ((), ())),
                                     preferred_element_type=F32)

    rev = lambda j, i: (nb - 1 - i, j)
    slab3 = lambda j, i: (j, 0, 0)
    return pl.pallas_call(
        body, name="ssm_scan_bwd", grid=(NS, nb),
        in_specs=[pl.BlockSpec((T, LANES), rev), pl.BlockSpec((T, W), rev), pl.BlockSpec((T, LANES), rev),
                  pl.BlockSpec((1, W, LANES), slab3), pl.BlockSpec((1, LANES, W), slab3),
                  pl.BlockSpec((1, 8 * SUBLANES, HS), slab3), pl.BlockSpec((1, LANES), lambda j, i: (0, j))],
        out_specs=[pl.BlockSpec((T, LANES), rev), pl.BlockSpec((1, W, LANES), slab3),
                   pl.BlockSpec((1, W, LANES), slab3), pl.BlockSpec((1, SUBLANES, W), slab3)],
        out_shape=[jax.ShapeDtypeStruct((S, D), F32), jax.ShapeDtypeStruct((NS, W, LANES), F32),
                   jax.ShapeDtypeStruct((NS, W, LANES), F32), jax.ShapeDtypeStruct((NS, SUBLANES, W), F32)],
        scratch_shapes=[pltpu.VMEM((T, W), F32), pltpu.VMEM((T, W), F32), pltpu.VMEM((SUBLANES, W), F32)],
        compiler_params=_cparams(("parallel", "arbitrary")),
    )(u, xs, dy, tabs['bmat_t'], tabs['cmat_t'], tabs['coef_b'], d_skip)


def _ssm_param_grads(tabs, b_re, b_im, gb, gc, q):
    NS = gb.shape[0]
    G = NS * SSM_SLAB_GROUPS
    N, C = SSM_STATE, SSM_GROUP

    def diag_blocks(t):
        t = t.reshape(NS, SSM_SLAB_GROUPS, N, SSM_SLAB_GROUPS, C)
        t = jnp.einsum('sgnhc,gh->sgnc', t, jnp.eye(SSM_SLAB_GROUPS, dtype=F32))
        return t.reshape(G, N, C)

    HS = SSM_SLAB_STATE
    g_bbar = lax.complex(diag_blocks(gb[:, :HS]), diag_blocks(gb[:, HS:]))
    g_c = lax.complex(diag_blocks(gc[:, :HS]), -diag_blocks(gc[:, HS:]))
    qs = jnp.sum(q, axis=1)
    qc = lax.complex(qs[:, :HS], qs[:, HS:]).reshape(G, N)
    lam, dt, abar, coef = tabs['lam'], tabs['dt'], tabs['abar'], tabs['coef']
    bmat = lax.complex(b_re, b_im)
    g_b = g_bbar * jnp.conj(coef)[..., None]
    g_coef = jnp.sum(g_bbar * jnp.conj(bmat), axis=-1)
    g_abar_coef = g_coef * jnp.conj(1.0 / lam)
    g_lam = g_coef * jnp.conj(-(abar - 1.0) / (lam * lam))
    g_ld = qc + jnp.conj(abar) * g_abar_coef
    g_lam = g_lam + g_ld * dt
    g_dt = jnp.sum(jnp.real(g_ld * jnp.conj(lam)), axis=-1)
    g_logdt = g_dt * dt[:, 0]
    g_ct = jnp.swapaxes(g_c, 1, 2)
    return (jnp.real(g_lam), jnp.imag(g_lam), g_logdt, jnp.real(g_b), jnp.imag(g_b), jnp.real(g_ct), jnp.imag(g_ct))


_GELU_C = math.sqrt(2.0 / math.pi)


def _gelu_fwd(y):
    S, D = y.shape

    def body(y_ref, z_ref):
        v = y_ref[...]
        z_ref[...] = (0.5 * v * (1.0 + jnp.tanh(_GELU_C * (v + 0.044715 * v * v * v)))).astype(BF16)

    return _row_call(body, [(y, 'rows')], [((S, D), BF16, 'rows')], name="ssm_gelu_fwd",
                     tr=_tile(S, 512, SUBLANES), n_rows=S)[0]


def _gelu_bwd(y, dz, u):
    S, D = y.shape

    def body(y_ref, dz_ref, u_ref, dy_ref, dd_ref):
        i = pl.program_id(0)
        v = y_ref[...]
        t = jnp.tanh(_GELU_C * (v + 0.044715 * v * v * v))
        g = 0.5 * (1.0 + t) + 0.5 * v * (1.0 - t * t) * _GELU_C * (1.0 + 3 * 0.044715 * v * v)
        dy = dz_ref[...] * g
        dy_ref[...] = dy
        part = jnp.sum(dy * u_ref[...], axis=0, keepdims=True)

        @pl.when(i == 0)
        def _():
            dd_ref[...] = part

        @pl.when(i > 0)
        def _():
            dd_ref[...] += part

    return _row_call(body, [(y, 'rows'), (dz, 'rows'), (u, 'rows')], [((S, D), F32, 'rows'), ((1, D), F32, 'acc')],
                     name="ssm_gelu_bwd", tr=_tile(S, 256, SUBLANES), n_rows=S)


def _glu_fwd(zz, x):
    S, D = x.shape

    def body(a_ref, g_ref, x_ref, o_ref):
        o_ref[...] = x_ref[...] + a_ref[...] * jax.nn.sigmoid(g_ref[...])

    return _row_call(body, [(zz, ('cols', D, 0)), (zz, ('cols', D, 1)), (x, 'rows')], [((S, D), F32, 'rows')],
                     name="ssm_glu_fwd", tr=_tile(S, 256, SUBLANES), n_rows=S)[0]


def _glu_bwd(zz, dm):
    S, D = dm.shape

    def body(a_ref, g_ref, dm_ref, o_ref):
        s = jax.nn.sigmoid(g_ref[...])
        d = dm_ref[...]
        o_ref[:, 0:D] = (d * s).astype(BF16)
        o_ref[:, D:] = (d * a_ref[...] * s * (1.0 - s)).astype(BF16)

    return _row_call(body, [(zz, ('cols', D, 0)), (zz, ('cols', D, 1)), (dm, 'rows')], [((S, 2 * D), BF16, 'rows')],
                     name="ssm_glu_bwd", tr=_tile(S, 256, SUBLANES), n_rows=S)[0]


def _ssm_mixer_fwd(x, gain, tabs, d_skip, w_glu):
    _, u = _rms_fwd(x, gain, name="ssm_norm", with_f32=True)
    yv, xs = _ssm_scan_fwd(u, tabs, d_skip)
    z = _gelu_fwd(yv)
    zz = _mm(z, w_glu, name="ssm_glu_in")
    y = _glu_fwd(zz, x)
    return y, (u, xs, yv, z, zz)


def _ssm_mixer_bwd(x, gain, tabs, d_skip, w_glu, b_re, b_im, saved, dy, dyb):
    u, xs, yv, z, zz = saved
    dzz = _glu_bwd(zz, dy)
    dz = _mm(dzz, w_glu, tb=True, name="ssm_dz")
    dw_glu = _mm(z, dzz, ta=True, out_dtypes=(BF16,), name="ssm_dwglu")
    dyv, dd = _gelu_bwd(yv, dz, u)
    du, gb, gc, q = _ssm_scan_bwd(u, xs, dyv, tabs, d_skip)
    small = _ssm_param_grads(tabs, b_re, b_im, gb, gc, q)
    dx, dxb, dgain = _rms_bwd(x, gain, du, dy, name="ssm_dnorm")
    return dx, dxb, dgain, small, dd, dw_glu


class _LocalWeights:
    def __init__(self, p):
        self.p = p

    def layer(self, i):
        return {n: (v[i] if n.startswith('mlp') else v) for n, v in self.p.items()}

    def now(self, i, parts):
        pass

    def side(self, i, parts):
        return None

    def deliver(self, i, parts, outs):
        pass


class _LocalGrads:
    def __init__(self):
        self.g = {}

    def stage_a(self, i, parts, mats):
        for n, v in mats.items():
            if n.startswith('mlp'):
                self.g.setdefault(n, {})[i] = v
            else:
                self.g[n] = v

    def side(self, i, parts):
        return None

    def finish(self, i, parts, outs):
        pass

    def now(self, i, parts):
        pass


def _local_step(x, target, p, weights=None, sink=None):
    depth = p['norm_mix'].shape[0]
    local = sink is None
    weights = weights or _LocalWeights({n: p[n] for n in _BIG})
    sink = sink or _LocalGrads()
    tabs = _ssm_tables(p['ssm_a_re'], p['ssm_a_im'], p['ssm_log_dt'], p['ssm_b_re'], p['ssm_b_im'], p['ssm_c_re'],
                       p['ssm_c_im'])
    xs_in, saved_mix, saved_mlp, lw = [], [], [], []
    weights.now(0, ['mix'])
    for i in range(depth):
        w = weights.layer(i)
        gm = p['norm_mix'][i:i + 1]
        xs_in.append(x)
        if i % 4 == 0:
            side = weights.side(0, ['w1', 'w2']) if i == 0 else None
            x, sv, side_out = _conv_mixer_fwd(x, gm, w['conv_w_in'], p['conv_w'], w['conv_w_out'], side=side)
            if side:
                weights.deliver(0, ['w1', 'w2'], side_out)
        elif i % 4 == 1:
            x, sv = _pool_mixer_fwd(x, gm, w['pool_w_in'], w['pool_w_group'], p['pool_scale'])
        elif i % 4 == 2:
            x, sv = _att_mixer_fwd(x, gm, w['att_w_qkv'], p['att_q_norm'], p['att_k_norm'], p['att_rel_bias'],
                                   w['att_w_out'])
        else:
            x, sv = _ssm_mixer_fwd(x, gm, tabs, p['ssm_d'], w['ssm_w_glu'])
        saved_mix.append(sv)
        xs_in.append(x)
        w = weights.layer(i)
        lw.append(w)
        nxt = [['mix', 'w1'], ['w2']] if i + 1 < depth else []
        sides = [weights.side(i + 1, parts) for parts in nxt] or [None, None]
        x, sv, side_out = _mlp_fwd(x, p['norm_mlp'][i:i + 1], w['mlp_w1'], w['mlp_w2'], tag=str(i), sides=sides)
        for parts, side, out in zip(nxt, sides, side_out):
            if side:
                weights.deliver(i + 1, parts, out)
        saved_mlp.append(sv)
    dx, dxb, loss_cols = _loss_head(x, target, name="loss_head")
    g = {'norm_mix': [None] * depth, 'norm_mlp': [None] * depth}
    todo = []
    for i in reversed(range(depth)):
        w = lw[i]
        big = {}
        sides = [sink.side(i + 1, parts) for parts in todo] or [None, None, None]
        dx, dxb, g['norm_mlp'][i], big['mlp_w1'], big['mlp_w2'], side_out = _mlp_bwd(
            xs_in[2 * i + 1], p['norm_mlp'][i:i + 1], w['mlp_w1'], w['mlp_w2'], saved_mlp[i], dx, dxb, tag=str(i),
            sides=sides)
        for parts, side, out in zip(todo, sides, side_out):
            if side:
                sink.finish(i + 1, parts, out)
        sink.stage_a(i, ['w1', 'w2'], {n: big[n] for n in ('mlp_w1', 'mlp_w2')})
        gm = p['norm_mix'][i:i + 1]
        xin, sv = xs_in[2 * i], saved_mix[i]
        if i % 4 == 0:
            mine = [['w1'], ['w2']] if i == 0 else []
            sides = [sink.side(i, parts) for parts in mine] or [None, None]
            dx, dxb, g['norm_mix'][i], big['conv_w_in'], g['conv_w'], big['conv_w_out'], side_out = _conv_mixer_bwd(
                xin, gm, w['conv_w_in'], p['conv_w'], w['conv_w_out'], sv, dx, dxb, sides=sides)
            for parts, side, out in zip(mine, sides, side_out):
                if side:
                    sink.finish(i, parts, out)
        elif i % 4 == 1:
            dx, dxb, g['norm_mix'][i], big['pool_w_in'], big['pool_w_group'], g['pool_scale'] = _pool_mixer_bwd(
                xin, gm, w['pool_w_in'], w['pool_w_group'], p['pool_scale'], sv, dx, dxb)
        elif i % 4 == 2:
            (dx, dxb, g['norm_mix'][i], big['att_w_qkv'], g['att_q_norm'], g['att_k_norm'], g['att_rel_bias'],
             big['att_w_out']) = _att_mixer_bwd(xin, gm, w['att_w_qkv'], p['att_q_norm'], p['att_k_norm'],
                                                w['att_w_out'], sv, dx, dxb)
        else:
            dx, dxb, g['norm_mix'][i], small, g['ssm_d'], big['ssm_w_glu'] = _ssm_mixer_bwd(
                xin, gm, tabs, p['ssm_d'], w['ssm_w_glu'], p['ssm_b_re'], p['ssm_b_im'], sv, dx, dxb)
            (g['ssm_a_re'], g['ssm_a_im'], g['ssm_log_dt'], g['ssm_b_re'], g['ssm_b_im'], g['ssm_c_re'],
             g['ssm_c_im']) = small
        sink.stage_a(i, ['mix'], {n: v for n, v in big.items() if not n.startswith('mlp')})
        if i == 0:
            sink.now(i, ['mix'])
        todo = [['w1'], ['w2'], ['mix']]
    g['norm_mix'] = jnp.concatenate(g['norm_mix'], axis=0)
    g['norm_mlp'] = jnp.concatenate(g['norm_mlp'], axis=0)
    if local:
        for n, v in sink.g.items():
            g[n] = [v[i] for i in range(depth)] if n.startswith('mlp') else v
    return loss_cols, dx, g


_ANY = pl.BlockSpec(memory_space=pl.ANY)
_VM = pl.BlockSpec(memory_space=pltpu.VMEM)
_REL_ALL = [(0, 0, 1), (0, 1, 0), (0, 1, 1), (1, 0, 0), (1, 0, 1), (1, 1, 0), (1, 1, 1)]
_REL_CHIPS = [(1, 0, 0), (0, 1, 0), (1, 1, 0)]


def _me():
    return lax.axis_index("x"), lax.axis_index("y"), lax.axis_index("c")


def _flip(pos, rel):
    return tuple(1 - p if r else p for p, r in zip(pos, rel))


def _chip_of(pos):
    return 2 * pos[0] + pos[1]


def _dev_of(pos):
    return 4 * pos[0] + 2 * pos[1] + pos[2]


def _gather_small(buf, *, reduce, name):
    rows = buf.shape[0]

    def body(in_ref, out_ref, *rest):
        if reduce:
            gath, send_sems, recv_sems = rest
        else:
            gath = out_ref
            send_sems, recv_sems = rest
        me = _me()
        gath[_dev_of(me)] = in_ref[...]
        copies = []
        for k, rel in enumerate(_REL_ALL):
            peer = _flip(me, rel)
            cp = pltpu.make_async_remote_copy(src_ref=in_ref, dst_ref=gath.at[_dev_of(me)], send_sem=send_sems.at[k],
                                              recv_sem=recv_sems.at[k], device_id=peer, device_id_type=MESH)
            cp.start()
            copies.append(cp)
        for k, rel in enumerate(_REL_ALL):
            peer = _flip(me, rel)
            pltpu.make_async_remote_copy(src_ref=in_ref, dst_ref=gath.at[_dev_of(peer)], send_sem=send_sems.at[k],
                                         recv_sem=recv_sems.at[k], device_id=peer, device_id_type=MESH).wait_recv()
        for cp in copies:
            cp.wait_send()
        if reduce:
            acc = gath[0]
            for s in range(1, N_DEV):
                acc = acc + gath[s]
            out_ref[...] = acc

    out_shape = (rows, LANES) if reduce else (N_DEV, rows, LANES)
    scratch = ([pltpu.VMEM((N_DEV, rows, LANES), F32)] if reduce else []) + [
        pltpu.SemaphoreType.DMA((len(_REL_ALL),)), pltpu.SemaphoreType.DMA((len(_REL_ALL),))]
    return pl.pallas_call(
        body, name=name, in_specs=[_VM], out_specs=_VM, out_shape=jax.ShapeDtypeStruct(out_shape, F32),
        scratch_shapes=scratch, compiler_params=pltpu.CompilerParams(vmem_limit_bytes=VMEM_LIMIT),
    )(buf)


def _region(ref, kind, k, shard_shape, half, quarter=None):
    r, c = shard_shape
    n = r // 2 if quarter is None else r // 4
    start = half * (r // 2) + (0 if quarter is None else quarter * n)
    if kind == 'col':
        return ref.at[pl.ds(pl.multiple_of(start, 16), n), pl.ds(pl.multiple_of(k * c, LANES), c)]
    return ref.at[pl.ds(pl.multiple_of(k * r + start, 16), n), :]


def _place_block(shard, dest, layer, chip, kind, *, name):
    _, r, c = shard.shape
    tr = _tile(r, max(16, (1 << 20) // c // 16 * 16), 16)
    nb = r // tr

    def body(chip_ref, s_ref, d_ref, o_ref):
        o_ref[...] = s_ref[0].astype(BF16)

    out_map = (lambda i, k: (i, k[0])) if kind == 'col' else (lambda i, k: (k[0] * nb + i, 0))
    grid_spec = pltpu.PrefetchScalarGridSpec(
        num_scalar_prefetch=1, grid=(nb,),
        in_specs=[pl.BlockSpec((1, tr, c), lambda i, k: (layer, i, 0)), _ANY],
        out_specs=pl.BlockSpec((tr, c), out_map))
    return pl.pallas_call(
        body, name=name, grid_spec=grid_spec, out_shape=jax.ShapeDtypeStruct(dest.shape, BF16),
        input_output_aliases={2: 0}, compiler_params=_cparams(("arbitrary",)),
    )(chip, shard, dest)


def _all_gather_matrices(fulls, kinds, *, name):
    return _run_side(_gather_side(fulls, kinds), name=name)


class _Ops:
    def __init__(self, active):
        self.active, self.cur, self.sends = active, 0, []

    def phase(self, p):
        self.cur = p

    def send(self, src, dst, sems_s, sems_r, idx, to):
        cp = pltpu.make_async_remote_copy(src_ref=src, dst_ref=dst, send_sem=sems_s.at[idx], recv_sem=sems_r.at[idx],
                                          device_id=to, device_id_type=MESH)
        if self.cur == self.active:
            cp.start()
        self.sends.append(cp)

    def arrived(self, where, sems_s, sems_r, idx, frm):
        if self.cur == self.active:
            pltpu.make_async_remote_copy(src_ref=where, dst_ref=where, send_sem=sems_s.at[idx], recv_sem=sems_r.at[idx],
                                         device_id=frm, device_id_type=MESH).wait_recv()


def _phases(plan, n_phases):
    def make(p):
        def run(ins, outs, sems):
            op = _Ops(p)
            plan(ins, outs, sems, op)
            if p == n_phases - 1:
                for cp in op.sends:
                    cp.wait_send()
        return run
    return [make(p) for p in range(n_phases)]


def _gather_side(fulls, kinds):
    n = len(fulls)
    shards = [(f.shape[0], f.shape[1] // N_CHIPS) if kd == 'col' else (f.shape[0] // N_CHIPS, f.shape[1])
              for f, kd in zip(fulls, kinds)]

    def plan(in_refs, out_refs, sems, op):
        ici_send, ici_recv, fwd_send, fwd_recv, d2d_send, d2d_recv = sems
        me = _me()
        core = me[2]
        sib = _flip(me, (0, 0, 1))
        nbr = [_flip(me, (1, 0, 0)), _flip(me, (0, 1, 0))]
        diag = _chip_of(_flip(me, (1, 1, 0)))
        op.phase(0)
        for m in range(n):
            src = _region(in_refs[m], kinds[m], _chip_of(me), shards[m], core)
            dst = _region(out_refs[m], kinds[m], _chip_of(me), shards[m], core)
            for j in range(2):
                op.send(src, dst, ici_send, ici_recv, (m, j), nbr[j])
        op.phase(1)
        for m in range(n):
            for j in range(2):
                landed = _region(out_refs[m], kinds[m], _chip_of(nbr[j]), shards[m], core)
                op.arrived(landed, ici_send, ici_recv, (m, j), nbr[j])
                part = _region(out_refs[m], kinds[m], _chip_of(nbr[j]), shards[m], core, quarter=j)
                op.send(part, part, fwd_send, fwd_recv, (m, j), nbr[1 - j])
                op.send(landed, landed, d2d_send, d2d_recv, (m, j), sib)
        op.phase(2)
        for m in range(n):
            for j in range(2):
                part = _region(out_refs[m], kinds[m], diag, shards[m], core, quarter=j)
                op.arrived(part, fwd_send, fwd_recv, (m, j), nbr[1 - j])
                op.send(part, part, d2d_send, d2d_recv, (m, 2 + j), sib)
        op.phase(3)
        for m in range(n):
            for j in range(2):
                op.arrived(_region(out_refs[m], kinds[m], _chip_of(nbr[j]), shards[m], 1 - core),
                           d2d_send, d2d_recv, (m, j), sib)
                op.arrived(_region(out_refs[m], kinds[m], diag, shards[m], 1 - core, quarter=j),
                           d2d_send, d2d_recv, (m, 2 + j), sib)

    return _Side(list(fulls), [jax.ShapeDtypeStruct(f.shape, BF16) for f in fulls], {m: m for m in range(n)},
                 [pltpu.SemaphoreType.DMA((n, 2))] * 4 + [pltpu.SemaphoreType.DMA((n, 4))] * 2, _phases(plan, 4))


def _piece_shape(full, kind):
    R, C = full
    return (R // 2, C // N_CHIPS) if kind == 'col' else (R // N_DEV, C)


def _piece(ref, kind, k, c, full):
    pr, pc = _piece_shape(full, kind)
    if kind == 'col':
        return ref.at[pl.ds(pl.multiple_of(c * pr, 16), pr), pl.ds(pl.multiple_of(k * pc, LANES), pc)]
    return ref.at[pl.ds(pl.multiple_of((2 * k + c) * pr, 16), pr), :]


def _pair_exchange(grads, kinds, *, name):
    n = len(grads)
    pieces = [_piece_shape(g.shape, kd) for g, kd in zip(grads, kinds)]

    def body(*refs):
        in_refs, out_refs = refs[:n], refs[n:2 * n]
        send_sems, recv_sems = refs[2 * n:]
        me = _me()
        core = me[2]
        sib = _flip(me, (0, 0, 1))
        sends = []
        for m in range(n):
            for k in range(N_CHIPS):
                cp = pltpu.make_async_remote_copy(
                    src_ref=_piece(in_refs[m], kinds[m], k, 1 - core, grads[m].shape), dst_ref=out_refs[m].at[k],
                    send_sem=send_sems.at[m, k], recv_sem=recv_sems.at[m, k], device_id=sib, device_id_type=MESH)
                cp.start()
                sends.append(cp)
        for m in range(n):
            for k in range(N_CHIPS):
                pltpu.make_async_remote_copy(
                    src_ref=_piece(in_refs[m], kinds[m], k, core, grads[m].shape), dst_ref=out_refs[m].at[k],
                    send_sem=send_sems.at[m, k], recv_sem=recv_sems.at[m, k], device_id=sib,
                    device_id_type=MESH).wait_recv()
        for cp in sends:
            cp.wait_send()

    return pl.pallas_call(
        body, name=name, in_specs=[_ANY] * n, out_specs=[_ANY] * n,
        out_shape=[jax.ShapeDtypeStruct((N_CHIPS,) + p, BF16) for p in pieces],
        scratch_shapes=[pltpu.SemaphoreType.DMA((n, N_CHIPS)), pltpu.SemaphoreType.DMA((n, N_CHIPS))],
    )(*grads)


def _pair_sum(g, recv, core, kind, *, name):
    pr, pc = _piece_shape(g.shape, kind)
    tr = _tile(pr, max(16, (1 << 19) // pc // 16 * 16), 16)
    nb = pr // tr

    def body(core_ref, g_ref, r_ref, o_ref):
        o_ref[0] = (g_ref[...].astype(F32) + r_ref[0].astype(F32)).astype(BF16)

    g_map = (lambda k, i, c: (c[0] * nb + i, k)) if kind == 'col' else (lambda k, i, c: ((2 * k + c[0]) * nb + i, 0))
    slot = pl.BlockSpec((1, tr, pc), lambda k, i, c: (k, i, 0))
    grid_spec = pltpu.PrefetchScalarGridSpec(
        num_scalar_prefetch=1, grid=(N_CHIPS, nb), in_specs=[pl.BlockSpec((tr, pc), g_map), slot], out_specs=slot)
    return pl.pallas_call(
        body, name=name, grid_spec=grid_spec, out_shape=jax.ShapeDtypeStruct((N_CHIPS, pr, pc), BF16),
        compiler_params=_cparams(("arbitrary", "arbitrary")),
    )(core, g, recv)


def _chip_scatter(sums, *, name):
    return _run_side(_chip_scatter_side(sums), name=name)


def _chip_scatter_side(sums):
    n = len(sums)
    nj = len(_REL_CHIPS)

    def plan(in_refs, out_refs, sems, op):
        send_sems, recv_sems = sems
        me = _me()
        op.phase(0)
        for m in range(n):
            for j, rel in enumerate(_REL_CHIPS):
                peer = _flip(me, rel)
                op.send(in_refs[m].at[_chip_of(peer)], out_refs[m].at[j], send_sems, recv_sems, (m, j), peer)
        op.phase(1)
        for m in range(n):
            for j, rel in enumerate(_REL_CHIPS):
                op.arrived(out_refs[m].at[j], send_sems, recv_sems, (m, j), _flip(me, rel))

    return _Side(list(sums), [jax.ShapeDtypeStruct((nj,) + s.shape[1:], BF16) for s in sums], {},
                 [pltpu.SemaphoreType.DMA((n, nj)), pltpu.SemaphoreType.DMA((n, nj))], _phases(plan, 2))


def _sum_into(own, recv, dest, layer, core, chip, *, name):
    _, pr, pc = own.shape
    tr = _tile(pr, max(16, (1 << 19) // pc // 16 * 16), 16)
    nb = pr // tr

    def body(core_ref, chip_ref, own_ref, r_ref, d_ref, o_ref):
        acc = own_ref[0].astype(F32)
        for s in range(len(_REL_CHIPS)):
            acc = acc + r_ref[s].astype(F32)
        o_ref[0] = acc

    grid_spec = pltpu.PrefetchScalarGridSpec(
        num_scalar_prefetch=2, grid=(nb,),
        in_specs=[pl.BlockSpec((1, tr, pc), lambda i, c, k: (k[0], i, 0)),
                  pl.BlockSpec((len(_REL_CHIPS), tr, pc), lambda i, c, k: (0, i, 0)), _ANY],
        out_specs=pl.BlockSpec((1, tr, pc), lambda i, c, k: (layer, c[0] * nb + i, 0)))
    return pl.pallas_call(
        body, name=name, grid_spec=grid_spec, out_shape=jax.ShapeDtypeStruct(dest.shape, F32),
        input_output_aliases={4: 0}, compiler_params=_cparams(("arbitrary",)),
    )(core, chip, own, recv, dest)


def _exchange_halves(blocks, *, name):
    n_out = len(blocks)
    n = sum(b.shape[0] for b in blocks)

    def body(*refs):
        in_refs, out_refs = refs[:n_out], refs[n_out:2 * n_out]
        send_sems, recv_sems = refs[2 * n_out:]
        me = _me()
        sib = _flip(me, (0, 0, 1))
        sends, m = [], 0
        for o in range(n_out):
            L, r2, _ = blocks[o].shape
            hr = r2 // 2
            for l in range(L):
                rows = pl.ds(pl.multiple_of(me[2] * hr, SUBLANES), hr)
                cp = pltpu.make_async_remote_copy(src_ref=in_refs[o].at[l, rows, :], dst_ref=out_refs[o].at[l, rows, :],
                                                  send_sem=send_sems.at[m], recv_sem=recv_sems.at[m], device_id=sib,
                                                  device_id_type=MESH)
                cp.start()
                sends.append(cp)
                m += 1
        m = 0
        for o in range(n_out):
            L, r2, _ = blocks[o].shape
            hr = r2 // 2
            for l in range(L):
                rows = pl.ds(pl.multiple_of(sib[2] * hr, SUBLANES), hr)
                theirs = out_refs[o].at[l, rows, :]
                pltpu.make_async_remote_copy(src_ref=theirs, dst_ref=theirs, send_sem=send_sems.at[m],
                                             recv_sem=recv_sems.at[m], device_id=sib, device_id_type=MESH).wait_recv()
                m += 1
        for cp in sends:
            cp.wait_send()

    return pl.pallas_call(
        body, name=name, in_specs=[_ANY] * n_out, out_specs=[_ANY] * n_out,
        out_shape=[jax.ShapeDtypeStruct(b.shape, F32) for b in blocks],
        input_output_aliases={o: o for o in range(n_out)},
        scratch_shapes=[pltpu.SemaphoreType.DMA((n,)), pltpu.SemaphoreType.DMA((n,))],
    )(*blocks)


def _adamw(w, g, m, v, *, name):
    R, C = w.shape
    tr = _tile(R, max(SUBLANES, (1 << 19) // C // SUBLANES * SUBLANES), SUBLANES)
    c1 = 1.0 / (1.0 - ADAM_B1 ** ADAM_STEP)
    c2 = 1.0 / (1.0 - ADAM_B2 ** ADAM_STEP)

    def body(w_ref, g_ref, m_ref, v_ref, go_ref, d_ref, nm_ref, nv_ref):
        gv = g_ref[...]
        go_ref[...] = gv
        nm = ADAM_B1 * m_ref[...] + (1.0 - ADAM_B1) * gv
        nv = ADAM_B2 * v_ref[...] + (1.0 - ADAM_B2) * (gv * gv)
        nm_ref[...] = nm
        nv_ref[...] = nv
        d_ref[...] = -ADAM_LR * ((nm * c1) / (jnp.sqrt(nv * c2) + ADAM_EPS) + ADAM_WD * w_ref[...])

    blk = pl.BlockSpec((tr, C), lambda i: (i, 0))
    return pl.pallas_call(
        body, name=name, grid=(R // tr,), in_specs=[blk] * 4, out_specs=[blk] * 4,
        out_shape=[jax.ShapeDtypeStruct((R, C), F32)] * 4, compiler_params=_cparams(("parallel",)),
    )(w, g, m, v)


_BIG = ['mlp_w1', 'mlp_w2', 'conv_w_in', 'conv_w_out', 'pool_w_in', 'pool_w_group', 'att_w_qkv', 'att_w_out',
        'ssm_w_glu']
_KIND = {'mlp_w1': 'col', 'mlp_w2': 'row', 'conv_w_in': 'col', 'conv_w_out': 'row', 'pool_w_in': 'row',
         'pool_w_group': 'row', 'att_w_qkv': 'col', 'att_w_out': 'row', 'ssm_w_glu': 'col'}
_TINY_SHARDED = ['conv_w', 'pool_scale', 'ssm_d']
_REPLICATED = ['norm_mix', 'norm_mlp', 'att_q_norm', 'att_k_norm', 'att_rel_bias', 'ssm_a_re', 'ssm_a_im',
               'ssm_log_dt', 'ssm_b_re', 'ssm_b_im', 'ssm_c_re', 'ssm_c_im']
_SMALL = _REPLICATED + _TINY_SHARDED
_ORDER = ['norm_mix', 'norm_mlp', 'mlp_w1', 'mlp_w2', 'conv_w_in', 'conv_w', 'conv_w_out', 'pool_w_in',
          'pool_w_group', 'pool_scale', 'att_w_qkv', 'att_q_norm', 'att_k_norm', 'att_rel_bias', 'att_w_out',
          'ssm_a_re', 'ssm_a_im', 'ssm_log_dt', 'ssm_b_re', 'ssm_b_im', 'ssm_c_re', 'ssm_c_im', 'ssm_d', 'ssm_w_glu']
_LAYER_OF = {'conv_w_in': 0, 'conv_w_out': 0, 'pool_w_in': 1, 'pool_w_group': 1, 'att_w_qkv': 2, 'att_w_out': 2,
             'ssm_w_glu': 3}


def _pack(arrays):
    flat = jnp.concatenate([a.reshape(-1).astype(F32) for a in arrays])
    n = flat.shape[0]
    total = -(-n // (SUBLANES * LANES)) * (SUBLANES * LANES)
    return jnp.pad(flat, (0, total - n)).reshape(total // LANES, LANES)


def _unpack(buf, shapes):
    flat = buf.reshape(-1)
    out, off = [], 0
    for s in shapes:
        n = int(np.prod(s))
        out.append(flat[off:off + n].reshape(s))
        off += n
    return out


def _matrices(w, name):
    t = w.reshape((-1,) + w.shape[-2:])
    return [t[l] for l in range(t.shape[0])]


def kernel(x, norm_mix, norm_mlp, mlp_w1, mlp_w2, conv_w_in, conv_w, conv_w_out, pool_w_in, pool_w_group, pool_scale, att_w_qkv, att_q_norm, att_k_norm, att_rel_bias, att_w_out, ssm_a_re, ssm_a_im, ssm_log_dt, ssm_b_re, ssm_b_im, ssm_c_re, ssm_c_im, ssm_d, ssm_w_glu, loss_target, m_norm_mix, m_norm_mlp, m_mlp_w1, m_mlp_w2, m_conv_w_in, m_conv_w, m_conv_w_out, m_pool_w_in, m_pool_w_group, m_pool_scale, m_att_w_qkv, m_att_q_norm, m_att_k_norm, m_att_rel_bias, m_att_w_out, m_ssm_a_re, m_ssm_a_im, m_ssm_log_dt, m_ssm_b_re, m_ssm_b_im, m_ssm_c_re, m_ssm_c_im, m_ssm_d, m_ssm_w_glu, v_norm_mix, v_norm_mlp, v_mlp_w1, v_mlp_w2, v_conv_w_in, v_conv_w, v_conv_w_out, v_pool_w_in, v_pool_w_group, v_pool_scale, v_att_w_qkv, v_att_q_norm, v_att_k_norm, v_att_rel_bias, v_att_w_out, v_ssm_a_re, v_ssm_a_im, v_ssm_log_dt, v_ssm_b_re, v_ssm_b_im, v_ssm_c_re, v_ssm_c_im, v_ssm_d, v_ssm_w_glu):
    args = dict(locals())
    W = {n: args[n] for n in _ORDER}
    M = {n: args['m_' + n] for n in _ORDER}
    V = {n: args['v_' + n] for n in _ORDER}
    depth = norm_mix.shape[0]
    d_model = x.shape[-1]
    chip = 2 * lax.axis_index("x") + lax.axis_index("y")

    tiny = _gather_small(_pack([W[n] for n in _TINY_SHARDED]), reduce=False, name="gather_vectors")
    tiny_shapes = [W[n].shape for n in _TINY_SHARDED]
    per_chip = [_unpack(tiny[2 * k], tiny_shapes) for k in range(N_CHIPS)]
    full_tiny = {n: jnp.concatenate([per_chip[k][i] for k in range(N_CHIPS)], axis=-1)
                 for i, n in enumerate(_TINY_SHARDED)}
    chip_arr = chip.astype(jnp.int32).reshape(1)
    groups = [[] for _ in range(depth)]
    shard3 = {n: W[n].reshape((-1,) + W[n].shape[-2:]) for n in _BIG}
    for n in _BIG:
        for l in range(shard3[n].shape[0]):
            groups[l if n.startswith('mlp') else _LAYER_OF[n]].append((n, l))
    core = lax.axis_index("c").astype(jnp.int32).reshape(1)
    def keys_of(i, parts):
        want = {'w1': lambda n: n == 'mlp_w1', 'w2': lambda n: n == 'mlp_w2', 'mix': lambda n: not n.startswith('mlp')}
        return [(n, l) for n, l in groups[i] if any(want[pt](n) for pt in parts)]

    class Weights:
        def __init__(self):
            self.mats = {}

        def side(self, i, parts):
            placed = []
            for n, l in keys_of(i, parts):
                _, r, c = shard3[n].shape
                shape = (r, c * N_CHIPS) if _KIND[n] == 'col' else (r * N_CHIPS, c)
                placed.append(_place_block(shard3[n], lax.empty(shape, BF16), l, chip_arr, _KIND[n],
                                           name=f"place_{n}_{l}"))
            return _gather_side(placed, [_KIND[n] for n, _ in keys_of(i, parts)])

        def deliver(self, i, parts, outs):
            for key, t in zip(keys_of(i, parts), outs):
                self.mats[key] = t

        def now(self, i, parts):
            self.deliver(i, parts, _run_side(self.side(i, parts), name=f"gather_weights_{i}_{'_'.join(parts)}"))

        def layer(self, i):
            got = {}
            for n, l in groups[i]:
                if (n, l) in self.mats:
                    got.setdefault(n, []).append(self.mats[(n, l)])
            return {n: (jnp.stack(v) if n == 'pool_w_group' else v[0]) for n, v in got.items()}

    class Sink:
        def __init__(self):
            self.blocks = {n: lax.empty((shard3[n].shape[0],) + shard3[n].shape[1:], F32) for n in _BIG}
            self.pair_sums = {}

        def stage_a(self, i, parts, big):
            keys = keys_of(i, parts)
            kinds = [_KIND[n] for n, _ in keys]
            mats = [(_matrices(big[n], n)[l] if n == 'pool_w_group' else big[n]) for n, l in keys]
            from_sibling = _pair_exchange(mats, kinds, name=f"pair_exchange_{i}_{'_'.join(parts)}")
            for key, gm, t, kd in zip(keys, mats, from_sibling, kinds):
                self.pair_sums[key] = _pair_sum(gm, t, core, kd, name=f"pair_sum_{key[0]}_{key[1]}")

        def side(self, i, parts):
            return _chip_scatter_side([self.pair_sums[key] for key in keys_of(i, parts)])

        def finish(self, i, parts, outs):
            for (n, l), t in zip(keys_of(i, parts), outs):
                self.blocks[n] = _sum_into(self.pair_sums[(n, l)], t, self.blocks[n], l, core, chip_arr,
                                           name=f"sum_grads_{n}_{l}")

        def now(self, i, parts):
            self.finish(i, parts, _run_side(self.side(i, parts), name=f"chip_scatter_{i}_{'_'.join(parts)}"))

    p = dict(
        norm_mix=norm_mix, norm_mlp=norm_mlp, conv_w=full_tiny['conv_w'][0], pool_scale=full_tiny['pool_scale'],
        att_q_norm=att_q_norm, att_k_norm=att_k_norm, att_rel_bias=att_rel_bias[0],
        ssm_a_re=ssm_a_re[0], ssm_a_im=ssm_a_im[0], ssm_log_dt=ssm_log_dt[0], ssm_b_re=ssm_b_re[0],
        ssm_b_im=ssm_b_im[0], ssm_c_re=ssm_c_re[0], ssm_c_im=ssm_c_im[0], ssm_d=full_tiny['ssm_d'])

    sink = Sink()
    loss_cols, dx, g = _local_step(x[0], loss_target[0], p, Weights(), sink)
    loss = lax.psum(0.5 * jnp.sum(loss_cols) / d_model, ("x", "y", "c"))
    reduced = _exchange_halves([sink.blocks[n] for n in _BIG], name="exchange_halves")
    grads = {n: t.reshape(W[n].shape) for n, t in zip(_BIG, reduced)}

    small_full_shapes = [W[n].shape for n in _REPLICATED] + [full_tiny[n].shape for n in _TINY_SHARDED]
    gsmall = _gather_small(_pack([g[n] for n in _SMALL]), reduce=True, name="reduce_small_grads")
    for n, t in zip(_SMALL, _unpack(gsmall, small_full_shapes)):
        if n in _TINY_SHARDED:
            width = W[n].shape[-1]
            t = lax.dynamic_slice_in_dim(t, chip * width, width, axis=t.ndim - 1)
        grads[n] = t.reshape(W[n].shape)

    delta, new_m, new_v = {}, {}, {}
    for n in _BIG:
        shp = W[n].shape
        two = (int(np.prod(shp[:-1])), shp[-1])
        gout, d, nm, nv = _adamw(W[n].reshape(two), grads[n].reshape(two), M[n].reshape(two), V[n].reshape(two),
                                 name=f"adamw_{n}")
        grads[n], delta[n], new_m[n], new_v[n] = gout.reshape(shp), d.reshape(shp), nm.reshape(shp), nv.reshape(shp)
    shapes = [W[n].shape for n in _SMALL]
    _, d, nm, nv = _adamw(_pack([W[n] for n in _SMALL]), _pack([grads[n] for n in _SMALL]),
                          _pack([M[n] for n in _SMALL]), _pack([V[n] for n in _SMALL]), name="adamw_small")
    for n, a, b, c in zip(_SMALL, _unpack(d, shapes), _unpack(nm, shapes), _unpack(nv, shapes)):
        delta[n], new_m[n], new_v[n] = a, b, c

    return (loss, dx.reshape(x.shape), *[grads[n] for n in _ORDER], *[delta[n] for n in _ORDER],
            *[new_m[n] for n in _ORDER], *[new_v[n] for n in _ORDER])
```

```python
import functools
import math

import numpy as np
import jax
import jax.numpy as jnp
from jax import lax
from jax.experimental import pallas as pl
from jax.experimental.pallas import tpu as pltpu

F32 = jnp.float32
BF16 = jnp.bfloat16
MESH = pl.DeviceIdType.MESH

V7X_VMEM_BYTES = 64 * 1024 * 1024
VMEM_LIMIT = V7X_VMEM_BYTES - 12 * 1024 * 1024
LANES = 128
SUBLANES = 8

CHUNK = 64
ATT_HEAD_DIM = 128
ATT_PAD = 8 * CHUNK
REL_CLIP = 256
MASK_VALUE = -1e30
POOL_WINDOWS = (2, 4, 8, 16)
POOL_HALO = 16
SSM_GROUP = 16
SSM_STATE = 64
SSM_SLAB_GROUPS = LANES // SSM_GROUP
SSM_SLAB_STATE = SSM_SLAB_GROUPS * SSM_STATE
RMS_EPS = 1e-6
ADAM_LR, ADAM_B1, ADAM_B2, ADAM_EPS, ADAM_WD, ADAM_STEP = 0.001, 0.9, 0.999, 1e-08, 0.01, 10
ATT_TQ = 256
N_CHIPS = 4
N_DEV = 8


def _cparams(sem=None, **kw):
    return pltpu.CompilerParams(dimension_semantics=sem, vmem_limit_bytes=VMEM_LIMIT, **kw)


def _tile(n, target, mult):
    if n <= target:
        return n
    t = (target // mult) * mult
    while t > mult and n % t:
        t -= mult
    assert n % t == 0, (n, target, mult)
    return t


class _Side:
    def __init__(self, arrays, outs, aliases, sems, phases):
        self.arrays, self.outs, self.aliases, self.sems, self.phases = arrays, outs, aliases, sems, phases


def _run_side(side, *, name):
    n_in, n_out = len(side.arrays), len(side.outs)

    def body(*refs):
        for phase in side.phases:
            phase(refs[:n_in], refs[n_in:n_in + n_out], refs[n_in + n_out:])

    return pl.pallas_call(
        body, name=name, in_specs=[_ANY] * n_in, out_specs=[_ANY] * n_out, out_shape=list(side.outs),
        input_output_aliases=dict(side.aliases), scratch_shapes=list(side.sems),
    )(*side.arrays)


def _mm(a, b, *, ta=False, tb=False, extras=(), epilogue=None, out_dtypes=(F32,), name,
        tm=1024, tn=1024, tk=2048, side=None):
    M, K = (a.shape[1], a.shape[0]) if ta else a.shape
    N = b.shape[0] if tb else b.shape[1]
    assert (b.shape[1] if tb else b.shape[0]) == K, (a.shape, b.shape, ta, tb)
    tm, tn, tk = _tile(M, tm, LANES), _tile(N, tn, LANES), _tile(K, tk, LANES)
    nk = K // tk
    gm, gn = M // tm, N // tn
    n_ex, n_out = len(extras), len(out_dtypes)
    n_sin = len(side.arrays) if side else 0
    n_sout = len(side.outs) if side else 0
    n_sems = len(side.sems) if side else 0
    dn = (((0 if ta else 1,), (1 if tb else 0,)), ((), ()))
    if side:
        steps = gm * gn * nk
        n_ph = len(side.phases)
        at = [int((steps - 1) * (p / max(n_ph - 1, 1)) ** 0.5) for p in range(n_ph)]

    def body(*refs):
        a_ref, b_ref = refs[0], refs[1]
        ex_refs = refs[2:2 + n_ex]
        n_in = 2 + n_ex + n_sin
        o_refs = refs[n_in:n_in + n_out]
        if side:
            s_in = refs[2 + n_ex:n_in]
            s_out = refs[n_in + n_out:n_in + n_out + n_sout]
            s_sems = refs[len(refs) - n_sems:]
            t = (pl.program_id(0) * gn + pl.program_id(1)) * nk + pl.program_id(2)
            for phase, when in zip(side.phases, at):
                @pl.when(t == when)
                def _(phase=phase):
                    phase(s_in, s_out, s_sems)

        p = lax.dot_general(a_ref[...], b_ref[...], dn, preferred_element_type=F32)

        def finish(acc):
            outs = (acc,) if epilogue is None else epilogue(acc, *[r[...] for r in ex_refs])
            for o_ref, o in zip(o_refs, outs):
                o_ref[...] = o.astype(o_ref.dtype)

        if nk == 1:
            finish(p)
        else:
            acc_ref = refs[n_in + n_out + n_sout]
            k = pl.program_id(2)

            @pl.when(k == 0)
            def _():
                acc_ref[...] = p

            @pl.when(k > 0)
            def _():
                acc_ref[...] += p

            @pl.when(k == nk - 1)
            def _():
                finish(acc_ref[...])

    a_spec = pl.BlockSpec((tk, tm), lambda i, j, k: (k, i)) if ta else pl.BlockSpec((tm, tk), lambda i, j, k: (i, k))
    b_spec = pl.BlockSpec((tn, tk), lambda i, j, k: (j, k)) if tb else pl.BlockSpec((tk, tn), lambda i, j, k: (k, j))
    mn_spec = pl.BlockSpec((tm, tn), lambda i, j, k: (i, j))
    outs = pl.pallas_call(
        body, name=name, grid=(gm, gn, nk),
        in_specs=[a_spec, b_spec] + [mn_spec] * n_ex + [_ANY] * n_sin,
        out_specs=[mn_spec] * n_out + [_ANY] * n_sout,
        out_shape=[jax.ShapeDtypeStruct((M, N), d) for d in out_dtypes] + (list(side.outs) if side else []),
        input_output_aliases={2 + n_ex + i: n_out + o for i, o in side.aliases.items()} if side else {},
        scratch_shapes=([pltpu.VMEM((tm, tn), F32)] if nk > 1 else []) + (list(side.sems) if side else []),
        compiler_params=_cparams(("arbitrary",) * 3 if side else ("parallel", "parallel", "arbitrary")),
    )(a, b, *extras, *(side.arrays if side else ()))
    res = outs[0] if n_out == 1 else tuple(outs[:n_out])
    return (res, list(outs[n_out:])) if side else res


def _row_call(body, ins, outs, *, name, tr, n_rows, acc_outs=(), scratch=(), halo=None):
    nb = n_rows // tr
    hb = halo or SUBLANES
    per = tr // hb
    last = n_rows // hb - 1

    def spec(arr_shape, kind):
        if kind == 'rows':
            return pl.BlockSpec((tr,) + tuple(arr_shape[1:]), lambda i: (i,) + (0,) * (len(arr_shape) - 1))
        if kind == 'full' or kind == 'acc':
            return pl.BlockSpec(tuple(arr_shape), lambda i: (0,) * len(arr_shape))
        tag, w, j = kind
        if tag == 'cols':
            return pl.BlockSpec((tr, w), lambda i: (i, j))
        if tag == 'rows_from':
            return pl.BlockSpec((tr, w), lambda i: (i + j // tr, 0))
        if tag == 'prev':
            return pl.BlockSpec((hb, w), lambda i: (jnp.maximum(i * per - 1, 0), j))
        if tag == 'next':
            return pl.BlockSpec((hb, w), lambda i: (jnp.minimum((i + 1) * per, last), j))
        raise ValueError(kind)

    return pl.pallas_call(
        body, name=name, grid=(nb,),
        in_specs=[spec(a.shape, k) for a, k in ins],
        out_specs=[spec(s, k) for s, _, k in outs],
        out_shape=[jax.ShapeDtypeStruct(s, d) for s, d, _ in outs],
        scratch_shapes=list(scratch),
        compiler_params=_cparams(("arbitrary",)),
    )(*[a for a, _ in ins])


def _rms_fwd(x, gain, *, name, with_f32=False):
    S, D = x.shape
    tr = _tile(S, 512, SUBLANES)

    def body(x_ref, g_ref, *o_refs):
        xv = x_ref[...]
        r = lax.rsqrt(jnp.mean(xv * xv, axis=-1, keepdims=True) + RMS_EPS)
        h = xv * r * g_ref[...]
        o_refs[0][...] = h.astype(BF16)
        if with_f32:
            o_refs[1][...] = h

    outs = [((S, D), BF16, 'rows')] + ([((S, D), F32, 'rows')] if with_f32 else [])
    res = _row_call(body, [(x, 'rows'), (gain, 'full')], outs, name=name, tr=tr, n_rows=S)
    return tuple(res) if with_f32 else res[0]


def _rms_bwd(x, gain, dh, dres, *, name):
    S, D = x.shape
    tr = _tile(S, 256, SUBLANES)

    def body(x_ref, g_ref, dh_ref, dr_ref, dx_ref, dxb_ref, dg_ref):
        i = pl.program_id(0)
        xv = x_ref[...]
        r = lax.rsqrt(jnp.mean(xv * xv, axis=-1, keepdims=True) + RMS_EPS)
        xn = xv * r
        dhv = dh_ref[...]
        dxn = dhv * g_ref[...]
        dx = r * (dxn - xn * jnp.mean(dxn * xn, axis=-1, keepdims=True)) + dr_ref[...]
        dx_ref[...] = dx
        dxb_ref[...] = dx.astype(BF16)
        part = jnp.sum(dhv * xn, axis=0, keepdims=True)

        @pl.when(i == 0)
        def _():
            dg_ref[...] = part

        @pl.when(i > 0)
        def _():
            dg_ref[...] += part

    return _row_call(body, [(x, 'rows'), (gain, 'full'), (dh, 'rows'), (dres, 'rows')],
                     [((S, D), F32, 'rows'), ((S, D), BF16, 'rows'), ((1, D), F32, 'acc')],
                     name=name, tr=tr, n_rows=S)


def _loss_head(y, target, *, name):
    S, D = y.shape
    tr = _tile(S, 512, SUBLANES)

    def body(y_ref, t_ref, d_ref, db_ref, l_ref):
        i = pl.program_id(0)
        e = y_ref[...] - t_ref[...]
        d = e * (1.0 / D)
        d_ref[...] = d
        db_ref[...] = d.astype(BF16)
        part = jnp.sum(e * e, axis=0, keepdims=True)

        @pl.when(i == 0)
        def _():
            l_ref[...] = part

        @pl.when(i > 0)
        def _():
            l_ref[...] += part

    return _row_call(body, [(y, 'rows'), (target, 'rows')],
                     [((S, D), F32, 'rows'), ((S, D), BF16, 'rows'), ((1, D), F32, 'acc')],
                     name=name, tr=tr, n_rows=S)


def _relu2_epilogue(acc):
    r = jnp.maximum(acc, 0.0)
    return r, r * r


def _with_side(res, side):
    return res if side else (res, None)


def _mlp_fwd(x, gain, w1, w2, *, tag, sides=(None, None)):
    h = _rms_fwd(x, gain, name=f"mlp_norm_{tag}")
    (r, act), out0 = _with_side(_mm(h, w1, epilogue=_relu2_epilogue, out_dtypes=(BF16, BF16), name=f"mlp_up_{tag}",
                                    side=sides[0]), sides[0])
    y, out1 = _with_side(_mm(act, w2, extras=(x,), epilogue=lambda acc, res: (acc + res,), name=f"mlp_down_{tag}",
                             side=sides[1]), sides[1])
    return y, (h, r, act), (out0, out1)


def _mlp_bwd(x, gain, w1, w2, saved, dy, dyb, *, tag, sides=(None, None, None)):
    h, r, act = saved
    da, out0 = _with_side(_mm(dyb, w2, tb=True, extras=(r,), epilogue=lambda acc, rr: (acc * (2.0 * rr.astype(F32)),),
                              out_dtypes=(BF16,), name=f"mlp_dact_{tag}", side=sides[0]), sides[0])
    dw2, out1 = _with_side(_mm(act, dyb, ta=True, out_dtypes=(BF16,), name=f"mlp_dw2_{tag}", side=sides[1]), sides[1])
    dw1, out2 = _with_side(_mm(h, da, ta=True, out_dtypes=(BF16,), name=f"mlp_dw1_{tag}", side=sides[2]), sides[2])
    dh = _mm(da, w1, tb=True, name=f"mlp_dh_{tag}")
    dx, dxb, dgain = _rms_bwd(x, gain, dh, dy, name=f"mlp_dnorm_{tag}")
    return dx, dxb, dgain, dw1, dw2, (out0, out1, out2)


def _conv_gate_fwd(z, conv_w):
    S, D3 = z.shape
    D = D3 // 3
    tr = _tile(S, 256, SUBLANES)

    def body(b_ref, c_ref, v_ref, cp_ref, vp_ref, w_ref, g_ref, scr):
        i = pl.program_id(0)
        u = c_ref[...] * v_ref[...]
        scr[0:SUBLANES, :] = cp_ref[...] * vp_ref[...] * (i > 0).astype(F32)
        scr[SUBLANES:, :] = u
        conv = (w_ref[0:1, :] * scr[pl.ds(SUBLANES - 2, tr), :] + w_ref[1:2, :] * scr[pl.ds(SUBLANES - 1, tr), :]
                + w_ref[2:3, :] * u)
        g_ref[...] = (b_ref[...] * conv).astype(BF16)

    ins = [(z, ('cols', D, 0)), (z, ('cols', D, 1)), (z, ('cols', D, 2)), (z, ('prev', D, 1)), (z, ('prev', D, 2)),
           (conv_w, 'full')]
    return _row_call(body, ins, [((S, D), BF16, 'rows')], name="conv_gate_fwd", tr=tr, n_rows=S,
                     scratch=[pltpu.VMEM((tr + SUBLANES, D), F32)])[0]


def _conv_gate_bwd(z, conv_w, dg):
    S, D3 = z.shape
    D = D3 // 3
    tr = _tile(S, 128, SUBLANES)
    nb = S // tr

    def body(b_ref, c_ref, v_ref, cp_ref, vp_ref, bn_ref, dg_ref, dgn_ref, w_ref, dz_ref, dw_ref, scr, scr2):
        i = pl.program_id(0)
        c, v, b, dgv = c_ref[...], v_ref[...], b_ref[...], dg_ref[...]
        u = c * v
        scr[0:SUBLANES, :] = cp_ref[...] * vp_ref[...] * (i > 0).astype(F32)
        scr[SUBLANES:, :] = u
        u1 = scr[pl.ds(SUBLANES - 1, tr), :]
        u2 = scr[pl.ds(SUBLANES - 2, tr), :]
        conv = w_ref[0:1, :] * u2 + w_ref[1:2, :] * u1 + w_ref[2:3, :] * u
        dconv = dgv * b
        scr2[0:tr, :] = dconv
        scr2[tr:, :] = dgn_ref[...] * bn_ref[...] * (i < nb - 1).astype(F32)
        du = (w_ref[2:3, :] * dconv + w_ref[1:2, :] * scr2[pl.ds(1, tr), :] + w_ref[0:1, :] * scr2[pl.ds(2, tr), :])
        dz_ref[:, 0:D] = (dgv * conv).astype(BF16)
        dz_ref[:, D:2 * D] = (du * v).astype(BF16)
        dz_ref[:, 2 * D:] = (du * c).astype(BF16)
        parts = [jnp.sum(dconv * t, axis=0, keepdims=True) for t in (u2, u1, u)]

        @pl.when(i == 0)
        def _():
            for k in range(3):
                dw_ref[k:k + 1, :] = parts[k]

        @pl.when(i > 0)
        def _():
            for k in range(3):
                dw_ref[k:k + 1, :] += parts[k]

    ins = [(z, ('cols', D, 0)), (z, ('cols', D, 1)), (z, ('cols', D, 2)), (z, ('prev', D, 1)), (z, ('prev', D, 2)),
           (z, ('next', D, 0)), (dg, 'rows'), (dg, ('next', D, 0)), (conv_w, 'full')]
    return _row_call(body, ins, [((S, D3), BF16, 'rows'), ((3, D), F32, 'acc')], name="conv_gate_bwd", tr=tr,
                     n_rows=S, scratch=[pltpu.VMEM((tr + SUBLANES, D), F32), pltpu.VMEM((tr + SUBLANES, D), F32)])


def _conv_mixer_fwd(x, gain, w_in, conv_w, w_out, side=None):
    h = _rms_fwd(x, gain, name="conv_norm")
    z, side_out = _with_side(_mm(h, w_in, name="conv_in", side=side), side)
    g = _conv_gate_fwd(z, conv_w)
    y = _mm(g, w_out, extras=(x,), epilogue=lambda acc, res: (acc + res,), name="conv_out")
    return y, (h, z, g), side_out


def _conv_mixer_bwd(x, gain, w_in, conv_w, w_out, saved, dy, dyb, sides=(None, None)):
    h, z, g = saved
    dg = _mm(dyb, w_out, tb=True, name="conv_dg")
    dw_out = _mm(g, dyb, ta=True, out_dtypes=(BF16,), name="conv_dwout")
    dz, dconv_w = _conv_gate_bwd(z, conv_w, dg)
    dh, out0 = _with_side(_mm(dz, w_in, tb=True, name="conv_dh", side=sides[0]), sides[0])
    dw_in, out1 = _with_side(_mm(h, dz, ta=True, out_dtypes=(BF16,), name="conv_dwin", side=sides[1]), sides[1])
    dx, dxb, dgain = _rms_bwd(x, gain, dh, dy, name="conv_dnorm")
    return dx, dxb, dgain, dw_in, dconv_w, dw_out, (out0, out1)


def _pool_fwd(u):
    S, D = u.shape
    G = D // len(POOL_WINDOWS)
    tr = _tile(S, 256, SUBLANES)
    H = POOL_HALO

    def body(u_ref, up_ref, p_ref, scr):
        i = pl.program_id(0)
        uv = u_ref[...]
        scr[0:H, :] = up_ref[...] * (i > 0).astype(F32)
        scr[H:, :] = uv
        t = (lax.broadcasted_iota(jnp.int32, (tr, 1), 0) + i * tr + 1).astype(F32)
        for gi, w in enumerate(POOL_WINDOWS):
            cols = slice(gi * G, (gi + 1) * G)
            acc = uv[:, cols]
            for j in range(1, w):
                acc = acc + scr[pl.ds(H - j, tr), cols]
            p_ref[:, cols] = (acc / jnp.minimum(t, float(w)) - uv[:, cols]).astype(BF16)

    return _row_call(body, [(u, 'rows'), (u, ('prev', D, 0))], [((S, D), BF16, 'rows')], name="pool_fwd", tr=tr,
                     n_rows=S, halo=H, scratch=[pltpu.VMEM((tr + H, D), F32)])[0]


def _pool_bwd(dp):
    S, D = dp.shape
    G = D // len(POOL_WINDOWS)
    tr = _tile(S, 256, SUBLANES)
    H = POOL_HALO
    nb = S // tr

    def body(d_ref, dn_ref, o_ref, scr):
        i = pl.program_id(0)
        dv = d_ref[...]
        t = (lax.broadcasted_iota(jnp.int32, (tr, 1), 0) + i * tr + 1).astype(F32)
        tn = (lax.broadcasted_iota(jnp.int32, (H, 1), 0) + (i + 1) * tr + 1).astype(F32)
        for gi, w in enumerate(POOL_WINDOWS):
            cols = slice(gi * G, (gi + 1) * G)
            scr[0:tr, cols] = dv[:, cols] / jnp.minimum(t, float(w))
            scr[tr:, cols] = dn_ref[:, cols] / jnp.minimum(tn, float(w)) * (i < nb - 1).astype(F32)
        for gi, w in enumerate(POOL_WINDOWS):
            cols = slice(gi * G, (gi + 1) * G)
            acc = scr[0:tr, cols]
            for j in range(1, w):
                acc = acc + scr[pl.ds(j, tr), cols]
            o_ref[:, cols] = (acc - dv[:, cols]).astype(BF16)

    return _row_call(body, [(dp, 'rows'), (dp, ('next', D, 0))], [((S, D), BF16, 'rows')], name="pool_bwd", tr=tr,
                     n_rows=S, halo=H, scratch=[pltpu.VMEM((tr + H, D), F32)])[0]


def _pool_group_fwd(p, wg, scale, x):
    S, D = p.shape
    NG, G, _ = wg.shape
    tm = _tile(S, 1024, SUBLANES)

    def body(p_ref, w_ref, s_ref, x_ref, o_ref, y_ref):
        y = jnp.dot(p_ref[...], w_ref[0], preferred_element_type=F32)
        y_ref[...] = y
        o_ref[...] = x_ref[...] + y * s_ref[...]

    blk = pl.BlockSpec((tm, G), lambda i, g: (i, g))
    return pl.pallas_call(
        body, name="pool_group_fwd", grid=(S // tm, NG),
        in_specs=[blk, pl.BlockSpec((1, G, G), lambda i, g: (g, 0, 0)), pl.BlockSpec((1, G), lambda i, g: (0, g)), blk],
        out_specs=[blk, blk],
        out_shape=[jax.ShapeDtypeStruct((S, D), F32), jax.ShapeDtypeStruct((S, D), F32)],
        compiler_params=_cparams(("parallel", "arbitrary")),
    )(p, wg, scale, x)


def _pool_group_bwd(p, wg, scale, y, dm):
    S, D = p.shape
    NG, G, _ = wg.shape
    tm = _tile(S, 1024, SUBLANES)
    nb = S // tm

    def body(p_ref, w_ref, s_ref, y_ref, dm_ref, dp_ref, dw_ref, ds_ref, acc_ref):
        i = pl.program_id(1)
        dmv = dm_ref[...]
        dy = (dmv * s_ref[...]).astype(BF16)
        dp_ref[...] = lax.dot_general(dy, w_ref[0], (((1,), (1,)), ((), ())), preferred_element_type=F32)
        dw = lax.dot_general(p_ref[...], dy, (((0,), (0,)), ((), ())), preferred_element_type=F32)
        dsp = jnp.sum(dmv * y_ref[...], axis=0, keepdims=True)

        @pl.when(i == 0)
        def _():
            acc_ref[...] = dw
            ds_ref[...] = dsp

        @pl.when(i > 0)
        def _():
            acc_ref[...] += dw
            ds_ref[...] += dsp

        @pl.when(i == nb - 1)
        def _():
            dw_ref[0] = acc_ref[...].astype(BF16)

    blk = pl.BlockSpec((tm, G), lambda g, i: (i, g))
    wspec = pl.BlockSpec((1, G, G), lambda g, i: (g, 0, 0))
    sspec = pl.BlockSpec((1, G), lambda g, i: (0, g))
    return pl.pallas_call(
        body, name="pool_group_bwd", grid=(NG, nb),
        in_specs=[blk, wspec, sspec, blk, blk],
        out_specs=[blk, wspec, sspec],
        out_shape=[jax.ShapeDtypeStruct((S, D), F32), jax.ShapeDtypeStruct((NG, G, G), BF16),
                   jax.ShapeDtypeStruct((1, D), F32)],
        scratch_shapes=[pltpu.VMEM((G, G), F32)],
        compiler_params=_cparams(("parallel", "arbitrary")),
    )(p, wg, scale, y, dm)


def _pool_mixer_fwd(x, gain, w_in, wg, scale):
    h = _rms_fwd(x, gain, name="pool_norm")
    u = _mm(h, w_in, name="pool_in")
    p = _pool_fwd(u)
    y, yg = _pool_group_fwd(p, wg, scale, x)
    return y, (h, p, yg)


def _pool_mixer_bwd(x, gain, w_in, wg, scale, saved, dy, dyb):
    h, p, yg = saved
    dp, dwg, dscale = _pool_group_bwd(p, wg, scale, yg, dy)
    du = _pool_bwd(dp)
    dh = _mm(du, w_in, tb=True, name="pool_dh")
    dw_in = _mm(h, du, ta=True, out_dtypes=(BF16,), name="pool_dwin")
    dx, dxb, dgain = _rms_bwd(x, gain, dh, dy, name="pool_dnorm")
    return dx, dxb, dgain, dw_in, dwg, dscale


def _qk_norm_fwd(qkv, qg, kg):
    S, D3 = qkv.shape
    D = D3 // 3
    NH = D // ATT_HEAD_DIM
    tr = _tile(S, 256, SUBLANES)

    def body(q_ref, k_ref, v_ref, qg_ref, kg_ref, qo_ref, ko_ref, vo_ref):
        for src, g_ref, dst in ((q_ref, qg_ref, qo_ref), (k_ref, kg_ref, ko_ref)):
            for hd in range(NH):
                cols = slice(hd * ATT_HEAD_DIM, (hd + 1) * ATT_HEAD_DIM)
                t = src[:, cols]
                r = lax.rsqrt(jnp.mean(t * t, axis=-1, keepdims=True) + RMS_EPS)
                dst[:, cols] = (t * r * g_ref[...]).astype(BF16)
        vo_ref[...] = v_ref[...].astype(BF16)

    ins = [(qkv, ('cols', D, 0)), (qkv, ('cols', D, 1)), (qkv, ('cols', D, 2)), (qg, 'full'), (kg, 'full')]
    return _row_call(body, ins, [((S, D), BF16, 'rows')] * 3, name="att_qknorm_fwd", tr=tr, n_rows=S)


def _qk_norm_bwd(qkv, qg, kg, dqn, dkn, dv):
    S, D3 = qkv.shape
    D = D3 // 3
    NH = D // ATT_HEAD_DIM
    tr = _tile(S, 128, SUBLANES)

    def body(q_ref, k_ref, qg_ref, kg_ref, dq_ref, dk_ref, dv_ref, o_ref, dqg_ref, dkg_ref):
        i = pl.program_id(0)
        for sec, (src, g_ref, d_ref, dg_ref) in enumerate(((q_ref, qg_ref, dq_ref, dqg_ref),
                                                            (k_ref, kg_ref, dk_ref, dkg_ref))):
            part = jnp.zeros((1, ATT_HEAD_DIM), F32)
            for hd in range(NH):
                cols = slice(hd * ATT_HEAD_DIM, (hd + 1) * ATT_HEAD_DIM)
                t = src[:, cols]
                r = lax.rsqrt(jnp.mean(t * t, axis=-1, keepdims=True) + RMS_EPS)
                tn = t * r
                d = d_ref[:, cols]
                dn = d * g_ref[...]
                dt = r * (dn - tn * jnp.mean(dn * tn, axis=-1, keepdims=True))
                o_ref[:, sec * D + hd * ATT_HEAD_DIM:sec * D + (hd + 1) * ATT_HEAD_DIM] = dt.astype(BF16)
                part = part + jnp.sum(d * tn, axis=0, keepdims=True)

            @pl.when(i == 0)
            def _():
                dg_ref[...] = part

            @pl.when(i > 0)
            def _():
                dg_ref[...] += part

        o_ref[:, 2 * D:] = dv_ref[...].astype(BF16)

    assert ATT_PAD % tr == 0 and dkn.shape[0] == S + ATT_PAD
    ins = [(qkv, ('cols', D, 0)), (qkv, ('cols', D, 1)), (qg, 'full'), (kg, 'full'), (dqn, 'rows'),
           (dkn, ('rows_from', D, ATT_PAD)), (dv, ('rows_from', D, ATT_PAD))]
    return _row_call(body, ins, [((S, D3), BF16, 'rows'), ((1, ATT_HEAD_DIM), F32, 'acc'),
                                 ((1, ATT_HEAD_DIM), F32, 'acc')], name="att_qknorm_bwd", tr=tr, n_rows=S)


def _att_band_mask():
    r = np.arange(ATT_TQ)[:, None]
    c = np.arange(ATT_TQ + ATT_PAD)[None, :]
    lo = (r // CHUNK) * CHUNK
    return (c >= lo) & (c < lo + ATT_PAD + CHUNK)


def _att_bias_toeplitz(rel_bias):
    H = rel_bias.shape[0]
    R, C = ATT_TQ, ATT_TQ + ATT_PAD
    L = C + R - 1
    assert R - 1 < REL_CLIP
    near = rel_bias[:, REL_CLIP - (R - 1):2 * REL_CLIP][:, ::-1]
    far = jnp.broadcast_to(rel_bias[:, 2 * REL_CLIP:], (H, L - near.shape[1]))
    v = jnp.concatenate([far, near, jnp.zeros((H, 1), rel_bias.dtype)], axis=1)
    skew = jnp.broadcast_to(v[:, None, :], (H, R, L + 1)).reshape(H, R * (L + 1))[:, :R * L].reshape(H, R, L)
    return skew[:, :, R - 1:R - 1 + C]


def _att_bias_tile(rel_bias):
    return jnp.where(_att_band_mask()[None], _att_bias_toeplitz(rel_bias), MASK_VALUE).astype(F32)


def _att_bias_grad(dtile, rel_bias):
    _, pull = jax.vjp(_att_bias_toeplitz, rel_bias)
    return pull(jnp.where(_att_band_mask()[None], dtile, 0.0))[0]


def _att_core_fwd(qn, kp, vp, bias):
    S, D = qn.shape
    NH = D // ATT_HEAD_DIM
    KW = ATT_TQ + ATT_PAD
    scale = ATT_HEAD_DIM ** -0.5

    def body(q_ref, k_ref, v_ref, b_ref, o_ref):
        qb = pl.program_id(1)
        start = pl.multiple_of(qb * ATT_TQ, ATT_TQ)
        ks = k_ref[pl.ds(start, KW), :]
        vs = v_ref[pl.ds(start, KW), :]
        s = lax.dot_general(q_ref[...], ks, (((1,), (1,)), ((), ())), preferred_element_type=F32) * scale + b_ref[0]
        kpos = lax.broadcasted_iota(jnp.int32, (1, KW), 1) + (qb * ATT_TQ - ATT_PAD)
        s = jnp.where(kpos >= 0, s, MASK_VALUE)
        m = jnp.max(s, axis=-1, keepdims=True)
        p = jnp.exp(s - m)
        l = jnp.sum(p, axis=-1, keepdims=True)
        o = jnp.dot(p.astype(BF16), vs, preferred_element_type=F32) / l
        o_ref[...] = o.astype(BF16)

    qspec = pl.BlockSpec((ATT_TQ, ATT_HEAD_DIM), lambda h, qb: (qb, h))
    kvspec = pl.BlockSpec((S + ATT_PAD, ATT_HEAD_DIM), lambda h, qb: (0, h))
    return pl.pallas_call(
        body, name="att_core_fwd", grid=(NH, S // ATT_TQ),
        in_specs=[qspec, kvspec, kvspec, pl.BlockSpec((1, ATT_TQ, KW), lambda h, qb: (h, 0, 0))],
        out_specs=qspec, out_shape=jax.ShapeDtypeStruct((S, D), BF16),
        compiler_params=_cparams(("parallel", "arbitrary")),
    )(qn, kp, vp, bias)


def _att_core_bwd(qn, kp, vp, bias, do):
    S, D = qn.shape
    NH = D // ATT_HEAD_DIM
    KW = ATT_TQ + ATT_PAD
    scale = ATT_HEAD_DIM ** -0.5

    def body(q_ref, k_ref, v_ref, b_ref, do_ref, dq_ref, dk_ref, dv_ref, db_ref):
        qb = pl.program_id(1)
        start = pl.multiple_of(qb * ATT_TQ, ATT_TQ)
        q = q_ref[...]
        dov = do_ref[...]
        ks = k_ref[pl.ds(start, KW), :]
        vs = v_ref[pl.ds(start, KW), :]
        s = lax.dot_general(q, ks, (((1,), (1,)), ((), ())), preferred_element_type=F32) * scale + b_ref[0]
        kpos = lax.broadcasted_iota(jnp.int32, (1, KW), 1) + (qb * ATT_TQ - ATT_PAD)
        s = jnp.where(kpos >= 0, s, MASK_VALUE)
        m = jnp.max(s, axis=-1, keepdims=True)
        e = jnp.exp(s - m)
        p = e / jnp.sum(e, axis=-1, keepdims=True)
        dp = lax.dot_general(dov, vs, (((1,), (1,)), ((), ())), preferred_element_type=F32)
        ds = p * (dp - jnp.sum(p * dp, axis=-1, keepdims=True))
        dsb = ds.astype(BF16)
        dq_ref[...] = jnp.dot(dsb, ks, preferred_element_type=F32) * scale
        dk = lax.dot_general(dsb, q, (((0,), (0,)), ((), ())), preferred_element_type=F32) * scale
        dv = lax.dot_general(p.astype(BF16), dov, (((0,), (0,)), ((), ())), preferred_element_type=F32)

        @pl.when(qb == 0)
        def _():
            dk_ref[...] = jnp.zeros_like(dk_ref)
            dv_ref[...] = jnp.zeros_like(dv_ref)
            db_ref[0] = ds

        @pl.when(qb > 0)
        def _():
            db_ref[0] += ds

        dk_ref[pl.ds(start, KW), :] += dk
        dv_ref[pl.ds(start, KW), :] += dv

    qspec = pl.BlockSpec((ATT_TQ, ATT_HEAD_DIM), lambda h, qb: (qb, h))
    kvspec = pl.BlockSpec((S + ATT_PAD, ATT_HEAD_DIM), lambda h, qb: (0, h))
    bspec = pl.BlockSpec((1, ATT_TQ, KW), lambda h, qb: (h, 0, 0))
    return pl.pallas_call(
        body, name="att_core_bwd", grid=(NH, S // ATT_TQ),
        in_specs=[qspec, kvspec, kvspec, bspec, qspec],
        out_specs=[qspec, kvspec, kvspec, bspec],
        out_shape=[jax.ShapeDtypeStruct((S, D), F32), jax.ShapeDtypeStruct((S + ATT_PAD, D), F32),
                   jax.ShapeDtypeStruct((S + ATT_PAD, D), F32), jax.ShapeDtypeStruct((NH, ATT_TQ, KW), F32)],
        compiler_params=_cparams(("parallel", "arbitrary")),
    )(qn, kp, vp, bias, do)


def _att_mixer_fwd(x, gain, w_qkv, qg, kg, rel_bias, w_out):
    h = _rms_fwd(x, gain, name="att_norm")
    qkv = _mm(h, w_qkv, name="att_qkv")
    qn, kn, v = _qk_norm_fwd(qkv, qg, kg)
    kp = jnp.pad(kn, ((ATT_PAD, 0), (0, 0)))
    vp = jnp.pad(v, ((ATT_PAD, 0), (0, 0)))
    bias = _att_bias_tile(rel_bias)
    o = _att_core_fwd(qn, kp, vp, bias)
    y = _mm(o, w_out, extras=(x,), epilogue=lambda acc, res: (acc + res,), name="att_out")
    return y, (h, qkv, qn, kp, vp, bias, o, rel_bias)


def _att_mixer_bwd(x, gain, w_qkv, qg, kg, w_out, saved, dy, dyb):
    h, qkv, qn, kp, vp, bias, o, rel_bias = saved
    do = _mm(dyb, w_out, tb=True, out_dtypes=(BF16,), name="att_do")
    dw_out = _mm(o, dyb, ta=True, out_dtypes=(BF16,), name="att_dwout")
    dqn, dkp, dvp, dbt = _att_core_bwd(qn, kp, vp, bias, do)
    drel = _att_bias_grad(dbt, rel_bias)
    dqkv, dqg, dkg = _qk_norm_bwd(qkv, qg, kg, dqn, dkp, dvp)
    dh = _mm(dqkv, w_qkv, tb=True, name="att_dh")
    dw_qkv = _mm(h, dqkv, ta=True, out_dtypes=(BF16,), name="att_dwqkv")
    dx, dxb, dgain = _rms_bwd(x, gain, dh, dy, name="att_dnorm")
    return dx, dxb, dgain, dw_qkv, dqg, dkg, drel, dw_out


def _ssm_tables(a_re, a_im, log_dt, b_re, b_im, c_re, c_im):
    G, N = a_re.shape
    NS = G // SSM_SLAB_GROUPS
    lam = lax.complex(a_re, a_im)
    dt = jnp.exp(log_dt)[:, None]
    abar = jnp.exp(lam * dt)
    coef = (abar - 1.0) / lam
    bbar = coef[..., None] * lax.complex(b_re, b_im)
    eye = jnp.eye(SSM_SLAB_GROUPS, dtype=F32)

    def blockdiag(t):
        P, Q = t.shape[1:]
        t = t.reshape(NS, SSM_SLAB_GROUPS, P, Q)
        return jnp.einsum('sgpq,gh->sgphq', t, eye).reshape(NS, SSM_SLAB_GROUPS * P, SSM_SLAB_GROUPS * Q)

    bt = jnp.swapaxes(bbar, 1, 2)
    bmat = jnp.concatenate([blockdiag(jnp.real(bt)), blockdiag(jnp.imag(bt))], axis=2)
    ct = jnp.swapaxes(lax.complex(c_re, c_im), 1, 2)
    cmat = jnp.concatenate([blockdiag(jnp.real(ct)), -blockdiag(jnp.imag(ct))], axis=1)
    al = abar.reshape(NS, 1, SSM_SLAB_STATE)
    rows = jnp.arange(SUBLANES)[None, :, None]
    fwd, bwd = [], []
    for k in (1, 2, 4):
        ak = al ** k
        f = jnp.where(rows >= k, ak, 0.0)
        b = jnp.where(rows < SUBLANES - k, ak, 0.0)
        fwd += [jnp.real(f), jnp.imag(f)]
        bwd += [jnp.real(b), jnp.imag(b)]
    pf = al ** (rows + 1)
    pb = al ** (SUBLANES - rows)
    fwd += [jnp.real(pf), jnp.imag(pf)]
    bwd += [jnp.real(pb), jnp.imag(pb)]
    coef_f = jnp.concatenate(fwd, axis=1).astype(F32)
    coef_b = jnp.concatenate(bwd, axis=1).astype(F32)
    return dict(lam=lam, dt=dt, abar=abar, coef=coef, bmat=bmat.astype(BF16), cmat=cmat.astype(BF16),
                bmat_t=jnp.swapaxes(bmat, 1, 2).astype(BF16), cmat_t=jnp.swapaxes(cmat, 1, 2).astype(BF16),
                coef_f=coef_f, coef_b=coef_b)


def _ssm_scan_fwd(u, tabs, d_skip):
    S, D = u.shape
    NS = D // LANES
    W = 2 * SSM_SLAB_STATE
    T = _tile(S, 512, SUBLANES)
    HS = SSM_SLAB_STATE

    def body(u_ref, bm_ref, cm_ref, cf_ref, d_ref, y_ref, xs_ref, bu_scr, carry_scr):
        i = pl.program_id(1)

        @pl.when(i == 0)
        def _():
            carry_scr[...] = jnp.zeros_like(carry_scr)

        uv = u_ref[...]
        bu_scr[...] = jnp.dot(uv.astype(BF16), bm_ref[0], preferred_element_type=F32)

        def step(r, carry):
            rows = pl.ds(pl.multiple_of(r * SUBLANES, SUBLANES), SUBLANES)
            xr = bu_scr[rows, 0:HS]
            xi = bu_scr[rows, HS:W]
            for n, k in enumerate((1, 2, 4)):
                ar = cf_ref[0, 16 * n:16 * n + 8, :]
                ai = cf_ref[0, 16 * n + 8:16 * n + 16, :]
                sr = pltpu.roll(xr, k, 0)
                si = pltpu.roll(xi, k, 0)
                xr, xi = xr + ar * sr - ai * si, xi + ar * si + ai * sr
            pr = cf_ref[0, 48:56, :]
            pi_ = cf_ref[0, 56:64, :]
            cr, ci = carry
            xr, xi = xr + pr * cr - pi_ * ci, xi + pr * ci + pi_ * cr
            xs_ref[rows, 0:HS] = xr
            xs_ref[rows, HS:W] = xi
            return xr[SUBLANES - 1:SUBLANES, :], xi[SUBLANES - 1:SUBLANES, :]

        cr, ci = lax.fori_loop(0, T // SUBLANES, step, (carry_scr[0:1, 0:HS], carry_scr[0:1, HS:W]), unroll=2)
        carry_scr[0:1, 0:HS] = cr
        carry_scr[0:1, HS:W] = ci
        y_ref[...] = jnp.dot(xs_ref[...].astype(BF16), cm_ref[0], preferred_element_type=F32) + d_ref[...] * uv

    return pl.pallas_call(
        body, name="ssm_scan_fwd", grid=(NS, S // T),
        in_specs=[pl.BlockSpec((T, LANES), lambda j, i: (i, j)),
                  pl.BlockSpec((1, LANES, W), lambda j, i: (j, 0, 0)),
                  pl.BlockSpec((1, W, LANES), lambda j, i: (j, 0, 0)),
                  pl.BlockSpec((1, 8 * SUBLANES, HS), lambda j, i: (j, 0, 0)),
                  pl.BlockSpec((1, LANES), lambda j, i: (0, j))],
        out_specs=[pl.BlockSpec((T, LANES), lambda j, i: (i, j)), pl.BlockSpec((T, W), lambda j, i: (i, j))],
        out_shape=[jax.ShapeDtypeStruct((S, D), F32), jax.ShapeDtypeStruct((S, NS * W), F32)],
        scratch_shapes=[pltpu.VMEM((T, W), F32), pltpu.VMEM((SUBLANES, W), F32)],
        compiler_params=_cparams(("parallel", "arbitrary")),
    )(u, tabs['bmat'], tabs['cmat'], tabs['coef_f'], d_skip)


def _ssm_scan_bwd(u, xs, dy, tabs, d_skip):
    S, D = u.shape
    NS = D // LANES
    W = 2 * SSM_SLAB_STATE
    T = _tile(S, 512, SUBLANES)
    HS = SSM_SLAB_STATE
    nb = S // T

    def body(u_ref, xs_ref, dy_ref, bt_ref, ct_ref, cf_ref, d_ref, du_ref, gb_ref, gc_ref, q_ref,
             cy_scr, lam_scr, carry_scr):
        i = pl.program_id(1)

        @pl.when(i == 0)
        def _():
            carry_scr[...] = jnp.zeros_like(carry_scr)
            gb_ref[...] = jnp.zeros_like(gb_ref)
            gc_ref[...] = jnp.zeros_like(gc_ref)
            q_ref[...] = jnp.zeros_like(q_ref)

        dyv = dy_ref[...]
        dyb = dyv.astype(BF16)
        cy_scr[...] = jnp.dot(dyb, ct_ref[0], preferred_element_type=F32)

        def step(n, carry):
            r = T // SUBLANES - 1 - n
            rows = pl.ds(pl.multiple_of(r * SUBLANES, SUBLANES), SUBLANES)
            cyr = cy_scr[rows, 0:HS]
            cyi = cy_scr[rows, HS:W]
            lr, li = cyr, cyi
            for m, k in enumerate((1, 2, 4)):
                br = cf_ref[0, 16 * m:16 * m + 8, :]
                bi = cf_ref[0, 16 * m + 8:16 * m + 16, :]
                sr = pltpu.roll(lr, SUBLANES - k, 0)
                si = pltpu.roll(li, SUBLANES - k, 0)
                lr, li = lr + br * sr + bi * si, li + br * si - bi * sr
            pr = cf_ref[0, 48:56, :]
            pi_ = cf_ref[0, 56:64, :]
            cr, ci, qr, qi = carry
            lr, li = lr + pr * cr + pi_ * ci, li + pr * ci - pi_ * cr
            lam_scr[rows, 0:HS] = lr
            lam_scr[rows, HS:W] = li
            mr, mi = lr - cyr, li - cyi
            xr = xs_ref[rows, 0:HS]
            xi = xs_ref[rows, HS:W]
            return lr[0:1, :], li[0:1, :], qr + mr * xr + mi * xi, qi + mi * xr - mr * xi

        zero = jnp.zeros((SUBLANES, HS), F32)
        cr, ci, qr, qi = lax.fori_loop(0, T // SUBLANES, step,
                                       (carry_scr[0:1, 0:HS], carry_scr[0:1, HS:W], zero, zero), unroll=2)
        carry_scr[0:1, 0:HS] = cr
        carry_scr[0:1, HS:W] = ci
        q_ref[0, :, 0:HS] += qr
        q_ref[0, :, HS:W] += qi
        lamb = lam_scr[...].astype(BF16)
        uv = u_ref[...]
        du_ref[...] = jnp.dot(lamb, bt_ref[0], preferred_element_type=F32) + d_ref[...] * dyv
        gb_ref[0] += lax.dot_general(lamb, uv.astype(BF16), (((0,), (0,)), ((), ())), preferred_element_type=F32)
        gc_ref[0] += lax.dot_general(xs_ref[...].astype(BF16), dyb, (((0,), (0,)), ((), ())),
                                     preferred_element_type=F32)

    rev = lambda j, i: (nb - 1 - i, j)
    slab3 = lambda j, i: (j, 0, 0)
    return pl.pallas_call(
        body, name="ssm_scan_bwd", grid=(NS, nb),
        in_specs=[pl.BlockSpec((T, LANES), rev), pl.BlockSpec((T, W), rev), pl.BlockSpec((T, LANES), rev),
                  pl.BlockSpec((1, W, LANES), slab3), pl.BlockSpec((1, LANES, W), slab3),
                  pl.BlockSpec((1, 8 * SUBLANES, HS), slab3), pl.BlockSpec((1, LANES), lambda j, i: (0, j))],
        out_specs=[pl.BlockSpec((T, LANES), rev), pl.BlockSpec((1, W, LANES), slab3),
                   pl.BlockSpec((1, W, LANES), slab3), pl.BlockSpec((1, SUBLANES, W), slab3)],
        out_shape=[jax.ShapeDtypeStruct((S, D), F32), jax.ShapeDtypeStruct((NS, W, LANES), F32),
                   jax.ShapeDtypeStruct((NS, W, LANES), F32), jax.ShapeDtypeStruct((NS, SUBLANES, W), F32)],
        scratch_shapes=[pltpu.VMEM((T, W), F32), pltpu.VMEM((T, W), F32), pltpu.VMEM((SUBLANES, W), F32)],
        compiler_params=_cparams(("parallel", "arbitrary")),
    )(u, xs, dy, tabs['bmat_t'], tabs['cmat_t'], tabs['coef_b'], d_skip)


def _ssm_param_grads(tabs, b_re, b_im, gb, gc, q):
    NS = gb.shape[0]
    G = NS * SSM_SLAB_GROUPS
    N, C = SSM_STATE, SSM_GROUP

    def diag_blocks(t):
        t = t.reshape(NS, SSM_SLAB_GROUPS, N, SSM_SLAB_GROUPS, C)
        t = jnp.einsum('sgnhc,gh->sgnc', t, jnp.eye(SSM_SLAB_GROUPS, dtype=F32))
        return t.reshape(G, N, C)

    HS = SSM_SLAB_STATE
    g_bbar = lax.complex(diag_blocks(gb[:, :HS]), diag_blocks(gb[:, HS:]))
    g_c = lax.complex(diag_blocks(gc[:, :HS]), -diag_blocks(gc[:, HS:]))
    qs = jnp.sum(q, axis=1)
    qc = lax.complex(qs[:, :HS], qs[:, HS:]).reshape(G, N)
    lam, dt, abar, coef = tabs['lam'], tabs['dt'], tabs['abar'], tabs['coef']
    bmat = lax.complex(b_re, b_im)
    g_b = g_bbar * jnp.conj(coef)[..., None]
    g_coef = jnp.sum(g_bbar * jnp.conj(bmat), axis=-1)
    g_abar_coef = g_coef * jnp.conj(1.0 / lam)
    g_lam = g_coef * jnp.conj(-(abar - 1.0) / (lam * lam))
    g_ld = qc + jnp.conj(abar) * g_abar_coef
    g_lam = g_lam + g_ld * dt
    g_dt = jnp.sum(jnp.real(g_ld * jnp.conj(lam)), axis=-1)
    g_logdt = g_dt * dt[:, 0]
    g_ct = jnp.swapaxes(g_c, 1, 2)
    return (jnp.real(g_lam), jnp.imag(g_lam), g_logdt, jnp.real(g_b), jnp.imag(g_b), jnp.real(g_ct), jnp.imag(g_ct))


_GELU_C = math.sqrt(2.0 / math.pi)


def _gelu_fwd(y):
    S, D = y.shape

    def body(y_ref, z_ref):
        v = y_ref[...]
        z_ref[...] = (0.5 * v * (1.0 + jnp.tanh(_GELU_C * (v + 0.044715 * v * v * v)))).astype(BF16)

    return _row_call(body, [(y, 'rows')], [((S, D), BF16, 'rows')], name="ssm_gelu_fwd",
                     tr=_tile(S, 512, SUBLANES), n_rows=S)[0]


def _gelu_bwd(y, dz, u):
    S, D = y.shape

    def body(y_ref, dz_ref, u_ref, dy_ref, dd_ref):
        i = pl.program_id(0)
        v = y_ref[...]
        t = jnp.tanh(_GELU_C * (v + 0.044715 * v * v * v))
        g = 0.5 * (1.0 + t) + 0.5 * v * (1.0 - t * t) * _GELU_C * (1.0 + 3 * 0.044715 * v * v)
        dy = dz_ref[...] * g
        dy_ref[...] = dy
        part = jnp.sum(dy * u_ref[...], axis=0, keepdims=True)

        @pl.when(i == 0)
        def _():
            dd_ref[...] = part

        @pl.when(i > 0)
        def _():
            dd_ref[...] += part

    return _row_call(body, [(y, 'rows'), (dz, 'rows'), (u, 'rows')], [((S, D), F32, 'rows'), ((1, D), F32, 'acc')],
                     name="ssm_gelu_bwd", tr=_tile(S, 256, SUBLANES), n_rows=S)


def _glu_fwd(zz, x):
    S, D = x.shape

    def body(a_ref, g_ref, x_ref, o_ref):
        o_ref[...] = x_ref[...] + a_ref[...] * jax.nn.sigmoid(g_ref[...])

    return _row_call(body, [(zz, ('cols', D, 0)), (zz, ('cols', D, 1)), (x, 'rows')], [((S, D), F32, 'rows')],
                     name="ssm_glu_fwd", tr=_tile(S, 256, SUBLANES), n_rows=S)[0]


def _glu_bwd(zz, dm):
    S, D = dm.shape

    def body(a_ref, g_ref, dm_ref, o_ref):
        s = jax.nn.sigmoid(g_ref[...])
        d = dm_ref[...]
        o_ref[:, 0:D] = (d * s).astype(BF16)
        o_ref[:, D:] = (d * a_ref[...] * s * (1.0 - s)).astype(BF16)

    return _row_call(body, [(zz, ('cols', D, 0)), (zz, ('cols', D, 1)), (dm, 'rows')], [((S, 2 * D), BF16, 'rows')],
                     name="ssm_glu_bwd", tr=_tile(S, 256, SUBLANES), n_rows=S)[0]


def _ssm_mixer_fwd(x, gain, tabs, d_skip, w_glu):
    _, u = _rms_fwd(x, gain, name="ssm_norm", with_f32=True)
    yv, xs = _ssm_scan_fwd(u, tabs, d_skip)
    z = _gelu_fwd(yv)
    zz = _mm(z, w_glu, name="ssm_glu_in")
    y = _glu_fwd(zz, x)
    return y, (u, xs, yv, z, zz)


def _ssm_mixer_bwd(x, gain, tabs, d_skip, w_glu, b_re, b_im, saved, dy, dyb):
    u, xs, yv, z, zz = saved
    dzz = _glu_bwd(zz, dy)
    dz = _mm(dzz, w_glu, tb=True, name="ssm_dz")
    dw_glu = _mm(z, dzz, ta=True, out_dtypes=(BF16,), name="ssm_dwglu")
    dyv, dd = _gelu_bwd(yv, dz, u)
    du, gb, gc, q = _ssm_scan_bwd(u, xs, dyv, tabs, d_skip)
    small = _ssm_param_grads(tabs, b_re, b_im, gb, gc, q)
    dx, dxb, dgain = _rms_bwd(x, gain, du, dy, name="ssm_dnorm")
    return dx, dxb, dgain, small, dd, dw_glu


class _LocalWeights:
    def __init__(self, p):
        self.p = p

    def layer(self, i):
        return {n: (v[i] if n.startswith('mlp') else v) for n, v in self.p.items()}

    def now(self, i, parts):
        pass

    def side(self, i, parts):
        return None

    def deliver(self, i, parts, outs):
        pass


class _LocalGrads:
    def __init__(self):
        self.g = {}

    def stage_a(self, i, parts, mats):
        for n, v in mats.items():
            if n.startswith('mlp'):
                self.g.setdefault(n, {})[i] = v
            else:
                self.g[n] = v

    def side(self, i, parts):
        return None

    def finish(self, i, parts, outs):
        pass

    def now(self, i, parts):
        pass


def _local_step(x, target, p, weights=None, sink=None):
    depth = p['norm_mix'].shape[0]
    local = sink is None
    weights = weights or _LocalWeights({n: p[n] for n in _BIG})
    sink = sink or _LocalGrads()
    tabs = _ssm_tables(p['ssm_a_re'], p['ssm_a_im'], p['ssm_log_dt'], p['ssm_b_re'], p['ssm_b_im'], p['ssm_c_re'],
                       p['ssm_c_im'])
    xs_in, saved_mix, saved_mlp, lw = [], [], [], []
    weights.now(0, ['mix'])
    for i in range(depth):
        w = weights.layer(i)
        gm = p['norm_mix'][i:i + 1]
        xs_in.append(x)
        if i % 4 == 0:
            side = weights.side(0, ['w1', 'w2']) if i == 0 else None
            x, sv, side_out = _conv_mixer_fwd(x, gm, w['conv_w_in'], p['conv_w'], w['conv_w_out'], side=side)
            if side:
                weights.deliver(0, ['w1', 'w2'], side_out)
        elif i % 4 == 1:
            x, sv = _pool_mixer_fwd(x, gm, w['pool_w_in'], w['pool_w_group'], p['pool_scale'])
        elif i % 4 == 2:
            x, sv = _att_mixer_fwd(x, gm, w['att_w_qkv'], p['att_q_norm'], p['att_k_norm'], p['att_rel_bias'],
                                   w['att_w_out'])
        else:
            x, sv = _ssm_mixer_fwd(x, gm, tabs, p['ssm_d'], w['ssm_w_glu'])
        saved_mix.append(sv)
        xs_in.append(x)
        w = weights.layer(i)
        lw.append(w)
        nxt = [['mix', 'w1'], ['w2']] if i + 1 < depth else []
        sides = [weights.side(i + 1, parts) for parts in nxt] or [None, None]
        x, sv, side_out = _mlp_fwd(x, p['norm_mlp'][i:i + 1], w['mlp_w1'], w['mlp_w2'], tag=str(i), sides=sides)
        for parts, side, out in zip(nxt, sides, side_out):
            if side:
                weights.deliver(i + 1, parts, out)
        saved_mlp.append(sv)
    dx, dxb, loss_cols = _loss_head(x, target, name="loss_head")
    g = {'norm_mix': [None] * depth, 'norm_mlp': [None] * depth}
    todo = []
    for i in reversed(range(depth)):
        w = lw[i]
        big = {}
        sides = [sink.side(i + 1, parts) for parts in todo] or [None, None, None]
        dx, dxb, g['norm_mlp'][i], big['mlp_w1'], big['mlp_w2'], side_out = _mlp_bwd(
            xs_in[2 * i + 1], p['norm_mlp'][i:i + 1], w['mlp_w1'], w['mlp_w2'], saved_mlp[i], dx, dxb, tag=str(i),
            sides=sides)
        for parts, side, out in zip(todo, sides, side_out):
            if side:
                sink.finish(i + 1, parts, out)
        sink.stage_a(i, ['w1', 'w2'], {n: big[n] for n in ('mlp_w1', 'mlp_w2')})
        gm = p['norm_mix'][i:i + 1]
        xin, sv = xs_in[2 * i], saved_mix[i]
        if i % 4 == 0:
            mine = [['w1'], ['w2']] if i == 0 else []
            sides = [sink.side(i, parts) for parts in mine] or [None, None]
            dx, dxb, g['norm_mix'][i], big['conv_w_in'], g['conv_w'], big['conv_w_out'], side_out = _conv_mixer_bwd(
                xin, gm, w['conv_w_in'], p['conv_w'], w['conv_w_out'], sv, dx, dxb, sides=sides)
            for parts, side, out in zip(mine, sides, side_out):
                if side:
                    sink.finish(i, parts, out)
        elif i % 4 == 1:
            dx, dxb, g['norm_mix'][i], big['pool_w_in'], big['pool_w_group'], g['pool_scale'] = _pool_mixer_bwd(
                xin, gm, w['pool_w_in'], w['pool_w_group'], p['pool_scale'], sv, dx, dxb)
        elif i % 4 == 2:
            (dx, dxb, g['norm_mix'][i], big['att_w_qkv'], g['att_q_norm'], g['att_k_norm'], g['att_rel_bias'],
             big['att_w_out']) = _att_mixer_bwd(xin, gm, w['att_w_qkv'], p['att_q_norm'], p['att_k_norm'],
                                                w['att_w_out'], sv, dx, dxb)
        else:
            dx, dxb, g['norm_mix'][i], small, g['ssm_d'], big['ssm_w_glu'] = _ssm_mixer_bwd(
                xin, gm, tabs, p['ssm_d'], w['ssm_w_glu'], p['ssm_b_re'], p['ssm_b_im'], sv, dx, dxb)
            (g['ssm_a_re'], g['ssm_a_im'], g['ssm_log_dt'], g['ssm_b_re'], g['ssm_b_im'], g['ssm_c_re'],
             g['ssm_c_im']) = small
        sink.stage_a(i, ['mix'], {n: v for n, v in big.items() if not n.startswith('mlp')})
        if i == 0:
            sink.now(i, ['mix'])
        todo = [['w1'], ['w2'], ['mix']]
    g['norm_mix'] = jnp.concatenate(g['norm_mix'], axis=0)
    g['norm_mlp'] = jnp.concatenate(g['norm_mlp'], axis=0)
    if local:
        for n, v in sink.g.items():
            g[n] = [v[i] for i in range(depth)] if n.startswith('mlp') else v
    return loss_cols, dx, g


_ANY = pl.BlockSpec(memory_space=pl.ANY)
_VM = pl.BlockSpec(memory_space=pltpu.VMEM)
_REL_ALL = [(0, 0, 1), (0, 1, 0), (0, 1, 1), (1, 0, 0), (1, 0, 1), (1, 1, 0), (1, 1, 1)]
_REL_CHIPS = [(1, 0, 0), (0, 1, 0), (1, 1, 0)]


def _me():
    return lax.axis_index("x"), lax.axis_index("y"), lax.axis_index("c")


def _flip(pos, rel):
    return tuple(1 - p if r else p for p, r in zip(pos, rel))


def _chip_of(pos):
    return 2 * pos[0] + pos[1]


def _dev_of(pos):
    return 4 * pos[0] + 2 * pos[1] + pos[2]


def _gather_small(buf, *, reduce, name):
    rows = buf.shape[0]

    def body(in_ref, out_ref, *rest):
        if reduce:
            gath, send_sems, recv_sems = rest
        else:
            gath = out_ref
            send_sems, recv_sems = rest
        me = _me()
        gath[_dev_of(me)] = in_ref[...]
        copies = []
        for k, rel in enumerate(_REL_ALL):
            peer = _flip(me, rel)
            cp = pltpu.make_async_remote_copy(src_ref=in_ref, dst_ref=gath.at[_dev_of(me)], send_sem=send_sems.at[k],
                                              recv_sem=recv_sems.at[k], device_id=peer, device_id_type=MESH)
            cp.start()
            copies.append(cp)
        for k, rel in enumerate(_REL_ALL):
            peer = _flip(me, rel)
            pltpu.make_async_remote_copy(src_ref=in_ref, dst_ref=gath.at[_dev_of(peer)], send_sem=send_sems.at[k],
                                         recv_sem=recv_sems.at[k], device_id=peer, device_id_type=MESH).wait_recv()
        for cp in copies:
            cp.wait_send()
        if reduce:
            acc = gath[0]
            for s in range(1, N_DEV):
                acc = acc + gath[s]
            out_ref[...] = acc

    out_shape = (rows, LANES) if reduce else (N_DEV, rows, LANES)
    scratch = ([pltpu.VMEM((N_DEV, rows, LANES), F32)] if reduce else []) + [
        pltpu.SemaphoreType.DMA((len(_REL_ALL),)), pltpu.SemaphoreType.DMA((len(_REL_ALL),))]
    return pl.pallas_call(
        body, name=name, in_specs=[_VM], out_specs=_VM, out_shape=jax.ShapeDtypeStruct(out_shape, F32),
        scratch_shapes=scratch, compiler_params=pltpu.CompilerParams(vmem_limit_bytes=VMEM_LIMIT),
    )(buf)


def _region(ref, kind, k, shard_shape, half, quarter=None):
    r, c = shard_shape
    n = r // 2 if quarter is None else r // 4
    start = half * (r // 2) + (0 if quarter is None else quarter * n)
    if kind == 'col':
        return ref.at[pl.ds(pl.multiple_of(start, 16), n), pl.ds(pl.multiple_of(k * c, LANES), c)]
    return ref.at[pl.ds(pl.multiple_of(k * r + start, 16), n), :]


def _place_block(shard, dest, layer, chip, kind, *, name):
    _, r, c = shard.shape
    tr = _tile(r, max(16, (1 << 20) // c // 16 * 16), 16)
    nb = r // tr

    def body(chip_ref, s_ref, d_ref, o_ref):
        o_ref[...] = s_ref[0].astype(BF16)

    out_map = (lambda i, k: (i, k[0])) if kind == 'col' else (lambda i, k: (k[0] * nb + i, 0))
    grid_spec = pltpu.PrefetchScalarGridSpec(
        num_scalar_prefetch=1, grid=(nb,),
        in_specs=[pl.BlockSpec((1, tr, c), lambda i, k: (layer, i, 0)), _ANY],
        out_specs=pl.BlockSpec((tr, c), out_map))
    return pl.pallas_call(
        body, name=name, grid_spec=grid_spec, out_shape=jax.ShapeDtypeStruct(dest.shape, BF16),
        input_output_aliases={2: 0}, compiler_params=_cparams(("arbitrary",)),
    )(chip, shard, dest)


def _all_gather_matrices(fulls, kinds, *, name):
    return _run_side(_gather_side(fulls, kinds), name=name)


class _Ops:
    def __init__(self, active):
        self.active, self.cur, self.sends = active, 0, []

    def phase(self, p):
        self.cur = p

    def send(self, src, dst, sems_s, sems_r, idx, to):
        cp = pltpu.make_async_remote_copy(src_ref=src, dst_ref=dst, send_sem=sems_s.at[idx], recv_sem=sems_r.at[idx],
                                          device_id=to, device_id_type=MESH)
        if self.cur == self.active:
            cp.start()
        self.sends.append(cp)

    def arrived(self, where, sems_s, sems_r, idx, frm):
        if self.cur == self.active:
            pltpu.make_async_remote_copy(src_ref=where, dst_ref=where, send_sem=sems_s.at[idx], recv_sem=sems_r.at[idx],
                                         device_id=frm, device_id_type=MESH).wait_recv()


def _phases(plan, n_phases):
    def make(p):
        def run(ins, outs, sems):
            op = _Ops(p)
            plan(ins, outs, sems, op)
            if p == n_phases - 1:
                for cp in op.sends:
                    cp.wait_send()
        return run
    return [make(p) for p in range(n_phases)]


def _gather_side(fulls, kinds):
    n = len(fulls)
    shards = [(f.shape[0], f.shape[1] // N_CHIPS) if kd == 'col' else (f.shape[0] // N_CHIPS, f.shape[1])
              for f, kd in zip(fulls, kinds)]

    def plan(in_refs, out_refs, sems, op):
        ici_send, ici_recv, fwd_send, fwd_recv, d2d_send, d2d_recv = sems
        me = _me()
        core = me[2]
        sib = _flip(me, (0, 0, 1))
        nbr = [_flip(me, (1, 0, 0)), _flip(me, (0, 1, 0))]
        diag = _chip_of(_flip(me, (1, 1, 0)))
        op.phase(0)
        for m in range(n):
            src = _region(in_refs[m], kinds[m], _chip_of(me), shards[m], core)
            dst = _region(out_refs[m], kinds[m], _chip_of(me), shards[m], core)
            for j in range(2):
                op.send(src, dst, ici_send, ici_recv, (m, j), nbr[j])
        op.phase(1)
        for m in range(n):
            for j in range(2):
                landed = _region(out_refs[m], kinds[m], _chip_of(nbr[j]), shards[m], core)
                op.arrived(landed, ici_send, ici_recv, (m, j), nbr[j])
                part = _region(out_refs[m], kinds[m], _chip_of(nbr[j]), shards[m], core, quarter=j)
                op.send(part, part, fwd_send, fwd_recv, (m, j), nbr[1 - j])
                op.send(landed, landed, d2d_send, d2d_recv, (m, j), sib)
        op.phase(2)
        for m in range(n):
            for j in range(2):
                part = _region(out_refs[m], kinds[m], diag, shards[m], core, quarter=j)
                op.arrived(part, fwd_send, fwd_recv, (m, j), nbr[1 - j])
                op.send(part, part, d2d_send, d2d_recv, (m, 2 + j), sib)
        op.phase(3)
        for m in range(n):
            for j in range(2):
                op.arrived(_region(out_refs[m], kinds[m], _chip_of(nbr[j]), shards[m], 1 - core),
                           d2d_send, d2d_recv, (m, j), sib)
                op.arrived(_region(out_refs[m], kinds[m], diag, shards[m], 1 - core, quarter=j),
                           d2d_send, d2d_recv, (m, 2 + j), sib)

    return _Side(list(fulls), [jax.ShapeDtypeStruct(f.shape, BF16) for f in fulls], {m: m for m in range(n)},
                 [pltpu.SemaphoreType.DMA((n, 2))] * 4 + [pltpu.SemaphoreType.DMA((n, 4))] * 2, _phases(plan, 4))


def _piece_shape(full, kind):
    R, C = full
    return (R // 2, C // N_CHIPS) if kind == 'col' else (R // N_DEV, C)


def _piece(ref, kind, k, c, full):
    pr, pc = _piece_shape(full, kind)
    if kind == 'col':
        return ref.at[pl.ds(pl.multiple_of(c * pr, 16), pr), pl.ds(pl.multiple_of(k * pc, LANES), pc)]
    return ref.at[pl.ds(pl.multiple_of((2 * k + c) * pr, 16), pr), :]


def _pair_exchange(grads, kinds, *, name):
    n = len(grads)
    pieces = [_piece_shape(g.shape, kd) for g, kd in zip(grads, kinds)]

    def body(*refs):
        in_refs, out_refs = refs[:n], refs[n:2 * n]
        send_sems, recv_sems = refs[2 * n:]
        me = _me()
        core = me[2]
        sib = _flip(me, (0, 0, 1))
        sends = []
        for m in range(n):
            for k in range(N_CHIPS):
                cp = pltpu.make_async_remote_copy(
                    src_ref=_piece(in_refs[m], kinds[m], k, 1 - core, grads[m].shape), dst_ref=out_refs[m].at[k],
                    send_sem=send_sems.at[m, k], recv_sem=recv_sems.at[m, k], device_id=sib, device_id_type=MESH)
                cp.start()
                sends.append(cp)
        for m in range(n):
            for k in range(N_CHIPS):
                pltpu.make_async_remote_copy(
                    src_ref=_piece(in_refs[m], kinds[m], k, core, grads[m].shape), dst_ref=out_refs[m].at[k],
                    send_sem=send_sems.at[m, k], recv_sem=recv_sems.at[m, k], device_id=sib,
                    device_id_type=MESH).wait_recv()
        for cp in sends:
            cp.wait_send()

    return pl.pallas_call(
        body, name=name, in_specs=[_ANY] * n, out_specs=[_ANY] * n,
        out_shape=[jax.ShapeDtypeStruct((N_CHIPS,) + p, BF16) for p in pieces],
        scratch_shapes=[pltpu.SemaphoreType.DMA((n, N_CHIPS)), pltpu.SemaphoreType.DMA((n, N_CHIPS))],
    )(*grads)


def _pair_sum(g, recv, core, kind, *, name):
    pr, pc = _piece_shape(g.shape, kind)
    tr = _tile(pr, max(16, (1 << 19) // pc // 16 * 16), 16)
    nb = pr // tr

    def body(core_ref, g_ref, r_ref, o_ref):
        o_ref[0] = (g_ref[...].astype(F32) + r_ref[0].astype(F32)).astype(BF16)

    g_map = (lambda k, i, c: (c[0] * nb + i, k)) if kind == 'col' else (lambda k, i, c: ((2 * k + c[0]) * nb + i, 0))
    slot = pl.BlockSpec((1, tr, pc), lambda k, i, c: (k, i, 0))
    grid_spec = pltpu.PrefetchScalarGridSpec(
        num_scalar_prefetch=1, grid=(N_CHIPS, nb), in_specs=[pl.BlockSpec((tr, pc), g_map), slot], out_specs=slot)
    return pl.pallas_call(
        body, name=name, grid_spec=grid_spec, out_shape=jax.ShapeDtypeStruct((N_CHIPS, pr, pc), BF16),
        compiler_params=_cparams(("arbitrary", "arbitrary")),
    )(core, g, recv)


def _chip_scatter(sums, *, name):
    return _run_side(_chip_scatter_side(sums), name=name)


def _chip_scatter_side(sums):
    n = len(sums)
    nj = len(_REL_CHIPS)

    def plan(in_refs, out_refs, sems, op):
        send_sems, recv_sems = sems
        me = _me()
        op.phase(0)
        for m in range(n):
            for j, rel in enumerate(_REL_CHIPS):
                peer = _flip(me, rel)
                op.send(in_refs[m].at[_chip_of(peer)], out_refs[m].at[j], send_sems, recv_sems, (m, j), peer)
        op.phase(1)
        for m in range(n):
            for j, rel in enumerate(_REL_CHIPS):
                op.arrived(out_refs[m].at[j], send_sems, recv_sems, (m, j), _flip(me, rel))

    return _Side(list(sums), [jax.ShapeDtypeStruct((nj,) + s.shape[1:], BF16) for s in sums], {},
                 [pltpu.SemaphoreType.DMA((n, nj)), pltpu.SemaphoreType.DMA((n, nj))], _phases(plan, 2))


def _sum_into(own, recv, dest, layer, core, chip, *, name):
    _, pr, pc = own.shape
    tr = _tile(pr, max(16, (1 << 19) // pc // 16 * 16), 16)
    nb = pr // tr

    def body(core_ref, chip_ref, own_ref, r_ref, d_ref, o_ref):
        acc = own_ref[0].astype(F32)
        for s in range(len(_REL_CHIPS)):
            acc = acc + r_ref[s].astype(F32)
        o_ref[0] = acc

    grid_spec = pltpu.PrefetchScalarGridSpec(
        num_scalar_prefetch=2, grid=(nb,),
        in_specs=[pl.BlockSpec((1, tr, pc), lambda i, c, k: (k[0], i, 0)),
                  pl.BlockSpec((len(_REL_CHIPS), tr, pc), lambda i, c, k: (0, i, 0)), _ANY],
        out_specs=pl.BlockSpec((1, tr, pc), lambda i, c, k: (layer, c[0] * nb + i, 0)))
    return pl.pallas_call(
        body, name=name, grid_spec=grid_spec, out_shape=jax.ShapeDtypeStruct(dest.shape, F32),
        input_output_aliases={4: 0}, compiler_params=_cparams(("arbitrary",)),
    )(core, chip, own, recv, dest)


def _exchange_halves(blocks, *, name):
    n_out = len(blocks)
    n = sum(b.shape[0] for b in blocks)

    def body(*refs):
        in_refs, out_refs = refs[:n_out], refs[n_out:2 * n_out]
        send_sems, recv_sems = refs[2 * n_out:]
        me = _me()
        sib = _flip(me, (0, 0, 1))
        sends, m = [], 0
        for o in range(n_out):
            L, r2, _ = blocks[o].shape
            hr = r2 // 2
            for l in range(L):
                rows = pl.ds(pl.multiple_of(me[2] * hr, SUBLANES), hr)
                cp = pltpu.make_async_remote_copy(src_ref=in_refs[o].at[l, rows, :], dst_ref=out_refs[o].at[l, rows, :],
                                                  send_sem=send_sems.at[m], recv_sem=recv_sems.at[m], device_id=sib,
                                                  device_id_type=MESH)
                cp.start()
                sends.append(cp)
                m += 1
        m = 0
        for o in range(n_out):
            L, r2, _ = blocks[o].shape
            hr = r2 // 2
            for l in range(L):
                rows = pl.ds(pl.multiple_of(sib[2] * hr, SUBLANES), hr)
                theirs = out_refs[o].at[l, rows, :]
                pltpu.make_async_remote_copy(src_ref=theirs, dst_ref=theirs, send_sem=send_sems.at[m],
                                             recv_sem=recv_sems.at[m], device_id=sib, device_id_type=MESH).wait_recv()
                m += 1
        for cp in sends:
            cp.wait_send()

    return pl.pallas_call(
        body, name=name, in_specs=[_ANY] * n_out, out_specs=[_ANY] * n_out,
        out_shape=[jax.ShapeDtypeStruct(b.shape, F32) for b in blocks],
        input_output_aliases={o: o for o in range(n_out)},
        scratch_shapes=[pltpu.SemaphoreType.DMA((n,)), pltpu.SemaphoreType.DMA((n,))],
    )(*blocks)


def _adamw(w, g, m, v, *, name):
    R, C = w.shape
    tr = _tile(R, max(SUBLANES, (1 << 19) // C // SUBLANES * SUBLANES), SUBLANES)
    c1 = 1.0 / (1.0 - ADAM_B1 ** ADAM_STEP)
    c2 = 1.0 / (1.0 - ADAM_B2 ** ADAM_STEP)

    def body(w_ref, g_ref, m_ref, v_ref, go_ref, d_ref, nm_ref, nv_ref):
        gv = g_ref[...]
        go_ref[...] = gv
        nm = ADAM_B1 * m_ref[...] + (1.0 - ADAM_B1) * gv
        nv = ADAM_B2 * v_ref[...] + (1.0 - ADAM_B2) * (gv * gv)
        nm_ref[...] = nm
        nv_ref[...] = nv
        d_ref[...] = -ADAM_LR * ((nm * c1) / (jnp.sqrt(nv * c2) + ADAM_EPS) + ADAM_WD * w_ref[...])

    blk = pl.BlockSpec((tr, C), lambda i: (i, 0))
    return pl.pallas_call(
        body, name=name, grid=(R // tr,), in_specs=[blk] * 4, out_specs=[blk] * 4,
        out_shape=[jax.ShapeDtypeStruct((R, C), F32)] * 4, compiler_params=_cparams(("parallel",)),
    )(w, g, m, v)


_BIG = ['mlp_w1', 'mlp_w2', 'conv_w_in', 'conv_w_out', 'pool_w_in', 'pool_w_group', 'att_w_qkv', 'att_w_out',
        'ssm_w_glu']
_KIND = {'mlp_w1': 'col', 'mlp_w2': 'row', 'conv_w_in': 'col', 'conv_w_out': 'row', 'pool_w_in': 'row',
         'pool_w_group': 'row', 'att_w_qkv': 'col', 'att_w_out': 'row', 'ssm_w_glu': 'col'}
_TINY_SHARDED = ['conv_w', 'pool_scale', 'ssm_d']
_REPLICATED = ['norm_mix', 'norm_mlp', 'att_q_norm', 'att_k_norm', 'att_rel_bias', 'ssm_a_re', 'ssm_a_im',
               'ssm_log_dt', 'ssm_b_re', 'ssm_b_im', 'ssm_c_re', 'ssm_c_im']
_SMALL = _REPLICATED + _TINY_SHARDED
_ORDER = ['norm_mix', 'norm_mlp', 'mlp_w1', 'mlp_w2', 'conv_w_in', 'conv_w', 'conv_w_out', 'pool_w_in',
          'pool_w_group', 'pool_scale', 'att_w_qkv', 'att_q_norm', 'att_k_norm', 'att_rel_bias', 'att_w_out',
          'ssm_a_re', 'ssm_a_im', 'ssm_log_dt', 'ssm_b_re', 'ssm_b_im', 'ssm_c_re', 'ssm_c_im', 'ssm_d', 'ssm_w_glu']
_LAYER_OF = {'conv_w_in': 0, 'conv_w_out': 0, 'pool_w_in': 1, 'pool_w_group': 1, 'att_w_qkv': 2, 'att_w_out': 2,
             'ssm_w_glu': 3}


def _pack(arrays):
    flat = jnp.concatenate([a.reshape(-1).astype(F32) for a in arrays])
    n = flat.shape[0]
    total = -(-n // (SUBLANES * LANES)) * (SUBLANES * LANES)
    return jnp.pad(flat, (0, total - n)).reshape(total // LANES, LANES)


def _unpack(buf, shapes):
    flat = buf.reshape(-1)
    out, off = [], 0
    for s in shapes:
        n = int(np.prod(s))
        out.append(flat[off:off + n].reshape(s))
        off += n
    return out


def _matrices(w, name):
    t = w.reshape((-1,) + w.shape[-2:])
    return [t[l] for l in range(t.shape[0])]


def kernel(x, norm_mix, norm_mlp, mlp_w1, mlp_w2, conv_w_in, conv_w, conv_w_out, pool_w_in, pool_w_group, pool_scale, att_w_qkv, att_q_norm, att_k_norm, att_rel_bias, att_w_out, ssm_a_re, ssm_a_im, ssm_log_dt, ssm_b_re, ssm_b_im, ssm_c_re, ssm_c_im, ssm_d, ssm_w_glu, loss_target, m_norm_mix, m_norm_mlp, m_mlp_w1, m_mlp_w2, m_conv_w_in, m_conv_w, m_conv_w_out, m_pool_w_in, m_pool_w_group, m_pool_scale, m_att_w_qkv, m_att_q_norm, m_att_k_norm, m_att_rel_bias, m_att_w_out, m_ssm_a_re, m_ssm_a_im, m_ssm_log_dt, m_ssm_b_re, m_ssm_b_im, m_ssm_c_re, m_ssm_c_im, m_ssm_d, m_ssm_w_glu, v_norm_mix, v_norm_mlp, v_mlp_w1, v_mlp_w2, v_conv_w_in, v_conv_w, v_conv_w_out, v_pool_w_in, v_pool_w_group, v_pool_scale, v_att_w_qkv, v_att_q_norm, v_att_k_norm, v_att_rel_bias, v_att_w_out, v_ssm_a_re, v_ssm_a_im, v_ssm_log_dt, v_ssm_b_re, v_ssm_b_im, v_ssm_c_re, v_ssm_c_im, v_ssm_d, v_ssm_w_glu):
    args = dict(locals())
    W = {n: args[n] for n in _ORDER}
    M = {n: args['m_' + n] for n in _ORDER}
    V = {n: args['v_' + n] for n in _ORDER}
    depth = norm_mix.shape[0]
    d_model = x.shape[-1]
    chip = 2 * lax.axis_index("x") + lax.axis_index("y")

    tiny = _gather_small(_pack([W[n] for n in _TINY_SHARDED]), reduce=False, name="gather_vectors")
    tiny_shapes = [W[n].shape for n in _TINY_SHARDED]
    per_chip = [_unpack(tiny[2 * k], tiny_shapes) for k in range(N_CHIPS)]
    full_tiny = {n: jnp.concatenate([per_chip[k][i] for k in range(N_CHIPS)], axis=-1)
                 for i, n in enumerate(_TINY_SHARDED)}
    chip_arr = chip.astype(jnp.int32).reshape(1)
    groups = [[] for _ in range(depth)]
    shard3 = {n: W[n].reshape((-1,) + W[n].shape[-2:]) for n in _BIG}
    for n in _BIG:
        for l in range(shard3[n].shape[0]):
            groups[l if n.startswith('mlp') else _LAYER_OF[n]].append((n, l))
    core = lax.axis_index("c").astype(jnp.int32).reshape(1)
    def keys_of(i, parts):
        want = {'w1': lambda n: n == 'mlp_w1', 'w2': lambda n: n == 'mlp_w2', 'mix': lambda n: not n.startswith('mlp')}
        return [(n, l) for n, l in groups[i] if any(want[pt](n) for pt in parts)]

    class Weights:
        def __init__(self):
            self.mats = {}

        def side(self, i, parts):
            placed = []
            for n, l in keys_of(i, parts):
                _, r, c = shard3[n].shape
                shape = (r, c * N_CHIPS) if _KIND[n] == 'col' else (r * N_CHIPS, c)
                placed.append(_place_block(shard3[n], lax.empty(shape, BF16), l, chip_arr, _KIND[n],
                                           name=f"place_{n}_{l}"))
            return _gather_side(placed, [_KIND[n] for n, _ in keys_of(i, parts)])

        def deliver(self, i, parts, outs):
            for key, t in zip(keys_of(i, parts), outs):
                self.mats[key] = t

        def now(self, i, parts):
            self.deliver(i, parts, _run_side(self.side(i, parts), name=f"gather_weights_{i}_{'_'.join(parts)}"))

        def layer(self, i):
            got = {}
            for n, l in groups[i]:
                if (n, l) in self.mats:
                    got.setdefault(n, []).append(self.mats[(n, l)])
            return {n: (jnp.stack(v) if n == 'pool_w_group' else v[0]) for n, v in got.items()}

    class Sink:
        def __init__(self):
            self.blocks = {n: lax.empty((shard3[n].shape[0],) + shard3[n].shape[1:], F32) for n in _BIG}
            self.pair_sums = {}

        def stage_a(self, i, parts, big):
            keys = keys_of(i, parts)
            kinds = [_KIND[n] for n, _ in keys]
            mats = [(_matrices(big[n], n)[l] if n == 'pool_w_group' else big[n]) for n, l in keys]
            from_sibling = _pair_exchange(mats, kinds, name=f"pair_exchange_{i}_{'_'.join(parts)}")
            for key, gm, t, kd in zip(keys, mats, from_sibling, kinds):
                self.pair_sums[key] = _pair_sum(gm, t, core, kd, name=f"pair_sum_{key[0]}_{key[1]}")

        def side(self, i, parts):
            return _chip_scatter_side([self.pair_sums[key] for key in keys_of(i, parts)])

        def finish(self, i, parts, outs):
            for (n, l), t in zip(keys_of(i, parts), outs):
                self.blocks[n] = _sum_into(self.pair_sums[(n, l)], t, self.blocks[n], l, core, chip_arr,
                                           name=f"sum_grads_{n}_{l}")

        def now(self, i, parts):
            self.finish(i, parts, _run_side(self.side(i, parts), name=f"chip_scatter_{i}_{'_'.join(parts)}"))

    p = dict(
        norm_mix=norm_mix, norm_mlp=norm_mlp, conv_w=full_tiny['conv_w'][0], pool_scale=full_tiny['pool_scale'],
        att_q_norm=att_q_norm, att_k_norm=att_k_norm, att_rel_bias=att_rel_bias[0],
        ssm_a_re=ssm_a_re[0], ssm_a_im=ssm_a_im[0], ssm_log_dt=ssm_log_dt[0], ssm_b_re=ssm_b_re[0],
        ssm_b_im=ssm_b_im[0], ssm_c_re=ssm_c_re[0], ssm_c_im=ssm_c_im[0], ssm_d=full_tiny['ssm_d'])

    sink = Sink()
    loss_cols, dx, g = _local_step(x[0], loss_target[0], p, Weights(), sink)
    loss = lax.psum(0.5 * jnp.sum(loss_cols) / d_model, ("x", "y", "c"))
    reduced = _exchange_halves([sink.blocks[n] for n in _BIG], name="exchange_halves")
    grads = {n: t.reshape(W[n].shape) for n, t in zip(_BIG, reduced)}

    small_full_shapes = [W[n].shape for n in _REPLICATED] + [full_tiny[n].shape for n in _TINY_SHARDED]
    gsmall = _gather_small(_pack([g[n] for n in _SMALL]), reduce=True, name="reduce_small_grads")
    for n, t in zip(_SMALL, _unpack(gsmall, small_full_shapes)):
        if n in _TINY_SHARDED:
            width = W[n].shape[-1]
            t = lax.dynamic_slice_in_dim(t, chip * width, width, axis=t.ndim - 1)
        grads[n] = t.reshape(W[n].shape)

    delta, new_m, new_v = {}, {}, {}
    for n in _BIG:
        shp = W[n].shape
        two = (int(np.prod(shp[:-1])), shp[-1])
        gout, d, nm, nv = _adamw(W[n].reshape(two), grads[n].reshape(two), M[n].reshape(two), V[n].reshape(two),
                                 name=f"adamw_{n}")
        grads[n], delta[n], new_m[n], new_v[n] = gout.reshape(shp), d.reshape(shp), nm.reshape(shp), nv.reshape(shp)
    shapes = [W[n].shape for n in _SMALL]
    _, d, nm, nv = _adamw(_pack([W[n] for n in _SMALL]), _pack([grads[n] for n in _SMALL]),
                          _pack([M[n] for n in _SMALL]), _pack([V[n] for n in _SMALL]), name="adamw_small")
    for n, a, b, c in zip(_SMALL, _unpack(d, shapes), _unpack(nm, shapes), _unpack(nv, shapes)):
        delta[n], new_m[n], new_v[n] = a, b, c

    return (loss, dx.reshape(x.shape), *[grads[n] for n in _ORDER], *[delta[n] for n in _ORDER],
            *[new_m[n] for n in _ORDER], *[new_v[n] for n in _ORDER])
```
